```python
import math
import jax, jax.numpy as jnp
from jax import lax
import numpy as np

D_MODEL = 1024
BATCH = 8
SEQ = 4096
DEPTH = 2

GRID_W = 64
CTX_LEN = 256
EPS = 1e-6

GLA_HEADS = 4
GLA_DK = D_MODEL // 16
GLA_DV = D_MODEL // 8
GLA_QK = GLA_HEADS * GLA_DK
GLA_V = GLA_HEADS * GLA_DV
GLA_RANK = 16
GLA_TAU = 16.0
GLA_CHUNK = 16

GDN_HEADS = 4
GDN_DK = D_MODEL // 8
GDN_DV = D_MODEL // 8
GDN_QK = GDN_HEADS * GDN_DK
GDN_V = GDN_HEADS * GDN_DV
GDN_CONV_DIM = 2 * GDN_QK + GDN_V
CONV_W = 3
GDN_CHUNK = 64

D_MIX = GLA_V + GDN_V
D_FF = 4 * D_MODEL
PROJ_WIDTHS = (GLA_QK, GLA_QK, GLA_V, GLA_V, GLA_RANK, GDN_CONV_DIM, GDN_V, 2 * GDN_HEADS, 2 * GDN_HEADS)
PROJ_DIM = GLA_QK * 2 + GLA_V * 2 + GLA_RANK + GDN_CONV_DIM + GDN_V + 4 * GDN_HEADS

kernel_name = "hybrid_gla_gdn_prefix_dit"


def rmsnorm(x, g):
    xf = x.astype(jnp.float32)
    y = xf * lax.rsqrt(jnp.mean(xf * xf, axis=-1, keepdims=True) + EPS)
    return (y * g.astype(jnp.float32)).astype(x.dtype)


def l2norm(x):
    return x * lax.rsqrt(jnp.sum(x * x, axis=-1, keepdims=True) + EPS)


def _heads(u, n_heads):
    B, T, _ = u.shape
    return u.reshape(B, T, n_heads, -1).transpose(0, 2, 1, 3)


def short_conv(u, w, rows):
    B, T, C = u.shape
    if rows is not None:
        u = u.reshape(B, rows, GRID_W, C)
    pad = CONV_W // 2
    up = jnp.pad(u, [(0, 0)] * (u.ndim - 2) + [(pad, pad), (0, 0)])
    L = u.shape[-2]
    y = sum(w[i] * up[..., i:i + L, :] for i in range(CONV_W))
    return y.reshape(B, T, C)


def gla_chunked(q, k, v, log_a, S0):
    B, H, T, DK = q.shape
    DV = v.shape[-1]
    C = GLA_CHUNK
    N = T // C
    q = q.reshape(B, H, N, C, DK)
    k = k.reshape(B, H, N, C, DK)
    v = v.reshape(B, H, N, C, DV)
    b = jnp.cumsum(log_a.reshape(B, H, N, C, DK), axis=3)
    b_last = b[:, :, :, -1:, :]
    causal = jnp.tril(jnp.ones((C, C), dtype=bool))[:, :, None]
    diff = b[:, :, :, :, None, :] - b[:, :, :, None, :, :]
    decay = jnp.exp(jnp.where(causal, diff, -jnp.inf))
    scores = jnp.sum(q[:, :, :, :, None, :] * k[:, :, :, None, :, :] * decay, axis=-1)
    intra = jnp.einsum('bhnij,bhnje->bhnie', scores, v)
    dS = jnp.einsum('bhnjd,bhnje->bhnde', k * jnp.exp(b_last - b), v)
    a_chunk = jnp.exp(b_last[:, :, :, 0, :])

    def step(S, inp):
        a_n, dS_n = inp
        return a_n[..., None] * S + dS_n, S

    S_fin, S_prev = lax.scan(step, S0, (jnp.moveaxis(a_chunk, 2, 0), jnp.moveaxis(dS, 2, 0)))
    S_prev = jnp.moveaxis(S_prev, 0, 2)
    inter = jnp.einsum('bhnid,bhnde->bhnie', q * jnp.exp(b), S_prev)
    return (intra + inter).reshape(B, H, T, DV), S_fin


def gdn_chunked(q, k, v, log_alpha, beta, S0):
    B, H, T, DK = q.shape
    DV = v.shape[-1]
    C = GDN_CHUNK
    N = T // C
    q = q.reshape(B, H, N, C, DK)
    k = k.reshape(B, H, N, C, DK)
    v = v.reshape(B, H, N, C, DV)
    beta = beta.reshape(B, H, N, C)
    g = jnp.cumsum(log_alpha.reshape(B, H, N, C), axis=-1)
    g_last = g[..., -1:]
    strict = jnp.tril(jnp.ones((C, C), dtype=bool), -1)
    incl = jnp.tril(jnp.ones((C, C), dtype=bool))
    gdiff = g[..., :, None] - g[..., None, :]
    kk = jnp.einsum('bhnid,bhnjd->bhnij', k, k)
    L = jnp.where(strict, beta[..., :, None] * kk * jnp.exp(jnp.where(strict, gdiff, 0.0)), 0.0)
    A = jnp.eye(C, dtype=L.dtype) + L
    U = lax.linalg.triangular_solve(A, beta[..., None] * v, left_side=True, lower=True, unit_diagonal=True)
    W = lax.linalg.triangular_solve(A, (beta * jnp.exp(g))[..., None] * k, left_side=True, lower=True,
                                    unit_diagonal=True)
    qk = jnp.einsum('bhnid,bhnjd->bhnij', q, k)
    Aqk = jnp.where(incl, qk * jnp.exp(jnp.where(incl, gdiff, 0.0)), 0.0)
    q_dec = q * jnp.exp(g)[..., None]
    k_dec = k * jnp.exp(g_last - g)[..., None]
    a_chunk = jnp.exp(g_last[..., 0])

    def step(S, inp):
        U_n, W_n, Aqk_n, qd_n, kd_n, a_n = inp
        delta = U_n - jnp.einsum('bhcd,bhde->bhce', W_n, S)
        o = jnp.einsum('bhcd,bhde->bhce', qd_n, S) + jnp.einsum('bhij,bhje->bhie', Aqk_n, delta)
        S_new = a_n[..., None, None] * S + jnp.einsum('bhcd,bhce->bhde', kd_n, delta)
        return S_new, o

    xs = tuple(jnp.moveaxis(t, 2, 0) for t in (U, W, Aqk, q_dec, k_dec, a_chunk))
    S_fin, o = lax.scan(step, S0, xs)
    return jnp.moveaxis(o, 0, 2).reshape(B, H, T, DV), S_fin


def prepare_mixers(h, w_in, w_lr, b_lr, conv_w, a_log, dt_bias, rows):
    f32 = jnp.float32
    B, T, _ = h.shape
    p = h @ w_in
    splits = np.cumsum(PROJ_WIDTHS)[:-1].tolist()
    q_a, k_a, v_a, g_a, r_a, qkv_b, g_b, a_b, beta_b = jnp.split(p, splits, axis=-1)
    gla_q = _heads(q_a.astype(f32), GLA_HEADS) * GLA_DK ** -0.5
    gla_k = _heads(k_a.astype(f32), GLA_HEADS)
    gla_v = _heads(v_a.astype(f32), GLA_HEADS)
    lr = jnp.einsum('btr,zrk->zbtk', r_a, w_lr) + b_lr[:, None, None, :]
    gla_log_a = (jax.nn.log_sigmoid(lr.astype(f32)) / GLA_TAU)
    gla_log_a = gla_log_a.reshape(2, B, T, GLA_HEADS, GLA_DK).transpose(0, 1, 3, 2, 4)
    qkv = jax.nn.silu(short_conv(qkv_b, conv_w, rows)).astype(f32)
    q_b, k_b, v_b = jnp.split(qkv, [GDN_QK, 2 * GDN_QK], axis=-1)
    gdn_q = l2norm(_heads(q_b, GDN_HEADS)) * GDN_DK ** -0.5
    gdn_k = l2norm(_heads(k_b, GDN_HEADS))
    gdn_v = _heads(v_b, GDN_HEADS)
    a = a_b.astype(f32).reshape(B, T, 2, GDN_HEADS)
    log_alpha = -jnp.exp(a_log.astype(f32)) * jax.nn.softplus(a + dt_bias.astype(f32))
    gdn_log_alpha = log_alpha.transpose(2, 0, 3, 1)
    gdn_beta = jax.nn.sigmoid(beta_b.astype(f32).reshape(B, T, 2, GDN_HEADS)).transpose(2, 0, 3, 1)
    return dict(gla_q=gla_q, gla_k=gla_k, gla_v=gla_v, gla_log_a=gla_log_a, gla_g=g_a.astype(f32),
                gdn_q=gdn_q, gdn_k=gdn_k, gdn_v=gdn_v, gdn_log_alpha=gdn_log_alpha, gdn_beta=gdn_beta,
                gdn_g=g_b.astype(f32))


def run_mixers(p, init):
    B = p['gla_q'].shape[0]
    if init is None:
        zg = jnp.zeros((B, GLA_HEADS, GLA_DK, GLA_DV), jnp.float32)
        zd = jnp.zeros((B, GDN_HEADS, GDN_DK, GDN_DV), jnp.float32)
        init = (zg, zg, zd, zd)
    o_gla, o_gdn, fin_gla, fin_gdn = 0.0, 0.0, [], []
    for d in range(2):
        f = (lambda u: jnp.flip(u, axis=2)) if d == 1 else (lambda u: u)
        oa, sa = gla_chunked(f(p['gla_q']), f(p['gla_k']), f(p['gla_v']), f(p['gla_log_a'][d]), init[d])
        ob, sb = gdn_chunked(f(p['gdn_q']), f(p['gdn_k']), f(p['gdn_v']), f(p['gdn_log_alpha'][d]),
                             f(p['gdn_beta'][d]), init[2 + d])
        o_gla = o_gla + f(oa)
        o_gdn = o_gdn + f(ob)
        fin_gla.append(sa)
        fin_gdn.append(sb)
    return o_gla, o_gdn, (fin_gla[0], fin_gla[1], fin_gdn[0], fin_gdn[1])


def merge_heads(o_gla, o_gdn, p, gla_norm_g, gdn_norm_g, w_out, dtype):
    B, _, T, _ = o_gla.shape
    ya = rmsnorm(o_gla, gla_norm_g).transpose(0, 2, 1, 3).reshape(B, T, GLA_V) * jax.nn.silu(p['gla_g'])
    yb = rmsnorm(o_gdn, gdn_norm_g).transpose(0, 2, 1, 3).reshape(B, T, GDN_V) * jax.nn.silu(p['gdn_g'])
    return jnp.concatenate([ya, yb], axis=-1).astype(dtype) @ w_out


def sq_relu_mlp(h, w1, w2):
    return jnp.square(jax.nn.relu(h @ w1)) @ w2


def setup_inputs(seed: int = 0) -> dict:
    key = jax.random.key(seed)
    ks = jax.random.split(key, 20)
    f32 = jnp.float32

    def nrm(k, shape, scale):
        return jax.random.normal(k, shape, f32) * scale

    x = nrm(ks[0], (BATCH, SEQ, D_MODEL), 1.0)
    c = nrm(ks[1], (BATCH, D_MODEL), 1.0)
    ctx = nrm(ks[2], (BATCH, CTX_LEN, D_MODEL), 1.0)
    c_ctx = nrm(ks[3], (D_MODEL,), 1.0)
    w_ada = nrm(ks[4], (DEPTH, D_MODEL, 6 * D_MODEL), 0.5 * D_MODEL ** -0.5)
    b_ada = nrm(ks[5], (DEPTH, 6 * D_MODEL), 0.01)
    norm1_g = 1.0 + nrm(ks[6], (DEPTH, D_MODEL), 0.02)
    norm2_g = 1.0 + nrm(ks[7], (DEPTH, D_MODEL), 0.02)
    w_in = nrm(ks[8], (DEPTH, D_MODEL, PROJ_DIM), D_MODEL ** -0.5)
    gla_w_lr = nrm(ks[9], (DEPTH, 2, GLA_RANK, GLA_QK), GLA_RANK ** -0.5)
    gla_b_lr = nrm(ks[10], (DEPTH, 2, GLA_QK), 0.1)
    gdn_conv_w = nrm(ks[11], (DEPTH, CONV_W, GDN_CONV_DIM), CONV_W ** -0.5)
    gdn_a_log = jnp.log(jax.random.uniform(ks[12], (DEPTH, 2, GDN_HEADS), f32, 1.0, 16.0))
    dt = jnp.exp(jax.random.uniform(ks[13], (DEPTH, 2, GDN_HEADS), f32, math.log(1e-3), math.log(1e-1)))
    gdn_dt_bias = dt + jnp.log(-jnp.expm1(-dt))
    gla_norm_g = 1.0 + nrm(ks[14], (DEPTH, GLA_DV), 0.02)
    gdn_norm_g = 1.0 + nrm(ks[15], (DEPTH, GDN_DV), 0.02)
    w_out = nrm(ks[16], (DEPTH, D_MIX, D_MODEL), D_MIX ** -0.5)
    w_ff1 = nrm(ks[17], (DEPTH, D_MODEL, D_FF), D_MODEL ** -0.5)
    w_ff2 = nrm(ks[18], (DEPTH, D_FF, D_MODEL), D_FF ** -0.5)
    final_norm_g = 1.0 + nrm(ks[19], (D_MODEL,), 0.02)
    return {"x": x, "c": c, "ctx": ctx, "c_ctx": c_ctx, "w_ada": w_ada, "b_ada": b_ada,
            "norm1_g": norm1_g, "norm2_g": norm2_g, "w_in": w_in, "gla_w_lr": gla_w_lr,
            "gla_b_lr": gla_b_lr, "gdn_conv_w": gdn_conv_w, "gdn_a_log": gdn_a_log,
            "gdn_dt_bias": gdn_dt_bias, "gla_norm_g": gla_norm_g, "gdn_norm_g": gdn_norm_g,
            "w_out": w_out, "w_ff1": w_ff1, "w_ff2": w_ff2, "final_norm_g": final_norm_g}


def reference(x, c, ctx, c_ctx, w_ada, b_ada, norm1_g, norm2_g, w_in, gla_w_lr, gla_b_lr, gdn_conv_w,
              gdn_a_log, gdn_dt_bias, gla_norm_g, gdn_norm_g, w_out, w_ff1, w_ff2, final_norm_g):
    B, T, _ = x.shape
    rows = T // GRID_W
    xc = ctx
    silu_c = jax.nn.silu(c)
    silu_cc = jax.nn.silu(c_ctx)
    for l in range(DEPTH):
        last = l == DEPTH - 1
        mod_x = (silu_c @ w_ada[l] + b_ada[l])[:, None, :]
        mod_c = silu_cc @ w_ada[l] + b_ada[l]
        sh1x, sc1x, gt1x, sh2x, sc2x, gt2x = jnp.split(mod_x, 6, axis=-1)
        sh1c, sc1c, gt1c, sh2c, sc2c, gt2c = jnp.split(mod_c, 6, axis=-1)
        hx = rmsnorm(x, norm1_g[l]) * (1.0 + sc1x) + sh1x
        hc = rmsnorm(xc, norm1_g[l]) * (1.0 + sc1c) + sh1c
        args = (w_in[l], gla_w_lr[l], gla_b_lr[l], gdn_conv_w[l], gdn_a_log[l], gdn_dt_bias[l])
        p_c = prepare_mixers(hc, *args, None)
        p_x = prepare_mixers(hx, *args, rows)
        oc_gla, oc_gdn, ctx_states = run_mixers(p_c, None)
        ox_gla, ox_gdn, _ = run_mixers(p_x, ctx_states)
        x = x + gt1x * merge_heads(ox_gla, ox_gdn, p_x, gla_norm_g[l], gdn_norm_g[l], w_out[l], x.dtype)
        hx2 = rmsnorm(x, norm2_g[l]) * (1.0 + sc2x) + sh2x
        x = x + gt2x * sq_relu_mlp(hx2, w_ff1[l], w_ff2[l])
        if not last:
            xc = xc + gt1c * merge_heads(oc_gla, oc_gdn, p_c, gla_norm_g[l], gdn_norm_g[l], w_out[l], xc.dtype)
            hc2 = rmsnorm(xc, norm2_g[l]) * (1.0 + sc2c) + sh2c
            xc = xc + gt2c * sq_relu_mlp(hc2, w_ff1[l], w_ff2[l])
    return rmsnorm(x, final_norm_g)
```

```python
import functools

import jax
import jax.numpy as jnp
from jax import lax
from jax.experimental import pallas as pl
from jax.experimental.pallas import tpu as pltpu

EPS = 1e-6
GRID_W = 64
GLA_HEADS = 4
GDN_HEADS = 4
GLA_TAU = 16.0
TOKEN_BLOCK = 256
CHUNK = 64
LANES = 128
MOD_ROWS = 16
VMEM_LIMIT = 56 * 1024 * 1024

F32 = jnp.float32
BF16 = jnp.bfloat16


def _dot(a, b):
    return jnp.dot(a, b, preferred_element_type=F32)


def _dot_nt(a, b):
    return lax.dot_general(a, b, (((1,), (1,)), ((), ())), preferred_element_type=F32)


def _dot_tn(a, b):
    return lax.dot_general(a, b, (((0,), (0,)), ((), ())), preferred_element_type=F32)


def _silu(x):
    return x / (1.0 + jnp.exp(-x))


def _sigmoid(x):
    return 1.0 / (1.0 + jnp.exp(-x))


def _softplus(x):
    return jnp.maximum(x, 0.0) + jnp.log(1.0 + jnp.exp(-jnp.abs(x)))


def _log_sigmoid(x):
    return jnp.minimum(x, 0.0) - jnp.log(1.0 + jnp.exp(-jnp.abs(x)))


def _rms(x):
    return x * lax.rsqrt(jnp.mean(x * x, axis=-1, keepdims=True) + EPS)


def _cumsum_dot(tri_bf16, x, left):
    hi = x.astype(BF16)
    lo = (x - hi.astype(F32)).astype(BF16)
    if left:
        return _dot(tri_bf16, hi) + _dot(tri_bf16, lo)
    return _dot(hi, tri_bf16) + _dot(lo, tri_bf16)


def _tri(n, upper, strict, dtype=F32):
    r = lax.broadcasted_iota(jnp.int32, (n, n), 0)
    c = lax.broadcasted_iota(jnp.int32, (n, n), 1)
    if upper:
        m = (r < c) if strict else (r <= c)
    else:
        m = (r > c) if strict else (r >= c)
    return m


def _mod_kernel(cc_ref, w_ref, b_ref, o_ref):
    s = _silu(cc_ref[...]).astype(BF16)
    o_ref[0] = _dot(s, w_ref[0]) + b_ref[0]


def _modulation(cc, w_ada_bf, b_ada):
    depth, d, d6 = w_ada_bf.shape
    nblk = d6 // d
    return pl.pallas_call(
        _mod_kernel,
        out_shape=jax.ShapeDtypeStruct((depth, MOD_ROWS, d6), F32),
        grid=(depth, nblk),
        in_specs=[
            pl.BlockSpec((MOD_ROWS, d), lambda l, n: (0, 0)),
            pl.BlockSpec((1, d, d), lambda l, n: (l, 0, n)),
            pl.BlockSpec((1, 1, d), lambda l, n: (l, 0, n)),
        ],
        out_specs=pl.BlockSpec((1, MOD_ROWS, d), lambda l, n: (l, 0, n)),
        compiler_params=pltpu.CompilerParams(dimension_semantics=("parallel", "parallel")),
        name="adaln_modulation",
    )(cc, w_ada_bf, b_ada.reshape(depth, 1, d6))


def _in_proj_kernel(x_ref, mod_ref, g1_ref, wm_ref, ws_ref, wst_ref, wlr_ref, blr_ref, cw_ref,
                    arow_ref, dtrow_ref, acol_ref, dtcol_ref,
                    gq_ref, gk_ref, gv_ref, gg_ref, gla_ref, dq_ref, dk_ref, dv_ref, dg_ref,
                    sc_ref, sr_ref, *, d, qk_a, v_a, rank, conv_dim, qk_b, v_b, nh_b):
    j = pl.program_id(1)
    x = x_ref[0]
    m = mod_ref[0]
    sh1, sc1 = m[:, 0:d], m[:, d:2 * d]
    h = _rms(x) * g1_ref[...] * (1.0 + sc1) + sh1
    hb = h.astype(BF16)
    p = _dot(hb, wm_ref[...])
    ps = _dot(hb, ws_ref[...])
    pst = _dot_nt(wst_ref[...], hb)

    o = 0
    gq_ref[0] = (p[:, o:o + qk_a] * (float(qk_a // GLA_HEADS) ** -0.5)).astype(BF16)
    o += qk_a
    gk_ref[0] = p[:, o:o + qk_a].astype(BF16)
    o += qk_a
    gv_ref[0] = p[:, o:o + v_a].astype(BF16)
    o += v_a
    gg_ref[0] = p[:, o:o + v_a].astype(BF16)
    o += v_a
    r_a = ps[:, 0:rank].astype(BF16)
    for dr in range(2):
        lr = _dot(r_a, wlr_ref[dr]) + blr_ref[dr]
        gla_ref[0, :, dr * qk_a:(dr + 1) * qk_a] = _log_sigmoid(lr) * (1.0 / GLA_TAU)

    tb = x.shape[0]
    u = p[:, o:o + conv_dim]
    o += conv_dim
    t = lax.broadcasted_iota(jnp.int32, (tb, 1), 0)
    seg_mask = jnp.where(j == 0, tb - 1, GRID_W - 1)
    first = (t & seg_mask) == 0
    last = (t & seg_mask) == seg_mask
    up = jnp.where(first, 0.0, pltpu.roll(u, 1, 0))
    un = jnp.where(last, 0.0, pltpu.roll(u, tb - 1, 0))
    cw = cw_ref[...]
    s = _silu(cw[0:1] * up + cw[1:2] * u + cw[2:3] * un)
    dkh = qk_b // nh_b
    for hd in range(nh_b):
        qh = s[:, hd * dkh:(hd + 1) * dkh]
        kh = s[:, qk_b + hd * dkh:qk_b + (hd + 1) * dkh]
        qn = qh * lax.rsqrt(jnp.sum(qh * qh, axis=-1, keepdims=True) + EPS) * (float(dkh) ** -0.5)
        kn = kh * lax.rsqrt(jnp.sum(kh * kh, axis=-1, keepdims=True) + EPS)
        dq_ref[0, :, hd * dkh:(hd + 1) * dkh] = qn.astype(BF16)
        dk_ref[0, :, hd * dkh:(hd + 1) * dkh] = kn.astype(BF16)
    dv_ref[0] = s[:, 2 * qk_b:2 * qk_b + v_b].astype(BF16)
    dg_ref[0] = p[:, o:o + v_b].astype(BF16)

    nd = 2 * nh_b
    a_c = ps[:, rank:rank + nd]
    b_c = ps[:, rank + nd:rank + 2 * nd]
    sc_ref[0, :, 0:nd] = -jnp.exp(arow_ref[...]) * _softplus(a_c + dtrow_ref[...])
    sc_ref[0, :, nd:2 * nd] = _sigmoid(b_c)
    sr_ref[0, 0, 0:nd, :] = -jnp.exp(acol_ref[...]) * _softplus(pst[0:nd] + dtcol_ref[...])
    sr_ref[0, 0, nd:2 * nd, :] = _sigmoid(pst[nd:2 * nd])


def _in_proj(xs, mod, layer, bsz, g1, wm, ws, wst, wlr, blr, cw, a_log, dt_bias, dims):
    b, s, d = xs.shape
    nb = s // TOKEN_BLOCK
    qk_a, v_a, rank, conv_dim, qk_b, v_b, nh_b = dims
    nd = 2 * nh_b
    tb = TOKEN_BLOCK
    const2 = lambda shape: pl.BlockSpec(shape, lambda i, j: (0,) * len(shape))
    tok = lambda n: pl.BlockSpec((1, tb, n), lambda i, j: (i, j, 0))
    mod_spec = pl.BlockSpec((1, 1, 6 * d), lambda i, j: (layer * MOD_ROWS + jnp.where(j == 0, bsz, i), 0, 0))
    out_shape = [
        jax.ShapeDtypeStruct((b, s, qk_a), BF16), jax.ShapeDtypeStruct((b, s, qk_a), BF16),
        jax.ShapeDtypeStruct((b, s, v_a), BF16), jax.ShapeDtypeStruct((b, s, v_a), BF16),
        jax.ShapeDtypeStruct((b, s, 2 * qk_a), F32),
        jax.ShapeDtypeStruct((b, s, qk_b), BF16), jax.ShapeDtypeStruct((b, s, qk_b), BF16),
        jax.ShapeDtypeStruct((b, s, v_b), BF16), jax.ShapeDtypeStruct((b, s, v_b), BF16),
        jax.ShapeDtypeStruct((b, s, 2 * nd), F32),
        jax.ShapeDtypeStruct((b, nb, 2 * nd, tb), F32),
    ]
    out_specs = [tok(qk_a), tok(qk_a), tok(v_a), tok(v_a), tok(2 * qk_a),
                 tok(qk_b), tok(qk_b), tok(v_b), tok(v_b), tok(2 * nd),
                 pl.BlockSpec((1, 1, 2 * nd, tb), lambda i, j: (i, j, 0, 0))]
    kern = functools.partial(_in_proj_kernel, d=d, qk_a=qk_a, v_a=v_a, rank=rank, conv_dim=conv_dim,
                             qk_b=qk_b, v_b=v_b, nh_b=nh_b)
    return pl.pallas_call(
        kern, out_shape=out_shape, grid=(b, nb),
        in_specs=[
            pl.BlockSpec((1, tb, d), lambda i, j: (i, j, 0)), mod_spec, const2((1, d)),
            const2(wm.shape), const2(ws.shape), const2(wst.shape), const2(wlr.shape), const2((2, 1, qk_a)),
            const2(cw.shape), const2((1, nd)), const2((1, nd)), const2((nd, 1)), const2((nd, 1)),
        ],
        out_specs=out_specs,
        compiler_params=pltpu.CompilerParams(dimension_semantics=("parallel", "parallel"),
                                             vmem_limit_bytes=VMEM_LIMIT),
        name="in_proj",
    )(xs, mod, g1.reshape(1, d), wm, ws, wst, wlr, blr.reshape(2, 1, -1), cw,
      a_log.reshape(1, nd), dt_bias.reshape(1, nd), a_log.reshape(nd, 1), dt_bias.reshape(nd, 1))


def _bwd_block(j, nb):
    return jnp.where(j == 0, 0, nb - j)


def _gla_kernel(qf_ref, kf_ref, vf_ref, laf_ref, qb_ref, kb_ref, vb_ref, lab_ref,
                of_ref, ob_ref, st_ref, *, dk, dv):
    j = pl.program_id(1)

    @pl.when(j == 0)
    def _():
        st_ref[...] = jnp.zeros_like(st_ref)

    tb = qf_ref.shape[1]
    nchunk = tb // CHUNK
    npair = (GLA_HEADS * dk) // LANES
    hpp = LANES // dk
    lane = lax.broadcasted_iota(jnp.int32, (1, LANES), 1)
    mid = CHUNK // 2

    for dr, (q_ref, k_ref, v_ref, la_ref, o_ref) in enumerate(
            ((qf_ref, kf_ref, vf_ref, laf_ref, of_ref), (qb_ref, kb_ref, vb_ref, lab_ref, ob_ref))):
        rev = dr == 1
        tri = _tri(CHUNK, rev, False).astype(BF16)
        causal = _tri(CHUNK, rev, False)
        last = 0 if rev else CHUNK - 1
        for pr in range(npair):
            st = st_ref[dr, pr]
            for ci in range(nchunk):
                c = nchunk - 1 - ci if rev else ci
                rows = slice(c * CHUNK, (c + 1) * CHUNK)
                lanes = slice(pr * LANES, (pr + 1) * LANES)
                bcum = _cumsum_dot(tri, la_ref[0, rows, lanes], True)
                b_last = bcum[last:last + 1]
                b_mid = bcum[mid:mid + 1]
                qc = q_ref[0, rows, lanes].astype(F32)
                kc = k_ref[0, rows, lanes].astype(F32)
                q_dec = qc * jnp.exp(bcum)
                k_hat = (kc * jnp.exp(b_last - bcum)).astype(BF16)
                q_mid = qc * jnp.exp(bcum - b_mid)
                k_mid = (kc * jnp.exp(b_mid - bcum)).astype(BF16)
                st_bf = st.astype(BF16)
                dst = None
                for hh in range(hpp):
                    head = pr * hpp + hh
                    lm = (lane >= hh * dk) & (lane < (hh + 1) * dk)
                    vh = v_ref[0, rows, head * dv:(head + 1) * dv]
                    sc = _dot_nt(jnp.where(lm, q_mid, 0.0).astype(BF16), k_mid)
                    pm = jnp.where(causal, sc, 0.0).astype(BF16)
                    intra = _dot(pm, vh)
                    inter = _dot_nt(jnp.where(lm, q_dec, 0.0).astype(BF16), st_bf)
                    o_ref[0, rows, head * dv:(head + 1) * dv] = intra + inter
                    dh = _dot_tn(vh, k_hat)
                    dst = dh if dst is None else jnp.where(lm, dh, dst)
                st = st * jnp.exp(b_last) + dst
            st_ref[dr, pr] = st


def _gla_scan(gq, gk, gv, gla, nheads):
    b, s, qk = gq.shape
    v = gv.shape[2]
    nb = s // TOKEN_BLOCK
    tb = TOKEN_BLOCK
    dk, dv = qk // nheads, v // nheads
    fwd = lambda n, col=0: pl.BlockSpec((1, tb, n), lambda i, j: (i, j, col))
    bwd = lambda n, col=0: pl.BlockSpec((1, tb, n), lambda i, j: (i, _bwd_block(j, nb), col))
    return pl.pallas_call(
        functools.partial(_gla_kernel, dk=dk, dv=dv),
        out_shape=[jax.ShapeDtypeStruct((b, s, v), F32), jax.ShapeDtypeStruct((b, s, v), F32)],
        grid=(b, nb),
        in_specs=[fwd(qk), fwd(qk), fwd(v), fwd(qk, 0), bwd(qk), bwd(qk), bwd(v), bwd(qk, 1)],
        out_specs=[fwd(v), bwd(v)],
        scratch_shapes=[pltpu.VMEM((2, qk // LANES, dv, LANES), F32)],
        compiler_params=pltpu.CompilerParams(dimension_semantics=("parallel", "arbitrary"),
                                             vmem_limit_bytes=VMEM_LIMIT),
        name="gla_scan",
    )(gq, gk, gv, gla, gq, gk, gv, gla)


def _gdn_kernel(qf_ref, kf_ref, vf_ref, scf_ref, srf_ref, qb_ref, kb_ref, vb_ref, scb_ref, srb_ref,
                of_ref, ob_ref, st_ref, *, dk, dv, nh):
    j = pl.program_id(1)

    @pl.when(j == 0)
    def _():
        st_ref[...] = jnp.zeros_like(st_ref)

    tb = qf_ref.shape[1]
    nchunk = tb // CHUNK
    eye = _tri(CHUNK, False, False) & _tri(CHUNK, True, False)

    for dr, (q_ref, k_ref, v_ref, sc_ref, sr_ref, o_ref) in enumerate(
            ((qf_ref, kf_ref, vf_ref, scf_ref, srf_ref, of_ref),
             (qb_ref, kb_ref, vb_ref, scb_ref, srb_ref, ob_ref))):
        rev = dr == 1
        tri = _tri(CHUNK, rev, False).astype(BF16)
        tri_t = _tri(CHUNK, not rev, False).astype(BF16)
        strict = _tri(CHUNK, rev, True)
        incl = _tri(CHUNK, rev, False)
        last = 0 if rev else CHUNK - 1
        states = [st_ref[dr, hd] for hd in range(nh)]
        for ci in range(nchunk):
            c = nchunk - 1 - ci if rev else ci
            rows = slice(c * CHUNK, (c + 1) * CHUNK)
            small_c = sc_ref[0, rows, :]
            small_r = sr_ref[0, 0, :, rows]
            g_cols = _cumsum_dot(tri, small_c, True)
            g_rows = _cumsum_dot(tri_t, small_r, False)
            for hd in range(nh):
                ia = dr * nh + hd
                ib = 2 * nh + dr * nh + hd
                g_c = g_cols[:, ia:ia + 1]
                g_r = g_rows[ia:ia + 1, :]
                be_c = small_c[:, ib:ib + 1]
                g_last = g_c[last:last + 1]
                e = jnp.exp(jnp.minimum(g_c - g_r, 0.0))
                cols = slice(hd * dk, (hd + 1) * dk)
                qh = q_ref[0, rows, cols]
                kh = k_ref[0, rows, cols]
                vh = v_ref[0, rows, hd * dv:(hd + 1) * dv]
                kk = _dot_nt(kh, kh)
                qk = _dot_nt(qh, kh)
                x = jnp.where(strict, -(be_c * kk * e), 0.0)
                aqk = jnp.where(incl, qk * e, 0.0).astype(BF16)
                t = jnp.where(eye, 1.0, x)
                pw = x
                nsq = max(CHUNK.bit_length() - 2, 0)
                for _ in range(nsq):
                    pb = pw.astype(BF16)
                    pw = _dot(pb, pb)
                    t = t + _dot(t.astype(BF16), pw.astype(BF16))
                eg = jnp.exp(g_c)
                khf = kh.astype(F32)
                rhs = jnp.concatenate([(be_c * vh.astype(F32)).astype(BF16),
                                       ((be_c * eg) * khf).astype(BF16)], axis=1)
                uw = _dot(t.astype(BF16), rhs)
                u_m = uw[:, :dv]
                w_m = uw[:, dv:].astype(BF16)
                q_dec = (qh.astype(F32) * eg).astype(BF16)
                k_dec = (khf * jnp.exp(g_last - g_c)).astype(BF16)
                st = states[hd]
                st_bf = st.astype(BF16)
                delta = u_m - _dot(w_m, st_bf)
                db = delta.astype(BF16)
                o_ref[0, rows, hd * dv:(hd + 1) * dv] = _dot(q_dec, st_bf) + _dot(aqk, db)
                states[hd] = st * jnp.exp(g_last) + _dot_tn(k_dec, db)
        for hd in range(nh):
            st_ref[dr, hd] = states[hd]


def _gdn_scan(dq, dk_, dv_, small_c, small_r, nheads):
    b, s, qk = dq.shape
    v = dv_.shape[2]
    nb = s // TOKEN_BLOCK
    tb = TOKEN_BLOCK
    dk, dv = qk // nheads, v // nheads
    ns = small_c.shape[2]
    fwd = lambda n: pl.BlockSpec((1, tb, n), lambda i, j: (i, j, 0))
    bwd = lambda n: pl.BlockSpec((1, tb, n), lambda i, j: (i, _bwd_block(j, nb), 0))
    rfwd = pl.BlockSpec((1, 1, ns, tb), lambda i, j: (i, j, 0, 0))
    rbwd = pl.BlockSpec((1, 1, ns, tb), lambda i, j: (i, _bwd_block(j, nb), 0, 0))
    return pl.pallas_call(
        functools.partial(_gdn_kernel, dk=dk, dv=dv, nh=nheads),
        out_shape=[jax.ShapeDtypeStruct((b, s, v), F32), jax.ShapeDtypeStruct((b, s, v), F32)],
        grid=(b, nb),
        in_specs=[fwd(qk), fwd(qk), fwd(v), fwd(ns), rfwd, bwd(qk), bwd(qk), bwd(v), bwd(ns), rbwd],
        out_specs=[fwd(v), bwd(v)],
        scratch_shapes=[pltpu.VMEM((2, nheads, dk, dv), F32)],
        compiler_params=pltpu.CompilerParams(dimension_semantics=("parallel", "arbitrary"),
                                             vmem_limit_bytes=VMEM_LIMIT),
        name="gdn_scan",
    )(dq, dk_, dv_, small_c, small_r, dq, dk_, dv_, small_c, small_r)


def _out_proj_kernel(x_ref, mod_ref, af_ref, ab_ref, bf_ref, bb_ref, ga_ref, gb_ref, na_ref, nb_ref,
                     wo_ref, o_ref, *, d, nh_a, nh_b):
    m = mod_ref[0]
    gt1 = m[:, 2 * d:3 * d]
    parts = []
    for (f_ref, b_ref, g_ref, n_ref, nh) in ((af_ref, ab_ref, ga_ref, na_ref, nh_a),
                                             (bf_ref, bb_ref, gb_ref, nb_ref, nh_b)):
        o = f_ref[0] + b_ref[0]
        gate = _silu(g_ref[0].astype(F32))
        hv = o.shape[1] // nh
        for hd in range(nh):
            sl = slice(hd * hv, (hd + 1) * hv)
            parts.append((_rms(o[:, sl]) * n_ref[...] * gate[:, sl]).astype(BF16))
    y = jnp.concatenate(parts, axis=1)
    o_ref[0] = x_ref[0] + gt1 * _dot(y, wo_ref[...])


def _out_proj(xs, mod, layer, drop_ctx, oaf, oab, obf, obb, gg, dg, na, nb_, wo, nh_a, nh_b):
    b, s, d = xs.shape
    off = 1 if drop_ctx else 0
    nblk = s // TOKEN_BLOCK - off
    tb = TOKEN_BLOCK
    tok = lambda n: pl.BlockSpec((1, tb, n), lambda i, j: (i, j + off, 0))
    const2 = lambda shape: pl.BlockSpec(shape, lambda i, j: (0,) * len(shape))
    bsz = b
    mod_spec = pl.BlockSpec((1, 1, 6 * d),
                            lambda i, j: (layer * MOD_ROWS + jnp.where(j + off == 0, bsz, i), 0, 0))
    va, vb = oaf.shape[2], obf.shape[2]
    return pl.pallas_call(
        functools.partial(_out_proj_kernel, d=d, nh_a=nh_a, nh_b=nh_b),
        out_shape=jax.ShapeDtypeStruct((b, nblk * tb, d), F32),
        grid=(b, nblk),
        in_specs=[tok(d), mod_spec, tok(va), tok(va), tok(vb), tok(vb), tok(va), tok(vb),
                  const2((1, va // nh_a)), const2((1, vb // nh_b)), const2(wo.shape)],
        out_specs=pl.BlockSpec((1, tb, d), lambda i, j: (i, j, 0)),
        compiler_params=pltpu.CompilerParams(dimension_semantics=("parallel", "parallel"),
                                             vmem_limit_bytes=VMEM_LIMIT),
        name="out_proj",
    )(xs, mod, oaf, oab, obf, obb, gg, dg, na.reshape(1, -1), nb_.reshape(1, -1), wo)


def _mlp_kernel(x_ref, mod_ref, g2_ref, w1_ref, w2_ref, fg_ref, o_ref, *, d, final):
    x = x_ref[0]
    m = mod_ref[0]
    sh2, sc2, gt2 = m[:, 3 * d:4 * d], m[:, 4 * d:5 * d], m[:, 5 * d:6 * d]
    h = (_rms(x) * g2_ref[...] * (1.0 + sc2) + sh2).astype(BF16)
    a = jnp.maximum(_dot(h, w1_ref[...]), 0.0)
    y = x + gt2 * _dot((a * a).astype(BF16), w2_ref[...])
    if final:
        y = _rms(y) * fg_ref[...]
    o_ref[0] = y


def _mlp(xs, mod, layer, final, g2, w1, w2, fg):
    b, s, d = xs.shape
    nblk = s // TOKEN_BLOCK
    tb = TOKEN_BLOCK
    const2 = lambda shape: pl.BlockSpec(shape, lambda i, j: (0,) * len(shape))
    bsz = b
    has_ctx = not final
    mod_spec = pl.BlockSpec(
        (1, 1, 6 * d),
        lambda i, j: (layer * MOD_ROWS + (jnp.where(j == 0, bsz, i) if has_ctx else i), 0, 0))
    return pl.pallas_call(
        functools.partial(_mlp_kernel, d=d, final=final),
        out_shape=jax.ShapeDtypeStruct((b, s, d), F32),
        grid=(b, nblk),
        in_specs=[pl.BlockSpec((1, tb, d), lambda i, j: (i, j, 0)), mod_spec, const2((1, d)),
                  const2(w1.shape), const2(w2.shape), const2((1, d))],
        out_specs=pl.BlockSpec((1, tb, d), lambda i, j: (i, j, 0)),
        compiler_params=pltpu.CompilerParams(dimension_semantics=("parallel", "parallel"),
                                             vmem_limit_bytes=VMEM_LIMIT),
        name="mlp",
    )(xs, mod, g2.reshape(1, d), w1, w2, fg.reshape(1, d))


def kernel(x, c, ctx, c_ctx, w_ada, b_ada, norm1_g, norm2_g, w_in, gla_w_lr, gla_b_lr, gdn_conv_w,
           gdn_a_log, gdn_dt_bias, gla_norm_g, gdn_norm_g, w_out, w_ff1, w_ff2, final_norm_g):
    bsz, t, d = x.shape
    depth = w_ada.shape[0]
    assert ctx.shape[1] == TOKEN_BLOCK and t % TOKEN_BLOCK == 0 and TOKEN_BLOCK % GRID_W == 0
    assert bsz + 1 <= MOD_ROWS
    qk_a = gla_w_lr.shape[3]
    rank = gla_w_lr.shape[2]
    v_a = gla_norm_g.shape[1] * GLA_HEADS
    conv_dim = gdn_conv_w.shape[2]
    v_b = gdn_norm_g.shape[1] * GDN_HEADS
    qk_b = (conv_dim - v_b) // 2
    nd = 2 * GDN_HEADS
    dims = (qk_a, v_a, rank, conv_dim, qk_b, v_b, GDN_HEADS)

    o_r = 2 * qk_a + 2 * v_a
    o_c = o_r + rank
    o_g = o_c + conv_dim
    o_s = o_g + v_b
    assert w_in.shape[2] == o_s + 2 * nd
    w_main = jnp.concatenate([w_in[:, :, :o_r], w_in[:, :, o_c:o_s]], axis=2).astype(BF16)
    w_sm = jnp.concatenate([w_in[:, :, o_r:o_c], w_in[:, :, o_s:]], axis=2)
    w_small = jnp.pad(w_sm, ((0, 0), (0, 0), (0, LANES - w_sm.shape[2]))).astype(BF16)
    w_small_t = jnp.swapaxes(w_in[:, :, o_s:], 1, 2).astype(BF16)

    cc = jnp.concatenate([c, c_ctx[None, :], jnp.zeros((MOD_ROWS - bsz - 1, d), F32)], axis=0)
    mod = _modulation(cc, w_ada.astype(BF16), b_ada).reshape(depth * MOD_ROWS, 1, 6 * d)

    xs = jnp.concatenate([ctx, x], axis=1)
    for l in range(depth):
        last = l == depth - 1
        (gq, gk, gv, gg, gla, dq, dk_, dv_, dg, small_c, small_r) = _in_proj(
            xs, mod, l, bsz, norm1_g[l], w_main[l], w_small[l], w_small_t[l],
            gla_w_lr[l].astype(BF16), gla_b_lr[l], gdn_conv_w[l], gdn_a_log[l], gdn_dt_bias[l], dims)
        oaf, oab = _gla_scan(gq, gk, gv, gla, GLA_HEADS)
        obf, obb = _gdn_scan(dq, dk_, dv_, small_c, small_r, GDN_HEADS)
        xs = _out_proj(xs, mod, l, last, oaf, oab, obf, obb, gg, dg, gla_norm_g[l], gdn_norm_g[l],
                       w_out[l].astype(BF16), GLA_HEADS, GDN_HEADS)
        xs = _mlp(xs, mod, l, last, norm2_g[l], w_ff1[l].astype(BF16), w_ff2[l].astype(BF16), final_norm_g)
    return xs
```

```python
import functools

import jax
import jax.numpy as jnp
from jax import lax
from jax.experimental import pallas as pl
from jax.experimental.pallas import tpu as pltpu

EPS = 1e-6
GRID_W = 64
GLA_HEADS = 4
GDN_HEADS = 4
GLA_TAU = 16.0
TOKEN_BLOCK = 256
CHUNK = 64
LANES = 128
MOD_ROWS = 16
VMEM_LIMIT = 56 * 1024 * 1024

F32 = jnp.float32
BF16 = jnp.bfloat16


def _dot(a, b):
    return jnp.dot(a, b, preferred_element_type=F32)


def _dot_nt(a, b):
    return lax.dot_general(a, b, (((1,), (1,)), ((), ())), preferred_element_type=F32)


def _dot_tn(a, b):
    return lax.dot_general(a, b, (((0,), (0,)), ((), ())), preferred_element_type=F32)


def _silu(x):
    return x / (1.0 + jnp.exp(-x))


def _sigmoid(x):
    return 1.0 / (1.0 + jnp.exp(-x))


def _softplus(x):
    return jnp.maximum(x, 0.0) + jnp.log(1.0 + jnp.exp(-jnp.abs(x)))


def _log_sigmoid(x):
    return jnp.minimum(x, 0.0) - jnp.log(1.0 + jnp.exp(-jnp.abs(x)))


def _rms(x):
    return x * lax.rsqrt(jnp.mean(x * x, axis=-1, keepdims=True) + EPS)


def _cumsum_dot(tri_bf16, x, left):
    hi = x.astype(BF16)
    lo = (x - hi.astype(F32)).astype(BF16)
    if left:
        return _dot(tri_bf16, hi) + _dot(tri_bf16, lo)
    return _dot(hi, tri_bf16) + _dot(lo, tri_bf16)


def _tri(n, upper, strict):
    r = lax.broadcasted_iota(jnp.int32, (n, n), 0)
    c = lax.broadcasted_iota(jnp.int32, (n, n), 1)
    if upper:
        m = (r < c) if strict else (r <= c)
    else:
        m = (r > c) if strict else (r >= c)
    return m


def _couple(n, s, upper):
    r = lax.broadcasted_iota(jnp.int32, (n, n), 0)
    c = lax.broadcasted_iota(jnp.int32, (n, n), 1)
    same = (r & ~(2 * s - 1)) == (c & ~(2 * s - 1))
    r_hi, c_hi = (r & s) != 0, (c & s) != 0
    return same & ((~r_hi & c_hi) if upper else (r_hi & ~c_hi))


def _tri_blocks(n, upper):
    r = lax.broadcasted_iota(jnp.int32, (n, n), 0)
    c = lax.broadcasted_iota(jnp.int32, (n, n), 1)
    same = (r & ~(CHUNK - 1)) == (c & ~(CHUNK - 1))
    return same & ((r <= c) if upper else (r >= c))


def _mod_kernel(cc_ref, w_ref, b_ref, o_ref):
    s = _silu(cc_ref[...]).astype(BF16)
    o_ref[0] = _dot(s, w_ref[0]) + b_ref[0]


def _modulation(cc, w_ada_bf, b_ada):
    depth, d, d6 = w_ada_bf.shape
    nblk = d6 // d
    return pl.pallas_call(
        _mod_kernel,
        out_shape=jax.ShapeDtypeStruct((depth, MOD_ROWS, d6), F32),
        grid=(depth, nblk),
        in_specs=[
            pl.BlockSpec((MOD_ROWS, d), lambda l, n: (0, 0)),
            pl.BlockSpec((1, d, d), lambda l, n: (l, 0, n)),
            pl.BlockSpec((1, 1, d), lambda l, n: (l, 0, n)),
        ],
        out_specs=pl.BlockSpec((1, MOD_ROWS, d), lambda l, n: (l, 0, n)),
        compiler_params=pltpu.CompilerParams(dimension_semantics=("parallel", "parallel")),
        name="adaln_modulation",
    )(cc, w_ada_bf, b_ada.reshape(depth, 1, d6))


def _in_proj_kernel(x_ref, mod_ref, g1_ref, wm_ref, ws_ref, wst_ref, wlr_ref, blr_ref, cw_ref,
                    arow_ref, dtrow_ref, acol_ref, dtcol_ref,
                    gq_ref, gk_ref, gv_ref, gg_ref, gla_ref, dq_ref, dk_ref, dv_ref, dg_ref,
                    sc_ref, sr_ref, *, d, qk_a, v_a, rank, conv_dim, qk_b, v_b, nh_b):
    j = pl.program_id(1)
    x = x_ref[0]
    m = mod_ref[0]
    sh1, sc1 = m[:, 0:d], m[:, d:2 * d]
    h = _rms(x) * g1_ref[...] * (1.0 + sc1) + sh1
    hb = h.astype(BF16)
    p = _dot(hb, wm_ref[...])
    ps = _dot(hb, ws_ref[...])
    pst = _dot_nt(wst_ref[...], hb)

    o = 0
    gq_ref[0] = (p[:, o:o + qk_a] * (float(qk_a // GLA_HEADS) ** -0.5)).astype(BF16)
    o += qk_a
    gk_ref[0] = p[:, o:o + qk_a].astype(BF16)
    o += qk_a
    gv_ref[0] = p[:, o:o + v_a].astype(BF16)
    o += v_a
    gg_ref[0] = p[:, o:o + v_a].astype(BF16)
    o += v_a
    r_a = ps[:, 0:rank].astype(BF16)
    for dr in range(2):
        lr = _dot(r_a, wlr_ref[dr]) + blr_ref[dr]
        gla_ref[0, :, dr * qk_a:(dr + 1) * qk_a] = _log_sigmoid(lr) * (1.0 / GLA_TAU)

    tb = x.shape[0]
    u = p[:, o:o + conv_dim]
    o += conv_dim
    t = lax.broadcasted_iota(jnp.int32, (tb, 1), 0)
    seg_mask = jnp.where(j == 0, tb - 1, GRID_W - 1)
    first = (t & seg_mask) == 0
    last = (t & seg_mask) == seg_mask
    up = jnp.where(first, 0.0, pltpu.roll(u, 1, 0))
    un = jnp.where(last, 0.0, pltpu.roll(u, tb - 1, 0))
    cw = cw_ref[...]
    s = _silu(cw[0:1] * up + cw[1:2] * u + cw[2:3] * un)
    dkh = qk_b // nh_b
    for hd in range(nh_b):
        qh = s[:, hd * dkh:(hd + 1) * dkh]
        kh = s[:, qk_b + hd * dkh:qk_b + (hd + 1) * dkh]
        qn = qh * lax.rsqrt(jnp.sum(qh * qh, axis=-1, keepdims=True) + EPS) * (float(dkh) ** -0.5)
        kn = kh * lax.rsqrt(jnp.sum(kh * kh, axis=-1, keepdims=True) + EPS)
        dq_ref[0, :, hd * dkh:(hd + 1) * dkh] = qn.astype(BF16)
        dk_ref[0, :, hd * dkh:(hd + 1) * dkh] = kn.astype(BF16)
    dv_ref[0] = s[:, 2 * qk_b:2 * qk_b + v_b].astype(BF16)
    dg_ref[0] = p[:, o:o + v_b].astype(BF16)

    nd = 2 * nh_b
    a_c = ps[:, rank:rank + nd]
    b_c = ps[:, rank + nd:rank + 2 * nd]
    sc_ref[0, :, 0:nd] = -jnp.exp(arow_ref[...]) * _softplus(a_c + dtrow_ref[...])
    sc_ref[0, :, nd:2 * nd] = _sigmoid(b_c)
    sr_ref[0, 0, 0:nd, :] = -jnp.exp(acol_ref[...]) * _softplus(pst[0:nd] + dtcol_ref[...])
    sr_ref[0, 0, nd:2 * nd, :] = _sigmoid(pst[nd:2 * nd])


def _in_proj(xs, mod, layer, bsz, g1, wm, ws, wst, wlr, blr, cw, a_log, dt_bias, dims):
    b, s, d = xs.shape
    nb = s // TOKEN_BLOCK
    qk_a, v_a, rank, conv_dim, qk_b, v_b, nh_b = dims
    nd = 2 * nh_b
    tb = TOKEN_BLOCK
    const2 = lambda shape: pl.BlockSpec(shape, lambda i, j: (0,) * len(shape))
    tok = lambda n: pl.BlockSpec((1, tb, n), lambda i, j: (i, j, 0))
    mod_spec = pl.BlockSpec((1, 1, 6 * d), lambda i, j: (layer * MOD_ROWS + jnp.where(j == 0, bsz, i), 0, 0))
    out_shape = [
        jax.ShapeDtypeStruct((b, s, qk_a), BF16), jax.ShapeDtypeStruct((b, s, qk_a), BF16),
        jax.ShapeDtypeStruct((b, s, v_a), BF16), jax.ShapeDtypeStruct((b, s, v_a), BF16),
        jax.ShapeDtypeStruct((b, s, 2 * qk_a), F32),
        jax.ShapeDtypeStruct((b, s, qk_b), BF16), jax.ShapeDtypeStruct((b, s, qk_b), BF16),
        jax.ShapeDtypeStruct((b, s, v_b), BF16), jax.ShapeDtypeStruct((b, s, v_b), BF16),
        jax.ShapeDtypeStruct((b, s, 2 * nd), F32),
        jax.ShapeDtypeStruct((b, nb, 2 * nd, tb), F32),
    ]
    out_specs = [tok(qk_a), tok(qk_a), tok(v_a), tok(v_a), tok(2 * qk_a),
                 tok(qk_b), tok(qk_b), tok(v_b), tok(v_b), tok(2 * nd),
                 pl.BlockSpec((1, 1, 2 * nd, tb), lambda i, j: (i, j, 0, 0))]
    kern = functools.partial(_in_proj_kernel, d=d, qk_a=qk_a, v_a=v_a, rank=rank, conv_dim=conv_dim,
                             qk_b=qk_b, v_b=v_b, nh_b=nh_b)
    return pl.pallas_call(
        kern, out_shape=out_shape, grid=(b, nb),
        in_specs=[
            pl.BlockSpec((1, tb, d), lambda i, j: (i, j, 0)), mod_spec, const2((1, d)),
            const2(wm.shape), const2(ws.shape), const2(wst.shape), const2(wlr.shape), const2((2, 1, qk_a)),
            const2(cw.shape), const2((1, nd)), const2((1, nd)), const2((nd, 1)), const2((nd, 1)),
        ],
        out_specs=out_specs,
        compiler_params=pltpu.CompilerParams(dimension_semantics=("parallel", "parallel"),
                                             vmem_limit_bytes=VMEM_LIMIT),
        name="in_proj",
    )(xs, mod, g1.reshape(1, d), wm, ws, wst, wlr, blr.reshape(2, 1, -1), cw,
      a_log.reshape(1, nd), dt_bias.reshape(1, nd), a_log.reshape(nd, 1), dt_bias.reshape(nd, 1))


def _bwd_block(j, nb):
    return jnp.where(j == 0, 0, nb - j)


def _gla_kernel(qf_ref, kf_ref, vf_ref, laf_ref, qb_ref, kb_ref, vb_ref, lab_ref,
                of_ref, ob_ref, st_ref, *, dk, dv):
    j = pl.program_id(1)

    @pl.when(j == 0)
    def _():
        st_ref[...] = jnp.zeros_like(st_ref)

    tb = qf_ref.shape[1]
    nchunk = tb // CHUNK
    npair = (GLA_HEADS * dk) // LANES
    hpp = LANES // dk
    lane = lax.broadcasted_iota(jnp.int32, (1, LANES), 1)
    lms = [(lane >= hh * dk) & (lane < (hh + 1) * dk) for hh in range(hpp)]
    mid = CHUNK // 2
    refs = ((qf_ref, kf_ref, vf_ref, laf_ref, of_ref), (qb_ref, kb_ref, vb_ref, lab_ref, ob_ref))
    chains = [(dr, pr) for dr in range(2) for pr in range(npair)]
    units = [(dr, pr, c) for (dr, pr) in chains for c in range(nchunk)]
    causal = {dr: _tri(CHUNK, dr == 1, False) for dr in range(2)}
    tri_blk = {dr: _tri_blocks(tb, dr == 1).astype(BF16) for dr in range(2)}

    bcum = {}
    for (dr, pr) in chains:
        la = refs[dr][3][0, :, pr * LANES:(pr + 1) * LANES]
        bcum[dr, pr] = _cumsum_dot(tri_blk[dr], la, True)

    q_dec, k_hat, q_mid, k_mid, a_last = {}, {}, {}, {}, {}
    for u in units:
        dr, pr, c = u
        q_ref, k_ref = refs[dr][0], refs[dr][1]
        rows = slice(c * CHUNK, (c + 1) * CHUNK)
        lanes = slice(pr * LANES, (pr + 1) * LANES)
        last = c * CHUNK + (0 if dr == 1 else CHUNK - 1)
        b = bcum[dr, pr][rows]
        b_last = bcum[dr, pr][last:last + 1]
        b_mid = bcum[dr, pr][c * CHUNK + mid:c * CHUNK + mid + 1]
        qc = q_ref[0, rows, lanes].astype(F32)
        kc = k_ref[0, rows, lanes].astype(F32)
        q_dec[u] = qc * jnp.exp(b)
        k_hat[u] = (kc * jnp.exp(b_last - b)).astype(BF16)
        q_mid[u] = qc * jnp.exp(b - b_mid)
        k_mid[u] = (kc * jnp.exp(b_mid - b)).astype(BF16)
        a_last[u] = jnp.exp(b_last)

    sc = {}
    for u in units:
        for hh in range(hpp):
            sc[u, hh] = _dot_nt(jnp.where(lms[hh], q_mid[u], 0.0).astype(BF16), k_mid[u])
    intra, dst = {}, {}
    for u in units:
        dr, pr, c = u
        rows = slice(c * CHUNK, (c + 1) * CHUNK)
        acc = None
        for hh in range(hpp):
            head = pr * hpp + hh
            vh = refs[dr][2][0, rows, head * dv:(head + 1) * dv]
            pm = jnp.where(causal[dr], sc[u, hh], 0.0).astype(BF16)
            intra[u, hh] = _dot(pm, vh)
            dh = _dot_tn(vh, k_hat[u])
            acc = dh if acc is None else jnp.where(lms[hh], dh, acc)
        dst[u] = acc

    st = {ch: st_ref[ch[0], ch[1]] for ch in chains}
    for ci in range(nchunk):
        for ch in chains:
            dr, pr = ch
            c = nchunk - 1 - ci if dr == 1 else ci
            u = (dr, pr, c)
            rows = slice(c * CHUNK, (c + 1) * CHUNK)
            st_bf = st[ch].astype(BF16)
            for hh in range(hpp):
                head = pr * hpp + hh
                inter = _dot_nt(jnp.where(lms[hh], q_dec[u], 0.0).astype(BF16), st_bf)
                refs[dr][4][0, rows, head * dv:(head + 1) * dv] = intra[u, hh] + inter
            st[ch] = st[ch] * a_last[u] + dst[u]
    for ch in chains:
        st_ref[ch[0], ch[1]] = st[ch]


def _gla_scan(gq, gk, gv, gla, nheads):
    b, s, qk = gq.shape
    v = gv.shape[2]
    nb = s // TOKEN_BLOCK
    tb = TOKEN_BLOCK
    dk, dv = qk // nheads, v // nheads
    fwd = lambda n, col=0: pl.BlockSpec((1, tb, n), lambda i, j: (i, j, col))
    bwd = lambda n, col=0: pl.BlockSpec((1, tb, n), lambda i, j: (i, _bwd_block(j, nb), col))
    return pl.pallas_call(
        functools.partial(_gla_kernel, dk=dk, dv=dv),
        out_shape=[jax.ShapeDtypeStruct((b, s, v), F32), jax.ShapeDtypeStruct((b, s, v), F32)],
        grid=(b, nb),
        in_specs=[fwd(qk), fwd(qk), fwd(v), fwd(qk, 0), bwd(qk), bwd(qk), bwd(v), bwd(qk, 1)],
        out_specs=[fwd(v), bwd(v)],
        scratch_shapes=[pltpu.VMEM((2, qk // LANES, dv, LANES), F32)],
        compiler_params=pltpu.CompilerParams(dimension_semantics=("parallel", "arbitrary"),
                                             vmem_limit_bytes=VMEM_LIMIT),
        name="gla_scan",
    )(gq, gk, gv, gla, gq, gk, gv, gla)


def _gdn_kernel(qf_ref, kf_ref, vf_ref, scf_ref, srf_ref, qb_ref, kb_ref, vb_ref, scb_ref, srb_ref,
                of_ref, ob_ref, st_ref, *, dk, dv, nh):
    j = pl.program_id(1)

    @pl.when(j == 0)
    def _():
        st_ref[...] = jnp.zeros_like(st_ref)

    tb = qf_ref.shape[1]
    nchunk = tb // CHUNK
    eye = _tri(CHUNK, False, False) & _tri(CHUNK, True, False)

    refs = ((qf_ref, kf_ref, vf_ref, scf_ref, srf_ref, of_ref),
            (qb_ref, kb_ref, vb_ref, scb_ref, srb_ref, ob_ref))
    chains = [(dr, hd) for dr in range(2) for hd in range(nh)]
    units = [(dr, c, hd) for dr in range(2) for c in range(nchunk) for hd in range(nh)]
    strict = {dr: _tri(CHUNK, dr == 1, True) for dr in range(2)}
    incl = {dr: _tri(CHUNK, dr == 1, False) for dr in range(2)}

    g_cols, g_rows, small = {}, {}, {}
    for dr in range(2):
        small[dr] = refs[dr][3][0]
        g_cols[dr] = _cumsum_dot(_tri_blocks(tb, dr == 1).astype(BF16), small[dr], True)
        g_rows[dr] = _cumsum_dot(_tri_blocks(tb, dr != 1).astype(BF16), refs[dr][4][0, 0], False)

    kk, qk = {}, {}
    for u in units:
        dr, c, hd = u
        rows = slice(c * CHUNK, (c + 1) * CHUNK)
        kh = refs[dr][1][0, rows, hd * dk:(hd + 1) * dk]
        kk[u] = _dot_nt(kh, kh)
        qk[u] = _dot_nt(refs[dr][0][0, rows, hd * dk:(hd + 1) * dk], kh)

    x, aqk, rhs, q_dec, k_dec, a_last = {}, {}, {}, {}, {}, {}
    for u in units:
        dr, c, hd = u
        rows = slice(c * CHUNK, (c + 1) * CHUNK)
        ia = dr * nh + hd
        ib = 2 * nh + dr * nh + hd
        last = c * CHUNK + (0 if dr == 1 else CHUNK - 1)
        g_c = g_cols[dr][rows, ia:ia + 1]
        g_r = g_rows[dr][ia:ia + 1, rows]
        be_c = small[dr][rows, ib:ib + 1]
        g_last = g_cols[dr][last:last + 1, ia:ia + 1]
        e = jnp.exp(jnp.minimum(g_c - g_r, 0.0))
        x[u] = jnp.where(strict[dr], be_c * kk[u] * e, 0.0)
        aqk[u] = jnp.where(incl[dr], qk[u] * e, 0.0).astype(BF16)
        eg = jnp.exp(g_c)
        qh = refs[dr][0][0, rows, hd * dk:(hd + 1) * dk].astype(F32)
        khf = refs[dr][1][0, rows, hd * dk:(hd + 1) * dk].astype(F32)
        vh = refs[dr][2][0, rows, hd * dv:(hd + 1) * dv].astype(F32)
        rhs[u] = jnp.concatenate([(be_c * vh).astype(BF16), ((be_c * eg) * khf).astype(BF16)], axis=1)
        q_dec[u] = (qh * eg).astype(BF16)
        k_dec[u] = (khf * jnp.exp(g_last - g_c)).astype(BF16)
        a_last[u] = jnp.exp(g_last)

    t = {u: jnp.where(eye, 1.0, jnp.where(_couple(CHUNK, 1, u[0] == 1), -x[u], 0.0)) for u in units}
    s = 2
    while s < CHUNK:
        a = {u: jnp.where(_couple(CHUNK, s, u[0] == 1), x[u], 0.0).astype(BF16) for u in units}
        tb16 = {u: t[u].astype(BF16) for u in units}
        ta = {u: _dot(tb16[u], a[u]).astype(BF16) for u in units}
        t = {u: t[u] - _dot(ta[u], tb16[u]) for u in units}
        s *= 2

    uw = {u: _dot(t[u].astype(BF16), rhs[u]) for u in units}

    st = {ch: st_ref[ch[0], ch[1]] for ch in chains}
    for ci in range(nchunk):
        for ch in chains:
            dr, hd = ch
            c = nchunk - 1 - ci if dr == 1 else ci
            u = (dr, c, hd)
            rows = slice(c * CHUNK, (c + 1) * CHUNK)
            st_bf = st[ch].astype(BF16)
            delta = uw[u][:, :dv] - _dot(uw[u][:, dv:].astype(BF16), st_bf)
            db = delta.astype(BF16)
            refs[dr][5][0, rows, hd * dv:(hd + 1) * dv] = _dot(q_dec[u], st_bf) + _dot(aqk[u], db)
            st[ch] = st[ch] * a_last[u] + _dot_tn(k_dec[u], db)
    for ch in chains:
        st_ref[ch[0], ch[1]] = st[ch]


def _gdn_scan(dq, dk_, dv_, small_c, small_r, nheads):
    b, s, qk = dq.shape
    v = dv_.shape[2]
    nb = s // TOKEN_BLOCK
    tb = TOKEN_BLOCK
    dk, dv = qk // nheads, v // nheads
    ns = small_c.shape[2]
    fwd = lambda n: pl.BlockSpec((1, tb, n), lambda i, j: (i, j, 0))
    bwd = lambda n: pl.BlockSpec((1, tb, n), lambda i, j: (i, _bwd_block(j, nb), 0))
    rfwd = pl.BlockSpec((1, 1, ns, tb), lambda i, j: (i, j, 0, 0))
    rbwd = pl.BlockSpec((1, 1, ns, tb), lambda i, j: (i, _bwd_block(j, nb), 0, 0))
    return pl.pallas_call(
        functools.partial(_gdn_kernel, dk=dk, dv=dv, nh=nheads),
        out_shape=[jax.ShapeDtypeStruct((b, s, v), F32), jax.ShapeDtypeStruct((b, s, v), F32)],
        grid=(b, nb),
        in_specs=[fwd(qk), fwd(qk), fwd(v), fwd(ns), rfwd, bwd(qk), bwd(qk), bwd(v), bwd(ns), rbwd],
        out_specs=[fwd(v), bwd(v)],
        scratch_shapes=[pltpu.VMEM((2, nheads, dk, dv), F32)],
        compiler_params=pltpu.CompilerParams(dimension_semantics=("parallel", "arbitrary"),
                                             vmem_limit_bytes=VMEM_LIMIT),
        name="gdn_scan",
    )(dq, dk_, dv_, small_c, small_r, dq, dk_, dv_, small_c, small_r)


def _out_proj_kernel(x_ref, mod_ref, af_ref, ab_ref, bf_ref, bb_ref, ga_ref, gb_ref, na_ref, nb_ref,
                     wo_ref, o_ref, *, d, nh_a, nh_b):
    m = mod_ref[0]
    gt1 = m[:, 2 * d:3 * d]
    parts = []
    for (f_ref, b_ref, g_ref, n_ref, nh) in ((af_ref, ab_ref, ga_ref, na_ref, nh_a),
                                             (bf_ref, bb_ref, gb_ref, nb_ref, nh_b)):
        o = f_ref[0] + b_ref[0]
        gate = _silu(g_ref[0].astype(F32))
        hv = o.shape[1] // nh
        for hd in range(nh):
            sl = slice(hd * hv, (hd + 1) * hv)
            parts.append((_rms(o[:, sl]) * n_ref[...] * gate[:, sl]).astype(BF16))
    y = jnp.concatenate(parts, axis=1)
    o_ref[0] = x_ref[0] + gt1 * _dot(y, wo_ref[...])


def _out_proj(xs, mod, layer, drop_ctx, oaf, oab, obf, obb, gg, dg, na, nb_, wo, nh_a, nh_b):
    b, s, d = xs.shape
    off = 1 if drop_ctx else 0
    nblk = s // TOKEN_BLOCK - off
    tb = TOKEN_BLOCK
    tok = lambda n: pl.BlockSpec((1, tb, n), lambda i, j: (i, j + off, 0))
    const2 = lambda shape: pl.BlockSpec(shape, lambda i, j: (0,) * len(shape))
    bsz = b
    mod_spec = pl.BlockSpec((1, 1, 6 * d),
                            lambda i, j: (layer * MOD_ROWS + jnp.where(j + off == 0, bsz, i), 0, 0))
    va, vb = oaf.shape[2], obf.shape[2]
    return pl.pallas_call(
        functools.partial(_out_proj_kernel, d=d, nh_a=nh_a, nh_b=nh_b),
        out_shape=jax.ShapeDtypeStruct((b, nblk * tb, d), F32),
        grid=(b, nblk),
        in_specs=[tok(d), mod_spec, tok(va), tok(va), tok(vb), tok(vb), tok(va), tok(vb),
                  const2((1, va // nh_a)), const2((1, vb // nh_b)), const2(wo.shape)],
        out_specs=pl.BlockSpec((1, tb, d), lambda i, j: (i, j, 0)),
        compiler_params=pltpu.CompilerParams(dimension_semantics=("parallel", "parallel"),
                                             vmem_limit_bytes=VMEM_LIMIT),
        name="out_proj",
    )(xs, mod, oaf, oab, obf, obb, gg, dg, na.reshape(1, -1), nb_.reshape(1, -1), wo)


def _mlp_kernel(x_ref, mod_ref, g2_ref, w1_ref, w2_ref, fg_ref, o_ref, *, d, final):
    x = x_ref[0]
    m = mod_ref[0]
    sh2, sc2, gt2 = m[:, 3 * d:4 * d], m[:, 4 * d:5 * d], m[:, 5 * d:6 * d]
    h = (_rms(x) * g2_ref[...] * (1.0 + sc2) + sh2).astype(BF16)
    a = jnp.maximum(_dot(h, w1_ref[...]), 0.0)
    y = x + gt2 * _dot((a * a).astype(BF16), w2_ref[...])
    if final:
        y = _rms(y) * fg_ref[...]
    o_ref[0] = y


def _mlp(xs, mod, layer, final, g2, w1, w2, fg):
    b, s, d = xs.shape
    nblk = s // TOKEN_BLOCK
    tb = TOKEN_BLOCK
    const2 = lambda shape: pl.BlockSpec(shape, lambda i, j: (0,) * len(shape))
    bsz = b
    has_ctx = not final
    mod_spec = pl.BlockSpec(
        (1, 1, 6 * d),
        lambda i, j: (layer * MOD_ROWS + (jnp.where(j == 0, bsz, i) if has_ctx else i), 0, 0))
    return pl.pallas_call(
        functools.partial(_mlp_kernel, d=d, final=final),
        out_shape=jax.ShapeDtypeStruct((b, s, d), F32),
        grid=(b, nblk),
        in_specs=[pl.BlockSpec((1, tb, d), lambda i, j: (i, j, 0)), mod_spec, const2((1, d)),
                  const2(w1.shape), const2(w2.shape), const2((1, d))],
        out_specs=pl.BlockSpec((1, tb, d), lambda i, j: (i, j, 0)),
        compiler_params=pltpu.CompilerParams(dimension_semantics=("parallel", "parallel"),
                                             vmem_limit_bytes=VMEM_LIMIT),
        name="mlp",
    )(xs, mod, g2.reshape(1, d), w1, w2, fg.reshape(1, d))


def kernel(x, c, ctx, c_ctx, w_ada, b_ada, norm1_g, norm2_g, w_in, gla_w_lr, gla_b_lr, gdn_conv_w,
           gdn_a_log, gdn_dt_bias, gla_norm_g, gdn_norm_g, w_out, w_ff1, w_ff2, final_norm_g):
    bsz, t, d = x.shape
    depth = w_ada.shape[0]
    assert ctx.shape[1] == TOKEN_BLOCK and t % TOKEN_BLOCK == 0 and TOKEN_BLOCK % GRID_W == 0
    assert bsz + 1 <= MOD_ROWS
    qk_a = gla_w_lr.shape[3]
    rank = gla_w_lr.shape[2]
    v_a = gla_norm_g.shape[1] * GLA_HEADS
    conv_dim = gdn_conv_w.shape[2]
    v_b = gdn_norm_g.shape[1] * GDN_HEADS
    qk_b = (conv_dim - v_b) // 2
    nd = 2 * GDN_HEADS
    dims = (qk_a, v_a, rank, conv_dim, qk_b, v_b, GDN_HEADS)

    o_r = 2 * qk_a + 2 * v_a
    o_c = o_r + rank
    o_g = o_c + conv_dim
    o_s = o_g + v_b
    assert w_in.shape[2] == o_s + 2 * nd
    w_main = jnp.concatenate([w_in[:, :, :o_r], w_in[:, :, o_c:o_s]], axis=2).astype(BF16)
    w_sm = jnp.concatenate([w_in[:, :, o_r:o_c], w_in[:, :, o_s:]], axis=2)
    w_small = jnp.pad(w_sm, ((0, 0), (0, 0), (0, LANES - w_sm.shape[2]))).astype(BF16)
    w_small_t = jnp.swapaxes(w_in[:, :, o_s:], 1, 2).astype(BF16)

    cc = jnp.concatenate([c, c_ctx[None, :], jnp.zeros((MOD_ROWS - bsz - 1, d), F32)], axis=0)
    mod = _modulation(cc, w_ada.astype(BF16), b_ada).reshape(depth * MOD_ROWS, 1, 6 * d)

    xs = jnp.concatenate([ctx, x], axis=1)
    for l in range(depth):
        last = l == depth - 1
        (gq, gk, gv, gg, gla, dq, dk_, dv_, dg, small_c, small_r) = _in_proj(
            xs, mod, l, bsz, norm1_g[l], w_main[l], w_small[l], w_small_t[l],
            gla_w_lr[l].astype(BF16), gla_b_lr[l], gdn_conv_w[l], gdn_a_log[l], gdn_dt_bias[l], dims)
        oaf, oab = _gla_scan(gq, gk, gv, gla, GLA_HEADS)
        obf, obb = _gdn_scan(dq, dk_, dv_, small_c, small_r, GDN_HEADS)
        xs = _out_proj(xs, mod, l, last, oaf, oab, obf, obb, gg, dg, gla_norm_g[l], gdn_norm_g[l],
                       w_out[l].astype(BF16), GLA_HEADS, GDN_HEADS)
        xs = _mlp(xs, mod, l, last, norm2_g[l], w_ff1[l].astype(BF16), w_ff2[l].astype(BF16), final_norm_g)
    return xs
```

```python
import functools

import jax
import jax.numpy as jnp
from jax import lax
from jax.experimental import pallas as pl
from jax.experimental.pallas import tpu as pltpu

EPS = 1e-6
GRID_W = 64
GLA_HEADS = 4
GDN_HEADS = 4
GLA_TAU = 16.0
TOKEN_BLOCK = 256
CHUNK = 64
LANES = 128
MOD_ROWS = 16
VMEM_LIMIT = 56 * 1024 * 1024

F32 = jnp.float32
BF16 = jnp.bfloat16


def _dot(a, b):
    return jnp.dot(a, b, preferred_element_type=F32)


def _dot_nt(a, b):
    return lax.dot_general(a, b, (((1,), (1,)), ((), ())), preferred_element_type=F32)


def _dot_tn(a, b):
    return lax.dot_general(a, b, (((0,), (0,)), ((), ())), preferred_element_type=F32)


def _silu(x):
    return x / (1.0 + jnp.exp(-x))


def _sigmoid(x):
    return 1.0 / (1.0 + jnp.exp(-x))


def _softplus(x):
    return jnp.maximum(x, 0.0) + jnp.log(1.0 + jnp.exp(-jnp.abs(x)))


def _log_sigmoid(x):
    return jnp.minimum(x, 0.0) - jnp.log(1.0 + jnp.exp(-jnp.abs(x)))


def _rms(x):
    return x * lax.rsqrt(jnp.mean(x * x, axis=-1, keepdims=True) + EPS)


def _cumsum_dot(tri_bf16, x, left):
    hi = x.astype(BF16)
    lo = (x - hi.astype(F32)).astype(BF16)
    if left:
        return _dot(tri_bf16, hi) + _dot(tri_bf16, lo)
    return _dot(hi, tri_bf16) + _dot(lo, tri_bf16)


def _tri(n, upper, strict):
    r = lax.broadcasted_iota(jnp.int32, (n, n), 0)
    c = lax.broadcasted_iota(jnp.int32, (n, n), 1)
    if upper:
        m = (r < c) if strict else (r <= c)
    else:
        m = (r > c) if strict else (r >= c)
    return m


def _couple(n, s, upper):
    r = lax.broadcasted_iota(jnp.int32, (n, n), 0)
    c = lax.broadcasted_iota(jnp.int32, (n, n), 1)
    same = (r & ~(2 * s - 1)) == (c & ~(2 * s - 1))
    r_hi, c_hi = (r & s) != 0, (c & s) != 0
    return same & ((~r_hi & c_hi) if upper else (r_hi & ~c_hi))


def _tri_blocks(n, upper):
    r = lax.broadcasted_iota(jnp.int32, (n, n), 0)
    c = lax.broadcasted_iota(jnp.int32, (n, n), 1)
    same = (r & ~(CHUNK - 1)) == (c & ~(CHUNK - 1))
    return same & ((r <= c) if upper else (r >= c))


def _mod_kernel(cc_ref, w_ref, b_ref, o_ref):
    s = _silu(cc_ref[...]).astype(BF16)
    o_ref[0] = _dot(s, w_ref[0]) + b_ref[0]


def _modulation(cc, w_ada_bf, b_ada):
    depth, d, d6 = w_ada_bf.shape
    nblk = d6 // d
    return pl.pallas_call(
        _mod_kernel,
        out_shape=jax.ShapeDtypeStruct((depth, MOD_ROWS, d6), F32),
        grid=(depth, nblk),
        in_specs=[
            pl.BlockSpec((MOD_ROWS, d), lambda l, n: (0, 0)),
            pl.BlockSpec((1, d, d), lambda l, n: (l, 0, n)),
            pl.BlockSpec((1, 1, d), lambda l, n: (l, 0, n)),
        ],
        out_specs=pl.BlockSpec((1, MOD_ROWS, d), lambda l, n: (l, 0, n)),
        compiler_params=pltpu.CompilerParams(dimension_semantics=("parallel", "parallel")),
        name="adaln_modulation",
    )(cc, w_ada_bf, b_ada.reshape(depth, 1, d6))


def _in_proj_kernel(x_ref, mod_ref, g1_ref, wm_ref, ws_ref, wst_ref, wlr_ref, blr_ref, cw_ref,
                    arow_ref, dtrow_ref, acol_ref, dtcol_ref,
                    gq_ref, gk_ref, gv_ref, gg_ref, gla_ref, dq_ref, dk_ref, dv_ref, dg_ref,
                    sc_ref, sr_ref, *, d, qk_a, v_a, rank, conv_dim, qk_b, v_b, nh_b):
    j = pl.program_id(1)
    x = x_ref[0]
    m = mod_ref[0]
    sh1, sc1 = m[:, 0:d], m[:, d:2 * d]
    h = _rms(x) * g1_ref[...] * (1.0 + sc1) + sh1
    hb = h.astype(BF16)
    p = _dot(hb, wm_ref[...])
    ps = _dot(hb, ws_ref[...])
    pst = _dot_nt(wst_ref[...], hb)

    o = 0
    gq_ref[0] = (p[:, o:o + qk_a] * (float(qk_a // GLA_HEADS) ** -0.5)).astype(BF16)
    o += qk_a
    gk_ref[0] = p[:, o:o + qk_a].astype(BF16)
    o += qk_a
    gv_ref[0] = p[:, o:o + v_a].astype(BF16)
    o += v_a
    gg_ref[0] = p[:, o:o + v_a].astype(BF16)
    o += v_a
    r_a = ps[:, 0:rank].astype(BF16)
    for dr in range(2):
        lr = _dot(r_a, wlr_ref[dr]) + blr_ref[dr]
        gla_ref[0, :, dr * qk_a:(dr + 1) * qk_a] = _log_sigmoid(lr) * (1.0 / GLA_TAU)

    tb = x.shape[0]
    u = p[:, o:o + conv_dim]
    o += conv_dim
    t = lax.broadcasted_iota(jnp.int32, (tb, 1), 0)
    seg_mask = jnp.where(j == 0, tb - 1, GRID_W - 1)
    first = (t & seg_mask) == 0
    last = (t & seg_mask) == seg_mask
    up = jnp.where(first, 0.0, pltpu.roll(u, 1, 0))
    un = jnp.where(last, 0.0, pltpu.roll(u, tb - 1, 0))
    cw = cw_ref[...]
    s = _silu(cw[0:1] * up + cw[1:2] * u + cw[2:3] * un)
    dkh = qk_b // nh_b
    for hd in range(nh_b):
        qh = s[:, hd * dkh:(hd + 1) * dkh]
        kh = s[:, qk_b + hd * dkh:qk_b + (hd + 1) * dkh]
        qn = qh * lax.rsqrt(jnp.sum(qh * qh, axis=-1, keepdims=True) + EPS) * (float(dkh) ** -0.5)
        kn = kh * lax.rsqrt(jnp.sum(kh * kh, axis=-1, keepdims=True) + EPS)
        dq_ref[0, :, hd * dkh:(hd + 1) * dkh] = qn.astype(BF16)
        dk_ref[0, :, hd * dkh:(hd + 1) * dkh] = kn.astype(BF16)
    dv_ref[0] = s[:, 2 * qk_b:2 * qk_b + v_b].astype(BF16)
    dg_ref[0] = p[:, o:o + v_b].astype(BF16)

    nd = 2 * nh_b
    a_c = ps[:, rank:rank + nd]
    b_c = ps[:, rank + nd:rank + 2 * nd]
    sc_ref[0, :, 0:nd] = -jnp.exp(arow_ref[...]) * _softplus(a_c + dtrow_ref[...])
    sc_ref[0, :, nd:2 * nd] = _sigmoid(b_c)
    sr_ref[0, 0, 0:nd, :] = -jnp.exp(acol_ref[...]) * _softplus(pst[0:nd] + dtcol_ref[...])
    sr_ref[0, 0, nd:2 * nd, :] = _sigmoid(pst[nd:2 * nd])


def _in_proj(xs, mod, layer, bsz, g1, wm, ws, wst, wlr, blr, cw, a_log, dt_bias, dims):
    b, s, d = xs.shape
    nb = s // TOKEN_BLOCK
    qk_a, v_a, rank, conv_dim, qk_b, v_b, nh_b = dims
    nd = 2 * nh_b
    tb = TOKEN_BLOCK
    const2 = lambda shape: pl.BlockSpec(shape, lambda i, j: (0,) * len(shape))
    tok = lambda n: pl.BlockSpec((1, tb, n), lambda i, j: (i, j, 0))
    mod_spec = pl.BlockSpec((1, 1, 6 * d), lambda i, j: (layer * MOD_ROWS + jnp.where(j == 0, bsz, i), 0, 0))
    out_shape = [
        jax.ShapeDtypeStruct((b, s, qk_a), BF16), jax.ShapeDtypeStruct((b, s, qk_a), BF16),
        jax.ShapeDtypeStruct((b, s, v_a), BF16), jax.ShapeDtypeStruct((b, s, v_a), BF16),
        jax.ShapeDtypeStruct((b, s, 2 * qk_a), F32),
        jax.ShapeDtypeStruct((b, s, qk_b), BF16), jax.ShapeDtypeStruct((b, s, qk_b), BF16),
        jax.ShapeDtypeStruct((b, s, v_b), BF16), jax.ShapeDtypeStruct((b, s, v_b), BF16),
        jax.ShapeDtypeStruct((b, s, 2 * nd), F32),
        jax.ShapeDtypeStruct((b, nb, 2 * nd, tb), F32),
    ]
    out_specs = [tok(qk_a), tok(qk_a), tok(v_a), tok(v_a), tok(2 * qk_a),
                 tok(qk_b), tok(qk_b), tok(v_b), tok(v_b), tok(2 * nd),
                 pl.BlockSpec((1, 1, 2 * nd, tb), lambda i, j: (i, j, 0, 0))]
    kern = functools.partial(_in_proj_kernel, d=d, qk_a=qk_a, v_a=v_a, rank=rank, conv_dim=conv_dim,
                             qk_b=qk_b, v_b=v_b, nh_b=nh_b)
    return pl.pallas_call(
        kern, out_shape=out_shape, grid=(b, nb),
        in_specs=[
            pl.BlockSpec((1, tb, d), lambda i, j: (i, j, 0)), mod_spec, const2((1, d)),
            const2(wm.shape), const2(ws.shape), const2(wst.shape), const2(wlr.shape), const2((2, 1, qk_a)),
            const2(cw.shape), const2((1, nd)), const2((1, nd)), const2((nd, 1)), const2((nd, 1)),
        ],
        out_specs=out_specs,
        compiler_params=pltpu.CompilerParams(dimension_semantics=("parallel", "parallel"),
                                             vmem_limit_bytes=VMEM_LIMIT),
        name="in_proj",
    )(xs, mod, g1.reshape(1, d), wm, ws, wst, wlr, blr.reshape(2, 1, -1), cw,
      a_log.reshape(1, nd), dt_bias.reshape(1, nd), a_log.reshape(nd, 1), dt_bias.reshape(nd, 1))


def _bwd_block(j, nb):
    return jnp.where(j == 0, 0, nb - j)


def _gla_kernel(qf_ref, kf_ref, vf_ref, laf_ref, qb_ref, kb_ref, vb_ref, lab_ref,
                of_ref, ob_ref, st_ref, *, dk, dv):
    j = pl.program_id(1)

    @pl.when(j == 0)
    def _():
        st_ref[...] = jnp.zeros_like(st_ref)

    tb = qf_ref.shape[1]
    nchunk = tb // CHUNK
    npair = (GLA_HEADS * dk) // LANES
    hpp = LANES // dk
    lane = lax.broadcasted_iota(jnp.int32, (1, LANES), 1)
    lms = [(lane >= hh * dk) & (lane < (hh + 1) * dk) for hh in range(hpp)]
    mid = CHUNK // 2
    refs = ((qf_ref, kf_ref, vf_ref, laf_ref, of_ref), (qb_ref, kb_ref, vb_ref, lab_ref, ob_ref))
    chains = [(dr, pr) for dr in range(2) for pr in range(npair)]
    units = [(dr, pr, c) for (dr, pr) in chains for c in range(nchunk)]
    causal = {dr: _tri(CHUNK, dr == 1, False) for dr in range(2)}
    tri_blk = {dr: _tri_blocks(tb, dr == 1).astype(BF16) for dr in range(2)}

    bcum = {}
    for (dr, pr) in chains:
        la = refs[dr][3][0, :, pr * LANES:(pr + 1) * LANES]
        bcum[dr, pr] = _cumsum_dot(tri_blk[dr], la, True)

    q_dec, k_hat, q_mid, k_mid, a_last = {}, {}, {}, {}, {}
    for u in units:
        dr, pr, c = u
        q_ref, k_ref = refs[dr][0], refs[dr][1]
        rows = slice(c * CHUNK, (c + 1) * CHUNK)
        lanes = slice(pr * LANES, (pr + 1) * LANES)
        last = c * CHUNK + (0 if dr == 1 else CHUNK - 1)
        b = bcum[dr, pr][rows]
        b_last = bcum[dr, pr][last:last + 1]
        b_mid = bcum[dr, pr][c * CHUNK + mid:c * CHUNK + mid + 1]
        qc = q_ref[0, rows, lanes].astype(F32)
        kc = k_ref[0, rows, lanes].astype(F32)
        q_dec[u] = qc * jnp.exp(b)
        k_hat[u] = (kc * jnp.exp(b_last - b)).astype(BF16)
        q_mid[u] = qc * jnp.exp(b - b_mid)
        k_mid[u] = (kc * jnp.exp(b_mid - b)).astype(BF16)
        a_last[u] = jnp.exp(b_last)

    sc = {}
    for u in units:
        for hh in range(hpp):
            sc[u, hh] = _dot_nt(jnp.where(lms[hh], q_mid[u], 0.0).astype(BF16), k_mid[u])
    intra, dst = {}, {}
    for u in units:
        dr, pr, c = u
        rows = slice(c * CHUNK, (c + 1) * CHUNK)
        acc = None
        for hh in range(hpp):
            head = pr * hpp + hh
            vh = refs[dr][2][0, rows, head * dv:(head + 1) * dv]
            pm = jnp.where(causal[dr], sc[u, hh], 0.0).astype(BF16)
            intra[u, hh] = _dot(pm, vh)
            dh = _dot_tn(vh, k_hat[u])
            acc = dh if acc is None else jnp.where(lms[hh], dh, acc)
        dst[u] = acc

    st = {ch: st_ref[ch[0], ch[1]] for ch in chains}
    for ci in range(nchunk):
        for ch in chains:
            dr, pr = ch
            c = nchunk - 1 - ci if dr == 1 else ci
            u = (dr, pr, c)
            rows = slice(c * CHUNK, (c + 1) * CHUNK)
            st_bf = st[ch].astype(BF16)
            for hh in range(hpp):
                head = pr * hpp + hh
                inter = _dot_nt(jnp.where(lms[hh], q_dec[u], 0.0).astype(BF16), st_bf)
                refs[dr][4][0, rows, head * dv:(head + 1) * dv] = intra[u, hh] + inter
            st[ch] = st[ch] * a_last[u] + dst[u]
    for ch in chains:
        st_ref[ch[0], ch[1]] = st[ch]


def _gla_scan(gq, gk, gv, gla, nheads):
    b, s, qk = gq.shape
    v = gv.shape[2]
    nb = s // TOKEN_BLOCK
    tb = TOKEN_BLOCK
    dk, dv = qk // nheads, v // nheads
    fwd = lambda n, col=0: pl.BlockSpec((1, tb, n), lambda i, j: (i, j, col))
    bwd = lambda n, col=0: pl.BlockSpec((1, tb, n), lambda i, j: (i, _bwd_block(j, nb), col))
    return pl.pallas_call(
        functools.partial(_gla_kernel, dk=dk, dv=dv),
        out_shape=[jax.ShapeDtypeStruct((b, s, v), F32), jax.ShapeDtypeStruct((b, s, v), F32)],
        grid=(b, nb),
        in_specs=[fwd(qk), fwd(qk), fwd(v), fwd(qk, 0), bwd(qk), bwd(qk), bwd(v), bwd(qk, 1)],
        out_specs=[fwd(v), bwd(v)],
        scratch_shapes=[pltpu.VMEM((2, qk // LANES, dv, LANES), F32)],
        compiler_params=pltpu.CompilerParams(dimension_semantics=("parallel", "arbitrary"),
                                             vmem_limit_bytes=VMEM_LIMIT),
        name="gla_scan",
    )(gq, gk, gv, gla, gq, gk, gv, gla)


def _gdn_kernel(qf_ref, kf_ref, vf_ref, scf_ref, srf_ref, qb_ref, kb_ref, vb_ref, scb_ref, srb_ref,
                of_ref, ob_ref, st_ref, *, dk, dv, nh):
    j = pl.program_id(1)

    @pl.when(j == 0)
    def _():
        st_ref[...] = jnp.zeros_like(st_ref)

    tb = qf_ref.shape[1]
    nchunk = tb // CHUNK
    eye = _tri(CHUNK, False, False) & _tri(CHUNK, True, False)

    refs = ((qf_ref, kf_ref, vf_ref, scf_ref, srf_ref, of_ref),
            (qb_ref, kb_ref, vb_ref, scb_ref, srb_ref, ob_ref))
    chains = [(dr, hd) for dr in range(2) for hd in range(nh)]
    units = [(dr, c, hd) for dr in range(2) for c in range(nchunk) for hd in range(nh)]
    strict = {dr: _tri(CHUNK, dr == 1, True) for dr in range(2)}
    incl = {dr: _tri(CHUNK, dr == 1, False) for dr in range(2)}

    g_cols, g_rows, small = {}, {}, {}
    for dr in range(2):
        small[dr] = refs[dr][3][0]
        g_cols[dr] = _cumsum_dot(_tri_blocks(tb, dr == 1).astype(BF16), small[dr], True)
        g_rows[dr] = _cumsum_dot(_tri_blocks(tb, dr != 1).astype(BF16), refs[dr][4][0, 0], False)

    kk, qk = {}, {}
    for u in units:
        dr, c, hd = u
        rows = slice(c * CHUNK, (c + 1) * CHUNK)
        kh = refs[dr][1][0, rows, hd * dk:(hd + 1) * dk]
        kk[u] = _dot_nt(kh, kh)
        qk[u] = _dot_nt(refs[dr][0][0, rows, hd * dk:(hd + 1) * dk], kh)

    x, aqk, rhs, q_dec, k_dec, a_last = {}, {}, {}, {}, {}, {}
    for u in units:
        dr, c, hd = u
        rows = slice(c * CHUNK, (c + 1) * CHUNK)
        ia = dr * nh + hd
        ib = 2 * nh + dr * nh + hd
        last = c * CHUNK + (0 if dr == 1 else CHUNK - 1)
        g_c = g_cols[dr][rows, ia:ia + 1]
        g_r = g_rows[dr][ia:ia + 1, rows]
        be_c = small[dr][rows, ib:ib + 1]
        g_last = g_cols[dr][last:last + 1, ia:ia + 1]
        e = jnp.exp(jnp.minimum(g_c - g_r, 0.0))
        x[u] = jnp.where(strict[dr], be_c * kk[u] * e, 0.0)
        aqk[u] = jnp.where(incl[dr], qk[u] * e, 0.0).astype(BF16)
        eg = jnp.exp(g_c)
        qh = refs[dr][0][0, rows, hd * dk:(hd + 1) * dk].astype(F32)
        khf = refs[dr][1][0, rows, hd * dk:(hd + 1) * dk].astype(F32)
        vh = refs[dr][2][0, rows, hd * dv:(hd + 1) * dv].astype(F32)
        rhs[u] = jnp.concatenate([(be_c * vh).astype(BF16), ((be_c * eg) * khf).astype(BF16)], axis=1)
        q_dec[u] = qh * eg
        k_dec[u] = (khf * jnp.exp(g_last - g_c)).astype(BF16)
        a_last[u] = jnp.exp(g_last)

    t = {u: jnp.where(eye, 1.0, jnp.where(_couple(CHUNK, 1, u[0] == 1), -x[u], 0.0)) for u in units}
    s = 2
    while s < CHUNK:
        a = {u: jnp.where(_couple(CHUNK, s, u[0] == 1), x[u], 0.0).astype(BF16) for u in units}
        tb16 = {u: t[u].astype(BF16) for u in units}
        ta = {u: _dot(tb16[u], a[u]).astype(BF16) for u in units}
        t = {u: t[u] - _dot(ta[u], tb16[u]) for u in units}
        s *= 2

    uw = {u: _dot(t[u].astype(BF16), rhs[u]).astype(BF16) for u in units}

    kd_uw = {u: _dot_tn(k_dec[u], uw[u]) for u in units}
    aq_uw = {u: _dot(aqk[u], uw[u]) for u in units}
    lhs = {u: jnp.concatenate([(-kd_uw[u][:, dv:]).astype(BF16),
                               (q_dec[u] - aq_uw[u][:, dv:]).astype(BF16)], axis=0) for u in units}

    st = {ch: st_ref[ch[0], ch[1]] for ch in chains}
    for ci in range(nchunk):
        for ch in chains:
            dr, hd = ch
            c = nchunk - 1 - ci if dr == 1 else ci
            u = (dr, c, hd)
            rows = slice(c * CHUNK, (c + 1) * CHUNK)
            res = _dot(lhs[u], st[ch].astype(BF16))
            refs[dr][5][0, rows, hd * dv:(hd + 1) * dv] = res[dk:] + aq_uw[u][:, :dv]
            st[ch] = st[ch] * a_last[u] + res[:dk] + kd_uw[u][:, :dv]
    for ch in chains:
        st_ref[ch[0], ch[1]] = st[ch]


def _gdn_scan(dq, dk_, dv_, small_c, small_r, nheads):
    b, s, qk = dq.shape
    v = dv_.shape[2]
    nb = s // TOKEN_BLOCK
    tb = TOKEN_BLOCK
    dk, dv = qk // nheads, v // nheads
    ns = small_c.shape[2]
    fwd = lambda n: pl.BlockSpec((1, tb, n), lambda i, j: (i, j, 0))
    bwd = lambda n: pl.BlockSpec((1, tb, n), lambda i, j: (i, _bwd_block(j, nb), 0))
    rfwd = pl.BlockSpec((1, 1, ns, tb), lambda i, j: (i, j, 0, 0))
    rbwd = pl.BlockSpec((1, 1, ns, tb), lambda i, j: (i, _bwd_block(j, nb), 0, 0))
    return pl.pallas_call(
        functools.partial(_gdn_kernel, dk=dk, dv=dv, nh=nheads),
        out_shape=[jax.ShapeDtypeStruct((b, s, v), F32), jax.ShapeDtypeStruct((b, s, v), F32)],
        grid=(b, nb),
        in_specs=[fwd(qk), fwd(qk), fwd(v), fwd(ns), rfwd, bwd(qk), bwd(qk), bwd(v), bwd(ns), rbwd],
        out_specs=[fwd(v), bwd(v)],
        scratch_shapes=[pltpu.VMEM((2, nheads, dk, dv), F32)],
        compiler_params=pltpu.CompilerParams(dimension_semantics=("parallel", "arbitrary"),
                                             vmem_limit_bytes=VMEM_LIMIT),
        name="gdn_scan",
    )(dq, dk_, dv_, small_c, small_r, dq, dk_, dv_, small_c, small_r)


def _out_proj_kernel(x_ref, mod_ref, af_ref, ab_ref, bf_ref, bb_ref, ga_ref, gb_ref, na_ref, nb_ref,
                     wo_ref, o_ref, *, d, nh_a, nh_b):
    m = mod_ref[0]
    gt1 = m[:, 2 * d:3 * d]
    parts = []
    for (f_ref, b_ref, g_ref, n_ref, nh) in ((af_ref, ab_ref, ga_ref, na_ref, nh_a),
                                             (bf_ref, bb_ref, gb_ref, nb_ref, nh_b)):
        o = f_ref[0] + b_ref[0]
        gate = _silu(g_ref[0].astype(F32))
        hv = o.shape[1] // nh
        for hd in range(nh):
            sl = slice(hd * hv, (hd + 1) * hv)
            parts.append((_rms(o[:, sl]) * n_ref[...] * gate[:, sl]).astype(BF16))
    y = jnp.concatenate(parts, axis=1)
    o_ref[0] = x_ref[0] + gt1 * _dot(y, wo_ref[...])


def _out_proj(xs, mod, layer, drop_ctx, oaf, oab, obf, obb, gg, dg, na, nb_, wo, nh_a, nh_b):
    b, s, d = xs.shape
    off = 1 if drop_ctx else 0
    nblk = s // TOKEN_BLOCK - off
    tb = TOKEN_BLOCK
    tok = lambda n: pl.BlockSpec((1, tb, n), lambda i, j: (i, j + off, 0))
    const2 = lambda shape: pl.BlockSpec(shape, lambda i, j: (0,) * len(shape))
    bsz = b
    mod_spec = pl.BlockSpec((1, 1, 6 * d),
                            lambda i, j: (layer * MOD_ROWS + jnp.where(j + off == 0, bsz, i), 0, 0))
    va, vb = oaf.shape[2], obf.shape[2]
    return pl.pallas_call(
        functools.partial(_out_proj_kernel, d=d, nh_a=nh_a, nh_b=nh_b),
        out_shape=jax.ShapeDtypeStruct((b, nblk * tb, d), F32),
        grid=(b, nblk),
        in_specs=[tok(d), mod_spec, tok(va), tok(va), tok(vb), tok(vb), tok(va), tok(vb),
                  const2((1, va // nh_a)), const2((1, vb // nh_b)), const2(wo.shape)],
        out_specs=pl.BlockSpec((1, tb, d), lambda i, j: (i, j, 0)),
        compiler_params=pltpu.CompilerParams(dimension_semantics=("parallel", "parallel"),
                                             vmem_limit_bytes=VMEM_LIMIT),
        name="out_proj",
    )(xs, mod, oaf, oab, obf, obb, gg, dg, na.reshape(1, -1), nb_.reshape(1, -1), wo)


def _mlp_kernel(x_ref, mod_ref, g2_ref, w1_ref, w2_ref, fg_ref, o_ref, *, d, final):
    x = x_ref[0]
    m = mod_ref[0]
    sh2, sc2, gt2 = m[:, 3 * d:4 * d], m[:, 4 * d:5 * d], m[:, 5 * d:6 * d]
    h = (_rms(x) * g2_ref[...] * (1.0 + sc2) + sh2).astype(BF16)
    a = jnp.maximum(_dot(h, w1_ref[...]), 0.0)
    y = x + gt2 * _dot((a * a).astype(BF16), w2_ref[...])
    if final:
        y = _rms(y) * fg_ref[...]
    o_ref[0] = y


def _mlp(xs, mod, layer, final, g2, w1, w2, fg):
    b, s, d = xs.shape
    nblk = s // TOKEN_BLOCK
    tb = TOKEN_BLOCK
    const2 = lambda shape: pl.BlockSpec(shape, lambda i, j: (0,) * len(shape))
    bsz = b
    has_ctx = not final
    mod_spec = pl.BlockSpec(
        (1, 1, 6 * d),
        lambda i, j: (layer * MOD_ROWS + (jnp.where(j == 0, bsz, i) if has_ctx else i), 0, 0))
    return pl.pallas_call(
        functools.partial(_mlp_kernel, d=d, final=final),
        out_shape=jax.ShapeDtypeStruct((b, s, d), F32),
        grid=(b, nblk),
        in_specs=[pl.BlockSpec((1, tb, d), lambda i, j: (i, j, 0)), mod_spec, const2((1, d)),
                  const2(w1.shape), const2(w2.shape), const2((1, d))],
        out_specs=pl.BlockSpec((1, tb, d), lambda i, j: (i, j, 0)),
        compiler_params=pltpu.CompilerParams(dimension_semantics=("parallel", "parallel"),
                                             vmem_limit_bytes=VMEM_LIMIT),
        name="mlp",
    )(xs, mod, g2.reshape(1, d), w1, w2, fg.reshape(1, d))


def kernel(x, c, ctx, c_ctx, w_ada, b_ada, norm1_g, norm2_g, w_in, gla_w_lr, gla_b_lr, gdn_conv_w,
           gdn_a_log, gdn_dt_bias, gla_norm_g, gdn_norm_g, w_out, w_ff1, w_ff2, final_norm_g):
    bsz, t, d = x.shape
    depth = w_ada.shape[0]
    assert ctx.shape[1] == TOKEN_BLOCK and t % TOKEN_BLOCK == 0 and TOKEN_BLOCK % GRID_W == 0
    assert bsz + 1 <= MOD_ROWS
    qk_a = gla_w_lr.shape[3]
    rank = gla_w_lr.shape[2]
    v_a = gla_norm_g.shape[1] * GLA_HEADS
    conv_dim = gdn_conv_w.shape[2]
    v_b = gdn_norm_g.shape[1] * GDN_HEADS
    qk_b = (conv_dim - v_b) // 2
    nd = 2 * GDN_HEADS
    dims = (qk_a, v_a, rank, conv_dim, qk_b, v_b, GDN_HEADS)

    o_r = 2 * qk_a + 2 * v_a
    o_c = o_r + rank
    o_g = o_c + conv_dim
    o_s = o_g + v_b
    assert w_in.shape[2] == o_s + 2 * nd
    w_main = jnp.concatenate([w_in[:, :, :o_r], w_in[:, :, o_c:o_s]], axis=2).astype(BF16)
    w_sm = jnp.concatenate([w_in[:, :, o_r:o_c], w_in[:, :, o_s:]], axis=2)
    w_small = jnp.pad(w_sm, ((0, 0), (0, 0), (0, LANES - w_sm.shape[2]))).astype(BF16)
    w_small_t = jnp.swapaxes(w_in[:, :, o_s:], 1, 2).astype(BF16)

    cc = jnp.concatenate([c, c_ctx[None, :], jnp.zeros((MOD_ROWS - bsz - 1, d), F32)], axis=0)
    mod = _modulation(cc, w_ada.astype(BF16), b_ada).reshape(depth * MOD_ROWS, 1, 6 * d)

    xs = jnp.concatenate([ctx, x], axis=1)
    for l in range(depth):
        last = l == depth - 1
        (gq, gk, gv, gg, gla, dq, dk_, dv_, dg, small_c, small_r) = _in_proj(
            xs, mod, l, bsz, norm1_g[l], w_main[l], w_small[l], w_small_t[l],
            gla_w_lr[l].astype(BF16), gla_b_lr[l], gdn_conv_w[l], gdn_a_log[l], gdn_dt_bias[l], dims)
        oaf, oab = _gla_scan(gq, gk, gv, gla, GLA_HEADS)
        obf, obb = _gdn_scan(dq, dk_, dv_, small_c, small_r, GDN_HEADS)
        xs = _out_proj(xs, mod, l, last, oaf, oab, obf, obb, gg, dg, gla_norm_g[l], gdn_norm_g[l],
                       w_out[l].astype(BF16), GLA_HEADS, GDN_HEADS)
        xs = _mlp(xs, mod, l, last, norm2_g[l], w_ff1[l].astype(BF16), w_ff2[l].astype(BF16), final_norm_g)
    return xs
```

```python
import functools

import jax
import jax.numpy as jnp
from jax import lax
from jax.experimental import pallas as pl
from jax.experimental.pallas import tpu as pltpu

EPS = 1e-6
GRID_W = 64
GLA_HEADS = 4
GDN_HEADS = 4
GLA_TAU = 16.0
TOKEN_BLOCK = 256
CHUNK = 64
LANES = 128
MOD_ROWS = 16
VMEM_LIMIT = 56 * 1024 * 1024

F32 = jnp.float32
BF16 = jnp.bfloat16


def _dot(a, b):
    return jnp.dot(a, b, preferred_element_type=F32)


def _dot_nt(a, b):
    return lax.dot_general(a, b, (((1,), (1,)), ((), ())), preferred_element_type=F32)


def _dot_tn(a, b):
    return lax.dot_general(a, b, (((0,), (0,)), ((), ())), preferred_element_type=F32)


def _silu(x):
    return x / (1.0 + jnp.exp(-x))


def _sigmoid(x):
    return 1.0 / (1.0 + jnp.exp(-x))


def _softplus(x):
    return jnp.maximum(x, 0.0) + jnp.log(1.0 + jnp.exp(-jnp.abs(x)))


def _log_sigmoid(x):
    return jnp.minimum(x, 0.0) - jnp.log(1.0 + jnp.exp(-jnp.abs(x)))


def _rms(x):
    return x * lax.rsqrt(jnp.mean(x * x, axis=-1, keepdims=True) + EPS)


def _cumsum_dot(tri_bf16, x, left):
    hi = x.astype(BF16)
    lo = (x - hi.astype(F32)).astype(BF16)
    if left:
        return _dot(tri_bf16, hi) + _dot(tri_bf16, lo)
    return _dot(hi, tri_bf16) + _dot(lo, tri_bf16)


def _tri(n, upper, strict):
    r = lax.broadcasted_iota(jnp.int32, (n, n), 0)
    c = lax.broadcasted_iota(jnp.int32, (n, n), 1)
    if upper:
        m = (r < c) if strict else (r <= c)
    else:
        m = (r > c) if strict else (r >= c)
    return m


def _couple(n, s, upper):
    r = lax.broadcasted_iota(jnp.int32, (n, n), 0)
    c = lax.broadcasted_iota(jnp.int32, (n, n), 1)
    same = (r & ~(2 * s - 1)) == (c & ~(2 * s - 1))
    r_hi, c_hi = (r & s) != 0, (c & s) != 0
    return same & ((~r_hi & c_hi) if upper else (r_hi & ~c_hi))


def _tri_blocks(n, upper):
    r = lax.broadcasted_iota(jnp.int32, (n, n), 0)
    c = lax.broadcasted_iota(jnp.int32, (n, n), 1)
    same = (r & ~(CHUNK - 1)) == (c & ~(CHUNK - 1))
    return same & ((r <= c) if upper else (r >= c))


def _mod_kernel(cc_ref, w_ref, b_ref, o_ref):
    s = _silu(cc_ref[...]).astype(BF16)
    o_ref[0] = _dot(s, w_ref[0]) + b_ref[0]


def _modulation(cc, w_ada_bf, b_ada):
    depth, d, d6 = w_ada_bf.shape
    nblk = d6 // d
    return pl.pallas_call(
        _mod_kernel,
        out_shape=jax.ShapeDtypeStruct((depth, MOD_ROWS, d6), F32),
        grid=(depth, nblk),
        in_specs=[
            pl.BlockSpec((MOD_ROWS, d), lambda l, n: (0, 0)),
            pl.BlockSpec((1, d, d), lambda l, n: (l, 0, n)),
            pl.BlockSpec((1, 1, d), lambda l, n: (l, 0, n)),
        ],
        out_specs=pl.BlockSpec((1, MOD_ROWS, d), lambda l, n: (l, 0, n)),
        compiler_params=pltpu.CompilerParams(dimension_semantics=("parallel", "parallel")),
        name="adaln_modulation",
    )(cc, w_ada_bf, b_ada.reshape(depth, 1, d6))


def _in_proj_kernel(x_ref, mod_ref, g1_ref, wm_ref, ws_ref, wst_ref, wlr_ref, blr_ref, cw_ref,
                    arow_ref, dtrow_ref, acol_ref, dtcol_ref,
                    gq_ref, gk_ref, gv_ref, gg_ref, gla_ref, dq_ref, dk_ref, dv_ref, dg_ref,
                    sc_ref, sr_ref, *, d, qk_a, v_a, rank, conv_dim, qk_b, v_b, nh_b):
    j = pl.program_id(1)
    x = x_ref[0]
    m = mod_ref[0]
    sh1, sc1 = m[:, 0:d], m[:, d:2 * d]
    h = _rms(x) * g1_ref[...] * (1.0 + sc1) + sh1
    hb = h.astype(BF16)
    o_conv = 2 * qk_a + 2 * v_a
    o_gate = o_conv + conv_dim
    u = _dot(hb, wm_ref[:, o_conv:o_gate])
    pa = _dot(hb, wm_ref[:, 0:o_conv])

    tb = x.shape[0]
    t = lax.broadcasted_iota(jnp.int32, (tb, 1), 0)
    seg_mask = jnp.where(j == 0, tb - 1, GRID_W - 1)
    first = (t & seg_mask) == 0
    last = (t & seg_mask) == seg_mask
    up = jnp.where(first, 0.0, pltpu.roll(u, 1, 0))
    un = jnp.where(last, 0.0, pltpu.roll(u, tb - 1, 0))
    cw = cw_ref[...]
    s = _silu(cw[0:1] * up + cw[1:2] * u + cw[2:3] * un)
    dkh = qk_b // nh_b
    for hd in range(nh_b):
        qh = s[:, hd * dkh:(hd + 1) * dkh]
        kh = s[:, qk_b + hd * dkh:qk_b + (hd + 1) * dkh]
        qn = qh * lax.rsqrt(jnp.sum(qh * qh, axis=-1, keepdims=True) + EPS) * (float(dkh) ** -0.5)
        kn = kh * lax.rsqrt(jnp.sum(kh * kh, axis=-1, keepdims=True) + EPS)
        dq_ref[0, :, hd * dkh:(hd + 1) * dkh] = qn.astype(BF16)
        dk_ref[0, :, hd * dkh:(hd + 1) * dkh] = kn.astype(BF16)
    dv_ref[0] = s[:, 2 * qk_b:2 * qk_b + v_b].astype(BF16)

    pg = _dot(hb, wm_ref[:, o_gate:o_gate + v_b])
    ps = _dot(hb, ws_ref[...])
    pst = _dot_nt(wst_ref[...], hb)
    dg_ref[0] = pg.astype(BF16)

    o = 0
    gq_ref[0] = (pa[:, o:o + qk_a] * (float(qk_a // GLA_HEADS) ** -0.5)).astype(BF16)
    o += qk_a
    gk_ref[0] = pa[:, o:o + qk_a].astype(BF16)
    o += qk_a
    gv_ref[0] = pa[:, o:o + v_a].astype(BF16)
    o += v_a
    gg_ref[0] = pa[:, o:o + v_a].astype(BF16)
    r_a = ps[:, 0:rank].astype(BF16)
    for dr in range(2):
        lr = _dot(r_a, wlr_ref[dr]) + blr_ref[dr]
        gla_ref[0, :, dr * qk_a:(dr + 1) * qk_a] = _log_sigmoid(lr) * (1.0 / GLA_TAU)

    nd = 2 * nh_b
    a_c = ps[:, rank:rank + nd]
    b_c = ps[:, rank + nd:rank + 2 * nd]
    sc_ref[0, :, 0:nd] = -jnp.exp(arow_ref[...]) * _softplus(a_c + dtrow_ref[...])
    sc_ref[0, :, nd:2 * nd] = _sigmoid(b_c)
    sr_ref[0, 0, 0:nd, :] = -jnp.exp(acol_ref[...]) * _softplus(pst[0:nd] + dtcol_ref[...])
    sr_ref[0, 0, nd:2 * nd, :] = _sigmoid(pst[nd:2 * nd])


def _in_proj(xs, mod, layer, bsz, g1, wm, ws, wst, wlr, blr, cw, a_log, dt_bias, dims):
    b, s, d = xs.shape
    nb = s // TOKEN_BLOCK
    qk_a, v_a, rank, conv_dim, qk_b, v_b, nh_b = dims
    nd = 2 * nh_b
    tb = TOKEN_BLOCK
    const2 = lambda shape: pl.BlockSpec(shape, lambda i, j: (0,) * len(shape))
    tok = lambda n: pl.BlockSpec((1, tb, n), lambda i, j: (i, j, 0))
    mod_spec = pl.BlockSpec((1, 1, 6 * d), lambda i, j: (layer * MOD_ROWS + jnp.where(j == 0, bsz, i), 0, 0))
    out_shape = [
        jax.ShapeDtypeStruct((b, s, qk_a), BF16), jax.ShapeDtypeStruct((b, s, qk_a), BF16),
        jax.ShapeDtypeStruct((b, s, v_a), BF16), jax.ShapeDtypeStruct((b, s, v_a), BF16),
        jax.ShapeDtypeStruct((b, s, 2 * qk_a), F32),
        jax.ShapeDtypeStruct((b, s, qk_b), BF16), jax.ShapeDtypeStruct((b, s, qk_b), BF16),
        jax.ShapeDtypeStruct((b, s, v_b), BF16), jax.ShapeDtypeStruct((b, s, v_b), BF16),
        jax.ShapeDtypeStruct((b, s, 2 * nd), F32),
        jax.ShapeDtypeStruct((b, nb, 2 * nd, tb), F32),
    ]
    out_specs = [tok(qk_a), tok(qk_a), tok(v_a), tok(v_a), tok(2 * qk_a),
                 tok(qk_b), tok(qk_b), tok(v_b), tok(v_b), tok(2 * nd),
                 pl.BlockSpec((1, 1, 2 * nd, tb), lambda i, j: (i, j, 0, 0))]
    kern = functools.partial(_in_proj_kernel, d=d, qk_a=qk_a, v_a=v_a, rank=rank, conv_dim=conv_dim,
                             qk_b=qk_b, v_b=v_b, nh_b=nh_b)
    return pl.pallas_call(
        kern, out_shape=out_shape, grid=(b, nb),
        in_specs=[
            pl.BlockSpec((1, tb, d), lambda i, j: (i, j, 0)), mod_spec, const2((1, d)),
            const2(wm.shape), const2(ws.shape), const2(wst.shape), const2(wlr.shape), const2((2, 1, qk_a)),
            const2(cw.shape), const2((1, nd)), const2((1, nd)), const2((nd, 1)), const2((nd, 1)),
        ],
        out_specs=out_specs,
        compiler_params=pltpu.CompilerParams(dimension_semantics=("parallel", "parallel"),
                                             vmem_limit_bytes=VMEM_LIMIT),
        name="in_proj",
    )(xs, mod, g1.reshape(1, d), wm, ws, wst, wlr, blr.reshape(2, 1, -1), cw,
      a_log.reshape(1, nd), dt_bias.reshape(1, nd), a_log.reshape(nd, 1), dt_bias.reshape(nd, 1))


def _bwd_block(j, nb):
    return jnp.where(j == 0, 0, nb - j)


def _gla_kernel(qf_ref, kf_ref, vf_ref, laf_ref, qb_ref, kb_ref, vb_ref, lab_ref,
                of_ref, ob_ref, st_ref, *, dk, dv):
    j = pl.program_id(1)

    @pl.when(j == 0)
    def _():
        st_ref[...] = jnp.zeros_like(st_ref)

    tb = qf_ref.shape[1]
    nchunk = tb // CHUNK
    npair = (GLA_HEADS * dk) // LANES
    hpp = LANES // dk
    lane = lax.broadcasted_iota(jnp.int32, (1, LANES), 1)
    lms = [(lane >= hh * dk) & (lane < (hh + 1) * dk) for hh in range(hpp)]
    mid = CHUNK // 2
    refs = ((qf_ref, kf_ref, vf_ref, laf_ref, of_ref), (qb_ref, kb_ref, vb_ref, lab_ref, ob_ref))
    chains = [(dr, pr) for dr in range(2) for pr in range(npair)]
    units = [(dr, pr, c) for (dr, pr) in chains for c in range(nchunk)]
    causal = {dr: _tri(CHUNK, dr == 1, False) for dr in range(2)}
    tri_blk = {dr: _tri_blocks(tb, dr == 1).astype(BF16) for dr in range(2)}

    bcum = {}
    for (dr, pr) in chains:
        la = refs[dr][3][0, :, pr * LANES:(pr + 1) * LANES]
        bcum[dr, pr] = _cumsum_dot(tri_blk[dr], la, True)

    q_dec, k_hat, q_mid, k_mid, a_last = {}, {}, {}, {}, {}
    for u in units:
        dr, pr, c = u
        q_ref, k_ref = refs[dr][0], refs[dr][1]
        rows = slice(c * CHUNK, (c + 1) * CHUNK)
        lanes = slice(pr * LANES, (pr + 1) * LANES)
        last = c * CHUNK + (0 if dr == 1 else CHUNK - 1)
        b = bcum[dr, pr][rows]
        b_last = bcum[dr, pr][last:last + 1]
        b_mid = bcum[dr, pr][c * CHUNK + mid:c * CHUNK + mid + 1]
        qc = q_ref[0, rows, lanes].astype(F32)
        kc = k_ref[0, rows, lanes].astype(F32)
        q_dec[u] = qc * jnp.exp(b)
        k_hat[u] = (kc * jnp.exp(b_last - b)).astype(BF16)
        q_mid[u] = qc * jnp.exp(b - b_mid)
        k_mid[u] = (kc * jnp.exp(b_mid - b)).astype(BF16)
        a_last[u] = jnp.exp(b_last)

    sc = {}
    for u in units:
        for hh in range(hpp):
            sc[u, hh] = _dot_nt(jnp.where(lms[hh], q_mid[u], 0.0).astype(BF16), k_mid[u])
    intra, dst = {}, {}
    for u in units:
        dr, pr, c = u
        rows = slice(c * CHUNK, (c + 1) * CHUNK)
        acc = None
        for hh in range(hpp):
            head = pr * hpp + hh
            vh = refs[dr][2][0, rows, head * dv:(head + 1) * dv]
            pm = jnp.where(causal[dr], sc[u, hh], 0.0).astype(BF16)
            intra[u, hh] = _dot(pm, vh)
            dh = _dot_tn(vh, k_hat[u])
            acc = dh if acc is None else jnp.where(lms[hh], dh, acc)
        dst[u] = acc

    st = {ch: st_ref[ch[0], ch[1]] for ch in chains}
    for ci in range(nchunk):
        for ch in chains:
            dr, pr = ch
            c = nchunk - 1 - ci if dr == 1 else ci
            u = (dr, pr, c)
            rows = slice(c * CHUNK, (c + 1) * CHUNK)
            st_bf = st[ch].astype(BF16)
            for hh in range(hpp):
                head = pr * hpp + hh
                inter = _dot_nt(jnp.where(lms[hh], q_dec[u], 0.0).astype(BF16), st_bf)
                refs[dr][4][0, rows, head * dv:(head + 1) * dv] = (intra[u, hh] + inter).astype(BF16)
            st[ch] = st[ch] * a_last[u] + dst[u]
    for ch in chains:
        st_ref[ch[0], ch[1]] = st[ch]


def _gla_scan(gq, gk, gv, gla, nheads):
    b, s, qk = gq.shape
    v = gv.shape[2]
    nb = s // TOKEN_BLOCK
    tb = TOKEN_BLOCK
    dk, dv = qk // nheads, v // nheads
    fwd = lambda n, col=0: pl.BlockSpec((1, tb, n), lambda i, j: (i, j, col))
    bwd = lambda n, col=0: pl.BlockSpec((1, tb, n), lambda i, j: (i, _bwd_block(j, nb), col))
    return pl.pallas_call(
        functools.partial(_gla_kernel, dk=dk, dv=dv),
        out_shape=[jax.ShapeDtypeStruct((b, s, v), BF16), jax.ShapeDtypeStruct((b, s, v), BF16)],
        grid=(b, nb),
        in_specs=[fwd(qk), fwd(qk), fwd(v), fwd(qk, 0), bwd(qk), bwd(qk), bwd(v), bwd(qk, 1)],
        out_specs=[fwd(v), bwd(v)],
        scratch_shapes=[pltpu.VMEM((2, qk // LANES, dv, LANES), F32)],
        compiler_params=pltpu.CompilerParams(dimension_semantics=("parallel", "arbitrary"),
                                             vmem_limit_bytes=VMEM_LIMIT),
        name="gla_scan",
    )(gq, gk, gv, gla, gq, gk, gv, gla)


def _gdn_kernel(qf_ref, kf_ref, vf_ref, scf_ref, srf_ref, qb_ref, kb_ref, vb_ref, scb_ref, srb_ref,
                of_ref, ob_ref, st_ref, *, dk, dv, nh):
    j = pl.program_id(1)

    @pl.when(j == 0)
    def _():
        st_ref[...] = jnp.zeros_like(st_ref)

    tb = qf_ref.shape[1]
    nchunk = tb // CHUNK
    eye = _tri(CHUNK, False, False) & _tri(CHUNK, True, False)

    refs = ((qf_ref, kf_ref, vf_ref, scf_ref, srf_ref, of_ref),
            (qb_ref, kb_ref, vb_ref, scb_ref, srb_ref, ob_ref))
    chains = [(dr, hd) for dr in range(2) for hd in range(nh)]
    units = [(dr, c, hd) for dr in range(2) for c in range(nchunk) for hd in range(nh)]
    strict = {dr: _tri(CHUNK, dr == 1, True) for dr in range(2)}
    incl = {dr: _tri(CHUNK, dr == 1, False) for dr in range(2)}

    g_cols, g_rows, small = {}, {}, {}
    for dr in range(2):
        small[dr] = refs[dr][3][0]
        g_cols[dr] = _cumsum_dot(_tri_blocks(tb, dr == 1).astype(BF16), small[dr], True)
        g_rows[dr] = _cumsum_dot(_tri_blocks(tb, dr != 1).astype(BF16), refs[dr][4][0, 0], False)

    kk, qk = {}, {}
    for u in units:
        dr, c, hd = u
        rows = slice(c * CHUNK, (c + 1) * CHUNK)
        kh = refs[dr][1][0, rows, hd * dk:(hd + 1) * dk]
        kk[u] = _dot_nt(kh, kh)
        qk[u] = _dot_nt(refs[dr][0][0, rows, hd * dk:(hd + 1) * dk], kh)

    x, aqk, rhs, q_dec, k_dec, a_last = {}, {}, {}, {}, {}, {}
    for u in units:
        dr, c, hd = u
        rows = slice(c * CHUNK, (c + 1) * CHUNK)
        ia = dr * nh + hd
        ib = 2 * nh + dr * nh + hd
        last = c * CHUNK + (0 if dr == 1 else CHUNK - 1)
        g_c = g_cols[dr][rows, ia:ia + 1]
        g_r = g_rows[dr][ia:ia + 1, rows]
        be_c = small[dr][rows, ib:ib + 1]
        g_last = g_cols[dr][last:last + 1, ia:ia + 1]
        e = jnp.exp(jnp.minimum(g_c - g_r, 0.0))
        x[u] = jnp.where(strict[dr], be_c * kk[u] * e, 0.0)
        aqk[u] = jnp.where(incl[dr], qk[u] * e, 0.0).astype(BF16)
        eg = jnp.exp(g_c)
        qh = refs[dr][0][0, rows, hd * dk:(hd + 1) * dk].astype(F32)
        khf = refs[dr][1][0, rows, hd * dk:(hd + 1) * dk].astype(F32)
        vh = refs[dr][2][0, rows, hd * dv:(hd + 1) * dv].astype(F32)
        rhs[u] = jnp.concatenate([(be_c * vh).astype(BF16), ((be_c * eg) * khf).astype(BF16)], axis=1)
        q_dec[u] = qh * eg
        k_dec[u] = (khf * jnp.exp(g_last - g_c)).astype(BF16)
        a_last[u] = jnp.exp(g_last)

    t = {u: jnp.where(eye, 1.0, jnp.where(_couple(CHUNK, 1, u[0] == 1), -x[u], 0.0)) for u in units}
    s = 2
    while s < CHUNK:
        a = {u: jnp.where(_couple(CHUNK, s, u[0] == 1), x[u], 0.0).astype(BF16) for u in units}
        tb16 = {u: t[u].astype(BF16) for u in units}
        ta = {u: _dot(tb16[u], a[u]).astype(BF16) for u in units}
        t = {u: t[u] - _dot(ta[u], tb16[u]) for u in units}
        s *= 2

    uw = {u: _dot(t[u].astype(BF16), rhs[u]).astype(BF16) for u in units}

    kd_uw = {u: _dot_tn(k_dec[u], uw[u]) for u in units}
    aq_uw = {u: _dot(aqk[u], uw[u]) for u in units}
    lhs = {u: jnp.concatenate([(-kd_uw[u][:, dv:]).astype(BF16),
                               (q_dec[u] - aq_uw[u][:, dv:]).astype(BF16)], axis=0) for u in units}

    st = {ch: st_ref[ch[0], ch[1]] for ch in chains}
    for ci in range(nchunk):
        for ch in chains:
            dr, hd = ch
            c = nchunk - 1 - ci if dr == 1 else ci
            u = (dr, c, hd)
            rows = slice(c * CHUNK, (c + 1) * CHUNK)
            res = _dot(lhs[u], st[ch].astype(BF16))
            refs[dr][5][0, rows, hd * dv:(hd + 1) * dv] = (res[dk:] + aq_uw[u][:, :dv]).astype(BF16)
            st[ch] = st[ch] * a_last[u] + res[:dk] + kd_uw[u][:, :dv]
    for ch in chains:
        st_ref[ch[0], ch[1]] = st[ch]


def _gdn_scan(dq, dk_, dv_, small_c, small_r, nheads):
    b, s, qk = dq.shape
    v = dv_.shape[2]
    nb = s // TOKEN_BLOCK
    tb = TOKEN_BLOCK
    dk, dv = qk // nheads, v // nheads
    ns = small_c.shape[2]
    fwd = lambda n: pl.BlockSpec((1, tb, n), lambda i, j: (i, j, 0))
    bwd = lambda n: pl.BlockSpec((1, tb, n), lambda i, j: (i, _bwd_block(j, nb), 0))
    rfwd = pl.BlockSpec((1, 1, ns, tb), lambda i, j: (i, j, 0, 0))
    rbwd = pl.BlockSpec((1, 1, ns, tb), lambda i, j: (i, _bwd_block(j, nb), 0, 0))
    return pl.pallas_call(
        functools.partial(_gdn_kernel, dk=dk, dv=dv, nh=nheads),
        out_shape=[jax.ShapeDtypeStruct((b, s, v), BF16), jax.ShapeDtypeStruct((b, s, v), BF16)],
        grid=(b, nb),
        in_specs=[fwd(qk), fwd(qk), fwd(v), fwd(ns), rfwd, bwd(qk), bwd(qk), bwd(v), bwd(ns), rbwd],
        out_specs=[fwd(v), bwd(v)],
        scratch_shapes=[pltpu.VMEM((2, nheads, dk, dv), F32)],
        compiler_params=pltpu.CompilerParams(dimension_semantics=("parallel", "arbitrary"),
                                             vmem_limit_bytes=VMEM_LIMIT),
        name="gdn_scan",
    )(dq, dk_, dv_, small_c, small_r, dq, dk_, dv_, small_c, small_r)


def _post_kernel(x_ref, mod_ref, af_ref, ab_ref, bf_ref, bb_ref, ga_ref, gb_ref, na_ref, nb_ref,
                 wo_ref, g2_ref, w1_ref, w2_ref, fg_ref, o_ref, *, d, nh_a, nh_b, final):
    m = mod_ref[0]
    gt1 = m[:, 2 * d:3 * d]
    sh2, sc2, gt2 = m[:, 3 * d:4 * d], m[:, 4 * d:5 * d], m[:, 5 * d:6 * d]
    parts = []
    for (f_ref, b_ref, g_ref, n_ref, nh) in ((af_ref, ab_ref, ga_ref, na_ref, nh_a),
                                             (bf_ref, bb_ref, gb_ref, nb_ref, nh_b)):
        o = f_ref[0].astype(F32) + b_ref[0].astype(F32)
        gate = _silu(g_ref[0].astype(F32))
        hv = o.shape[1] // nh
        for hd in range(nh):
            sl = slice(hd * hv, (hd + 1) * hv)
            parts.append((_rms(o[:, sl]) * n_ref[...] * gate[:, sl]).astype(BF16))
    y = jnp.concatenate(parts, axis=1)
    x1 = x_ref[0] + gt1 * _dot(y, wo_ref[...])
    h = (_rms(x1) * g2_ref[...] * (1.0 + sc2) + sh2).astype(BF16)
    a = jnp.maximum(_dot(h, w1_ref[...]), 0.0)
    x2 = x1 + gt2 * _dot((a * a).astype(BF16), w2_ref[...])
    if final:
        x2 = _rms(x2) * fg_ref[...]
    o_ref[0] = x2


def _post(xs, mod, layer, final, oaf, oab, obf, obb, gg, dg, na, nb_, wo, g2, w1, w2, fg, nh_a, nh_b):
    b, s, d = xs.shape
    off = 1 if final else 0
    nblk = s // TOKEN_BLOCK - off
    tb = TOKEN_BLOCK
    tok = lambda n: pl.BlockSpec((1, tb, n), lambda i, j: (i, j + off, 0))
    const2 = lambda shape: pl.BlockSpec(shape, lambda i, j: (0,) * len(shape))
    weight = lambda shape: pl.BlockSpec(shape, lambda i, j: (0,) * len(shape), pipeline_mode=pl.Buffered(1))
    bsz = b
    mod_spec = pl.BlockSpec((1, 1, 6 * d),
                            lambda i, j: (layer * MOD_ROWS + jnp.where(j + off == 0, bsz, i), 0, 0))
    va, vb = oaf.shape[2], obf.shape[2]
    return pl.pallas_call(
        functools.partial(_post_kernel, d=d, nh_a=nh_a, nh_b=nh_b, final=final),
        out_shape=jax.ShapeDtypeStruct((b, nblk * tb, d), F32),
        grid=(b, nblk),
        in_specs=[tok(d), mod_spec, tok(va), tok(va), tok(vb), tok(vb), tok(va), tok(vb),
                  const2((1, va // nh_a)), const2((1, vb // nh_b)), weight(wo.shape),
                  const2((1, d)), weight(w1.shape), weight(w2.shape), const2((1, d))],
        out_specs=pl.BlockSpec((1, tb, d), lambda i, j: (i, j, 0)),
        compiler_params=pltpu.CompilerParams(dimension_semantics=("parallel", "parallel"),
                                             vmem_limit_bytes=VMEM_LIMIT),
        name="post",
    )(xs, mod, oaf, oab, obf, obb, gg, dg, na.reshape(1, -1), nb_.reshape(1, -1), wo,
      g2.reshape(1, d), w1, w2, fg.reshape(1, d))


def kernel(x, c, ctx, c_ctx, w_ada, b_ada, norm1_g, norm2_g, w_in, gla_w_lr, gla_b_lr, gdn_conv_w,
           gdn_a_log, gdn_dt_bias, gla_norm_g, gdn_norm_g, w_out, w_ff1, w_ff2, final_norm_g):
    bsz, t, d = x.shape
    depth = w_ada.shape[0]
    assert ctx.shape[1] == TOKEN_BLOCK and t % TOKEN_BLOCK == 0 and TOKEN_BLOCK % GRID_W == 0
    assert bsz + 1 <= MOD_ROWS
    qk_a = gla_w_lr.shape[3]
    rank = gla_w_lr.shape[2]
    v_a = gla_norm_g.shape[1] * GLA_HEADS
    conv_dim = gdn_conv_w.shape[2]
    v_b = gdn_norm_g.shape[1] * GDN_HEADS
    qk_b = (conv_dim - v_b) // 2
    nd = 2 * GDN_HEADS
    dims = (qk_a, v_a, rank, conv_dim, qk_b, v_b, GDN_HEADS)

    o_r = 2 * qk_a + 2 * v_a
    o_c = o_r + rank
    o_g = o_c + conv_dim
    o_s = o_g + v_b
    assert w_in.shape[2] == o_s + 2 * nd
    w_main = jnp.concatenate([w_in[:, :, :o_r], w_in[:, :, o_c:o_s]], axis=2).astype(BF16)
    w_sm = jnp.concatenate([w_in[:, :, o_r:o_c], w_in[:, :, o_s:]], axis=2)
    w_small = jnp.pad(w_sm, ((0, 0), (0, 0), (0, LANES - w_sm.shape[2]))).astype(BF16)
    w_small_t = jnp.swapaxes(w_in[:, :, o_s:], 1, 2).astype(BF16)

    cc = jnp.concatenate([c, c_ctx[None, :], jnp.zeros((MOD_ROWS - bsz - 1, d), F32)], axis=0)
    mod = _modulation(cc, w_ada.astype(BF16), b_ada).reshape(depth * MOD_ROWS, 1, 6 * d)

    xs = jnp.concatenate([ctx, x], axis=1)
    for l in range(depth):
        last = l == depth - 1
        (gq, gk, gv, gg, gla, dq, dk_, dv_, dg, small_c, small_r) = _in_proj(
            xs, mod, l, bsz, norm1_g[l], w_main[l], w_small[l], w_small_t[l],
            gla_w_lr[l].astype(BF16), gla_b_lr[l], gdn_conv_w[l], gdn_a_log[l], gdn_dt_bias[l], dims)
        oaf, oab = _gla_scan(gq, gk, gv, gla, GLA_HEADS)
        obf, obb = _gdn_scan(dq, dk_, dv_, small_c, small_r, GDN_HEADS)
        xs = _post(xs, mod, l, last, oaf, oab, obf, obb, gg, dg, gla_norm_g[l], gdn_norm_g[l],
                   w_out[l].astype(BF16), norm2_g[l], w_ff1[l].astype(BF16), w_ff2[l].astype(BF16),
                   final_norm_g, GLA_HEADS, GDN_HEADS)
    return xs
```

```python
import functools

import jax
import jax.numpy as jnp
from jax import lax
from jax.experimental import pallas as pl
from jax.experimental.pallas import tpu as pltpu

EPS = 1e-6
GRID_W = 64
GLA_HEADS = 4
GDN_HEADS = 4
GLA_TAU = 16.0
TOKEN_BLOCK = 256
CHUNK = 64
LANES = 128
MOD_ROWS = 16
VMEM_LIMIT = 56 * 1024 * 1024

F32 = jnp.float32
BF16 = jnp.bfloat16


def _dot(a, b):
    return jnp.dot(a, b, preferred_element_type=F32)


def _dot_nt(a, b):
    return lax.dot_general(a, b, (((1,), (1,)), ((), ())), preferred_element_type=F32)


def _dot_tn(a, b):
    return lax.dot_general(a, b, (((0,), (0,)), ((), ())), preferred_element_type=F32)


def _silu(x):
    return x / (1.0 + jnp.exp(-x))


def _sigmoid(x):
    return 1.0 / (1.0 + jnp.exp(-x))


def _softplus(x):
    return jnp.maximum(x, 0.0) + jnp.log(1.0 + jnp.exp(-jnp.abs(x)))


def _log_sigmoid(x):
    return jnp.minimum(x, 0.0) - jnp.log(1.0 + jnp.exp(-jnp.abs(x)))


def _rms(x):
    return x * lax.rsqrt(jnp.mean(x * x, axis=-1, keepdims=True) + EPS)


def _cumsum_dot(tri_bf16, x, left):
    hi = x.astype(BF16)
    lo = (x - hi.astype(F32)).astype(BF16)
    if left:
        return _dot(tri_bf16, hi) + _dot(tri_bf16, lo)
    return _dot(hi, tri_bf16) + _dot(lo, tri_bf16)


def _tri(n, upper, strict):
    r = lax.broadcasted_iota(jnp.int32, (n, n), 0)
    c = lax.broadcasted_iota(jnp.int32, (n, n), 1)
    if upper:
        m = (r < c) if strict else (r <= c)
    else:
        m = (r > c) if strict else (r >= c)
    return m


def _couple(n, s, upper):
    r = lax.broadcasted_iota(jnp.int32, (n, n), 0)
    c = lax.broadcasted_iota(jnp.int32, (n, n), 1)
    same = (r & ~(2 * s - 1)) == (c & ~(2 * s - 1))
    r_hi, c_hi = (r & s) != 0, (c & s) != 0
    return same & ((~r_hi & c_hi) if upper else (r_hi & ~c_hi))


def _tri_blocks(n, upper):
    r = lax.broadcasted_iota(jnp.int32, (n, n), 0)
    c = lax.broadcasted_iota(jnp.int32, (n, n), 1)
    same = (r & ~(CHUNK - 1)) == (c & ~(CHUNK - 1))
    return same & ((r <= c) if upper else (r >= c))


def _mod_kernel(cc_ref, w_ref, b_ref, o_ref):
    s = _silu(cc_ref[...]).astype(BF16)
    o_ref[0] = _dot(s, w_ref[0]) + b_ref[0]


def _modulation(cc, w_ada_bf, b_ada):
    depth, d, d6 = w_ada_bf.shape
    nblk = d6 // d
    return pl.pallas_call(
        _mod_kernel,
        out_shape=jax.ShapeDtypeStruct((depth, MOD_ROWS, d6), F32),
        grid=(depth, nblk),
        in_specs=[
            pl.BlockSpec((MOD_ROWS, d), lambda l, n: (0, 0)),
            pl.BlockSpec((1, d, d), lambda l, n: (l, 0, n)),
            pl.BlockSpec((1, 1, d), lambda l, n: (l, 0, n)),
        ],
        out_specs=pl.BlockSpec((1, MOD_ROWS, d), lambda l, n: (l, 0, n)),
        compiler_params=pltpu.CompilerParams(dimension_semantics=("parallel", "parallel")),
        name="adaln_modulation",
    )(cc, w_ada_bf, b_ada.reshape(depth, 1, d6))


def _in_proj_kernel(c_ref, x_ref, mod_ref, g1_ref, wm_ref, ws_ref, wst_ref, wlr_ref, blr_ref, cw_ref,
                    arow_ref, dtrow_ref, acol_ref, dtcol_ref,
                    gq_ref, gk_ref, gv_ref, gg_ref, gla_ref, dq_ref, dk_ref, dv_ref, dg_ref,
                    sc_ref, sr_ref, *, d, qk_a, v_a, rank, conv_dim, qk_b, v_b, nh_b):
    j = pl.program_id(1)
    x = jnp.where(j == 0, c_ref[0], x_ref[0])
    m = mod_ref[0]
    sh1, sc1 = m[:, 0:d], m[:, d:2 * d]
    h = _rms(x) * g1_ref[...] * (1.0 + sc1) + sh1
    hb = h.astype(BF16)
    o_conv = 2 * qk_a + 2 * v_a
    o_gate = o_conv + conv_dim
    grp = 2 * LANES
    tb = x.shape[0]
    t = lax.broadcasted_iota(jnp.int32, (tb, 1), 0)
    seg_mask = jnp.where(j == 0, tb - 1, GRID_W - 1)
    first = (t & seg_mask) == 0
    last = (t & seg_mask) == seg_mask
    dkh = qk_b // nh_b

    def conv_dot(k):
        return _dot(hb, wm_ref[:, o_conv + k * grp:o_conv + (k + 1) * grp])

    def conv_group(u, k):
        up = jnp.where(first, 0.0, pltpu.roll(u, 1, 0))
        un = jnp.where(last, 0.0, pltpu.roll(u, tb - 1, 0))
        cw = cw_ref[:, k * grp:(k + 1) * grp]
        s = _silu(cw[0:1] * up + cw[1:2] * u + cw[2:3] * un)
        c0 = k * grp
        if c0 < 2 * qk_b:
            is_q = c0 < qk_b
            dst = dq_ref if is_q else dk_ref
            base = c0 if is_q else c0 - qk_b
            for hh in range(grp // dkh):
                sh = s[:, hh * dkh:(hh + 1) * dkh]
                nrm = sh * lax.rsqrt(jnp.sum(sh * sh, axis=-1, keepdims=True) + EPS)
                if is_q:
                    nrm = nrm * (float(dkh) ** -0.5)
                dst[0, :, base + hh * dkh:base + (hh + 1) * dkh] = nrm.astype(BF16)
        else:
            base = c0 - 2 * qk_b
            dv_ref[0, :, base:base + grp] = s.astype(BF16)

    def light_group(k):
        c0 = k * grp
        w0 = c0 if c0 < o_conv else o_gate + (c0 - o_conv)
        pa = _dot(hb, wm_ref[:, w0:w0 + grp])
        if c0 < qk_a:
            gq_ref[0, :, c0:c0 + grp] = (pa * (float(qk_a // GLA_HEADS) ** -0.5)).astype(BF16)
        elif c0 < 2 * qk_a:
            gk_ref[0, :, c0 - qk_a:c0 - qk_a + grp] = pa.astype(BF16)
        elif c0 < 2 * qk_a + v_a:
            gv_ref[0, :, c0 - 2 * qk_a:c0 - 2 * qk_a + grp] = pa.astype(BF16)
        elif c0 < o_conv:
            gg_ref[0, :, c0 - 2 * qk_a - v_a:c0 - 2 * qk_a - v_a + grp] = pa.astype(BF16)
        else:
            dg_ref[0, :, c0 - o_conv:c0 - o_conv + grp] = pa.astype(BF16)

    def small_group(ps, pst):
        r_a = ps[:, 0:rank].astype(BF16)
        for dr in range(2):
            lr = _dot(r_a, wlr_ref[dr]) + blr_ref[dr]
            gla_ref[0, :, dr * qk_a:(dr + 1) * qk_a] = _log_sigmoid(lr) * (1.0 / GLA_TAU)
        nd = 2 * nh_b
        a_c = ps[:, rank:rank + nd]
        b_c = ps[:, rank + nd:rank + 2 * nd]
        sc_ref[0, :, 0:nd] = -jnp.exp(arow_ref[...]) * _softplus(a_c + dtrow_ref[...])
        sc_ref[0, :, nd:2 * nd] = _sigmoid(b_c)
        sr_ref[0, 0, 0:nd, :] = -jnp.exp(acol_ref[...]) * _softplus(pst[0:nd] + dtcol_ref[...])
        sr_ref[0, 0, nd:2 * nd, :] = _sigmoid(pst[nd:2 * nd])

    n_conv = conv_dim // grp
    n_light = (o_conv + v_b) // grp
    us = {0: conv_dot(0)}
    ps = _dot(hb, ws_ref[...])
    pst = _dot_nt(wst_ref[...], hb)
    if n_conv > 1:
        us[1] = conv_dot(1)
    g_next = 0
    for k in range(n_conv):
        if k + 2 < n_conv:
            us[k + 2] = conv_dot(k + 2)
        conv_group(us.pop(k), k)
        if k == 1 or n_conv == 1:
            small_group(ps, pst)
        todo = (n_light - g_next + (n_conv - k) - 1) // (n_conv - k) if k >= n_conv // 2 else 1
        for _ in range(min(todo, n_light - g_next)):
            light_group(g_next)
            g_next += 1
    while g_next < n_light:
        light_group(g_next)
        g_next += 1


def _in_proj(stream, mod, layer, bsz, g1, wm, ws, wst, wlr, blr, cw, a_log, dt_bias, dims):
    head, body, body_off = stream
    b, _, d = body.shape
    nb = body.shape[1] // TOKEN_BLOCK + body_off
    s = nb * TOKEN_BLOCK
    qk_a, v_a, rank, conv_dim, qk_b, v_b, nh_b = dims
    nd = 2 * nh_b
    tb = TOKEN_BLOCK
    const2 = lambda shape: pl.BlockSpec(shape, lambda i, j: (0,) * len(shape))
    tok = lambda n: pl.BlockSpec((1, tb, n), lambda i, j: (i, j, 0))
    mod_spec = pl.BlockSpec((1, 1, 6 * d), lambda i, j: (layer * MOD_ROWS + jnp.where(j == 0, bsz, i), 0, 0))
    out_shape = [
        jax.ShapeDtypeStruct((b, s, qk_a), BF16), jax.ShapeDtypeStruct((b, s, qk_a), BF16),
        jax.ShapeDtypeStruct((b, s, v_a), BF16), jax.ShapeDtypeStruct((b, s, v_a), BF16),
        jax.ShapeDtypeStruct((b, s, 2 * qk_a), F32),
        jax.ShapeDtypeStruct((b, s, qk_b), BF16), jax.ShapeDtypeStruct((b, s, qk_b), BF16),
        jax.ShapeDtypeStruct((b, s, v_b), BF16), jax.ShapeDtypeStruct((b, s, v_b), BF16),
        jax.ShapeDtypeStruct((b, s, 2 * nd), F32),
        jax.ShapeDtypeStruct((b, nb, 2 * nd, tb), F32),
    ]
    out_specs = [tok(qk_a), tok(qk_a), tok(v_a), tok(v_a), tok(2 * qk_a),
                 tok(qk_b), tok(qk_b), tok(v_b), tok(v_b), tok(2 * nd),
                 pl.BlockSpec((1, 1, 2 * nd, tb), lambda i, j: (i, j, 0, 0))]
    kern = functools.partial(_in_proj_kernel, d=d, qk_a=qk_a, v_a=v_a, rank=rank, conv_dim=conv_dim,
                             qk_b=qk_b, v_b=v_b, nh_b=nh_b)
    return pl.pallas_call(
        kern, out_shape=out_shape, grid=(b, nb),
        in_specs=[
            pl.BlockSpec((1, tb, d), lambda i, j: (i, 0, 0)),
            pl.BlockSpec((1, tb, d), lambda i, j: (i, jnp.maximum(j - body_off, 0), 0)), mod_spec, const2((1, d)),
            const2(wm.shape), const2(ws.shape), const2(wst.shape), const2(wlr.shape), const2((2, 1, qk_a)),
            const2(cw.shape), const2((1, nd)), const2((1, nd)), const2((nd, 1)), const2((nd, 1)),
        ],
        out_specs=out_specs,
        compiler_params=pltpu.CompilerParams(dimension_semantics=("parallel", "parallel"),
                                             vmem_limit_bytes=VMEM_LIMIT),
        name="in_proj",
    )(head, body, mod, g1.reshape(1, d), wm, ws, wst, wlr, blr.reshape(2, 1, -1), cw,
      a_log.reshape(1, nd), dt_bias.reshape(1, nd), a_log.reshape(nd, 1), dt_bias.reshape(nd, 1))


def _bwd_block(j, nb):
    return jnp.where(j == 0, 0, nb - j)


def _gla_kernel(qf_ref, kf_ref, vf_ref, laf_ref, qb_ref, kb_ref, vb_ref, lab_ref,
                of_ref, ob_ref, st_ref, *, dk, dv):
    j = pl.program_id(1)

    @pl.when(j == 0)
    def _():
        st_ref[...] = jnp.zeros_like(st_ref)

    tb = qf_ref.shape[1]
    nchunk = tb // CHUNK
    npair = (GLA_HEADS * dk) // LANES
    hpp = LANES // dk
    lane = lax.broadcasted_iota(jnp.int32, (1, LANES), 1)
    lms = [(lane >= hh * dk) & (lane < (hh + 1) * dk) for hh in range(hpp)]
    mid = CHUNK // 2
    refs = ((qf_ref, kf_ref, vf_ref, laf_ref, of_ref), (qb_ref, kb_ref, vb_ref, lab_ref, ob_ref))
    chains = [(dr, pr) for dr in range(2) for pr in range(npair)]
    units = [(dr, pr, c) for (dr, pr) in chains for c in range(nchunk)]
    causal = {dr: _tri(CHUNK, dr == 1, False) for dr in range(2)}
    tri_blk = {dr: _tri_blocks(tb, dr == 1).astype(BF16) for dr in range(2)}

    bcum = {}
    for (dr, pr) in chains:
        la = refs[dr][3][0, :, pr * LANES:(pr + 1) * LANES]
        bcum[dr, pr] = _cumsum_dot(tri_blk[dr], la, True)

    q_dec, k_hat, q_mid, k_mid, a_last = {}, {}, {}, {}, {}
    for u in units:
        dr, pr, c = u
        q_ref, k_ref = refs[dr][0], refs[dr][1]
        rows = slice(c * CHUNK, (c + 1) * CHUNK)
        lanes = slice(pr * LANES, (pr + 1) * LANES)
        last = c * CHUNK + (0 if dr == 1 else CHUNK - 1)
        b = bcum[dr, pr][rows]
        b_last = bcum[dr, pr][last:last + 1]
        b_mid = bcum[dr, pr][c * CHUNK + mid:c * CHUNK + mid + 1]
        qc = q_ref[0, rows, lanes].astype(F32)
        kc = k_ref[0, rows, lanes].astype(F32)
        q_dec[u] = qc * jnp.exp(b)
        k_hat[u] = (kc * jnp.exp(b_last - b)).astype(BF16)
        q_mid[u] = qc * jnp.exp(b - b_mid)
        k_mid[u] = (kc * jnp.exp(b_mid - b)).astype(BF16)
        a_last[u] = jnp.exp(b_last)

    sc = {}
    for u in units:
        for hh in range(hpp):
            sc[u, hh] = _dot_nt(jnp.where(lms[hh], q_mid[u], 0.0).astype(BF16), k_mid[u])
    intra, dst = {}, {}
    for u in units:
        dr, pr, c = u
        rows = slice(c * CHUNK, (c + 1) * CHUNK)
        acc = None
        for hh in range(hpp):
            head = pr * hpp + hh
            vh = refs[dr][2][0, rows, head * dv:(head + 1) * dv]
            pm = jnp.where(causal[dr], sc[u, hh], 0.0).astype(BF16)
            intra[u, hh] = _dot(pm, vh)
            dh = _dot_tn(vh, k_hat[u])
            acc = dh if acc is None else jnp.where(lms[hh], dh, acc)
        dst[u] = acc

    st = {ch: st_ref[ch[0], ch[1]] for ch in chains}
    for ci in range(nchunk):
        for ch in chains:
            dr, pr = ch
            c = nchunk - 1 - ci if dr == 1 else ci
            u = (dr, pr, c)
            rows = slice(c * CHUNK, (c + 1) * CHUNK)
            st_bf = st[ch].astype(BF16)
            for hh in range(hpp):
                head = pr * hpp + hh
                inter = _dot_nt(jnp.where(lms[hh], q_dec[u], 0.0).astype(BF16), st_bf)
                refs[dr][4][0, rows, head * dv:(head + 1) * dv] = (intra[u, hh] + inter).astype(BF16)
            st[ch] = st[ch] * a_last[u] + dst[u]
    for ch in chains:
        st_ref[ch[0], ch[1]] = st[ch]


def _gla_scan(gq, gk, gv, gla, nheads):
    b, s, qk = gq.shape
    v = gv.shape[2]
    nb = s // TOKEN_BLOCK
    tb = TOKEN_BLOCK
    dk, dv = qk // nheads, v // nheads
    fwd = lambda n, col=0: pl.BlockSpec((1, tb, n), lambda i, j: (i, j, col))
    bwd = lambda n, col=0: pl.BlockSpec((1, tb, n), lambda i, j: (i, _bwd_block(j, nb), col))
    return pl.pallas_call(
        functools.partial(_gla_kernel, dk=dk, dv=dv),
        out_shape=[jax.ShapeDtypeStruct((b, s, v), BF16), jax.ShapeDtypeStruct((b, s, v), BF16)],
        grid=(b, nb),
        in_specs=[fwd(qk), fwd(qk), fwd(v), fwd(qk, 0), bwd(qk), bwd(qk), bwd(v), bwd(qk, 1)],
        out_specs=[fwd(v), bwd(v)],
        scratch_shapes=[pltpu.VMEM((2, qk // LANES, dv, LANES), F32)],
        compiler_params=pltpu.CompilerParams(dimension_semantics=("parallel", "arbitrary"),
                                             vmem_limit_bytes=VMEM_LIMIT),
        name="gla_scan",
    )(gq, gk, gv, gla, gq, gk, gv, gla)


def _gdn_kernel(qf_ref, kf_ref, vf_ref, scf_ref, srf_ref, qb_ref, kb_ref, vb_ref, scb_ref, srb_ref,
                of_ref, ob_ref, st_ref, *, dk, dv, nh):
    j = pl.program_id(1)

    @pl.when(j == 0)
    def _():
        st_ref[...] = jnp.zeros_like(st_ref)

    tb = qf_ref.shape[1]
    nchunk = tb // CHUNK
    eye = _tri(CHUNK, False, False) & _tri(CHUNK, True, False)

    refs = ((qf_ref, kf_ref, vf_ref, scf_ref, srf_ref, of_ref),
            (qb_ref, kb_ref, vb_ref, scb_ref, srb_ref, ob_ref))
    chains = [(dr, hd) for dr in range(2) for hd in range(nh)]
    units = [(dr, c, hd) for dr in range(2) for c in range(nchunk) for hd in range(nh)]
    strict = {dr: _tri(CHUNK, dr == 1, True) for dr in range(2)}
    incl = {dr: _tri(CHUNK, dr == 1, False) for dr in range(2)}

    g_cols, g_rows, small = {}, {}, {}
    for dr in range(2):
        small[dr] = refs[dr][3][0]
        g_cols[dr] = _cumsum_dot(_tri_blocks(tb, dr == 1).astype(BF16), small[dr], True)
        g_rows[dr] = _cumsum_dot(_tri_blocks(tb, dr != 1).astype(BF16), refs[dr][4][0, 0], False)

    kk, qk = {}, {}
    for u in units:
        dr, c, hd = u
        rows = slice(c * CHUNK, (c + 1) * CHUNK)
        kh = refs[dr][1][0, rows, hd * dk:(hd + 1) * dk]
        kk[u] = _dot_nt(kh, kh)
        qk[u] = _dot_nt(refs[dr][0][0, rows, hd * dk:(hd + 1) * dk], kh)

    x, aqk, rhs, q_dec, k_dec, a_last = {}, {}, {}, {}, {}, {}
    for u in units:
        dr, c, hd = u
        rows = slice(c * CHUNK, (c + 1) * CHUNK)
        ia = dr * nh + hd
        ib = 2 * nh + dr * nh + hd
        last = c * CHUNK + (0 if dr == 1 else CHUNK - 1)
        g_c = g_cols[dr][rows, ia:ia + 1]
        g_r = g_rows[dr][ia:ia + 1, rows]
        be_c = small[dr][rows, ib:ib + 1]
        g_last = g_cols[dr][last:last + 1, ia:ia + 1]
        e = jnp.exp(jnp.minimum(g_c - g_r, 0.0))
        x[u] = jnp.where(strict[dr], be_c * kk[u] * e, 0.0)
        aqk[u] = jnp.where(incl[dr], qk[u] * e, 0.0).astype(BF16)
        eg = jnp.exp(g_c)
        qh = refs[dr][0][0, rows, hd * dk:(hd + 1) * dk].astype(F32)
        khf = refs[dr][1][0, rows, hd * dk:(hd + 1) * dk].astype(F32)
        vh = refs[dr][2][0, rows, hd * dv:(hd + 1) * dv].astype(F32)
        rhs[u] = jnp.concatenate([(be_c * vh).astype(BF16), ((be_c * eg) * khf).astype(BF16)], axis=1)
        q_dec[u] = qh * eg
        k_dec[u] = (khf * jnp.exp(g_last - g_c)).astype(BF16)
        a_last[u] = jnp.exp(g_last)

    t = {u: jnp.where(eye, 1.0, jnp.where(_couple(CHUNK, 1, u[0] == 1), -x[u], 0.0)) for u in units}
    s = 2
    while s < CHUNK:
        a = {u: jnp.where(_couple(CHUNK, s, u[0] == 1), x[u], 0.0).astype(BF16) for u in units}
        tb16 = {u: t[u].astype(BF16) for u in units}
        ta = {u: _dot(tb16[u], a[u]).astype(BF16) for u in units}
        t = {u: t[u] - _dot(ta[u], tb16[u]) for u in units}
        s *= 2

    uw = {u: _dot(t[u].astype(BF16), rhs[u]).astype(BF16) for u in units}

    kd_uw = {u: _dot_tn(k_dec[u], uw[u]) for u in units}
    aq_uw = {u: _dot(aqk[u], uw[u]) for u in units}
    lhs = {u: jnp.concatenate([(-kd_uw[u][:, dv:]).astype(BF16),
                               (q_dec[u] - aq_uw[u][:, dv:]).astype(BF16)], axis=0) for u in units}

    st = {ch: st_ref[ch[0], ch[1]] for ch in chains}
    for ci in range(nchunk):
        for ch in chains:
            dr, hd = ch
            c = nchunk - 1 - ci if dr == 1 else ci
            u = (dr, c, hd)
            rows = slice(c * CHUNK, (c + 1) * CHUNK)
            res = _dot(lhs[u], st[ch].astype(BF16))
            refs[dr][5][0, rows, hd * dv:(hd + 1) * dv] = (res[dk:] + aq_uw[u][:, :dv]).astype(BF16)
            st[ch] = st[ch] * a_last[u] + res[:dk] + kd_uw[u][:, :dv]
    for ch in chains:
        st_ref[ch[0], ch[1]] = st[ch]


def _gdn_scan(dq, dk_, dv_, small_c, small_r, nheads):
    b, s, qk = dq.shape
    v = dv_.shape[2]
    nb = s // TOKEN_BLOCK
    tb = TOKEN_BLOCK
    dk, dv = qk // nheads, v // nheads
    ns = small_c.shape[2]
    fwd = lambda n: pl.BlockSpec((1, tb, n), lambda i, j: (i, j, 0))
    bwd = lambda n: pl.BlockSpec((1, tb, n), lambda i, j: (i, _bwd_block(j, nb), 0))
    rfwd = pl.BlockSpec((1, 1, ns, tb), lambda i, j: (i, j, 0, 0))
    rbwd = pl.BlockSpec((1, 1, ns, tb), lambda i, j: (i, _bwd_block(j, nb), 0, 0))
    return pl.pallas_call(
        functools.partial(_gdn_kernel, dk=dk, dv=dv, nh=nheads),
        out_shape=[jax.ShapeDtypeStruct((b, s, v), BF16), jax.ShapeDtypeStruct((b, s, v), BF16)],
        grid=(b, nb),
        in_specs=[fwd(qk), fwd(qk), fwd(v), fwd(ns), rfwd, bwd(qk), bwd(qk), bwd(v), bwd(ns), rbwd],
        out_specs=[fwd(v), bwd(v)],
        scratch_shapes=[pltpu.VMEM((2, nheads, dk, dv), F32)],
        compiler_params=pltpu.CompilerParams(dimension_semantics=("parallel", "arbitrary"),
                                             vmem_limit_bytes=VMEM_LIMIT),
        name="gdn_scan",
    )(dq, dk_, dv_, small_c, small_r, dq, dk_, dv_, small_c, small_r)


def _post_kernel(c_ref, x_ref, mod_ref, af_ref, ab_ref, bf_ref, bb_ref, ga_ref, gb_ref, na_ref, nb_ref,
                 wo_ref, g2_ref, w1_ref, w2_ref, fg_ref, o_ref, *, d, nh_a, nh_b, final):
    m = mod_ref[0]
    gt1 = m[:, 2 * d:3 * d]
    sh2, sc2, gt2 = m[:, 3 * d:4 * d], m[:, 4 * d:5 * d], m[:, 5 * d:6 * d]
    slab = 2 * LANES
    acc = None
    row = 0
    for (f_ref, b_ref, g_ref, n_ref, nh) in ((af_ref, ab_ref, ga_ref, na_ref, nh_a),
                                             (bf_ref, bb_ref, gb_ref, nb_ref, nh_b)):
        width = f_ref.shape[2]
        hv = width // nh
        for c0 in range(0, width, slab):
            o = f_ref[0, :, c0:c0 + slab].astype(F32) + b_ref[0, :, c0:c0 + slab].astype(F32)
            gate = _silu(g_ref[0, :, c0:c0 + slab].astype(F32))
            parts = [(_rms(o[:, i:i + hv]) * n_ref[...] * gate[:, i:i + hv]).astype(BF16)
                     for i in range(0, slab, hv)]
            part = _dot(jnp.concatenate(parts, axis=1), wo_ref[row:row + slab, :])
            acc = part if acc is None else acc + part
            row += slab
    x_in = x_ref[0] if final else jnp.where(pl.program_id(1) == 0, c_ref[0], x_ref[0])
    x1 = x_in + gt1 * acc
    h = (_rms(x1) * g2_ref[...] * (1.0 + sc2) + sh2).astype(BF16)
    a = jnp.maximum(_dot(h, w1_ref[...]), 0.0)
    x2 = x1 + gt2 * _dot((a * a).astype(BF16), w2_ref[...])
    if final:
        x2 = _rms(x2) * fg_ref[...]
    o_ref[0] = x2


def _post(stream, mod, layer, final, oaf, oab, obf, obb, gg, dg, na, nb_, wo, g2, w1, w2, fg, nh_a, nh_b):
    head, body, body_off = stream
    b, _, d = body.shape
    off = 1 if final else 0
    nblk = body.shape[1] // TOKEN_BLOCK + body_off - off
    tb = TOKEN_BLOCK
    tok = lambda n: pl.BlockSpec((1, tb, n), lambda i, j: (i, j + off, 0))
    const2 = lambda shape: pl.BlockSpec(shape, lambda i, j: (0,) * len(shape))
    weight = lambda shape: pl.BlockSpec(shape, lambda i, j: (0,) * len(shape), pipeline_mode=pl.Buffered(1))
    bsz = b
    mod_spec = pl.BlockSpec((1, 1, 6 * d),
                            lambda i, j: (layer * MOD_ROWS + jnp.where(j + off == 0, bsz, i), 0, 0))
    va, vb = oaf.shape[2], obf.shape[2]
    return pl.pallas_call(
        functools.partial(_post_kernel, d=d, nh_a=nh_a, nh_b=nh_b, final=final),
        out_shape=jax.ShapeDtypeStruct((b, nblk * tb, d), F32),
        grid=(b, nblk),
        in_specs=[pl.BlockSpec((1, tb, d), lambda i, j: (i, 0, 0)),
                  pl.BlockSpec((1, tb, d), lambda i, j: (i, jnp.maximum(j + off - body_off, 0), 0)),
                  mod_spec, tok(va), tok(va), tok(vb), tok(vb), tok(va), tok(vb),
                  const2((1, va // nh_a)), const2((1, vb // nh_b)), weight(wo.shape),
                  const2((1, d)), weight(w1.shape), weight(w2.shape), const2((1, d))],
        out_specs=pl.BlockSpec((1, tb, d), lambda i, j: (i, j, 0)),
        compiler_params=pltpu.CompilerParams(dimension_semantics=("parallel", "parallel"),
                                             vmem_limit_bytes=VMEM_LIMIT),
        name="post",
    )(head, body, mod, oaf, oab, obf, obb, gg, dg, na.reshape(1, -1), nb_.reshape(1, -1), wo,
      g2.reshape(1, d), w1, w2, fg.reshape(1, d))


def kernel(x, c, ctx, c_ctx, w_ada, b_ada, norm1_g, norm2_g, w_in, gla_w_lr, gla_b_lr, gdn_conv_w,
           gdn_a_log, gdn_dt_bias, gla_norm_g, gdn_norm_g, w_out, w_ff1, w_ff2, final_norm_g):
    bsz, t, d = x.shape
    depth = w_ada.shape[0]
    assert ctx.shape[1] == TOKEN_BLOCK and t % TOKEN_BLOCK == 0 and TOKEN_BLOCK % GRID_W == 0
    assert bsz + 1 <= MOD_ROWS
    qk_a = gla_w_lr.shape[3]
    rank = gla_w_lr.shape[2]
    v_a = gla_norm_g.shape[1] * GLA_HEADS
    conv_dim = gdn_conv_w.shape[2]
    v_b = gdn_norm_g.shape[1] * GDN_HEADS
    qk_b = (conv_dim - v_b) // 2
    nd = 2 * GDN_HEADS
    dims = (qk_a, v_a, rank, conv_dim, qk_b, v_b, GDN_HEADS)

    o_r = 2 * qk_a + 2 * v_a
    o_c = o_r + rank
    o_g = o_c + conv_dim
    o_s = o_g + v_b
    assert w_in.shape[2] == o_s + 2 * nd
    w_main = jnp.concatenate([w_in[:, :, :o_r], w_in[:, :, o_c:o_s]], axis=2).astype(BF16)
    w_sm = jnp.concatenate([w_in[:, :, o_r:o_c], w_in[:, :, o_s:]], axis=2)
    w_small = jnp.pad(w_sm, ((0, 0), (0, 0), (0, LANES - w_sm.shape[2]))).astype(BF16)
    w_small_t = jnp.swapaxes(w_in[:, :, o_s:], 1, 2).astype(BF16)

    cc = jnp.concatenate([c, c_ctx[None, :], jnp.zeros((MOD_ROWS - bsz - 1, d), F32)], axis=0)
    mod = _modulation(cc, w_ada.astype(BF16), b_ada).reshape(depth * MOD_ROWS, 1, 6 * d)

    stream = (ctx, x, 1)
    for l in range(depth):
        last = l == depth - 1
        (gq, gk, gv, gg, gla, dq, dk_, dv_, dg, small_c, small_r) = _in_proj(
            stream, mod, l, bsz, norm1_g[l], w_main[l], w_small[l], w_small_t[l],
            gla_w_lr[l].astype(BF16), gla_b_lr[l], gdn_conv_w[l], gdn_a_log[l], gdn_dt_bias[l], dims)
        oaf, oab = _gla_scan(gq, gk, gv, gla, GLA_HEADS)
        obf, obb = _gdn_scan(dq, dk_, dv_, small_c, small_r, GDN_HEADS)
        xs = _post(stream, mod, l, last, oaf, oab, obf, obb, gg, dg, gla_norm_g[l], gdn_norm_g[l],
                   w_out[l].astype(BF16), norm2_g[l], w_ff1[l].astype(BF16), w_ff2[l].astype(BF16),
                   final_norm_g, GLA_HEADS, GDN_HEADS)
        stream = (xs, xs, 0)
    return xs
```

```python
import functools

import jax
import jax.numpy as jnp
from jax import lax
from jax.experimental import pallas as pl
from jax.experimental.pallas import tpu as pltpu

EPS = 1e-6
GRID_W = 64
GLA_HEADS = 4
GDN_HEADS = 4
GLA_TAU = 16.0
TOKEN_BLOCK = 256
CHUNK = 64
LANES = 128
MOD_ROWS = 16
VMEM_LIMIT = 56 * 1024 * 1024
SCAN_ORDER = (("b", 0), ("b", 1), ("a", 0), ("a", 1), ("b", 2), ("a", 2), ("a", 3), ("b", 3),
              ("a", 4), ("b", 4), ("b", 5), ("b", 6))

F32 = jnp.float32
BF16 = jnp.bfloat16


def _dot(a, b):
    return jnp.dot(a, b, preferred_element_type=F32)


def _dot_nt(a, b):
    return lax.dot_general(a, b, (((1,), (1,)), ((), ())), preferred_element_type=F32)


def _dot_tn(a, b):
    return lax.dot_general(a, b, (((0,), (0,)), ((), ())), preferred_element_type=F32)


def _silu(x):
    return x / (1.0 + jnp.exp(-x))


def _sigmoid(x):
    return 1.0 / (1.0 + jnp.exp(-x))


def _softplus(x):
    return jnp.maximum(x, 0.0) + jnp.log(1.0 + jnp.exp(-jnp.abs(x)))


def _log_sigmoid(x):
    return jnp.minimum(x, 0.0) - jnp.log(1.0 + jnp.exp(-jnp.abs(x)))


def _rms(x):
    return x * lax.rsqrt(jnp.mean(x * x, axis=-1, keepdims=True) + EPS)


def _cumsum_dot(tri_bf16, x, left):
    hi = x.astype(BF16)
    lo = (x - hi.astype(F32)).astype(BF16)
    if left:
        return _dot(tri_bf16, hi) + _dot(tri_bf16, lo)
    return _dot(hi, tri_bf16) + _dot(lo, tri_bf16)


def _tri(n, upper, strict):
    r = lax.broadcasted_iota(jnp.int32, (n, n), 0)
    c = lax.broadcasted_iota(jnp.int32, (n, n), 1)
    if upper:
        m = (r < c) if strict else (r <= c)
    else:
        m = (r > c) if strict else (r >= c)
    return m


def _couple(n, s, upper):
    r = lax.broadcasted_iota(jnp.int32, (n, n), 0)
    c = lax.broadcasted_iota(jnp.int32, (n, n), 1)
    same = (r & ~(2 * s - 1)) == (c & ~(2 * s - 1))
    r_hi, c_hi = (r & s) != 0, (c & s) != 0
    return same & ((~r_hi & c_hi) if upper else (r_hi & ~c_hi))


def _tri_blocks(n, upper):
    r = lax.broadcasted_iota(jnp.int32, (n, n), 0)
    c = lax.broadcasted_iota(jnp.int32, (n, n), 1)
    same = (r & ~(CHUNK - 1)) == (c & ~(CHUNK - 1))
    return same & ((r <= c) if upper else (r >= c))


def _mod_kernel(cc_ref, w_ref, b_ref, o_ref):
    s = _silu(cc_ref[...]).astype(BF16)
    o_ref[0] = _dot(s, w_ref[0]) + b_ref[0]


def _modulation(cc, w_ada_bf, b_ada):
    depth, d, d6 = w_ada_bf.shape
    nblk = d6 // d
    return pl.pallas_call(
        _mod_kernel,
        out_shape=jax.ShapeDtypeStruct((depth, MOD_ROWS, d6), F32),
        grid=(depth, nblk),
        in_specs=[
            pl.BlockSpec((MOD_ROWS, d), lambda l, n: (0, 0)),
            pl.BlockSpec((1, d, d), lambda l, n: (l, 0, n)),
            pl.BlockSpec((1, 1, d), lambda l, n: (l, 0, n)),
        ],
        out_specs=pl.BlockSpec((1, MOD_ROWS, d), lambda l, n: (l, 0, n)),
        compiler_params=pltpu.CompilerParams(dimension_semantics=("parallel", "parallel")),
        name="adaln_modulation",
    )(cc, w_ada_bf, b_ada.reshape(depth, 1, d6))


def _in_proj_kernel(c_ref, x_ref, mod_ref, g1_ref, wm_ref, ws_ref, wst_ref, wlr_ref, blr_ref, cw_ref,
                    arow_ref, dtrow_ref, acol_ref, dtcol_ref,
                    gq_ref, gk_ref, gv_ref, gg_ref, gla_ref, dq_ref, dk_ref, dv_ref, dg_ref,
                    sc_ref, sr_ref, *, d, qk_a, v_a, rank, conv_dim, qk_b, v_b, nh_b):
    j = pl.program_id(1)
    x = jnp.where(j == 0, c_ref[0], x_ref[0])
    m = mod_ref[0]
    sh1, sc1 = m[:, 0:d], m[:, d:2 * d]
    h = _rms(x) * g1_ref[...] * (1.0 + sc1) + sh1
    hb = h.astype(BF16)
    o_conv = 2 * qk_a + 2 * v_a
    o_gate = o_conv + conv_dim
    grp = 2 * LANES
    tb = x.shape[0]
    t = lax.broadcasted_iota(jnp.int32, (tb, 1), 0)
    seg_mask = jnp.where(j == 0, tb - 1, GRID_W - 1)
    first = (t & seg_mask) == 0
    last = (t & seg_mask) == seg_mask
    dkh = qk_b // nh_b

    def conv_dot(k):
        return _dot(hb, wm_ref[:, o_conv + k * grp:o_conv + (k + 1) * grp])

    def conv_group(u, k):
        up = jnp.where(first, 0.0, pltpu.roll(u, 1, 0))
        un = jnp.where(last, 0.0, pltpu.roll(u, tb - 1, 0))
        cw = cw_ref[:, k * grp:(k + 1) * grp]
        s = _silu(cw[0:1] * up + cw[1:2] * u + cw[2:3] * un)
        c0 = k * grp
        if c0 < 2 * qk_b:
            is_q = c0 < qk_b
            dst = dq_ref if is_q else dk_ref
            base = c0 if is_q else c0 - qk_b
            for hh in range(grp // dkh):
                sh = s[:, hh * dkh:(hh + 1) * dkh]
                nrm = sh * lax.rsqrt(jnp.sum(sh * sh, axis=-1, keepdims=True) + EPS)
                if is_q:
                    nrm = nrm * (float(dkh) ** -0.5)
                dst[0, :, base + hh * dkh:base + (hh + 1) * dkh] = nrm.astype(BF16)
        else:
            base = c0 - 2 * qk_b
            dv_ref[0, :, base:base + grp] = s.astype(BF16)

    def light_group(k):
        c0 = k * grp
        w0 = c0 if c0 < o_conv else o_gate + (c0 - o_conv)
        pa = _dot(hb, wm_ref[:, w0:w0 + grp])
        if c0 < qk_a:
            gq_ref[0, :, c0:c0 + grp] = (pa * (float(qk_a // GLA_HEADS) ** -0.5)).astype(BF16)
        elif c0 < 2 * qk_a:
            gk_ref[0, :, c0 - qk_a:c0 - qk_a + grp] = pa.astype(BF16)
        elif c0 < 2 * qk_a + v_a:
            gv_ref[0, :, c0 - 2 * qk_a:c0 - 2 * qk_a + grp] = pa.astype(BF16)
        elif c0 < o_conv:
            gg_ref[0, :, c0 - 2 * qk_a - v_a:c0 - 2 * qk_a - v_a + grp] = pa.astype(BF16)
        else:
            dg_ref[0, :, c0 - o_conv:c0 - o_conv + grp] = pa.astype(BF16)

    def small_group(ps, pst):
        r_a = ps[:, 0:rank].astype(BF16)
        for dr in range(2):
            lr = _dot(r_a, wlr_ref[dr]) + blr_ref[dr]
            gla_ref[0, :, dr * qk_a:(dr + 1) * qk_a] = _log_sigmoid(lr) * (1.0 / GLA_TAU)
        nd = 2 * nh_b
        a_c = ps[:, rank:rank + nd]
        b_c = ps[:, rank + nd:rank + 2 * nd]
        sc_ref[0, :, 0:nd] = -jnp.exp(arow_ref[...]) * _softplus(a_c + dtrow_ref[...])
        sc_ref[0, :, nd:2 * nd] = _sigmoid(b_c)
        sr_ref[0, 0, 0:nd, :] = -jnp.exp(acol_ref[...]) * _softplus(pst[0:nd] + dtcol_ref[...])
        sr_ref[0, 0, nd:2 * nd, :] = _sigmoid(pst[nd:2 * nd])

    n_conv = conv_dim // grp
    n_light = (o_conv + v_b) // grp
    us = {0: conv_dot(0)}
    ps = _dot(hb, ws_ref[...])
    pst = _dot_nt(wst_ref[...], hb)
    if n_conv > 1:
        us[1] = conv_dot(1)
    g_next = 0
    for k in range(n_conv):
        if k + 2 < n_conv:
            us[k + 2] = conv_dot(k + 2)
        conv_group(us.pop(k), k)
        if k == 1 or n_conv == 1:
            small_group(ps, pst)
        todo = (n_light - g_next + (n_conv - k) - 1) // (n_conv - k) if k >= n_conv // 2 else 1
        for _ in range(min(todo, n_light - g_next)):
            light_group(g_next)
            g_next += 1
    while g_next < n_light:
        light_group(g_next)
        g_next += 1


def _in_proj(stream, mod, layer, bsz, g1, wm, ws, wst, wlr, blr, cw, a_log, dt_bias, dims):
    head, body, body_off = stream
    b, _, d = body.shape
    nb = body.shape[1] // TOKEN_BLOCK + body_off
    s = nb * TOKEN_BLOCK
    qk_a, v_a, rank, conv_dim, qk_b, v_b, nh_b = dims
    nd = 2 * nh_b
    tb = TOKEN_BLOCK
    const2 = lambda shape: pl.BlockSpec(shape, lambda i, j: (0,) * len(shape))
    tok = lambda n: pl.BlockSpec((1, tb, n), lambda i, j: (i, j, 0))
    mod_spec = pl.BlockSpec((1, 1, 6 * d), lambda i, j: (layer * MOD_ROWS + jnp.where(j == 0, bsz, i), 0, 0))
    out_shape = [
        jax.ShapeDtypeStruct((b, s, qk_a), BF16), jax.ShapeDtypeStruct((b, s, qk_a), BF16),
        jax.ShapeDtypeStruct((b, s, v_a), BF16), jax.ShapeDtypeStruct((b, s, v_a), BF16),
        jax.ShapeDtypeStruct((b, s, 2 * qk_a), F32),
        jax.ShapeDtypeStruct((b, s, qk_b), BF16), jax.ShapeDtypeStruct((b, s, qk_b), BF16),
        jax.ShapeDtypeStruct((b, s, v_b), BF16), jax.ShapeDtypeStruct((b, s, v_b), BF16),
        jax.ShapeDtypeStruct((b, s, 2 * nd), F32),
        jax.ShapeDtypeStruct((b, nb, 2 * nd, tb), F32),
    ]
    out_specs = [tok(qk_a), tok(qk_a), tok(v_a), tok(v_a), tok(2 * qk_a),
                 tok(qk_b), tok(qk_b), tok(v_b), tok(v_b), tok(2 * nd),
                 pl.BlockSpec((1, 1, 2 * nd, tb), lambda i, j: (i, j, 0, 0))]
    kern = functools.partial(_in_proj_kernel, d=d, qk_a=qk_a, v_a=v_a, rank=rank, conv_dim=conv_dim,
                             qk_b=qk_b, v_b=v_b, nh_b=nh_b)
    return pl.pallas_call(
        kern, out_shape=out_shape, grid=(b, nb),
        in_specs=[
            pl.BlockSpec((1, tb, d), lambda i, j: (i, 0, 0)),
            pl.BlockSpec((1, tb, d), lambda i, j: (i, jnp.maximum(j - body_off, 0), 0)), mod_spec, const2((1, d)),
            const2(wm.shape), const2(ws.shape), const2(wst.shape), const2(wlr.shape), const2((2, 1, qk_a)),
            const2(cw.shape), const2((1, nd)), const2((1, nd)), const2((nd, 1)), const2((nd, 1)),
        ],
        out_specs=out_specs,
        compiler_params=pltpu.CompilerParams(dimension_semantics=("parallel", "parallel"),
                                             vmem_limit_bytes=VMEM_LIMIT),
        name="in_proj",
    )(head, body, mod, g1.reshape(1, d), wm, ws, wst, wlr, blr.reshape(2, 1, -1), cw,
      a_log.reshape(1, nd), dt_bias.reshape(1, nd), a_log.reshape(nd, 1), dt_bias.reshape(nd, 1))


def _bwd_block(j, nb):
    return jnp.where(j == 0, 0, nb - j)


def _gla_stages(refs, st_ref, dk, dv):
    tb = refs[0][0].shape[1]
    nchunk = tb // CHUNK
    npair = (GLA_HEADS * dk) // LANES
    hpp = LANES // dk
    lane = lax.broadcasted_iota(jnp.int32, (1, LANES), 1)
    lms = [(lane >= hh * dk) & (lane < (hh + 1) * dk) for hh in range(hpp)]
    mid = CHUNK // 2
    chains = [(dr, pr) for dr in range(2) for pr in range(npair)]
    units = [(dr, pr, c) for (dr, pr) in chains for c in range(nchunk)]
    bcum, q_dec, k_hat, q_mid, k_mid, a_last, sc, intra, dst = ({} for _ in range(9))

    def cumulative_decay():
        for (dr, pr) in chains:
            la = refs[dr][3][0, :, pr * LANES:(pr + 1) * LANES]
            bcum[dr, pr] = _cumsum_dot(_tri_blocks(tb, dr == 1).astype(BF16), la, True)

    def decayed_operands():
        for u in units:
            dr, pr, c = u
            rows = slice(c * CHUNK, (c + 1) * CHUNK)
            lanes = slice(pr * LANES, (pr + 1) * LANES)
            last = c * CHUNK + (0 if dr == 1 else CHUNK - 1)
            b = bcum[dr, pr][rows]
            b_last = bcum[dr, pr][last:last + 1]
            b_mid = bcum[dr, pr][c * CHUNK + mid:c * CHUNK + mid + 1]
            qc = refs[dr][0][0, rows, lanes].astype(F32)
            kc = refs[dr][1][0, rows, lanes].astype(F32)
            q_dec[u] = qc * jnp.exp(b)
            k_hat[u] = (kc * jnp.exp(b_last - b)).astype(BF16)
            q_mid[u] = qc * jnp.exp(b - b_mid)
            k_mid[u] = (kc * jnp.exp(b_mid - b)).astype(BF16)
            a_last[u] = jnp.exp(b_last)

    def per_head(m):
        return jnp.concatenate([jnp.where(lm, m, 0.0).astype(BF16) for lm in lms], axis=0)

    def scores():
        for u in units:
            sc[u] = _dot_nt(per_head(q_mid[u]), k_mid[u])

    def intra_and_increments():
        for u in units:
            dr, pr, c = u
            rows = slice(c * CHUNK, (c + 1) * CHUNK)
            causal = _tri(CHUNK, dr == 1, False)
            v_grp = refs[dr][2][0, rows, pr * hpp * dv:(pr + 1) * hpp * dv]
            dh = _dot_tn(v_grp, k_hat[u])
            acc = None
            for hh in range(hpp):
                pm = jnp.where(causal, sc[u][hh * CHUNK:(hh + 1) * CHUNK], 0.0).astype(BF16)
                intra[u, hh] = _dot(pm, v_grp[:, hh * dv:(hh + 1) * dv])
                part = dh[hh * dv:(hh + 1) * dv]
                acc = part if acc is None else jnp.where(lms[hh], part, acc)
            dst[u] = acc

    def recurrence():
        st = {ch: st_ref[ch[0], ch[1]] for ch in chains}
        for ci in range(nchunk):
            for ch in chains:
                dr, pr = ch
                c = nchunk - 1 - ci if dr == 1 else ci
                u = (dr, pr, c)
                rows = slice(c * CHUNK, (c + 1) * CHUNK)
                inter = _dot_nt(per_head(q_dec[u]), st[ch].astype(BF16))
                for hh in range(hpp):
                    head = pr * hpp + hh
                    refs[dr][4][0, rows, head * dv:(head + 1) * dv] = (
                        intra[u, hh] + inter[hh * CHUNK:(hh + 1) * CHUNK]).astype(BF16)
                st[ch] = st[ch] * a_last[u] + dst[u]
        for ch in chains:
            st_ref[ch[0], ch[1]] = st[ch]

    return [cumulative_decay, decayed_operands, scores, intra_and_increments, recurrence]


def _gdn_stages(refs, st_ref, dk, dv, nh):
    tb = refs[0][0].shape[1]
    nchunk = tb // CHUNK
    eye = _tri(CHUNK, False, False) & _tri(CHUNK, True, False)
    chains = [(dr, hd) for dr in range(2) for hd in range(nh)]
    units = [(dr, c, hd) for dr in range(2) for c in range(nchunk) for hd in range(nh)]
    g_cols, g_rows, small, kk, qk, x, aqk, rhs, q_dec, k_dec, a_last = ({} for _ in range(11))
    t, uw, kd_uw, aq_uw, lhs = ({} for _ in range(5))

    def cumulative_decay():
        for dr in range(2):
            small[dr] = refs[dr][3][0]
            g_cols[dr] = _cumsum_dot(_tri_blocks(tb, dr == 1).astype(BF16), small[dr], True)
            g_rows[dr] = _cumsum_dot(_tri_blocks(tb, dr != 1).astype(BF16), refs[dr][4][0, 0], False)

    def grams():
        for u in units:
            dr, c, hd = u
            rows = slice(c * CHUNK, (c + 1) * CHUNK)
            kh = refs[dr][1][0, rows, hd * dk:(hd + 1) * dk]
            qh = refs[dr][0][0, rows, hd * dk:(hd + 1) * dk]
            kq = _dot_nt(jnp.concatenate([kh, qh], axis=0), kh)
            kk[u], qk[u] = kq[:CHUNK], kq[CHUNK:]

    def triangular_operands():
        for u in units:
            dr, c, hd = u
            rows = slice(c * CHUNK, (c + 1) * CHUNK)
            ia = dr * nh + hd
            ib = 2 * nh + dr * nh + hd
            last = c * CHUNK + (0 if dr == 1 else CHUNK - 1)
            g_c = g_cols[dr][rows, ia:ia + 1]
            g_r = g_rows[dr][ia:ia + 1, rows]
            be_c = small[dr][rows, ib:ib + 1]
            g_last = g_cols[dr][last:last + 1, ia:ia + 1]
            e = jnp.exp(jnp.minimum(g_c - g_r, 0.0))
            x[u] = jnp.where(_tri(CHUNK, dr == 1, True), be_c * kk[u] * e, 0.0)
            aqk[u] = jnp.where(_tri(CHUNK, dr == 1, False), qk[u] * e, 0.0).astype(BF16)
            eg = jnp.exp(g_c)
            qh = refs[dr][0][0, rows, hd * dk:(hd + 1) * dk].astype(F32)
            khf = refs[dr][1][0, rows, hd * dk:(hd + 1) * dk].astype(F32)
            vh = refs[dr][2][0, rows, hd * dv:(hd + 1) * dv].astype(F32)
            rhs[u] = jnp.concatenate([(be_c * vh).astype(BF16), ((be_c * eg) * khf).astype(BF16)], axis=1)
            q_dec[u] = qh * eg
            k_dec[u] = khf * jnp.exp(g_last - g_c)
            a_last[u] = jnp.exp(g_last)

    def inverse():
        for u in units:
            t[u] = jnp.where(eye, 1.0, jnp.where(_couple(CHUNK, 1, u[0] == 1), -x[u], 0.0))
        s = 2
        while s < CHUNK:
            a = {u: jnp.where(_couple(CHUNK, s, u[0] == 1), x[u], 0.0).astype(BF16) for u in units}
            tb16 = {u: t[u].astype(BF16) for u in units}
            ta = {u: _dot(tb16[u], a[u]).astype(BF16) for u in units}
            for u in units:
                t[u] = t[u] - _dot(ta[u], tb16[u])
            s *= 2

    def solve():
        for u in units:
            uw[u] = _dot(t[u].astype(BF16), rhs[u]).astype(BF16)

    def fold():
        for u in units:
            both = _dot(jnp.concatenate([k_dec[u].T.astype(BF16), aqk[u]], axis=0), uw[u])
            kd_uw[u], aq_uw[u] = both[:dk], both[dk:]
        for u in units:
            lhs[u] = jnp.concatenate([(-kd_uw[u][:, dv:]).astype(BF16),
                                      (q_dec[u] - aq_uw[u][:, dv:]).astype(BF16)], axis=0)

    def recurrence():
        st = {ch: st_ref[ch[0], ch[1]] for ch in chains}
        for ci in range(nchunk):
            for ch in chains:
                dr, hd = ch
                c = nchunk - 1 - ci if dr == 1 else ci
                u = (dr, c, hd)
                rows = slice(c * CHUNK, (c + 1) * CHUNK)
                res = _dot(lhs[u], st[ch].astype(BF16))
                refs[dr][5][0, rows, hd * dv:(hd + 1) * dv] = (res[dk:] + aq_uw[u][:, :dv]).astype(BF16)
                st[ch] = st[ch] * a_last[u] + res[:dk] + kd_uw[u][:, :dv]
        for ch in chains:
            st_ref[ch[0], ch[1]] = st[ch]

    return [cumulative_decay, grams, triangular_operands, inverse, solve, fold, recurrence]


def _scan_kernel(aqf, akf, avf, alf, aqb, akb, avb, alb,
                 bqf, bkf, bvf, bcf, brf, bqb, bkb, bvb, bcb, brb,
                 oaf, oab, obf, obb, sta_ref, stb_ref, *, dk_a, dv_a, dk_b, dv_b, nh_b):
    @pl.when(pl.program_id(1) == 0)
    def _():
        sta_ref[...] = jnp.zeros_like(sta_ref)
        stb_ref[...] = jnp.zeros_like(stb_ref)

    gla = _gla_stages(((aqf, akf, avf, alf, oaf), (aqb, akb, avb, alb, oab)), sta_ref, dk_a, dv_a)
    gdn = _gdn_stages(((bqf, bkf, bvf, bcf, brf, obf), (bqb, bkb, bvb, bcb, brb, obb)),
                      stb_ref, dk_b, dv_b, nh_b)
    for name, idx in SCAN_ORDER:
        (gla if name == "a" else gdn)[idx]()


def _scan(gq, gk, gv, gla, dq, dk_, dv_, small_c, small_r):
    b, s, qk_a = gq.shape
    v_a, qk_b, v_b = gv.shape[2], dq.shape[2], dv_.shape[2]
    nb = s // TOKEN_BLOCK
    tb = TOKEN_BLOCK
    ns = small_c.shape[2]
    fwd = lambda n, col=0: pl.BlockSpec((1, tb, n), lambda i, j: (i, j, col))
    bwd = lambda n, col=0: pl.BlockSpec((1, tb, n), lambda i, j: (i, _bwd_block(j, nb), col))
    rfwd = pl.BlockSpec((1, 1, ns, tb), lambda i, j: (i, j, 0, 0))
    rbwd = pl.BlockSpec((1, 1, ns, tb), lambda i, j: (i, _bwd_block(j, nb), 0, 0))
    dk_a, dv_a = qk_a // GLA_HEADS, v_a // GLA_HEADS
    dk_b, dv_b = qk_b // GDN_HEADS, v_b // GDN_HEADS
    out = lambda v: jax.ShapeDtypeStruct((b, s, v), BF16)
    return pl.pallas_call(
        functools.partial(_scan_kernel, dk_a=dk_a, dv_a=dv_a, dk_b=dk_b, dv_b=dv_b, nh_b=GDN_HEADS),
        out_shape=[out(v_a), out(v_a), out(v_b), out(v_b)],
        grid=(b, nb),
        in_specs=[fwd(qk_a), fwd(qk_a), fwd(v_a), fwd(qk_a, 0), bwd(qk_a), bwd(qk_a), bwd(v_a), bwd(qk_a, 1),
                  fwd(qk_b), fwd(qk_b), fwd(v_b), fwd(ns), rfwd, bwd(qk_b), bwd(qk_b), bwd(v_b), bwd(ns), rbwd],
        out_specs=[fwd(v_a), bwd(v_a), fwd(v_b), bwd(v_b)],
        scratch_shapes=[pltpu.VMEM((2, qk_a // LANES, dv_a, LANES), F32),
                        pltpu.VMEM((2, GDN_HEADS, dk_b, dv_b), F32)],
        compiler_params=pltpu.CompilerParams(dimension_semantics=("parallel", "arbitrary"),
                                             vmem_limit_bytes=VMEM_LIMIT),
        name="scan",
    )(gq, gk, gv, gla, gq, gk, gv, gla, dq, dk_, dv_, small_c, small_r, dq, dk_, dv_, small_c, small_r)


def _post_kernel(c_ref, x_ref, mod_ref, af_ref, ab_ref, bf_ref, bb_ref, ga_ref, gb_ref, na_ref, nb_ref,
                 wo_ref, g2_ref, w1_ref, w2_ref, fg_ref, o_ref, *, d, nh_a, nh_b, final):
    m = mod_ref[0]
    gt1 = m[:, 2 * d:3 * d]
    sh2, sc2, gt2 = m[:, 3 * d:4 * d], m[:, 4 * d:5 * d], m[:, 5 * d:6 * d]
    slab = 2 * LANES
    acc = None
    row = 0
    for (f_ref, b_ref, g_ref, n_ref, nh) in ((af_ref, ab_ref, ga_ref, na_ref, nh_a),
                                             (bf_ref, bb_ref, gb_ref, nb_ref, nh_b)):
        width = f_ref.shape[2]
        hv = width // nh
        for c0 in range(0, width, slab):
            o = f_ref[0, :, c0:c0 + slab].astype(F32) + b_ref[0, :, c0:c0 + slab].astype(F32)
            gate = _silu(g_ref[0, :, c0:c0 + slab].astype(F32))
            parts = [(_rms(o[:, i:i + hv]) * n_ref[...] * gate[:, i:i + hv]).astype(BF16)
                     for i in range(0, slab, hv)]
            part = _dot(jnp.concatenate(parts, axis=1), wo_ref[row:row + slab, :])
            acc = part if acc is None else acc + part
            row += slab
    x_in = x_ref[0] if final else jnp.where(pl.program_id(1) == 0, c_ref[0], x_ref[0])
    x1 = x_in + gt1 * acc
    h = (_rms(x1) * g2_ref[...] * (1.0 + sc2) + sh2).astype(BF16)
    a = jnp.maximum(_dot(h, w1_ref[...]), 0.0)
    x2 = x1 + gt2 * _dot((a * a).astype(BF16), w2_ref[...])
    if final:
        x2 = _rms(x2) * fg_ref[...]
    o_ref[0] = x2


def _post(stream, mod, layer, final, oaf, oab, obf, obb, gg, dg, na, nb_, wo, g2, w1, w2, fg, nh_a, nh_b):
    head, body, body_off = stream
    b, _, d = body.shape
    off = 1 if final else 0
    nblk = body.shape[1] // TOKEN_BLOCK + body_off - off
    tb = TOKEN_BLOCK
    tok = lambda n: pl.BlockSpec((1, tb, n), lambda i, j: (i, j + off, 0))
    const2 = lambda shape: pl.BlockSpec(shape, lambda i, j: (0,) * len(shape))
    weight = lambda shape: pl.BlockSpec(shape, lambda i, j: (0,) * len(shape), pipeline_mode=pl.Buffered(1))
    bsz = b
    mod_spec = pl.BlockSpec((1, 1, 6 * d),
                            lambda i, j: (layer * MOD_ROWS + jnp.where(j + off == 0, bsz, i), 0, 0))
    va, vb = oaf.shape[2], obf.shape[2]
    return pl.pallas_call(
        functools.partial(_post_kernel, d=d, nh_a=nh_a, nh_b=nh_b, final=final),
        out_shape=jax.ShapeDtypeStruct((b, nblk * tb, d), F32),
        grid=(b, nblk),
        in_specs=[pl.BlockSpec((1, tb, d), lambda i, j: (i, 0, 0)),
                  pl.BlockSpec((1, tb, d), lambda i, j: (i, jnp.maximum(j + off - body_off, 0), 0)),
                  mod_spec, tok(va), tok(va), tok(vb), tok(vb), tok(va), tok(vb),
                  const2((1, va // nh_a)), const2((1, vb // nh_b)), weight(wo.shape),
                  const2((1, d)), weight(w1.shape), weight(w2.shape), const2((1, d))],
        out_specs=pl.BlockSpec((1, tb, d), lambda i, j: (i, j, 0)),
        compiler_params=pltpu.CompilerParams(dimension_semantics=("parallel", "parallel"),
                                             vmem_limit_bytes=VMEM_LIMIT),
        name="post",
    )(head, body, mod, oaf, oab, obf, obb, gg, dg, na.reshape(1, -1), nb_.reshape(1, -1), wo,
      g2.reshape(1, d), w1, w2, fg.reshape(1, d))


def kernel(x, c, ctx, c_ctx, w_ada, b_ada, norm1_g, norm2_g, w_in, gla_w_lr, gla_b_lr, gdn_conv_w,
           gdn_a_log, gdn_dt_bias, gla_norm_g, gdn_norm_g, w_out, w_ff1, w_ff2, final_norm_g):
    bsz, t, d = x.shape
    depth = w_ada.shape[0]
    assert ctx.shape[1] == TOKEN_BLOCK and t % TOKEN_BLOCK == 0 and TOKEN_BLOCK % GRID_W == 0
    assert bsz + 1 <= MOD_ROWS
    qk_a = gla_w_lr.shape[3]
    rank = gla_w_lr.shape[2]
    v_a = gla_norm_g.shape[1] * GLA_HEADS
    conv_dim = gdn_conv_w.shape[2]
    v_b = gdn_norm_g.shape[1] * GDN_HEADS
    qk_b = (conv_dim - v_b) // 2
    nd = 2 * GDN_HEADS
    dims = (qk_a, v_a, rank, conv_dim, qk_b, v_b, GDN_HEADS)

    o_r = 2 * qk_a + 2 * v_a
    o_c = o_r + rank
    o_g = o_c + conv_dim
    o_s = o_g + v_b
    assert w_in.shape[2] == o_s + 2 * nd
    w_main = jnp.concatenate([w_in[:, :, :o_r], w_in[:, :, o_c:o_s]], axis=2).astype(BF16)
    w_sm = jnp.concatenate([w_in[:, :, o_r:o_c], w_in[:, :, o_s:]], axis=2)
    w_small = jnp.pad(w_sm, ((0, 0), (0, 0), (0, LANES - w_sm.shape[2]))).astype(BF16)
    w_small_t = jnp.swapaxes(w_in[:, :, o_s:], 1, 2).astype(BF16)

    cc = jnp.concatenate([c, c_ctx[None, :], jnp.zeros((MOD_ROWS - bsz - 1, d), F32)], axis=0)
    mod = _modulation(cc, w_ada.astype(BF16), b_ada).reshape(depth * MOD_ROWS, 1, 6 * d)

    stream = (ctx, x, 1)
    for l in range(depth):
        last = l == depth - 1
        (gq, gk, gv, gg, gla, dq, dk_, dv_, dg, small_c, small_r) = _in_proj(
            stream, mod, l, bsz, norm1_g[l], w_main[l], w_small[l], w_small_t[l],
            gla_w_lr[l].astype(BF16), gla_b_lr[l], gdn_conv_w[l], gdn_a_log[l], gdn_dt_bias[l], dims)
        oaf, oab, obf, obb = _scan(gq, gk, gv, gla, dq, dk_, dv_, small_c, small_r)
        xs = _post(stream, mod, l, last, oaf, oab, obf, obb, gg, dg, gla_norm_g[l], gdn_norm_g[l],
                   w_out[l].astype(BF16), norm2_g[l], w_ff1[l].astype(BF16), w_ff2[l].astype(BF16),
                   final_norm_g, GLA_HEADS, GDN_HEADS)
        stream = (xs, xs, 0)
    return xs
```

```python
import functools

import jax
import jax.numpy as jnp
from jax import lax
from jax.experimental import pallas as pl
from jax.experimental.pallas import tpu as pltpu

EPS = 1e-6
GRID_W = 64
GLA_HEADS = 4
GDN_HEADS = 4
GLA_TAU = 16.0
TOKEN_BLOCK = 256
CHUNK = 64
LANES = 128
MOD_ROWS = 16
VMEM_LIMIT = 56 * 1024 * 1024
SCAN_ORDER = (("b", 0), ("b", 1), ("a", 0), ("a", 1), ("b", 2), ("b", 3), ("a", 2), ("a", 3), ("b", 4),
              ("b", 5), ("b", 6), ("b", 7), ("b", 8), ("b", 9), ("b", 10), ("b", 11), ("a", 4), ("b", 12))

F32 = jnp.float32
BF16 = jnp.bfloat16


def _dot(a, b):
    return jnp.dot(a, b, preferred_element_type=F32)


def _dot_nt(a, b):
    return lax.dot_general(a, b, (((1,), (1,)), ((), ())), preferred_element_type=F32)


def _dot_tn(a, b):
    return lax.dot_general(a, b, (((0,), (0,)), ((), ())), preferred_element_type=F32)


def _silu(x):
    return x / (1.0 + jnp.exp(-x))


def _sigmoid(x):
    return 1.0 / (1.0 + jnp.exp(-x))


def _softplus(x):
    return jnp.maximum(x, 0.0) + jnp.log(1.0 + jnp.exp(-jnp.abs(x)))


def _log_sigmoid(x):
    return jnp.minimum(x, 0.0) - jnp.log(1.0 + jnp.exp(-jnp.abs(x)))


def _rms(x):
    return x * lax.rsqrt(jnp.mean(x * x, axis=-1, keepdims=True) + EPS)


def _cumsum_dot(tri_bf16, x, left):
    hi = x.astype(BF16)
    lo = (x - hi.astype(F32)).astype(BF16)
    if left:
        return _dot(tri_bf16, hi) + _dot(tri_bf16, lo)
    return _dot(hi, tri_bf16) + _dot(lo, tri_bf16)


def _tri(n, upper, strict):
    r = lax.broadcasted_iota(jnp.int32, (n, n), 0)
    c = lax.broadcasted_iota(jnp.int32, (n, n), 1)
    if upper:
        m = (r < c) if strict else (r <= c)
    else:
        m = (r > c) if strict else (r >= c)
    return m


def _couple(n, s, upper):
    r = lax.broadcasted_iota(jnp.int32, (n, n), 0)
    c = lax.broadcasted_iota(jnp.int32, (n, n), 1)
    same = (r & ~(2 * s - 1)) == (c & ~(2 * s - 1))
    r_hi, c_hi = (r & s) != 0, (c & s) != 0
    return same & ((~r_hi & c_hi) if upper else (r_hi & ~c_hi))


def _tri_blocks(n, upper):
    r = lax.broadcasted_iota(jnp.int32, (n, n), 0)
    c = lax.broadcasted_iota(jnp.int32, (n, n), 1)
    same = (r & ~(CHUNK - 1)) == (c & ~(CHUNK - 1))
    return same & ((r <= c) if upper else (r >= c))


def _mod_kernel(cc_ref, w_ref, b_ref, o_ref):
    s = _silu(cc_ref[...]).astype(BF16)
    o_ref[0] = _dot(s, w_ref[0].astype(BF16)) + b_ref[0]


def _modulation(cc, w_ada, b_ada):
    depth, d, d6 = w_ada.shape
    nblk = d6 // d
    return pl.pallas_call(
        _mod_kernel,
        out_shape=jax.ShapeDtypeStruct((depth, MOD_ROWS, d6), F32),
        grid=(depth, nblk),
        in_specs=[
            pl.BlockSpec((MOD_ROWS, d), lambda l, n: (0, 0)),
            pl.BlockSpec((1, d, d), lambda l, n: (l, 0, n)),
            pl.BlockSpec((1, 1, d), lambda l, n: (l, 0, n)),
        ],
        out_specs=pl.BlockSpec((1, MOD_ROWS, d), lambda l, n: (l, 0, n)),
        compiler_params=pltpu.CompilerParams(dimension_semantics=("parallel", "parallel")),
        name="adaln_modulation",
    )(cc, w_ada, b_ada.reshape(depth, 1, d6))


def _in_proj_kernel(c_ref, x_ref, mod_ref, g1_ref, wm_ref, ws_ref, wst_ref, wlr_ref, blr_ref, cw_ref,
                    arow_ref, dtrow_ref, acol_ref, dtcol_ref,
                    gqkv_ref, gg_ref, gla_ref, dqkv_ref, dg_ref, sc_ref, sr_ref, *, d, qk_a, v_a, rank, conv_dim, qk_b, v_b, nh_b):
    j = pl.program_id(1)
    x = jnp.where(j == 0, c_ref[0], x_ref[0])
    m = mod_ref[0]
    sh1, sc1 = m[:, 0:d], m[:, d:2 * d]
    h = _rms(x) * g1_ref[...] * (1.0 + sc1) + sh1
    hb = h.astype(BF16)
    o_conv = 2 * qk_a + 2 * v_a
    o_gate = o_conv + conv_dim
    grp = 2 * LANES
    tb = x.shape[0]
    t = lax.broadcasted_iota(jnp.int32, (tb, 1), 0)
    seg_mask = jnp.where(j == 0, tb - 1, GRID_W - 1)
    first = (t & seg_mask) == 0
    last = (t & seg_mask) == seg_mask
    dkh = qk_b // nh_b

    def conv_dot(k):
        return _dot(hb, wm_ref[:, o_conv + k * grp:o_conv + (k + 1) * grp])

    def conv_group(u, k):
        up = jnp.where(first, 0.0, pltpu.roll(u, 1, 0))
        un = jnp.where(last, 0.0, pltpu.roll(u, tb - 1, 0))
        cw = cw_ref[:, k * grp:(k + 1) * grp]
        s = _silu(cw[0:1] * up + cw[1:2] * u + cw[2:3] * un)
        c0 = k * grp
        if c0 < 2 * qk_b:
            for hh in range(grp // dkh):
                sh = s[:, hh * dkh:(hh + 1) * dkh]
                nrm = sh * lax.rsqrt(jnp.sum(sh * sh, axis=-1, keepdims=True) + EPS)
                if c0 < qk_b:
                    nrm = nrm * (float(dkh) ** -0.5)
                dqkv_ref[0, :, c0 + hh * dkh:c0 + (hh + 1) * dkh] = nrm.astype(BF16)
        else:
            dqkv_ref[0, :, c0:c0 + grp] = s.astype(BF16)

    def light_group(k):
        c0 = k * grp
        w0 = c0 if c0 < o_conv else o_gate + (c0 - o_conv)
        pa = _dot(hb, wm_ref[:, w0:w0 + grp])
        if c0 < qk_a:
            gqkv_ref[0, :, c0:c0 + grp] = (pa * (float(qk_a // GLA_HEADS) ** -0.5)).astype(BF16)
        elif c0 < 2 * qk_a + v_a:
            gqkv_ref[0, :, c0:c0 + grp] = pa.astype(BF16)
        elif c0 < o_conv:
            gg_ref[0, :, c0 - 2 * qk_a - v_a:c0 - 2 * qk_a - v_a + grp] = pa.astype(BF16)
        else:
            dg_ref[0, :, c0 - o_conv:c0 - o_conv + grp] = pa.astype(BF16)

    def small_group(ps, pst):
        r_a = ps[:, 0:rank].astype(BF16)
        for dr in range(2):
            lr = _dot(r_a, wlr_ref[dr]) + blr_ref[dr]
            gla_ref[0, :, dr * qk_a:(dr + 1) * qk_a] = _log_sigmoid(lr) * (1.0 / GLA_TAU)
        nd = 2 * nh_b
        a_c = ps[:, rank:rank + nd]
        b_c = ps[:, rank + nd:rank + 2 * nd]
        sc_ref[0, :, 0:nd] = -jnp.exp(arow_ref[...]) * _softplus(a_c + dtrow_ref[...])
        sc_ref[0, :, nd:2 * nd] = _sigmoid(b_c)
        sr_ref[0, 0, 0:nd, :] = -jnp.exp(acol_ref[...]) * _softplus(pst[0:nd] + dtcol_ref[...])
        sr_ref[0, 0, nd:2 * nd, :] = _sigmoid(pst[nd:2 * nd])

    n_conv = conv_dim // grp
    n_light = (o_conv + v_b) // grp
    us = {0: conv_dot(0)}
    ps = _dot(hb, ws_ref[...])
    pst = _dot_nt(wst_ref[...], hb)
    if n_conv > 1:
        us[1] = conv_dot(1)
    g_next = 0
    for k in range(n_conv):
        if k + 2 < n_conv:
            us[k + 2] = conv_dot(k + 2)
        conv_group(us.pop(k), k)
        if k == 1 or n_conv == 1:
            small_group(ps, pst)
        todo = (n_light - g_next + (n_conv - k) - 1) // (n_conv - k) if k >= n_conv // 2 else 1
        for _ in range(min(todo, n_light - g_next)):
            light_group(g_next)
            g_next += 1
    while g_next < n_light:
        light_group(g_next)
        g_next += 1


def _in_proj(stream, mod, layer, bsz, params, dims):
    g1, wm, ws, wst, wlr, blr, cw, a_row, dt_row, a_col, dt_col = params
    head, body, body_off = stream
    b, _, d = body.shape
    nb = body.shape[1] // TOKEN_BLOCK + body_off
    s = nb * TOKEN_BLOCK
    qk_a, v_a, rank, conv_dim, qk_b, v_b, nh_b = dims
    nd = 2 * nh_b
    tb = TOKEN_BLOCK
    lay = lambda a: pl.BlockSpec((None,) + a.shape[1:], lambda i, j: (layer,) + (0,) * (a.ndim - 1))
    tok = lambda n: pl.BlockSpec((1, tb, n), lambda i, j: (i, j, 0))
    mod_spec = pl.BlockSpec((1, 1, 6 * d), lambda i, j: (layer * MOD_ROWS + jnp.where(j == 0, bsz, i), 0, 0))
    out_shape = [
        jax.ShapeDtypeStruct((b, s, 2 * qk_a + v_a), BF16), jax.ShapeDtypeStruct((b, s, v_a), BF16),
        jax.ShapeDtypeStruct((b, s, 2 * qk_a), F32),
        jax.ShapeDtypeStruct((b, s, conv_dim), BF16), jax.ShapeDtypeStruct((b, s, v_b), BF16),
        jax.ShapeDtypeStruct((b, s, 2 * nd), F32),
        jax.ShapeDtypeStruct((b, nb, 2 * nd, tb), F32),
    ]
    out_specs = [tok(2 * qk_a + v_a), tok(v_a), tok(2 * qk_a), tok(conv_dim), tok(v_b), tok(2 * nd),
                 pl.BlockSpec((1, 1, 2 * nd, tb), lambda i, j: (i, j, 0, 0))]
    kern = functools.partial(_in_proj_kernel, d=d, qk_a=qk_a, v_a=v_a, rank=rank, conv_dim=conv_dim,
                             qk_b=qk_b, v_b=v_b, nh_b=nh_b)
    return pl.pallas_call(
        kern, out_shape=out_shape, grid=(b, nb),
        in_specs=[
            pl.BlockSpec((1, tb, d), lambda i, j: (i, 0, 0)),
            pl.BlockSpec((1, tb, d), lambda i, j: (i, jnp.maximum(j - body_off, 0), 0)), mod_spec, lay(g1),
            lay(wm), lay(ws), lay(wst), lay(wlr), lay(blr), lay(cw), lay(a_row), lay(dt_row), lay(a_col), lay(dt_col),
        ],
        out_specs=out_specs,
        compiler_params=pltpu.CompilerParams(dimension_semantics=("parallel", "parallel"),
                                             vmem_limit_bytes=VMEM_LIMIT),
        name="in_proj",
    )(head, body, mod, g1, wm, ws, wst, wlr, blr, cw, a_row, dt_row, a_col, dt_col)


def _bwd_block(j, nb):
    return jnp.where(j == 0, 0, nb - j)


class _Cols:
    def __init__(self, ref, off):
        self.ref, self.off = ref, off

    def __getitem__(self, idx):
        z, rows, cols = idx
        return self.ref[z, rows, self.off + cols.start:self.off + cols.stop]


def _gla_stages(refs, st_ref, dk, dv, dirs):
    tb = refs[0][3].shape[1]
    nchunk = tb // CHUNK
    npair = (GLA_HEADS * dk) // LANES
    hpp = LANES // dk
    lane = lax.broadcasted_iota(jnp.int32, (1, LANES), 1)
    lms = [(lane >= hh * dk) & (lane < (hh + 1) * dk) for hh in range(hpp)]
    mid = CHUNK // 2
    chains = [(dr, pr) for dr in dirs for pr in range(npair)]
    units = [(dr, pr, c) for (dr, pr) in chains for c in range(nchunk)]
    bcum, q_dec, k_hat, q_mid, k_mid, a_last, sc, intra, dst = ({} for _ in range(9))

    def cumulative_decay():
        for dr in dirs:
            both = _cumsum_dot(_tri_blocks(tb, dr == 1).astype(BF16), refs[dr][3][0], True)
            for pr in range(npair):
                bcum[dr, pr] = both[:, pr * LANES:(pr + 1) * LANES]

    def decayed_operands():
        for u in units:
            dr, pr, c = u
            rows = slice(c * CHUNK, (c + 1) * CHUNK)
            lanes = slice(pr * LANES, (pr + 1) * LANES)
            last = c * CHUNK + (0 if dr == 1 else CHUNK - 1)
            b = bcum[dr, pr][rows]
            b_last = bcum[dr, pr][last:last + 1]
            b_mid = bcum[dr, pr][c * CHUNK + mid:c * CHUNK + mid + 1]
            qc = refs[dr][0][0, rows, lanes].astype(F32)
            kc = refs[dr][1][0, rows, lanes].astype(F32)
            q_dec[u] = qc * jnp.exp(b)
            k_hat[u] = (kc * jnp.exp(b_last - b)).astype(BF16)
            q_mid[u] = qc * jnp.exp(b - b_mid)
            k_mid[u] = (kc * jnp.exp(b_mid - b)).astype(BF16)
            a_last[u] = jnp.exp(b_last)

    def per_head(m):
        return jnp.concatenate([jnp.where(lm, m, 0.0).astype(BF16) for lm in lms], axis=0)

    def scores():
        for u in units:
            sc[u] = _dot_nt(per_head(q_mid[u]), k_mid[u])

    def intra_and_increments():
        for u in units:
            dr, pr, c = u
            rows = slice(c * CHUNK, (c + 1) * CHUNK)
            causal = _tri(CHUNK, dr == 1, False)
            v_grp = refs[dr][2][0, rows, pr * hpp * dv:(pr + 1) * hpp * dv]
            dh = _dot_tn(v_grp, k_hat[u])
            acc = None
            for hh in range(hpp):
                pm = jnp.where(causal, sc[u][hh * CHUNK:(hh + 1) * CHUNK], 0.0).astype(BF16)
                intra[u, hh] = _dot(pm, v_grp[:, hh * dv:(hh + 1) * dv])
                part = dh[hh * dv:(hh + 1) * dv]
                acc = part if acc is None else jnp.where(lms[hh], part, acc)
            dst[u] = acc

    def recurrence():
        st = {ch: st_ref[ch[0], ch[1]] for ch in chains}
        for ci in range(nchunk):
            for ch in chains:
                dr, pr = ch
                c = nchunk - 1 - ci if dr == 1 else ci
                u = (dr, pr, c)
                rows = slice(c * CHUNK, (c + 1) * CHUNK)
                inter = _dot_nt(per_head(q_dec[u]), st[ch].astype(BF16))
                for hh in range(hpp):
                    head = pr * hpp + hh
                    refs[dr][4][0, rows, head * dv:(head + 1) * dv] = (
                        intra[u, hh] + inter[hh * CHUNK:(hh + 1) * CHUNK]).astype(BF16)
                st[ch] = st[ch] * a_last[u] + dst[u]
        for ch in chains:
            st_ref[ch[0], ch[1]] = st[ch]

    return [cumulative_decay, decayed_operands, scores, intra_and_increments, recurrence]


def _gdn_stages(refs, st_ref, dk, dv, nh, dirs):
    tb = refs[0][3].shape[1]
    nchunk = tb // CHUNK
    eye = _tri(CHUNK, False, False) & _tri(CHUNK, True, False)
    chains = [(dr, hd) for dr in dirs for hd in range(nh)]
    units = [(dr, c, hd) for dr in dirs for c in range(nchunk) for hd in range(nh)]
    g_cols, g_rows, small, kk, qk, x, aqk, rhs, q_dec, k_dec, a_last = ({} for _ in range(11))
    t, uw, kd_uw, aq_uw, lhs, decay = ({} for _ in range(6))

    def cumulative_decay():
        for dr in dirs:
            small[dr] = refs[dr][3][0]
            g_cols[dr] = _cumsum_dot(_tri_blocks(tb, dr == 1).astype(BF16), small[dr], True)
            g_rows[dr] = _cumsum_dot(_tri_blocks(tb, dr != 1).astype(BF16), refs[dr][4][0, 0], False)

    def grams():
        for u in units:
            dr, c, hd = u
            rows = slice(c * CHUNK, (c + 1) * CHUNK)
            kh = refs[dr][1][0, rows, hd * dk:(hd + 1) * dk]
            qh = refs[dr][0][0, rows, hd * dk:(hd + 1) * dk]
            kq = _dot_nt(jnp.concatenate([kh, qh], axis=0), kh)
            kk[u], qk[u] = kq[:CHUNK], kq[CHUNK:]

    def solve_operands():
        for u in units:
            dr, c, hd = u
            rows = slice(c * CHUNK, (c + 1) * CHUNK)
            ia = dr * nh + hd
            ib = 2 * nh + dr * nh + hd
            g_c = g_cols[dr][rows, ia:ia + 1]
            g_r = g_rows[dr][ia:ia + 1, rows]
            be_c = small[dr][rows, ib:ib + 1]
            decay[u] = jnp.exp(jnp.minimum(g_c - g_r, 0.0))
            x[u] = jnp.where(_tri(CHUNK, dr == 1, True), be_c * kk[u] * decay[u], 0.0)

    def other_operands():
        for u in units:
            dr, c, hd = u
            rows = slice(c * CHUNK, (c + 1) * CHUNK)
            ia = dr * nh + hd
            ib = 2 * nh + dr * nh + hd
            last = c * CHUNK + (0 if dr == 1 else CHUNK - 1)
            g_c = g_cols[dr][rows, ia:ia + 1]
            be_c = small[dr][rows, ib:ib + 1]
            g_last = g_cols[dr][last:last + 1, ia:ia + 1]
            aqk[u] = jnp.where(_tri(CHUNK, dr == 1, False), qk[u] * decay[u], 0.0).astype(BF16)
            eg = jnp.exp(g_c)
            qh = refs[dr][0][0, rows, hd * dk:(hd + 1) * dk].astype(F32)
            khf = refs[dr][1][0, rows, hd * dk:(hd + 1) * dk].astype(F32)
            vh = refs[dr][2][0, rows, hd * dv:(hd + 1) * dv].astype(F32)
            rhs[u] = jnp.concatenate([(be_c * vh).astype(BF16), ((be_c * eg) * khf).astype(BF16)], axis=1)
            q_dec[u] = qh * eg
            k_dec[u] = khf * jnp.exp(g_last - g_c)
            a_last[u] = jnp.exp(g_last)

    def inverse_start():
        for u in units:
            t[u] = jnp.where(eye, 1.0, jnp.where(_couple(CHUNK, 1, u[0] == 1), -x[u], 0.0))

    def inverse_level(s):
        a = {u: jnp.where(_couple(CHUNK, s, u[0] == 1), x[u], 0.0).astype(BF16) for u in units}
        tb16 = {u: t[u].astype(BF16) for u in units}
        ta = {u: _dot(tb16[u], a[u]).astype(BF16) for u in units}
        for u in units:
            t[u] = t[u] - _dot(ta[u], tb16[u])

    levels = []
    s = 2
    while s < CHUNK:
        levels.append(functools.partial(inverse_level, s))
        s *= 2

    def solve():
        for u in units:
            uw[u] = _dot(t[u].astype(BF16), rhs[u]).astype(BF16)

    def fold():
        for u in units:
            both = _dot(jnp.concatenate([k_dec[u].T.astype(BF16), aqk[u]], axis=0), uw[u])
            kd_uw[u], aq_uw[u] = both[:dk], both[dk:]
        for u in units:
            lhs[u] = jnp.concatenate([(-kd_uw[u][:, dv:]).astype(BF16),
                                      (q_dec[u] - aq_uw[u][:, dv:]).astype(BF16)], axis=0)

    def recurrence():
        st = {ch: st_ref[ch[0], ch[1]] for ch in chains}
        for ci in range(nchunk):
            for ch in chains:
                dr, hd = ch
                c = nchunk - 1 - ci if dr == 1 else ci
                u = (dr, c, hd)
                rows = slice(c * CHUNK, (c + 1) * CHUNK)
                res = _dot(lhs[u], st[ch].astype(BF16))
                refs[dr][5][0, rows, hd * dv:(hd + 1) * dv] = (res[dk:] + aq_uw[u][:, :dv]).astype(BF16)
                st[ch] = st[ch] * a_last[u] + res[:dk] + kd_uw[u][:, :dv]
        for ch in chains:
            st_ref[ch[0], ch[1]] = st[ch]

    return [cumulative_decay, grams, solve_operands, other_operands, inverse_start, *levels,
            solve, fold, recurrence]


def _scan_kernel(af, alf, ab, alb, bf, bcf, brf, bb, bcb, brb,
                 oaf, oab, obf, obb, sta_ref, stb_ref, *, dk_a, dv_a, dk_b, dv_b, nh_b):
    @pl.when(pl.program_id(1) == 0)
    def _():
        sta_ref[...] = jnp.zeros_like(sta_ref)
        stb_ref[...] = jnp.zeros_like(stb_ref)

    qk_a, qk_b = GLA_HEADS * dk_a, nh_b * dk_b
    qkv = lambda ref, qk: (_Cols(ref, 0), _Cols(ref, qk), _Cols(ref, 2 * qk))
    refs_a = ((*qkv(af, qk_a), alf, oaf), (*qkv(ab, qk_a), alb, oab))
    refs_b = ((*qkv(bf, qk_b), bcf, brf, obf), (*qkv(bb, qk_b), bcb, brb, obb))
    stages = {"a": _gla_stages(refs_a, sta_ref, dk_a, dv_a, (0, 1)),
              "b": _gdn_stages(refs_b, stb_ref, dk_b, dv_b, nh_b, (0, 1))}
    for name, idx in SCAN_ORDER:
        stages[name][idx]()


def _scan(gqkv, gla, dqkv, small_c, small_r, dims):
    b, s, _ = gqkv.shape
    qk_a, v_a, _, _, qk_b, v_b, _ = dims
    nb = s // TOKEN_BLOCK
    tb = TOKEN_BLOCK
    ns = small_c.shape[2]
    fwd = lambda n, col=0: pl.BlockSpec((1, tb, n), lambda i, j: (i, j, col))
    bwd = lambda n, col=0: pl.BlockSpec((1, tb, n), lambda i, j: (i, _bwd_block(j, nb), col))
    rfwd = pl.BlockSpec((1, 1, ns, tb), lambda i, j: (i, j, 0, 0))
    rbwd = pl.BlockSpec((1, 1, ns, tb), lambda i, j: (i, _bwd_block(j, nb), 0, 0))
    dk_a, dv_a = qk_a // GLA_HEADS, v_a // GLA_HEADS
    dk_b, dv_b = qk_b // GDN_HEADS, v_b // GDN_HEADS
    out = lambda v: jax.ShapeDtypeStruct((b, s, v), BF16)
    return pl.pallas_call(
        functools.partial(_scan_kernel, dk_a=dk_a, dv_a=dv_a, dk_b=dk_b, dv_b=dv_b, nh_b=GDN_HEADS),
        out_shape=[out(v_a), out(v_a), out(v_b), out(v_b)],
        grid=(b, nb),
        in_specs=[fwd(2 * qk_a + v_a), fwd(qk_a, 0), bwd(2 * qk_a + v_a), bwd(qk_a, 1),
                  fwd(2 * qk_b + v_b), fwd(ns), rfwd, bwd(2 * qk_b + v_b), bwd(ns), rbwd],
        out_specs=[fwd(v_a), bwd(v_a), fwd(v_b), bwd(v_b)],
        scratch_shapes=[pltpu.VMEM((2, qk_a // LANES, dv_a, LANES), F32),
                        pltpu.VMEM((2, GDN_HEADS, dk_b, dv_b), F32)],
        compiler_params=pltpu.CompilerParams(dimension_semantics=("parallel", "arbitrary"),
                                             vmem_limit_bytes=VMEM_LIMIT),
        name="scan",
    )(gqkv, gla, gqkv, gla, dqkv, small_c, small_r, dqkv, small_c, small_r)


def _post_kernel(c_ref, x_ref, mod_ref, af_ref, ab_ref, bf_ref, bb_ref, ga_ref, gb_ref, na_ref, nb_ref,
                 wo_ref, g2_ref, w1_ref, w2_ref, fg_ref, o_ref, *, d, nh_a, nh_b, final):
    m = mod_ref[0]
    gt1 = m[:, 2 * d:3 * d]
    sh2, sc2, gt2 = m[:, 3 * d:4 * d], m[:, 4 * d:5 * d], m[:, 5 * d:6 * d]
    slab = 2 * LANES
    acc = None
    row = 0
    for (f_ref, b_ref, g_ref, n_ref, nh) in ((af_ref, ab_ref, ga_ref, na_ref, nh_a),
                                             (bf_ref, bb_ref, gb_ref, nb_ref, nh_b)):
        width = f_ref.shape[2]
        hv = width // nh
        for c0 in range(0, width, slab):
            o = f_ref[0, :, c0:c0 + slab].astype(F32) + b_ref[0, :, c0:c0 + slab].astype(F32)
            gate = _silu(g_ref[0, :, c0:c0 + slab].astype(F32))
            parts = [(_rms(o[:, i:i + hv]) * n_ref[...] * gate[:, i:i + hv]).astype(BF16)
                     for i in range(0, slab, hv)]
            part = _dot(jnp.concatenate(parts, axis=1), wo_ref[row:row + slab, :])
            acc = part if acc is None else acc + part
            row += slab
    x_in = x_ref[0] if final else jnp.where(pl.program_id(1) == 0, c_ref[0], x_ref[0])
    x1 = x_in + gt1 * acc
    h = (_rms(x1) * g2_ref[...] * (1.0 + sc2) + sh2).astype(BF16)
    a = jnp.maximum(_dot(h, w1_ref[...]), 0.0)
    x2 = x1 + gt2 * _dot((a * a).astype(BF16), w2_ref[...])
    if final:
        x2 = _rms(x2) * fg_ref[...]
    o_ref[0] = x2


def _post(stream, mod, layer, final, oaf, oab, obf, obb, gg, dg, params, fg, nh_a, nh_b):
    na, nb_, wo, g2, w1, w2 = params
    head, body, body_off = stream
    b, _, d = body.shape
    off = 1 if final else 0
    nblk = body.shape[1] // TOKEN_BLOCK + body_off - off
    tb = TOKEN_BLOCK
    tok = lambda n: pl.BlockSpec((1, tb, n), lambda i, j: (i, j + off, 0))
    lay = lambda a: pl.BlockSpec((None,) + a.shape[1:], lambda i, j: (layer,) + (0,) * (a.ndim - 1))
    weight = lambda a: pl.BlockSpec((None,) + a.shape[1:], lambda i, j: (layer,) + (0,) * (a.ndim - 1),
                                    pipeline_mode=pl.Buffered(1))
    bsz = b
    mod_spec = pl.BlockSpec((1, 1, 6 * d),
                            lambda i, j: (layer * MOD_ROWS + jnp.where(j + off == 0, bsz, i), 0, 0))
    va, vb = oaf.shape[2], obf.shape[2]
    return pl.pallas_call(
        functools.partial(_post_kernel, d=d, nh_a=nh_a, nh_b=nh_b, final=final),
        out_shape=jax.ShapeDtypeStruct((b, nblk * tb, d), F32),
        grid=(b, nblk),
        in_specs=[pl.BlockSpec((1, tb, d), lambda i, j: (i, 0, 0)),
                  pl.BlockSpec((1, tb, d), lambda i, j: (i, jnp.maximum(j + off - body_off, 0), 0)),
                  mod_spec, tok(va), tok(va), tok(vb), tok(vb), tok(va), tok(vb),
                  lay(na), lay(nb_), weight(wo), lay(g2), weight(w1), weight(w2),
                  pl.BlockSpec((1, d), lambda i, j: (0, 0))],
        out_specs=pl.BlockSpec((1, tb, d), lambda i, j: (i, j, 0)),
        compiler_params=pltpu.CompilerParams(dimension_semantics=("parallel", "parallel"),
                                             vmem_limit_bytes=VMEM_LIMIT),
        name="post",
    )(head, body, mod, oaf, oab, obf, obb, gg, dg, na, nb_, wo, g2, w1, w2, fg.reshape(1, d))


def kernel(x, c, ctx, c_ctx, w_ada, b_ada, norm1_g, norm2_g, w_in, gla_w_lr, gla_b_lr, gdn_conv_w,
           gdn_a_log, gdn_dt_bias, gla_norm_g, gdn_norm_g, w_out, w_ff1, w_ff2, final_norm_g):
    bsz, t, d = x.shape
    depth = w_ada.shape[0]
    assert ctx.shape[1] == TOKEN_BLOCK and t % TOKEN_BLOCK == 0 and TOKEN_BLOCK % GRID_W == 0
    assert bsz + 1 <= MOD_ROWS
    qk_a = gla_w_lr.shape[3]
    rank = gla_w_lr.shape[2]
    v_a = gla_norm_g.shape[1] * GLA_HEADS
    conv_dim = gdn_conv_w.shape[2]
    v_b = gdn_norm_g.shape[1] * GDN_HEADS
    qk_b = (conv_dim - v_b) // 2
    nd = 2 * GDN_HEADS
    dims = (qk_a, v_a, rank, conv_dim, qk_b, v_b, GDN_HEADS)

    o_r = 2 * qk_a + 2 * v_a
    o_c = o_r + rank
    o_g = o_c + conv_dim
    o_s = o_g + v_b
    assert w_in.shape[2] == o_s + 2 * nd
    w_main = jnp.concatenate([w_in[:, :, :o_r], w_in[:, :, o_c:o_s]], axis=2).astype(BF16)
    w_sm = jnp.concatenate([w_in[:, :, o_r:o_c], w_in[:, :, o_s:]], axis=2)
    w_small = jnp.pad(w_sm, ((0, 0), (0, 0), (0, LANES - w_sm.shape[2]))).astype(BF16)
    w_small_t = jnp.swapaxes(w_in[:, :, o_s:], 1, 2).astype(BF16)

    cc = jnp.concatenate([c, c_ctx[None, :], jnp.zeros((MOD_ROWS - bsz - 1, d), F32)], axis=0)
    mod = _modulation(cc, w_ada, b_ada).reshape(depth * MOD_ROWS, 1, 6 * d)

    in_params = (norm1_g.reshape(depth, 1, d), w_main, w_small, w_small_t, gla_w_lr.astype(BF16),
                 gla_b_lr.reshape(depth, 2, 1, qk_a), gdn_conv_w,
                 gdn_a_log.reshape(depth, 1, nd), gdn_dt_bias.reshape(depth, 1, nd),
                 gdn_a_log.reshape(depth, nd, 1), gdn_dt_bias.reshape(depth, nd, 1))
    post_params = (gla_norm_g.reshape(depth, 1, -1), gdn_norm_g.reshape(depth, 1, -1), w_out.astype(BF16),
                   norm2_g.reshape(depth, 1, d), w_ff1.astype(BF16), w_ff2.astype(BF16))

    stream = (ctx, x, 1)
    for l in range(depth):
        last = l == depth - 1
        gqkv, gg, gla, dqkv, dg, small_c, small_r = _in_proj(stream, mod, l, bsz, in_params, dims)
        oaf, oab, obf, obb = _scan(gqkv, gla, dqkv, small_c, small_r, dims)
        xs = _post(stream, mod, l, last, oaf, oab, obf, obb, gg, dg, post_params, final_norm_g,
                   GLA_HEADS, GDN_HEADS)
        stream = (xs, xs, 0)
    return xs
```

```python
import functools

import jax
import jax.numpy as jnp
from jax import lax
from jax.experimental import pallas as pl
from jax.experimental.pallas import tpu as pltpu

EPS = 1e-6
GRID_W = 64
GLA_HEADS = 4
GDN_HEADS = 4
GLA_TAU = 16.0
TOKEN_BLOCK = 256
CHUNK = 64
LANES = 128
SUBLANES = 8
MOD_ROWS = 16
VMEM_LIMIT = 56 * 1024 * 1024
SCAN_ORDER = (("b", 0), ("b", 1), ("a", 0), ("a", 1), ("b", 2), ("b", 3), ("a", 2), ("a", 3), ("b", 4),
              ("b", 5), ("b", 6), ("b", 7), ("b", 8), ("b", 9), ("a", 4), ("b", 10), ("b", 11), ("b", 12))

F32 = jnp.float32
BF16 = jnp.bfloat16


def _dot(a, b):
    return jnp.dot(a, b, preferred_element_type=F32)


def _dot_nt(a, b):
    return lax.dot_general(a, b, (((1,), (1,)), ((), ())), preferred_element_type=F32)


def _dot_tn(a, b):
    return lax.dot_general(a, b, (((0,), (0,)), ((), ())), preferred_element_type=F32)


def _silu(x):
    return x / (1.0 + jnp.exp(-x))


def _sigmoid(x):
    return 1.0 / (1.0 + jnp.exp(-x))


def _softplus(x):
    return jnp.maximum(x, 0.0) + jnp.log(1.0 + jnp.exp(-jnp.abs(x)))


def _log_sigmoid(x):
    return jnp.minimum(x, 0.0) - jnp.log(1.0 + jnp.exp(-jnp.abs(x)))


def _rms(x):
    return x * lax.rsqrt(jnp.mean(x * x, axis=-1, keepdims=True) + EPS)


def _cumsum_dot(tri_bf16, x, left):
    hi = x.astype(BF16)
    lo = (x - hi.astype(F32)).astype(BF16)
    if left:
        return _dot(tri_bf16, hi) + _dot(tri_bf16, lo)
    return _dot(hi, tri_bf16) + _dot(lo, tri_bf16)


def _tri(n, upper, strict):
    r = lax.broadcasted_iota(jnp.int32, (n, n), 0)
    c = lax.broadcasted_iota(jnp.int32, (n, n), 1)
    if upper:
        m = (r < c) if strict else (r <= c)
    else:
        m = (r > c) if strict else (r >= c)
    return m


def _couple(n, s, upper):
    r = lax.broadcasted_iota(jnp.int32, (n, n), 0)
    c = lax.broadcasted_iota(jnp.int32, (n, n), 1)
    same = (r & ~(2 * s - 1)) == (c & ~(2 * s - 1))
    r_hi, c_hi = (r & s) != 0, (c & s) != 0
    return same & ((~r_hi & c_hi) if upper else (r_hi & ~c_hi))


def _tri_blocks(n, upper):
    r = lax.broadcasted_iota(jnp.int32, (n, n), 0)
    c = lax.broadcasted_iota(jnp.int32, (n, n), 1)
    same = (r & ~(CHUNK - 1)) == (c & ~(CHUNK - 1))
    return same & ((r <= c) if upper else (r >= c))


def _mod_kernel(cc_ref, w_ref, b_ref, o_ref):
    s = _silu(cc_ref[...]).astype(BF16)
    o_ref[0] = _dot(s, w_ref[0].astype(BF16)) + b_ref[0]


def _modulation(cc, w_ada, b_ada):
    depth, d, d6 = w_ada.shape
    nblk = d6 // d
    return pl.pallas_call(
        _mod_kernel,
        out_shape=jax.ShapeDtypeStruct((depth, MOD_ROWS, d6), F32),
        grid=(depth, nblk),
        in_specs=[
            pl.BlockSpec((MOD_ROWS, d), lambda l, n: (0, 0)),
            pl.BlockSpec((1, d, d), lambda l, n: (l, 0, n)),
            pl.BlockSpec((1, 1, d), lambda l, n: (l, 0, n)),
        ],
        out_specs=pl.BlockSpec((1, MOD_ROWS, d), lambda l, n: (l, 0, n)),
        compiler_params=pltpu.CompilerParams(dimension_semantics=("parallel", "parallel")),
        name="adaln_modulation",
    )(cc, w_ada, b_ada.reshape(depth, 1, d6))


def _in_proj_kernel(c_ref, x_ref, mod_ref, g1_ref, wm_ref, ws_ref, wst_ref, wlr_ref, blr_ref, cw_ref,
                    arow_ref, dtrow_ref, acol_ref, dtcol_ref,
                    gqkv_ref, gg_ref, gla_ref, dqkv_ref, dg_ref, sc_ref, sr_ref, *, d, qk_a, v_a, rank, conv_dim, qk_b, v_b, nh_b):
    j = pl.program_id(1)
    x = jnp.where(j == 0, c_ref[0], x_ref[0])
    m = mod_ref[0]
    sh1, sc1 = m[:, 0:d], m[:, d:2 * d]
    h = _rms(x) * (g1_ref[...] * (1.0 + sc1)) + sh1
    hb = h.astype(BF16)
    o_conv = 2 * qk_a + 2 * v_a
    o_gate = o_conv + conv_dim
    grp = 2 * LANES
    tb = x.shape[0]
    t = lax.broadcasted_iota(jnp.int32, (tb, 1), 0)
    seg_mask = jnp.where(j == 0, tb - 1, GRID_W - 1)
    first = (t & seg_mask) == 0
    last = (t & seg_mask) == seg_mask
    dkh = qk_b // nh_b

    def conv_dot(k):
        return _dot(hb, wm_ref[:, o_conv + k * grp:o_conv + (k + 1) * grp])

    def zero_rows(a, mask, at):
        pieces = []
        for s0 in range(0, tb, GRID_W):
            r0 = s0 + at
            pieces += [a[s0:r0], jnp.where(mask[r0:r0 + SUBLANES], 0.0, a[r0:r0 + SUBLANES]),
                       a[r0 + SUBLANES:s0 + GRID_W]]
        return jnp.concatenate([p for p in pieces if p.shape[0]], axis=0)

    def conv_group(u, k):
        up = zero_rows(pltpu.roll(u, 1, 0), first, 0)
        un = zero_rows(pltpu.roll(u, tb - 1, 0), last, GRID_W - SUBLANES)
        cw = cw_ref[:, k * grp:(k + 1) * grp]
        s = _silu(cw[0:1] * up + cw[1:2] * u + cw[2:3] * un)
        c0 = k * grp
        if c0 < 2 * qk_b:
            for hh in range(grp // dkh):
                sh = s[:, hh * dkh:(hh + 1) * dkh]
                inv = lax.rsqrt(jnp.sum(sh * sh, axis=-1, keepdims=True) + EPS)
                if c0 < qk_b:
                    inv = inv * (float(dkh) ** -0.5)
                dqkv_ref[0, :, c0 + hh * dkh:c0 + (hh + 1) * dkh] = (sh * inv).astype(BF16)
        else:
            dqkv_ref[0, :, c0:c0 + grp] = s.astype(BF16)

    def light_group(k):
        c0 = k * grp
        w0 = c0 if c0 < o_conv else o_gate + (c0 - o_conv)
        pa = _dot(hb, wm_ref[:, w0:w0 + grp])
        if c0 < qk_a:
            gqkv_ref[0, :, c0:c0 + grp] = (pa * (float(qk_a // GLA_HEADS) ** -0.5)).astype(BF16)
        elif c0 < 2 * qk_a + v_a:
            gqkv_ref[0, :, c0:c0 + grp] = pa.astype(BF16)
        elif c0 < o_conv:
            gg_ref[0, :, c0 - 2 * qk_a - v_a:c0 - 2 * qk_a - v_a + grp] = pa.astype(BF16)
        else:
            dg_ref[0, :, c0 - o_conv:c0 - o_conv + grp] = pa.astype(BF16)

    def small_group(ps, pst):
        r_a = ps[:, 0:rank].astype(BF16)
        for dr in range(2):
            lr = _dot(r_a, wlr_ref[dr]) + blr_ref[dr]
            gla_ref[0, :, dr * qk_a:(dr + 1) * qk_a] = _log_sigmoid(lr) * (1.0 / GLA_TAU)
        nd = 2 * nh_b
        a_c = ps[:, rank:rank + nd]
        b_c = ps[:, rank + nd:rank + 2 * nd]
        sc_ref[0, :, 0:nd] = -jnp.exp(arow_ref[...]) * _softplus(a_c + dtrow_ref[...])
        sc_ref[0, :, nd:2 * nd] = _sigmoid(b_c)
        sr_ref[0, 0, 0:nd, :] = -jnp.exp(acol_ref[...]) * _softplus(pst[0:nd] + dtcol_ref[...])
        sr_ref[0, 0, nd:2 * nd, :] = _sigmoid(pst[nd:2 * nd])

    n_conv = conv_dim // grp
    n_light = (o_conv + v_b) // grp
    us = {0: conv_dot(0)}
    ps = _dot(hb, ws_ref[...])
    pst = _dot_nt(wst_ref[...], hb)
    if n_conv > 1:
        us[1] = conv_dot(1)
    g_next = 0
    for k in range(n_conv):
        if k + 2 < n_conv:
            us[k + 2] = conv_dot(k + 2)
        conv_group(us.pop(k), k)
        if k == 1 or n_conv == 1:
            small_group(ps, pst)
        todo = (n_light - g_next + (n_conv - k) - 1) // (n_conv - k) if k >= n_conv // 2 else 1
        for _ in range(min(todo, n_light - g_next)):
            light_group(g_next)
            g_next += 1
    while g_next < n_light:
        light_group(g_next)
        g_next += 1


def _in_proj(stream, mod, layer, bsz, params, dims):
    g1, wm, ws, wst, wlr, blr, cw, a_row, dt_row, a_col, dt_col = params
    head, body, body_off = stream
    b, _, d = body.shape
    nb = body.shape[1] // TOKEN_BLOCK + body_off
    s = nb * TOKEN_BLOCK
    qk_a, v_a, rank, conv_dim, qk_b, v_b, nh_b = dims
    nd = 2 * nh_b
    tb = TOKEN_BLOCK
    lay = lambda a: pl.BlockSpec((None,) + a.shape[1:], lambda i, j: (layer,) + (0,) * (a.ndim - 1))
    tok = lambda n: pl.BlockSpec((1, tb, n), lambda i, j: (i, j, 0))
    mod_spec = pl.BlockSpec((1, 1, 6 * d), lambda i, j: (layer * MOD_ROWS + jnp.where(j == 0, bsz, i), 0, 0))
    out_shape = [
        jax.ShapeDtypeStruct((b, s, 2 * qk_a + v_a), BF16), jax.ShapeDtypeStruct((b, s, v_a), BF16),
        jax.ShapeDtypeStruct((b, s, 2 * qk_a), F32),
        jax.ShapeDtypeStruct((b, s, conv_dim), BF16), jax.ShapeDtypeStruct((b, s, v_b), BF16),
        jax.ShapeDtypeStruct((b, s, 2 * nd), F32),
        jax.ShapeDtypeStruct((b, nb, 2 * nd, tb), F32),
    ]
    out_specs = [tok(2 * qk_a + v_a), tok(v_a), tok(2 * qk_a), tok(conv_dim), tok(v_b), tok(2 * nd),
                 pl.BlockSpec((1, 1, 2 * nd, tb), lambda i, j: (i, j, 0, 0))]
    kern = functools.partial(_in_proj_kernel, d=d, qk_a=qk_a, v_a=v_a, rank=rank, conv_dim=conv_dim,
                             qk_b=qk_b, v_b=v_b, nh_b=nh_b)
    return pl.pallas_call(
        kern, out_shape=out_shape, grid=(b, nb),
        in_specs=[
            pl.BlockSpec((1, tb, d), lambda i, j: (i, 0, 0)),
            pl.BlockSpec((1, tb, d), lambda i, j: (i, jnp.maximum(j - body_off, 0), 0)), mod_spec, lay(g1),
            lay(wm), lay(ws), lay(wst), lay(wlr), lay(blr), lay(cw), lay(a_row), lay(dt_row), lay(a_col), lay(dt_col),
        ],
        out_specs=out_specs,
        compiler_params=pltpu.CompilerParams(dimension_semantics=("parallel", "parallel"),
                                             vmem_limit_bytes=VMEM_LIMIT),
        name="in_proj",
    )(head, body, mod, g1, wm, ws, wst, wlr, blr, cw, a_row, dt_row, a_col, dt_col)


def _bwd_block(j, nb):
    return jnp.where(j == 0, 0, nb - j)


class _Cols:
    def __init__(self, ref, off):
        self.ref, self.off = ref, off

    def __getitem__(self, idx):
        z, rows, cols = idx
        return self.ref[z, rows, self.off + cols.start:self.off + cols.stop]


def _gla_stages(refs, st_ref, dk, dv, dirs):
    tb = refs[0][3].shape[1]
    nchunk = tb // CHUNK
    npair = (GLA_HEADS * dk) // LANES
    hpp = LANES // dk
    lane = lax.broadcasted_iota(jnp.int32, (1, LANES), 1)
    lms = [(lane >= hh * dk) & (lane < (hh + 1) * dk) for hh in range(hpp)]
    mid = CHUNK // 2
    chains = [(dr, pr) for dr in dirs for pr in range(npair)]
    units = [(dr, pr, c) for (dr, pr) in chains for c in range(nchunk)]
    bcum, q_dec, k_hat, q_mid, k_mid, a_last, sc, intra, dst = ({} for _ in range(9))

    def cumulative_decay():
        for dr in dirs:
            both = _cumsum_dot(_tri_blocks(tb, dr == 1).astype(BF16), refs[dr][3][0], True)
            for pr in range(npair):
                bcum[dr, pr] = both[:, pr * LANES:(pr + 1) * LANES]

    def decayed_operands():
        for u in units:
            dr, pr, c = u
            rows = slice(c * CHUNK, (c + 1) * CHUNK)
            lanes = slice(pr * LANES, (pr + 1) * LANES)
            last = c * CHUNK + (0 if dr == 1 else CHUNK - 1)
            b = bcum[dr, pr][rows]
            b_last = bcum[dr, pr][last:last + 1]
            b_mid = bcum[dr, pr][c * CHUNK + mid:c * CHUNK + mid + 1]
            qc = refs[dr][0][0, rows, lanes].astype(F32)
            kc = refs[dr][1][0, rows, lanes].astype(F32)
            q_dec[u] = qc * jnp.exp(b)
            k_hat[u] = (kc * jnp.exp(b_last - b)).astype(BF16)
            q_mid[u] = qc * jnp.exp(b - b_mid)
            k_mid[u] = (kc * jnp.exp(b_mid - b)).astype(BF16)
            a_last[u] = jnp.exp(b_last)

    def per_head(m):
        return jnp.concatenate([jnp.where(lm, m, 0.0).astype(BF16) for lm in lms], axis=0)

    def scores():
        for u in units:
            sc[u] = _dot_nt(per_head(q_mid[u]), k_mid[u])

    def intra_and_increments():
        for u in units:
            dr, pr, c = u
            rows = slice(c * CHUNK, (c + 1) * CHUNK)
            causal = _tri(CHUNK, dr == 1, False)
            v_grp = refs[dr][2][0, rows, pr * hpp * dv:(pr + 1) * hpp * dv]
            dh = _dot_tn(v_grp, k_hat[u])
            acc = None
            for hh in range(hpp):
                pm = jnp.where(causal, sc[u][hh * CHUNK:(hh + 1) * CHUNK], 0.0).astype(BF16)
                intra[u, hh] = _dot(pm, v_grp[:, hh * dv:(hh + 1) * dv])
                part = dh[hh * dv:(hh + 1) * dv]
                acc = part if acc is None else jnp.where(lms[hh], part, acc)
            dst[u] = acc

    def recurrence():
        st = {ch: st_ref[ch[0], ch[1]] for ch in chains}
        for ci in range(nchunk):
            for ch in chains:
                dr, pr = ch
                c = nchunk - 1 - ci if dr == 1 else ci
                u = (dr, pr, c)
                rows = slice(c * CHUNK, (c + 1) * CHUNK)
                inter = _dot_nt(per_head(q_dec[u]), st[ch].astype(BF16))
                for hh in range(hpp):
                    head = pr * hpp + hh
                    refs[dr][4][0, rows, head * dv:(head + 1) * dv] = (
                        intra[u, hh] + inter[hh * CHUNK:(hh + 1) * CHUNK]).astype(BF16)
                st[ch] = st[ch] * a_last[u] + dst[u]
        for ch in chains:
            st_ref[ch[0], ch[1]] = st[ch]

    return [cumulative_decay, decayed_operands, scores, intra_and_increments, recurrence]


def _gdn_stages(refs, st_ref, dk, dv, nh, dirs):
    tb = refs[0][3].shape[1]
    nchunk = tb // CHUNK
    eye = _tri(CHUNK, False, False) & _tri(CHUNK, True, False)
    chains = [(dr, hd) for dr in dirs for hd in range(nh)]
    units = [(dr, c, hd) for dr in dirs for c in range(nchunk) for hd in range(nh)]
    g_cols, g_rows, small, kk, qk, x, aqk, rhs, q_dec, k_dec, a_last = ({} for _ in range(11))
    t, uw, kd_uw, aq_uw, lhs, decay = ({} for _ in range(6))

    def cumulative_decay():
        for dr in dirs:
            small[dr] = refs[dr][3][0]
            g_cols[dr] = _cumsum_dot(_tri_blocks(tb, dr == 1).astype(BF16), small[dr], True)
            g_rows[dr] = _cumsum_dot(_tri_blocks(tb, dr != 1).astype(BF16), refs[dr][4][0, 0], False)

    def grams():
        for u in units:
            dr, c, hd = u
            rows = slice(c * CHUNK, (c + 1) * CHUNK)
            kh = refs[dr][1][0, rows, hd * dk:(hd + 1) * dk]
            qh = refs[dr][0][0, rows, hd * dk:(hd + 1) * dk]
            kq = _dot_nt(jnp.concatenate([kh, qh], axis=0), kh)
            kk[u], qk[u] = kq[:CHUNK], kq[CHUNK:]

    def solve_operands():
        for u in units:
            dr, c, hd = u
            rows = slice(c * CHUNK, (c + 1) * CHUNK)
            ia = dr * nh + hd
            ib = 2 * nh + dr * nh + hd
            g_c = g_cols[dr][rows, ia:ia + 1]
            g_r = g_rows[dr][ia:ia + 1, rows]
            be_c = small[dr][rows, ib:ib + 1]
            decay[u] = jnp.exp(jnp.minimum(g_c - g_r, 0.0))
            x[u] = jnp.where(_tri(CHUNK, dr == 1, True), be_c * kk[u] * decay[u], 0.0)

    def other_operands():
        for u in units:
            dr, c, hd = u
            rows = slice(c * CHUNK, (c + 1) * CHUNK)
            ia = dr * nh + hd
            ib = 2 * nh + dr * nh + hd
            last = c * CHUNK + (0 if dr == 1 else CHUNK - 1)
            g_c = g_cols[dr][rows, ia:ia + 1]
            be_c = small[dr][rows, ib:ib + 1]
            g_last = g_cols[dr][last:last + 1, ia:ia + 1]
            aqk[u] = jnp.where(_tri(CHUNK, dr == 1, False), qk[u] * decay[u], 0.0).astype(BF16)
            eg = jnp.exp(g_c)
            qh = refs[dr][0][0, rows, hd * dk:(hd + 1) * dk].astype(F32)
            khf = refs[dr][1][0, rows, hd * dk:(hd + 1) * dk].astype(F32)
            vh = refs[dr][2][0, rows, hd * dv:(hd + 1) * dv].astype(F32)
            rhs[u] = jnp.concatenate([(be_c * vh).astype(BF16), ((be_c * eg) * khf).astype(BF16)], axis=1)
            q_dec[u] = qh * eg
            k_dec[u] = khf * jnp.exp(g_last - g_c)
            a_last[u] = jnp.exp(g_last)

    def inverse_start():
        for u in units:
            t[u] = jnp.where(eye, 1.0, jnp.where(_couple(CHUNK, 1, u[0] == 1), -x[u], 0.0))

    def inverse_level(s):
        a = {u: jnp.where(_couple(CHUNK, s, u[0] == 1), x[u], 0.0).astype(BF16) for u in units}
        tb16 = {u: t[u].astype(BF16) for u in units}
        ta = {u: _dot(tb16[u], a[u]).astype(BF16) for u in units}
        for u in units:
            t[u] = t[u] - _dot(ta[u], tb16[u])

    levels = []
    s = 2
    while s < CHUNK:
        levels.append(functools.partial(inverse_level, s))
        s *= 2

    def solve():
        for u in units:
            uw[u] = _dot(t[u].astype(BF16), rhs[u]).astype(BF16)

    def fold():
        for u in units:
            both = _dot(jnp.concatenate([k_dec[u].T.astype(BF16), aqk[u]], axis=0), uw[u])
            kd_uw[u], aq_uw[u] = both[:dk], both[dk:]
        for u in units:
            lhs[u] = jnp.concatenate([(-kd_uw[u][:, dv:]).astype(BF16),
                                      (q_dec[u] - aq_uw[u][:, dv:]).astype(BF16)], axis=0)

    def recurrence():
        st = {ch: st_ref[ch[0], ch[1]] for ch in chains}
        for ci in range(nchunk):
            for ch in chains:
                dr, hd = ch
                c = nchunk - 1 - ci if dr == 1 else ci
                u = (dr, c, hd)
                rows = slice(c * CHUNK, (c + 1) * CHUNK)
                res = _dot(lhs[u], st[ch].astype(BF16))
                refs[dr][5][0, rows, hd * dv:(hd + 1) * dv] = (res[dk:] + aq_uw[u][:, :dv]).astype(BF16)
                st[ch] = st[ch] * a_last[u] + res[:dk] + kd_uw[u][:, :dv]
        for ch in chains:
            st_ref[ch[0], ch[1]] = st[ch]

    return [cumulative_decay, grams, solve_operands, other_operands, inverse_start, *levels,
            solve, fold, recurrence]


def _scan_kernel(af, alf, ab, alb, bf, bcf, brf, bb, bcb, brb,
                 oaf, oab, obf, obb, sta_ref, stb_ref, *, dk_a, dv_a, dk_b, dv_b, nh_b):
    @pl.when(pl.program_id(1) == 0)
    def _():
        sta_ref[...] = jnp.zeros_like(sta_ref)
        stb_ref[...] = jnp.zeros_like(stb_ref)

    qk_a, qk_b = GLA_HEADS * dk_a, nh_b * dk_b
    qkv = lambda ref, qk: (_Cols(ref, 0), _Cols(ref, qk), _Cols(ref, 2 * qk))
    refs_a = ((*qkv(af, qk_a), alf, oaf), (*qkv(ab, qk_a), alb, oab))
    refs_b = ((*qkv(bf, qk_b), bcf, brf, obf), (*qkv(bb, qk_b), bcb, brb, obb))
    stages = {"a": _gla_stages(refs_a, sta_ref, dk_a, dv_a, (0, 1)),
              "b": _gdn_stages(refs_b, stb_ref, dk_b, dv_b, nh_b, (0, 1))}
    for name, idx in SCAN_ORDER:
        stages[name][idx]()


def _scan(gqkv, gla, dqkv, small_c, small_r, dims):
    b, s, _ = gqkv.shape
    qk_a, v_a, _, _, qk_b, v_b, _ = dims
    nb = s // TOKEN_BLOCK
    tb = TOKEN_BLOCK
    ns = small_c.shape[2]
    fwd = lambda n, col=0: pl.BlockSpec((1, tb, n), lambda i, j: (i, j, col))
    bwd = lambda n, col=0: pl.BlockSpec((1, tb, n), lambda i, j: (i, _bwd_block(j, nb), col))
    rfwd = pl.BlockSpec((1, 1, ns, tb), lambda i, j: (i, j, 0, 0))
    rbwd = pl.BlockSpec((1, 1, ns, tb), lambda i, j: (i, _bwd_block(j, nb), 0, 0))
    dk_a, dv_a = qk_a // GLA_HEADS, v_a // GLA_HEADS
    dk_b, dv_b = qk_b // GDN_HEADS, v_b // GDN_HEADS
    out = lambda v: jax.ShapeDtypeStruct((b, s, v), BF16)
    return pl.pallas_call(
        functools.partial(_scan_kernel, dk_a=dk_a, dv_a=dv_a, dk_b=dk_b, dv_b=dv_b, nh_b=GDN_HEADS),
        out_shape=[out(v_a), out(v_a), out(v_b), out(v_b)],
        grid=(b, nb),
        in_specs=[fwd(2 * qk_a + v_a), fwd(qk_a, 0), bwd(2 * qk_a + v_a), bwd(qk_a, 1),
                  fwd(2 * qk_b + v_b), fwd(ns), rfwd, bwd(2 * qk_b + v_b), bwd(ns), rbwd],
        out_specs=[fwd(v_a), bwd(v_a), fwd(v_b), bwd(v_b)],
        scratch_shapes=[pltpu.VMEM((2, qk_a // LANES, dv_a, LANES), F32),
                        pltpu.VMEM((2, GDN_HEADS, dk_b, dv_b), F32)],
        compiler_params=pltpu.CompilerParams(dimension_semantics=("parallel", "arbitrary"),
                                             vmem_limit_bytes=VMEM_LIMIT),
        name="scan",
    )(gqkv, gla, gqkv, gla, dqkv, small_c, small_r, dqkv, small_c, small_r)


def _post_kernel(c_ref, x_ref, mod_ref, af_ref, ab_ref, bf_ref, bb_ref, ga_ref, gb_ref, na_ref, nb_ref,
                 wo_ref, g2_ref, w1_ref, w2_ref, fg_ref, o_ref, *, d, nh_a, nh_b, final):
    m = mod_ref[0]
    gt1 = m[:, 2 * d:3 * d]
    sh2, sc2, gt2 = m[:, 3 * d:4 * d], m[:, 4 * d:5 * d], m[:, 5 * d:6 * d]
    slab = 2 * LANES
    acc = None
    row = 0
    for (f_ref, b_ref, g_ref, n_ref, nh) in ((af_ref, ab_ref, ga_ref, na_ref, nh_a),
                                             (bf_ref, bb_ref, gb_ref, nb_ref, nh_b)):
        width = f_ref.shape[2]
        hv = width // nh
        for c0 in range(0, width, slab):
            o = f_ref[0, :, c0:c0 + slab].astype(F32) + b_ref[0, :, c0:c0 + slab].astype(F32)
            gate = _silu(g_ref[0, :, c0:c0 + slab].astype(F32))
            parts = [(_rms(o[:, i:i + hv]) * n_ref[...] * gate[:, i:i + hv]).astype(BF16)
                     for i in range(0, slab, hv)]
            part = _dot(jnp.concatenate(parts, axis=1), wo_ref[row:row + slab, :])
            acc = part if acc is None else acc + part
            row += slab
    x_in = x_ref[0] if final else jnp.where(pl.program_id(1) == 0, c_ref[0], x_ref[0])
    x1 = x_in + gt1 * acc
    h = (_rms(x1) * g2_ref[...] * (1.0 + sc2) + sh2).astype(BF16)
    a = jnp.maximum(_dot(h, w1_ref[...]), 0.0)
    x2 = x1 + gt2 * _dot((a * a).astype(BF16), w2_ref[...])
    if final:
        x2 = _rms(x2) * fg_ref[...]
    o_ref[0] = x2


def _post(stream, mod, layer, final, oaf, oab, obf, obb, gg, dg, params, fg, nh_a, nh_b):
    na, nb_, wo, g2, w1, w2 = params
    head, body, body_off = stream
    b, _, d = body.shape
    off = 1 if final else 0
    nblk = body.shape[1] // TOKEN_BLOCK + body_off - off
    tb = TOKEN_BLOCK
    tok = lambda n: pl.BlockSpec((1, tb, n), lambda i, j: (i, j + off, 0))
    lay = lambda a: pl.BlockSpec((None,) + a.shape[1:], lambda i, j: (layer,) + (0,) * (a.ndim - 1))
    weight = lambda a: pl.BlockSpec((None,) + a.shape[1:], lambda i, j: (layer,) + (0,) * (a.ndim - 1),
                                    pipeline_mode=pl.Buffered(1))
    bsz = b
    mod_spec = pl.BlockSpec((1, 1, 6 * d),
                            lambda i, j: (layer * MOD_ROWS + jnp.where(j + off == 0, bsz, i), 0, 0))
    va, vb = oaf.shape[2], obf.shape[2]
    return pl.pallas_call(
        functools.partial(_post_kernel, d=d, nh_a=nh_a, nh_b=nh_b, final=final),
        out_shape=jax.ShapeDtypeStruct((b, nblk * tb, d), F32),
        grid=(b, nblk),
        in_specs=[pl.BlockSpec((1, tb, d), lambda i, j: (i, 0, 0)),
                  pl.BlockSpec((1, tb, d), lambda i, j: (i, jnp.maximum(j + off - body_off, 0), 0)),
                  mod_spec, tok(va), tok(va), tok(vb), tok(vb), tok(va), tok(vb),
                  lay(na), lay(nb_), weight(wo), lay(g2), weight(w1), weight(w2),
                  pl.BlockSpec((1, d), lambda i, j: (0, 0))],
        out_specs=pl.BlockSpec((1, tb, d), lambda i, j: (i, j, 0)),
        compiler_params=pltpu.CompilerParams(dimension_semantics=("parallel", "parallel"),
                                             vmem_limit_bytes=VMEM_LIMIT),
        name="post",
    )(head, body, mod, oaf, oab, obf, obb, gg, dg, na, nb_, wo, g2, w1, w2, fg.reshape(1, d))


def kernel(x, c, ctx, c_ctx, w_ada, b_ada, norm1_g, norm2_g, w_in, gla_w_lr, gla_b_lr, gdn_conv_w,
           gdn_a_log, gdn_dt_bias, gla_norm_g, gdn_norm_g, w_out, w_ff1, w_ff2, final_norm_g):
    bsz, t, d = x.shape
    depth = w_ada.shape[0]
    assert ctx.shape[1] == TOKEN_BLOCK and t % TOKEN_BLOCK == 0 and TOKEN_BLOCK % GRID_W == 0
    assert bsz + 1 <= MOD_ROWS
    qk_a = gla_w_lr.shape[3]
    rank = gla_w_lr.shape[2]
    v_a = gla_norm_g.shape[1] * GLA_HEADS
    conv_dim = gdn_conv_w.shape[2]
    v_b = gdn_norm_g.shape[1] * GDN_HEADS
    qk_b = (conv_dim - v_b) // 2
    nd = 2 * GDN_HEADS
    dims = (qk_a, v_a, rank, conv_dim, qk_b, v_b, GDN_HEADS)

    o_r = 2 * qk_a + 2 * v_a
    o_c = o_r + rank
    o_g = o_c + conv_dim
    o_s = o_g + v_b
    assert w_in.shape[2] == o_s + 2 * nd
    w_main = jnp.concatenate([w_in[:, :, :o_r], w_in[:, :, o_c:o_s]], axis=2).astype(BF16)
    w_sm = jnp.concatenate([w_in[:, :, o_r:o_c], w_in[:, :, o_s:]], axis=2)
    w_small = jnp.pad(w_sm, ((0, 0), (0, 0), (0, LANES - w_sm.shape[2]))).astype(BF16)
    w_small_t = jnp.swapaxes(w_in[:, :, o_s:], 1, 2).astype(BF16)

    cc = jnp.concatenate([c, c_ctx[None, :], jnp.zeros((MOD_ROWS - bsz - 1, d), F32)], axis=0)
    mod = _modulation(cc, w_ada, b_ada).reshape(depth * MOD_ROWS, 1, 6 * d)

    in_params = (norm1_g.reshape(depth, 1, d), w_main, w_small, w_small_t, gla_w_lr.astype(BF16),
                 gla_b_lr.reshape(depth, 2, 1, qk_a), gdn_conv_w,
                 gdn_a_log.reshape(depth, 1, nd), gdn_dt_bias.reshape(depth, 1, nd),
                 gdn_a_log.reshape(depth, nd, 1), gdn_dt_bias.reshape(depth, nd, 1))
    post_params = (gla_norm_g.reshape(depth, 1, -1), gdn_norm_g.reshape(depth, 1, -1), w_out.astype(BF16),
                   norm2_g.reshape(depth, 1, d), w_ff1.astype(BF16), w_ff2.astype(BF16))

    stream = (ctx, x, 1)
    for l in range(depth):
        last = l == depth - 1
        gqkv, gg, gla, dqkv, dg, small_c, small_r = _in_proj(stream, mod, l, bsz, in_params, dims)
        oaf, oab, obf, obb = _scan(gqkv, gla, dqkv, small_c, small_r, dims)
        xs = _post(stream, mod, l, last, oaf, oab, obf, obb, gg, dg, post_params, final_norm_g,
                   GLA_HEADS, GDN_HEADS)
        stream = (xs, xs, 0)
    return xs
```

```python
import functools

import jax
import jax.numpy as jnp
from jax import lax
from jax.experimental import pallas as pl
from jax.experimental.pallas import tpu as pltpu

EPS = 1e-6
GRID_W = 64
GLA_HEADS = 4
GDN_HEADS = 4
GLA_TAU = 16.0
TOKEN_BLOCK = 256
CHUNK = 64
LANES = 128
SUBLANES = 8
POST_SUB = 2
MOD_ROWS = 16
VMEM_LIMIT = 56 * 1024 * 1024
SCAN_ORDER = (("b", 0), ("b", 1), ("a", 0), ("a", 1), ("b", 2), ("b", 3), ("a", 2), ("a", 3), ("b", 4),
              ("b", 5), ("b", 6), ("b", 7), ("b", 8), ("b", 9), ("a", 4), ("b", 10), ("b", 11), ("b", 12))

F32 = jnp.float32
BF16 = jnp.bfloat16


def _dot(a, b):
    return jnp.dot(a, b, preferred_element_type=F32)


def _dot_nt(a, b):
    return lax.dot_general(a, b, (((1,), (1,)), ((), ())), preferred_element_type=F32)


def _dot_tn(a, b):
    return lax.dot_general(a, b, (((0,), (0,)), ((), ())), preferred_element_type=F32)


def _silu(x):
    return x / (1.0 + jnp.exp(-x))


def _sigmoid(x):
    return 1.0 / (1.0 + jnp.exp(-x))


def _softplus(x):
    return jnp.maximum(x, 0.0) + jnp.log(1.0 + jnp.exp(-jnp.abs(x)))


def _log_sigmoid(x):
    return jnp.minimum(x, 0.0) - jnp.log(1.0 + jnp.exp(-jnp.abs(x)))


def _rms(x):
    return x * lax.rsqrt(jnp.mean(x * x, axis=-1, keepdims=True) + EPS)


def _cumsum_dot(tri_bf16, x, left):
    hi = x.astype(BF16)
    lo = (x - hi.astype(F32)).astype(BF16)
    if left:
        return _dot(tri_bf16, hi) + _dot(tri_bf16, lo)
    return _dot(hi, tri_bf16) + _dot(lo, tri_bf16)


def _tri(n, upper, strict):
    r = lax.broadcasted_iota(jnp.int32, (n, n), 0)
    c = lax.broadcasted_iota(jnp.int32, (n, n), 1)
    if upper:
        m = (r < c) if strict else (r <= c)
    else:
        m = (r > c) if strict else (r >= c)
    return m


def _couple(n, s, upper):
    r = lax.broadcasted_iota(jnp.int32, (n, n), 0)
    c = lax.broadcasted_iota(jnp.int32, (n, n), 1)
    same = (r & ~(2 * s - 1)) == (c & ~(2 * s - 1))
    r_hi, c_hi = (r & s) != 0, (c & s) != 0
    return same & ((~r_hi & c_hi) if upper else (r_hi & ~c_hi))


def _tri_blocks(n, upper):
    r = lax.broadcasted_iota(jnp.int32, (n, n), 0)
    c = lax.broadcasted_iota(jnp.int32, (n, n), 1)
    same = (r & ~(CHUNK - 1)) == (c & ~(CHUNK - 1))
    return same & ((r <= c) if upper else (r >= c))


def _mod_kernel(cc_ref, w_ref, b_ref, o_ref):
    s = _silu(cc_ref[...]).astype(BF16)
    o_ref[0] = _dot(s, w_ref[0].astype(BF16)) + b_ref[0]


def _modulation(cc, w_ada, b_ada):
    depth, d, d6 = w_ada.shape
    nblk = d6 // d
    return pl.pallas_call(
        _mod_kernel,
        out_shape=jax.ShapeDtypeStruct((depth, MOD_ROWS, d6), F32),
        grid=(depth, nblk),
        in_specs=[
            pl.BlockSpec((MOD_ROWS, d), lambda l, n: (0, 0)),
            pl.BlockSpec((1, d, d), lambda l, n: (l, 0, n)),
            pl.BlockSpec((1, 1, d), lambda l, n: (l, 0, n)),
        ],
        out_specs=pl.BlockSpec((1, MOD_ROWS, d), lambda l, n: (l, 0, n)),
        compiler_params=pltpu.CompilerParams(dimension_semantics=("parallel", "parallel")),
        name="adaln_modulation",
    )(cc, w_ada, b_ada.reshape(depth, 1, d6))


def _in_proj_kernel(c_ref, x_ref, mod_ref, g1_ref, wm_ref, ws_ref, wst_ref, wlr_ref, blr_ref, cw_ref,
                    arow_ref, dtrow_ref, acol_ref, dtcol_ref,
                    gqkv_ref, gg_ref, gla_ref, dqkv_ref, dg_ref, sc_ref, sr_ref, *, d, qk_a, v_a, rank, conv_dim, qk_b, v_b, nh_b):
    j = pl.program_id(1)
    x = jnp.where(j == 0, c_ref[0], x_ref[0])
    m = mod_ref[0]
    sh1, sc1 = m[:, 0:d], m[:, d:2 * d]
    h = _rms(x) * (g1_ref[...] * (1.0 + sc1)) + sh1
    hb = h.astype(BF16)
    o_conv = 2 * qk_a + 2 * v_a
    o_gate = o_conv + conv_dim
    grp = 2 * LANES
    tb = x.shape[0]
    t = lax.broadcasted_iota(jnp.int32, (tb, 1), 0)
    seg_mask = jnp.where(j == 0, tb - 1, GRID_W - 1)
    first = (t & seg_mask) == 0
    last = (t & seg_mask) == seg_mask
    dkh = qk_b // nh_b

    def conv_dot(k):
        return _dot(hb, wm_ref[:, o_conv + k * grp:o_conv + (k + 1) * grp])

    def zero_rows(a, mask, at):
        pieces = []
        for s0 in range(0, tb, GRID_W):
            r0 = s0 + at
            pieces += [a[s0:r0], jnp.where(mask[r0:r0 + SUBLANES], 0.0, a[r0:r0 + SUBLANES]),
                       a[r0 + SUBLANES:s0 + GRID_W]]
        return jnp.concatenate([p for p in pieces if p.shape[0]], axis=0)

    def conv_group(u, k):
        up = zero_rows(pltpu.roll(u, 1, 0), first, 0)
        un = zero_rows(pltpu.roll(u, tb - 1, 0), last, GRID_W - SUBLANES)
        cw = cw_ref[:, k * grp:(k + 1) * grp]
        s = _silu(cw[0:1] * up + cw[1:2] * u + cw[2:3] * un)
        c0 = k * grp
        if c0 < 2 * qk_b:
            for hh in range(grp // dkh):
                sh = s[:, hh * dkh:(hh + 1) * dkh]
                inv = lax.rsqrt(jnp.sum(sh * sh, axis=-1, keepdims=True) + EPS)
                if c0 < qk_b:
                    inv = inv * (float(dkh) ** -0.5)
                dqkv_ref[0, :, c0 + hh * dkh:c0 + (hh + 1) * dkh] = (sh * inv).astype(BF16)
        else:
            dqkv_ref[0, :, c0:c0 + grp] = s.astype(BF16)

    def light_group(k):
        c0 = k * grp
        w0 = c0 if c0 < o_conv else o_gate + (c0 - o_conv)
        pa = _dot(hb, wm_ref[:, w0:w0 + grp])
        if c0 < qk_a:
            gqkv_ref[0, :, c0:c0 + grp] = (pa * (float(qk_a // GLA_HEADS) ** -0.5)).astype(BF16)
        elif c0 < 2 * qk_a + v_a:
            gqkv_ref[0, :, c0:c0 + grp] = pa.astype(BF16)
        elif c0 < o_conv:
            gg_ref[0, :, c0 - 2 * qk_a - v_a:c0 - 2 * qk_a - v_a + grp] = pa.astype(BF16)
        else:
            dg_ref[0, :, c0 - o_conv:c0 - o_conv + grp] = pa.astype(BF16)

    def small_group(ps, pst):
        r_a = ps[:, 0:rank].astype(BF16)
        for dr in range(2):
            lr = _dot(r_a, wlr_ref[dr]) + blr_ref[dr]
            gla_ref[0, :, dr * qk_a:(dr + 1) * qk_a] = _log_sigmoid(lr) * (1.0 / GLA_TAU)
        nd = 2 * nh_b
        a_c = ps[:, rank:rank + nd]
        b_c = ps[:, rank + nd:rank + 2 * nd]
        sc_ref[0, :, 0:nd] = -jnp.exp(arow_ref[...]) * _softplus(a_c + dtrow_ref[...])
        sc_ref[0, :, nd:2 * nd] = _sigmoid(b_c)
        sr_ref[0, 0, 0:nd, :] = -jnp.exp(acol_ref[...]) * _softplus(pst[0:nd] + dtcol_ref[...])
        sr_ref[0, 0, nd:2 * nd, :] = _sigmoid(pst[nd:2 * nd])

    n_conv = conv_dim // grp
    n_light = (o_conv + v_b) // grp
    us = {0: conv_dot(0)}
    ps = _dot(hb, ws_ref[...])
    pst = _dot_nt(wst_ref[...], hb)
    if n_conv > 1:
        us[1] = conv_dot(1)
    g_next = 0
    for k in range(n_conv):
        if k + 2 < n_conv:
            us[k + 2] = conv_dot(k + 2)
        conv_group(us.pop(k), k)
        if k == 1 or n_conv == 1:
            small_group(ps, pst)
        todo = (n_light - g_next + (n_conv - k) - 1) // (n_conv - k) if k >= n_conv // 2 else 1
        for _ in range(min(todo, n_light - g_next)):
            light_group(g_next)
            g_next += 1
    while g_next < n_light:
        light_group(g_next)
        g_next += 1


def _in_proj(stream, mod, layer, bsz, params, dims):
    g1, wm, ws, wst, wlr, blr, cw, a_row, dt_row, a_col, dt_col = params
    head, body = stream
    b, _, d = body.shape
    nb = body.shape[1] // TOKEN_BLOCK + 1
    s = nb * TOKEN_BLOCK
    qk_a, v_a, rank, conv_dim, qk_b, v_b, nh_b = dims
    nd = 2 * nh_b
    tb = TOKEN_BLOCK
    lay = lambda a: pl.BlockSpec((None,) + a.shape[1:], lambda i, j: (layer,) + (0,) * (a.ndim - 1))
    tok = lambda n: pl.BlockSpec((1, tb, n), lambda i, j: (i, j, 0))
    mod_spec = pl.BlockSpec((1, 1, 6 * d), lambda i, j: (layer * MOD_ROWS + jnp.where(j == 0, bsz, i), 0, 0))
    out_shape = [
        jax.ShapeDtypeStruct((b, s, 2 * qk_a + v_a), BF16), jax.ShapeDtypeStruct((b, s, v_a), BF16),
        jax.ShapeDtypeStruct((b, s, 2 * qk_a), F32),
        jax.ShapeDtypeStruct((b, s, conv_dim), BF16), jax.ShapeDtypeStruct((b, s, v_b), BF16),
        jax.ShapeDtypeStruct((b, s, 2 * nd), F32),
        jax.ShapeDtypeStruct((b, nb, 2 * nd, tb), F32),
    ]
    out_specs = [tok(2 * qk_a + v_a), tok(v_a), tok(2 * qk_a), tok(conv_dim), tok(v_b), tok(2 * nd),
                 pl.BlockSpec((1, 1, 2 * nd, tb), lambda i, j: (i, j, 0, 0))]
    kern = functools.partial(_in_proj_kernel, d=d, qk_a=qk_a, v_a=v_a, rank=rank, conv_dim=conv_dim,
                             qk_b=qk_b, v_b=v_b, nh_b=nh_b)
    return pl.pallas_call(
        kern, out_shape=out_shape, grid=(b, nb),
        in_specs=[
            pl.BlockSpec((1, tb, d), lambda i, j: (i, 0, 0)),
            pl.BlockSpec((1, tb, d), lambda i, j: (i, jnp.maximum(j - 1, 0), 0)), mod_spec, lay(g1),
            lay(wm), lay(ws), lay(wst), lay(wlr), lay(blr), lay(cw), lay(a_row), lay(dt_row), lay(a_col), lay(dt_col),
        ],
        out_specs=out_specs,
        compiler_params=pltpu.CompilerParams(dimension_semantics=("parallel", "parallel"),
                                             vmem_limit_bytes=VMEM_LIMIT),
        name="in_proj",
    )(head, body, mod, g1, wm, ws, wst, wlr, blr, cw, a_row, dt_row, a_col, dt_col)


def _bwd_block(j, nb):
    return jnp.where(j == 0, 0, nb - j)


class _Cols:
    def __init__(self, ref, off):
        self.ref, self.off = ref, off

    def __getitem__(self, idx):
        z, rows, cols = idx
        return self.ref[z, rows, self.off + cols.start:self.off + cols.stop]


def _gla_stages(refs, st_ref, dk, dv, dirs):
    tb = refs[0][3].shape[1]
    nchunk = tb // CHUNK
    npair = (GLA_HEADS * dk) // LANES
    hpp = LANES // dk
    lane = lax.broadcasted_iota(jnp.int32, (1, LANES), 1)
    lms = [(lane >= hh * dk) & (lane < (hh + 1) * dk) for hh in range(hpp)]
    mid = CHUNK // 2
    chains = [(dr, pr) for dr in dirs for pr in range(npair)]
    units = [(dr, pr, c) for (dr, pr) in chains for c in range(nchunk)]
    bcum, q_dec, k_hat, q_mid, k_mid, a_last, sc, intra, dst = ({} for _ in range(9))

    def cumulative_decay():
        for dr in dirs:
            both = _cumsum_dot(_tri_blocks(tb, dr == 1).astype(BF16), refs[dr][3][0], True)
            for pr in range(npair):
                bcum[dr, pr] = both[:, pr * LANES:(pr + 1) * LANES]

    def decayed_operands():
        for u in units:
            dr, pr, c = u
            rows = slice(c * CHUNK, (c + 1) * CHUNK)
            lanes = slice(pr * LANES, (pr + 1) * LANES)
            last = c * CHUNK + (0 if dr == 1 else CHUNK - 1)
            b = bcum[dr, pr][rows]
            b_last = bcum[dr, pr][last:last + 1]
            b_mid = bcum[dr, pr][c * CHUNK + mid:c * CHUNK + mid + 1]
            qc = refs[dr][0][0, rows, lanes].astype(F32)
            kc = refs[dr][1][0, rows, lanes].astype(F32)
            q_dec[u] = qc * jnp.exp(b)
            k_hat[u] = (kc * jnp.exp(b_last - b)).astype(BF16)
            q_mid[u] = qc * jnp.exp(b - b_mid)
            k_mid[u] = (kc * jnp.exp(b_mid - b)).astype(BF16)
            a_last[u] = jnp.exp(b_last)

    def per_head(m):
        return jnp.concatenate([jnp.where(lm, m, 0.0).astype(BF16) for lm in lms], axis=0)

    def scores():
        for u in units:
            sc[u] = _dot_nt(per_head(q_mid[u]), k_mid[u])

    def intra_and_increments():
        for u in units:
            dr, pr, c = u
            rows = slice(c * CHUNK, (c + 1) * CHUNK)
            causal = _tri(CHUNK, dr == 1, False)
            v_grp = refs[dr][2][0, rows, pr * hpp * dv:(pr + 1) * hpp * dv]
            dh = _dot_tn(v_grp, k_hat[u])
            acc = None
            for hh in range(hpp):
                pm = jnp.where(causal, sc[u][hh * CHUNK:(hh + 1) * CHUNK], 0.0).astype(BF16)
                intra[u, hh] = _dot(pm, v_grp[:, hh * dv:(hh + 1) * dv])
                part = dh[hh * dv:(hh + 1) * dv]
                acc = part if acc is None else jnp.where(lms[hh], part, acc)
            dst[u] = acc

    def recurrence():
        st = {ch: st_ref[ch[0], ch[1]] for ch in chains}
        for ci in range(nchunk):
            for ch in chains:
                dr, pr = ch
                c = nchunk - 1 - ci if dr == 1 else ci
                u = (dr, pr, c)
                rows = slice(c * CHUNK, (c + 1) * CHUNK)
                inter = _dot_nt(per_head(q_dec[u]), st[ch].astype(BF16))
                for hh in range(hpp):
                    head = pr * hpp + hh
                    refs[dr][4][0, rows, head * dv:(head + 1) * dv] = (
                        intra[u, hh] + inter[hh * CHUNK:(hh + 1) * CHUNK]).astype(BF16)
                st[ch] = st[ch] * a_last[u] + dst[u]
        for ch in chains:
            st_ref[ch[0], ch[1]] = st[ch]

    return [cumulative_decay, decayed_operands, scores, intra_and_increments, recurrence]


def _gdn_stages(refs, st_ref, dk, dv, nh, dirs):
    tb = refs[0][3].shape[1]
    nchunk = tb // CHUNK
    eye = _tri(CHUNK, False, False) & _tri(CHUNK, True, False)
    chains = [(dr, hd) for dr in dirs for hd in range(nh)]
    units = [(dr, c, hd) for dr in dirs for c in range(nchunk) for hd in range(nh)]
    g_cols, g_rows, small, kk, qk, x, aqk, rhs, q_dec, k_dec, a_last = ({} for _ in range(11))
    t, uw, kd_uw, aq_uw, lhs, decay = ({} for _ in range(6))

    def cumulative_decay():
        for dr in dirs:
            small[dr] = refs[dr][3][0]
            g_cols[dr] = _cumsum_dot(_tri_blocks(tb, dr == 1).astype(BF16), small[dr], True)
            g_rows[dr] = _cumsum_dot(_tri_blocks(tb, dr != 1).astype(BF16), refs[dr][4][0, 0], False)

    def grams():
        for u in units:
            dr, c, hd = u
            rows = slice(c * CHUNK, (c + 1) * CHUNK)
            kh = refs[dr][1][0, rows, hd * dk:(hd + 1) * dk]
            qh = refs[dr][0][0, rows, hd * dk:(hd + 1) * dk]
            kq = _dot_nt(jnp.concatenate([kh, qh], axis=0), kh)
            kk[u], qk[u] = kq[:CHUNK], kq[CHUNK:]

    def solve_operands():
        for u in units:
            dr, c, hd = u
            rows = slice(c * CHUNK, (c + 1) * CHUNK)
            ia = dr * nh + hd
            ib = 2 * nh + dr * nh + hd
            g_c = g_cols[dr][rows, ia:ia + 1]
            g_r = g_rows[dr][ia:ia + 1, rows]
            be_c = small[dr][rows, ib:ib + 1]
            decay[u] = jnp.exp(jnp.minimum(g_c - g_r, 0.0))
            x[u] = jnp.where(_tri(CHUNK, dr == 1, True), be_c * kk[u] * decay[u], 0.0)

    def other_operands():
        for u in units:
            dr, c, hd = u
            rows = slice(c * CHUNK, (c + 1) * CHUNK)
            ia = dr * nh + hd
            ib = 2 * nh + dr * nh + hd
            last = c * CHUNK + (0 if dr == 1 else CHUNK - 1)
            g_c = g_cols[dr][rows, ia:ia + 1]
            be_c = small[dr][rows, ib:ib + 1]
            g_last = g_cols[dr][last:last + 1, ia:ia + 1]
            aqk[u] = jnp.where(_tri(CHUNK, dr == 1, False), qk[u] * decay[u], 0.0).astype(BF16)
            eg = jnp.exp(g_c)
            qh = refs[dr][0][0, rows, hd * dk:(hd + 1) * dk].astype(F32)
            khf = refs[dr][1][0, rows, hd * dk:(hd + 1) * dk].astype(F32)
            vh = refs[dr][2][0, rows, hd * dv:(hd + 1) * dv].astype(F32)
            rhs[u] = jnp.concatenate([(be_c * vh).astype(BF16), ((be_c * eg) * khf).astype(BF16)], axis=1)
            q_dec[u] = qh * eg
            k_dec[u] = khf * jnp.exp(g_last - g_c)
            a_last[u] = jnp.exp(g_last)

    def inverse_start():
        for u in units:
            t[u] = jnp.where(eye, 1.0, jnp.where(_couple(CHUNK, 1, u[0] == 1), -x[u], 0.0))

    def inverse_level(s):
        a = {u: jnp.where(_couple(CHUNK, s, u[0] == 1), x[u], 0.0).astype(BF16) for u in units}
        tb16 = {u: t[u].astype(BF16) for u in units}
        ta = {u: _dot(tb16[u], a[u]).astype(BF16) for u in units}
        for u in units:
            t[u] = t[u] - _dot(ta[u], tb16[u])

    levels = []
    s = 2
    while s < CHUNK:
        levels.append(functools.partial(inverse_level, s))
        s *= 2

    def solve():
        for u in units:
            uw[u] = _dot(t[u].astype(BF16), rhs[u]).astype(BF16)

    def fold():
        for u in units:
            both = _dot(jnp.concatenate([k_dec[u].T.astype(BF16), aqk[u]], axis=0), uw[u])
            kd_uw[u], aq_uw[u] = both[:dk], both[dk:]
        for u in units:
            lhs[u] = jnp.concatenate([(-kd_uw[u][:, dv:]).astype(BF16),
                                      (q_dec[u] - aq_uw[u][:, dv:]).astype(BF16)], axis=0)

    def recurrence():
        st = {ch: st_ref[ch[0], ch[1]] for ch in chains}
        for ci in range(nchunk):
            for ch in chains:
                dr, hd = ch
                c = nchunk - 1 - ci if dr == 1 else ci
                u = (dr, c, hd)
                rows = slice(c * CHUNK, (c + 1) * CHUNK)
                res = _dot(lhs[u], st[ch].astype(BF16))
                refs[dr][5][0, rows, hd * dv:(hd + 1) * dv] = (res[dk:] + aq_uw[u][:, :dv]).astype(BF16)
                st[ch] = st[ch] * a_last[u] + res[:dk] + kd_uw[u][:, :dv]
        for ch in chains:
            st_ref[ch[0], ch[1]] = st[ch]

    return [cumulative_decay, grams, solve_operands, other_operands, inverse_start, *levels,
            solve, fold, recurrence]


def _scan_kernel(af, alf, ab, alb, bf, bcf, brf, bb, bcb, brb,
                 oaf, oab, obf, obb, sta_ref, stb_ref, *, dk_a, dv_a, dk_b, dv_b, nh_b):
    @pl.when(pl.program_id(1) == 0)
    def _():
        sta_ref[...] = jnp.zeros_like(sta_ref)
        stb_ref[...] = jnp.zeros_like(stb_ref)

    qk_a, qk_b = GLA_HEADS * dk_a, nh_b * dk_b
    qkv = lambda ref, qk: (_Cols(ref, 0), _Cols(ref, qk), _Cols(ref, 2 * qk))
    refs_a = ((*qkv(af, qk_a), alf, oaf), (*qkv(ab, qk_a), alb, oab))
    refs_b = ((*qkv(bf, qk_b), bcf, brf, obf), (*qkv(bb, qk_b), bcb, brb, obb))
    stages = {"a": _gla_stages(refs_a, sta_ref, dk_a, dv_a, (0, 1)),
              "b": _gdn_stages(refs_b, stb_ref, dk_b, dv_b, nh_b, (0, 1))}
    for name, idx in SCAN_ORDER:
        stages[name][idx]()


def _scan(gqkv, gla, dqkv, small_c, small_r, dims):
    b, s, _ = gqkv.shape
    qk_a, v_a, _, _, qk_b, v_b, _ = dims
    nb = s // TOKEN_BLOCK
    tb = TOKEN_BLOCK
    ns = small_c.shape[2]
    fwd = lambda n, col=0: pl.BlockSpec((1, tb, n), lambda i, j: (i, j, col))
    bwd = lambda n, col=0: pl.BlockSpec((1, tb, n), lambda i, j: (i, _bwd_block(j, nb), col))
    rfwd = pl.BlockSpec((1, 1, ns, tb), lambda i, j: (i, j, 0, 0))
    rbwd = pl.BlockSpec((1, 1, ns, tb), lambda i, j: (i, _bwd_block(j, nb), 0, 0))
    dk_a, dv_a = qk_a // GLA_HEADS, v_a // GLA_HEADS
    dk_b, dv_b = qk_b // GDN_HEADS, v_b // GDN_HEADS
    out = lambda v: jax.ShapeDtypeStruct((b, s, v), BF16)
    return pl.pallas_call(
        functools.partial(_scan_kernel, dk_a=dk_a, dv_a=dv_a, dk_b=dk_b, dv_b=dv_b, nh_b=GDN_HEADS),
        out_shape=[out(v_a), out(v_a), out(v_b), out(v_b)],
        grid=(b, nb),
        in_specs=[fwd(2 * qk_a + v_a), fwd(qk_a, 0), bwd(2 * qk_a + v_a), bwd(qk_a, 1),
                  fwd(2 * qk_b + v_b), fwd(ns), rfwd, bwd(2 * qk_b + v_b), bwd(ns), rbwd],
        out_specs=[fwd(v_a), bwd(v_a), fwd(v_b), bwd(v_b)],
        scratch_shapes=[pltpu.VMEM((2, qk_a // LANES, dv_a, LANES), F32),
                        pltpu.VMEM((2, GDN_HEADS, dk_b, dv_b), F32)],
        compiler_params=pltpu.CompilerParams(dimension_semantics=("parallel", "arbitrary"),
                                             vmem_limit_bytes=VMEM_LIMIT),
        name="scan",
    )(gqkv, gla, gqkv, gla, dqkv, small_c, small_r, dqkv, small_c, small_r)


def _post_kernel(x_ref, mod_ref, *refs, d, nh_a, nh_b, final, nsub):
    mix = [refs[6 * i:6 * i + 6] for i in range(nsub)]
    na_ref, nb_ref, wo_ref, g2_ref, w1_ref, w2_ref, fg_ref, o_ref = refs[6 * nsub:]
    tb = x_ref.shape[1] // nsub
    m = mod_ref[0]
    gt1 = m[:, 2 * d:3 * d]
    sh2, sc2, gt2 = m[:, 3 * d:4 * d], m[:, 4 * d:5 * d], m[:, 5 * d:6 * d]
    slab = 2 * LANES
    x1, hb, act = {}, {}, {}

    def merge_and_project(i):
        af_ref, ab_ref, bf_ref, bb_ref, ga_ref, gb_ref = mix[i]
        acc = None
        row = 0
        for (f_ref, b_ref, g_ref, n_ref, nh) in ((af_ref, ab_ref, ga_ref, na_ref, nh_a),
                                                 (bf_ref, bb_ref, gb_ref, nb_ref, nh_b)):
            width = f_ref.shape[2]
            hv = width // nh
            for c0 in range(0, width, slab):
                o = f_ref[0, :, c0:c0 + slab].astype(F32) + b_ref[0, :, c0:c0 + slab].astype(F32)
                gate = _silu(g_ref[0, :, c0:c0 + slab].astype(F32))
                parts = [(_rms(o[:, k:k + hv]) * n_ref[...] * gate[:, k:k + hv]).astype(BF16)
                         for k in range(0, slab, hv)]
                part = _dot(jnp.concatenate(parts, axis=1), wo_ref[row:row + slab, :])
                acc = part if acc is None else acc + part
                row += slab
        x1[i] = x_ref[0, i * tb:(i + 1) * tb, :] + gt1 * acc
        hb[i] = (_rms(x1[i]) * (g2_ref[...] * (1.0 + sc2)) + sh2).astype(BF16)

    def mlp_up(i):
        a = jnp.maximum(_dot(hb[i], w1_ref[...]), 0.0)
        act[i] = (a * a).astype(BF16)

    def mlp_down(i):
        x2 = x1[i] + gt2 * _dot(act[i], w2_ref[...])
        if final:
            x2 = _rms(x2) * fg_ref[...]
        o_ref[0, i * tb:(i + 1) * tb, :] = x2

    merge_and_project(0)
    for i in range(nsub):
        mlp_up(i)
        if i + 1 < nsub:
            merge_and_project(i + 1)
        mlp_down(i)


def _post(stream, part, mod, layer, final, oaf, oab, obf, obb, gg, dg, params, fg, nh_a, nh_b):
    na, nb_, wo, g2, w1, w2 = params
    ctx_arr, lat_arr = stream
    b, t, d = lat_arr.shape
    tb = TOKEN_BLOCK
    ctx = part == "ctx"
    nsub = 1 if ctx else POST_SUB
    nstep = 1 if ctx else t // (nsub * tb)
    assert ctx or t % (nsub * tb) == 0
    first = 0 if ctx else 1
    lay = lambda a: pl.BlockSpec((None,) + a.shape[1:], lambda i, j: (layer,) + (0,) * (a.ndim - 1))
    weight = lambda a: pl.BlockSpec((None,) + a.shape[1:], lambda i, j: (layer,) + (0,) * (a.ndim - 1),
                                    pipeline_mode=pl.Buffered(1))
    bsz = b
    mod_spec = pl.BlockSpec((1, 1, 6 * d), lambda i, j: (layer * MOD_ROWS + (bsz if ctx else i), 0, 0))
    va, vb = oaf.shape[2], obf.shape[2]
    mix_specs, mix_args = [], []
    for k in range(nsub):
        blk = lambda n, k=k: pl.BlockSpec((1, tb, n), lambda i, j: (i, first + j * nsub + k, 0))
        mix_specs += [blk(va), blk(va), blk(vb), blk(vb), blk(va), blk(vb)]
        mix_args += [oaf, oab, obf, obb, gg, dg]
    return pl.pallas_call(
        functools.partial(_post_kernel, d=d, nh_a=nh_a, nh_b=nh_b, final=final, nsub=nsub),
        out_shape=jax.ShapeDtypeStruct((b, nstep * nsub * tb, d), F32),
        grid=(b, nstep),
        in_specs=[pl.BlockSpec((1, nsub * tb, d), lambda i, j: (i, j, 0)), mod_spec, *mix_specs,
                  lay(na), lay(nb_), weight(wo), lay(g2), weight(w1), weight(w2),
                  pl.BlockSpec((1, d), lambda i, j: (0, 0))],
        out_specs=pl.BlockSpec((1, nsub * tb, d), lambda i, j: (i, j, 0)),
        compiler_params=pltpu.CompilerParams(dimension_semantics=("parallel", "parallel"),
                                             vmem_limit_bytes=VMEM_LIMIT),
        name="post_ctx" if ctx else "post",
    )(ctx_arr if ctx else lat_arr, mod, *mix_args, na, nb_, wo, g2, w1, w2, fg.reshape(1, d))


def kernel(x, c, ctx, c_ctx, w_ada, b_ada, norm1_g, norm2_g, w_in, gla_w_lr, gla_b_lr, gdn_conv_w,
           gdn_a_log, gdn_dt_bias, gla_norm_g, gdn_norm_g, w_out, w_ff1, w_ff2, final_norm_g):
    bsz, t, d = x.shape
    depth = w_ada.shape[0]
    assert ctx.shape[1] == TOKEN_BLOCK and t % TOKEN_BLOCK == 0 and TOKEN_BLOCK % GRID_W == 0
    assert bsz + 1 <= MOD_ROWS
    qk_a = gla_w_lr.shape[3]
    rank = gla_w_lr.shape[2]
    v_a = gla_norm_g.shape[1] * GLA_HEADS
    conv_dim = gdn_conv_w.shape[2]
    v_b = gdn_norm_g.shape[1] * GDN_HEADS
    qk_b = (conv_dim - v_b) // 2
    nd = 2 * GDN_HEADS
    dims = (qk_a, v_a, rank, conv_dim, qk_b, v_b, GDN_HEADS)

    o_r = 2 * qk_a + 2 * v_a
    o_c = o_r + rank
    o_g = o_c + conv_dim
    o_s = o_g + v_b
    assert w_in.shape[2] == o_s + 2 * nd
    w_main = jnp.concatenate([w_in[:, :, :o_r], w_in[:, :, o_c:o_s]], axis=2).astype(BF16)
    w_sm = jnp.concatenate([w_in[:, :, o_r:o_c], w_in[:, :, o_s:]], axis=2)
    w_small = jnp.pad(w_sm, ((0, 0), (0, 0), (0, LANES - w_sm.shape[2]))).astype(BF16)
    w_small_t = jnp.swapaxes(w_in[:, :, o_s:], 1, 2).astype(BF16)

    cc = jnp.concatenate([c, c_ctx[None, :], jnp.zeros((MOD_ROWS - bsz - 1, d), F32)], axis=0)
    mod = _modulation(cc, w_ada, b_ada).reshape(depth * MOD_ROWS, 1, 6 * d)

    in_params = (norm1_g.reshape(depth, 1, d), w_main, w_small, w_small_t, gla_w_lr.astype(BF16),
                 gla_b_lr.reshape(depth, 2, 1, qk_a), gdn_conv_w,
                 gdn_a_log.reshape(depth, 1, nd), gdn_dt_bias.reshape(depth, 1, nd),
                 gdn_a_log.reshape(depth, nd, 1), gdn_dt_bias.reshape(depth, nd, 1))
    post_params = (gla_norm_g.reshape(depth, 1, -1), gdn_norm_g.reshape(depth, 1, -1), w_out.astype(BF16),
                   norm2_g.reshape(depth, 1, d), w_ff1.astype(BF16), w_ff2.astype(BF16))

    stream = (ctx, x)
    for l in range(depth):
        last = l == depth - 1
        gqkv, gg, gla, dqkv, dg, small_c, small_r = _in_proj(stream, mod, l, bsz, in_params, dims)
        oaf, oab, obf, obb = _scan(gqkv, gla, dqkv, small_c, small_r, dims)
        args = (mod, l, last, oaf, oab, obf, obb, gg, dg, post_params, final_norm_g, GLA_HEADS, GDN_HEADS)
        xs = _post(stream, "latent", *args)
        if not last:
            stream = (_post(stream, "ctx", *args), xs)
    return xs
```

```python
import functools

import jax
import jax.numpy as jnp
from jax import lax
from jax.experimental import pallas as pl
from jax.experimental.pallas import tpu as pltpu

EPS = 1e-6
GRID_W = 64
GLA_HEADS = 4
GDN_HEADS = 4
GLA_TAU = 16.0
TOKEN_BLOCK = 256
CHUNK = 64
LANES = 128
SUBLANES = 8
POST_SUB = 2
SCAN_ROWS = 2
SCAN_ROW_LAG = 10
MOD_ROWS = 16
VMEM_LIMIT = 56 * 1024 * 1024
SCAN_ORDER = (("b", 0), ("b", 1), ("a", 0), ("a", 1), ("b", 2), ("b", 3), ("a", 2), ("a", 3), ("b", 4),
              ("b", 5), ("b", 6), ("b", 7), ("b", 8), ("b", 9), ("a", 4), ("b", 10), ("b", 11), ("b", 12))

F32 = jnp.float32
BF16 = jnp.bfloat16


def _dot(a, b):
    return jnp.dot(a, b, preferred_element_type=F32)


def _dot_nt(a, b):
    return lax.dot_general(a, b, (((1,), (1,)), ((), ())), preferred_element_type=F32)


def _dot_tn(a, b):
    return lax.dot_general(a, b, (((0,), (0,)), ((), ())), preferred_element_type=F32)


def _silu(x):
    return x / (1.0 + jnp.exp(-x))


def _sigmoid(x):
    return 1.0 / (1.0 + jnp.exp(-x))


def _softplus(x):
    return jnp.maximum(x, 0.0) + jnp.log(1.0 + jnp.exp(-jnp.abs(x)))


def _log_sigmoid(x):
    return jnp.minimum(x, 0.0) - jnp.log(1.0 + jnp.exp(-jnp.abs(x)))


def _rms(x):
    return x * lax.rsqrt(jnp.mean(x * x, axis=-1, keepdims=True) + EPS)


def _cumsum_dot(tri_bf16, x, left):
    hi = x.astype(BF16)
    lo = (x - hi.astype(F32)).astype(BF16)
    if left:
        return _dot(tri_bf16, hi) + _dot(tri_bf16, lo)
    return _dot(hi, tri_bf16) + _dot(lo, tri_bf16)


def _tri(n, upper, strict):
    r = lax.broadcasted_iota(jnp.int32, (n, n), 0)
    c = lax.broadcasted_iota(jnp.int32, (n, n), 1)
    if upper:
        m = (r < c) if strict else (r <= c)
    else:
        m = (r > c) if strict else (r >= c)
    return m


def _couple(n, s, upper):
    r = lax.broadcasted_iota(jnp.int32, (n, n), 0)
    c = lax.broadcasted_iota(jnp.int32, (n, n), 1)
    same = (r & ~(2 * s - 1)) == (c & ~(2 * s - 1))
    r_hi, c_hi = (r & s) != 0, (c & s) != 0
    return same & ((~r_hi & c_hi) if upper else (r_hi & ~c_hi))


def _tri_blocks(n, upper):
    r = lax.broadcasted_iota(jnp.int32, (n, n), 0)
    c = lax.broadcasted_iota(jnp.int32, (n, n), 1)
    same = (r & ~(CHUNK - 1)) == (c & ~(CHUNK - 1))
    return same & ((r <= c) if upper else (r >= c))


def _mod_kernel(cc_ref, w_ref, b_ref, o_ref):
    s = _silu(cc_ref[...]).astype(BF16)
    o_ref[0] = _dot(s, w_ref[0].astype(BF16)) + b_ref[0]


def _modulation(cc, w_ada, b_ada):
    depth, d, d6 = w_ada.shape
    nblk = d6 // d
    return pl.pallas_call(
        _mod_kernel,
        out_shape=jax.ShapeDtypeStruct((depth, MOD_ROWS, d6), F32),
        grid=(depth, nblk),
        in_specs=[
            pl.BlockSpec((MOD_ROWS, d), lambda l, n: (0, 0)),
            pl.BlockSpec((1, d, d), lambda l, n: (l, 0, n)),
            pl.BlockSpec((1, 1, d), lambda l, n: (l, 0, n)),
        ],
        out_specs=pl.BlockSpec((1, MOD_ROWS, d), lambda l, n: (l, 0, n)),
        compiler_params=pltpu.CompilerParams(dimension_semantics=("parallel", "parallel")),
        name="adaln_modulation",
    )(cc, w_ada, b_ada.reshape(depth, 1, d6))


def _in_proj_kernel(c_ref, x_ref, mod_ref, g1_ref, wm_ref, ws_ref, wst_ref, wlr_ref, blr_ref, cw_ref,
                    arow_ref, dtrow_ref, acol_ref, dtcol_ref,
                    gqkv_ref, gg_ref, gla_ref, dqkv_ref, dg_ref, sc_ref, sr_ref, *, d, qk_a, v_a, rank, conv_dim, qk_b, v_b, nh_b):
    j = pl.program_id(1)
    x = jnp.where(j == 0, c_ref[0], x_ref[0])
    m = mod_ref[0]
    sh1, sc1 = m[:, 0:d], m[:, d:2 * d]
    h = _rms(x) * (g1_ref[...] * (1.0 + sc1)) + sh1
    hb = h.astype(BF16)
    o_conv = 2 * qk_a + 2 * v_a
    o_gate = o_conv + conv_dim
    grp = 2 * LANES
    tb = x.shape[0]
    t = lax.broadcasted_iota(jnp.int32, (tb, 1), 0)
    seg_mask = jnp.where(j == 0, tb - 1, GRID_W - 1)
    first = (t & seg_mask) == 0
    last = (t & seg_mask) == seg_mask
    dkh = qk_b // nh_b

    def conv_dot(k):
        return _dot(hb, wm_ref[:, o_conv + k * grp:o_conv + (k + 1) * grp])

    def zero_rows(a, mask, at):
        pieces = []
        for s0 in range(0, tb, GRID_W):
            r0 = s0 + at
            pieces += [a[s0:r0], jnp.where(mask[r0:r0 + SUBLANES], 0.0, a[r0:r0 + SUBLANES]),
                       a[r0 + SUBLANES:s0 + GRID_W]]
        return jnp.concatenate([p for p in pieces if p.shape[0]], axis=0)

    def conv_group(u, k):
        up = zero_rows(pltpu.roll(u, 1, 0), first, 0)
        un = zero_rows(pltpu.roll(u, tb - 1, 0), last, GRID_W - SUBLANES)
        cw = cw_ref[:, k * grp:(k + 1) * grp]
        s = _silu(cw[0:1] * up + cw[1:2] * u + cw[2:3] * un)
        c0 = k * grp
        if c0 < 2 * qk_b:
            for hh in range(grp // dkh):
                sh = s[:, hh * dkh:(hh + 1) * dkh]
                inv = lax.rsqrt(jnp.sum(sh * sh, axis=-1, keepdims=True) + EPS)
                if c0 < qk_b:
                    inv = inv * (float(dkh) ** -0.5)
                dqkv_ref[0, :, c0 + hh * dkh:c0 + (hh + 1) * dkh] = (sh * inv).astype(BF16)
        else:
            dqkv_ref[0, :, c0:c0 + grp] = s.astype(BF16)

    def light_group(k):
        c0 = k * grp
        w0 = c0 if c0 < o_conv else o_gate + (c0 - o_conv)
        pa = _dot(hb, wm_ref[:, w0:w0 + grp])
        if c0 < qk_a:
            gqkv_ref[0, :, c0:c0 + grp] = (pa * (float(qk_a // GLA_HEADS) ** -0.5)).astype(BF16)
        elif c0 < 2 * qk_a + v_a:
            gqkv_ref[0, :, c0:c0 + grp] = pa.astype(BF16)
        elif c0 < o_conv:
            gg_ref[0, :, c0 - 2 * qk_a - v_a:c0 - 2 * qk_a - v_a + grp] = pa.astype(BF16)
        else:
            dg_ref[0, :, c0 - o_conv:c0 - o_conv + grp] = pa.astype(BF16)

    def small_group(ps, pst):
        r_a = ps[:, 0:rank].astype(BF16)
        for dr in range(2):
            lr = _dot(r_a, wlr_ref[dr]) + blr_ref[dr]
            gla_ref[0, :, dr * qk_a:(dr + 1) * qk_a] = _log_sigmoid(lr) * (1.0 / GLA_TAU)
        nd = 2 * nh_b
        a_c = ps[:, rank:rank + nd]
        b_c = ps[:, rank + nd:rank + 2 * nd]
        sc_ref[0, :, 0:nd] = -jnp.exp(arow_ref[...]) * _softplus(a_c + dtrow_ref[...])
        sc_ref[0, :, nd:2 * nd] = _sigmoid(b_c)
        sr_ref[0, 0, 0:nd, :] = -jnp.exp(acol_ref[...]) * _softplus(pst[0:nd] + dtcol_ref[...])
        sr_ref[0, 0, nd:2 * nd, :] = _sigmoid(pst[nd:2 * nd])

    n_conv = conv_dim // grp
    n_light = (o_conv + v_b) // grp
    us = {0: conv_dot(0)}
    ps = _dot(hb, ws_ref[...])
    pst = _dot_nt(wst_ref[...], hb)
    if n_conv > 1:
        us[1] = conv_dot(1)
    g_next = 0
    for k in range(n_conv):
        if k + 2 < n_conv:
            us[k + 2] = conv_dot(k + 2)
        conv_group(us.pop(k), k)
        if k == 1 or n_conv == 1:
            small_group(ps, pst)
        todo = (n_light - g_next + (n_conv - k) - 1) // (n_conv - k) if k >= n_conv // 2 else 1
        for _ in range(min(todo, n_light - g_next)):
            light_group(g_next)
            g_next += 1
    while g_next < n_light:
        light_group(g_next)
        g_next += 1


def _in_proj(stream, mod, layer, bsz, params, dims):
    g1, wm, ws, wst, wlr, blr, cw, a_row, dt_row, a_col, dt_col = params
    head, body = stream
    b, _, d = body.shape
    nb = body.shape[1] // TOKEN_BLOCK + 1
    s = nb * TOKEN_BLOCK
    qk_a, v_a, rank, conv_dim, qk_b, v_b, nh_b = dims
    nd = 2 * nh_b
    tb = TOKEN_BLOCK
    lay = lambda a: pl.BlockSpec((None,) + a.shape[1:], lambda i, j: (layer,) + (0,) * (a.ndim - 1))
    tok = lambda n: pl.BlockSpec((1, tb, n), lambda i, j: (i, j, 0))
    mod_spec = pl.BlockSpec((1, 1, 6 * d), lambda i, j: (layer * MOD_ROWS + jnp.where(j == 0, bsz, i), 0, 0))
    out_shape = [
        jax.ShapeDtypeStruct((b, s, 2 * qk_a + v_a), BF16), jax.ShapeDtypeStruct((b, s, v_a), BF16),
        jax.ShapeDtypeStruct((b, s, 2 * qk_a), F32),
        jax.ShapeDtypeStruct((b, s, conv_dim), BF16), jax.ShapeDtypeStruct((b, s, v_b), BF16),
        jax.ShapeDtypeStruct((b, s, 2 * nd), F32),
        jax.ShapeDtypeStruct((b, nb, 2 * nd, tb), F32),
    ]
    out_specs = [tok(2 * qk_a + v_a), tok(v_a), tok(2 * qk_a), tok(conv_dim), tok(v_b), tok(2 * nd),
                 pl.BlockSpec((1, 1, 2 * nd, tb), lambda i, j: (i, j, 0, 0))]
    kern = functools.partial(_in_proj_kernel, d=d, qk_a=qk_a, v_a=v_a, rank=rank, conv_dim=conv_dim,
                             qk_b=qk_b, v_b=v_b, nh_b=nh_b)
    return pl.pallas_call(
        kern, out_shape=out_shape, grid=(b, nb),
        in_specs=[
            pl.BlockSpec((1, tb, d), lambda i, j: (i, 0, 0)),
            pl.BlockSpec((1, tb, d), lambda i, j: (i, jnp.maximum(j - 1, 0), 0)), mod_spec, lay(g1),
            lay(wm), lay(ws), lay(wst), lay(wlr), lay(blr), lay(cw), lay(a_row), lay(dt_row), lay(a_col), lay(dt_col),
        ],
        out_specs=out_specs,
        compiler_params=pltpu.CompilerParams(dimension_semantics=("parallel", "parallel"),
                                             vmem_limit_bytes=VMEM_LIMIT),
        name="in_proj",
    )(head, body, mod, g1, wm, ws, wst, wlr, blr, cw, a_row, dt_row, a_col, dt_col)


def _bwd_block(j, nb):
    return jnp.where(j == 0, 0, nb - j)


class _Cols:
    def __init__(self, ref, off):
        self.ref, self.off = ref, off

    def __getitem__(self, idx):
        z, rows, cols = idx
        return self.ref[z, rows, self.off + cols.start:self.off + cols.stop]


def _gla_stages(refs, st_ref, dk, dv, dirs, bi):
    tb = refs[0][3].shape[1]
    nchunk = tb // CHUNK
    npair = (GLA_HEADS * dk) // LANES
    hpp = LANES // dk
    lane = lax.broadcasted_iota(jnp.int32, (1, LANES), 1)
    lms = [(lane >= hh * dk) & (lane < (hh + 1) * dk) for hh in range(hpp)]
    mid = CHUNK // 2
    chains = [(dr, pr) for dr in dirs for pr in range(npair)]
    units = [(dr, pr, c) for (dr, pr) in chains for c in range(nchunk)]
    bcum, q_dec, k_hat, q_mid, k_mid, a_last, sc, intra, dst = ({} for _ in range(9))

    def cumulative_decay():
        for dr in dirs:
            both = _cumsum_dot(_tri_blocks(tb, dr == 1).astype(BF16), refs[dr][3][bi], True)
            for pr in range(npair):
                bcum[dr, pr] = both[:, pr * LANES:(pr + 1) * LANES]

    def decayed_operands():
        for u in units:
            dr, pr, c = u
            rows = slice(c * CHUNK, (c + 1) * CHUNK)
            lanes = slice(pr * LANES, (pr + 1) * LANES)
            last = c * CHUNK + (0 if dr == 1 else CHUNK - 1)
            b = bcum[dr, pr][rows]
            b_last = bcum[dr, pr][last:last + 1]
            b_mid = bcum[dr, pr][c * CHUNK + mid:c * CHUNK + mid + 1]
            qc = refs[dr][0][bi, rows, lanes].astype(F32)
            kc = refs[dr][1][bi, rows, lanes].astype(F32)
            q_dec[u] = qc * jnp.exp(b)
            k_hat[u] = (kc * jnp.exp(b_last - b)).astype(BF16)
            q_mid[u] = qc * jnp.exp(b - b_mid)
            k_mid[u] = (kc * jnp.exp(b_mid - b)).astype(BF16)
            a_last[u] = jnp.exp(b_last)

    def per_head(m):
        return jnp.concatenate([jnp.where(lm, m, 0.0).astype(BF16) for lm in lms], axis=0)

    def scores():
        for u in units:
            sc[u] = _dot_nt(per_head(q_mid[u]), k_mid[u])

    def intra_and_increments():
        for u in units:
            dr, pr, c = u
            rows = slice(c * CHUNK, (c + 1) * CHUNK)
            causal = _tri(CHUNK, dr == 1, False)
            v_grp = refs[dr][2][bi, rows, pr * hpp * dv:(pr + 1) * hpp * dv]
            dh = _dot_tn(v_grp, k_hat[u])
            acc = None
            for hh in range(hpp):
                pm = jnp.where(causal, sc[u][hh * CHUNK:(hh + 1) * CHUNK], 0.0).astype(BF16)
                intra[u, hh] = _dot(pm, v_grp[:, hh * dv:(hh + 1) * dv])
                part = dh[hh * dv:(hh + 1) * dv]
                acc = part if acc is None else jnp.where(lms[hh], part, acc)
            dst[u] = acc

    def recurrence():
        st = {ch: st_ref[bi, ch[0], ch[1]] for ch in chains}
        for ci in range(nchunk):
            for ch in chains:
                dr, pr = ch
                c = nchunk - 1 - ci if dr == 1 else ci
                u = (dr, pr, c)
                rows = slice(c * CHUNK, (c + 1) * CHUNK)
                inter = _dot_nt(per_head(q_dec[u]), st[ch].astype(BF16))
                for hh in range(hpp):
                    head = pr * hpp + hh
                    refs[dr][4][bi, rows, head * dv:(head + 1) * dv] = (
                        intra[u, hh] + inter[hh * CHUNK:(hh + 1) * CHUNK]).astype(BF16)
                st[ch] = st[ch] * a_last[u] + dst[u]
        for ch in chains:
            st_ref[bi, ch[0], ch[1]] = st[ch]

    return [cumulative_decay, decayed_operands, scores, intra_and_increments, recurrence]


def _gdn_stages(refs, st_ref, dk, dv, nh, dirs, bi):
    tb = refs[0][3].shape[1]
    nchunk = tb // CHUNK
    eye = _tri(CHUNK, False, False) & _tri(CHUNK, True, False)
    chains = [(dr, hd) for dr in dirs for hd in range(nh)]
    units = [(dr, c, hd) for dr in dirs for c in range(nchunk) for hd in range(nh)]
    g_cols, g_rows, small, kk, qk, x, aqk, rhs, q_dec, k_dec, a_last = ({} for _ in range(11))
    t, uw, kd_uw, aq_uw, lhs, decay = ({} for _ in range(6))

    def cumulative_decay():
        for dr in dirs:
            small[dr] = refs[dr][3][bi]
            g_cols[dr] = _cumsum_dot(_tri_blocks(tb, dr == 1).astype(BF16), small[dr], True)
            g_rows[dr] = _cumsum_dot(_tri_blocks(tb, dr != 1).astype(BF16), refs[dr][4][bi, 0], False)

    def grams():
        for u in units:
            dr, c, hd = u
            rows = slice(c * CHUNK, (c + 1) * CHUNK)
            kh = refs[dr][1][bi, rows, hd * dk:(hd + 1) * dk]
            qh = refs[dr][0][bi, rows, hd * dk:(hd + 1) * dk]
            kq = _dot_nt(jnp.concatenate([kh, qh], axis=0), kh)
            kk[u], qk[u] = kq[:CHUNK], kq[CHUNK:]

    def solve_operands():
        for u in units:
            dr, c, hd = u
            rows = slice(c * CHUNK, (c + 1) * CHUNK)
            ia = dr * nh + hd
            ib = 2 * nh + dr * nh + hd
            g_c = g_cols[dr][rows, ia:ia + 1]
            g_r = g_rows[dr][ia:ia + 1, rows]
            be_c = small[dr][rows, ib:ib + 1]
            decay[u] = jnp.exp(jnp.minimum(g_c - g_r, 0.0))
            x[u] = jnp.where(_tri(CHUNK, dr == 1, True), be_c * kk[u] * decay[u], 0.0)

    def other_operands():
        for u in units:
            dr, c, hd = u
            rows = slice(c * CHUNK, (c + 1) * CHUNK)
            ia = dr * nh + hd
            ib = 2 * nh + dr * nh + hd
            last = c * CHUNK + (0 if dr == 1 else CHUNK - 1)
            g_c = g_cols[dr][rows, ia:ia + 1]
            be_c = small[dr][rows, ib:ib + 1]
            g_last = g_cols[dr][last:last + 1, ia:ia + 1]
            aqk[u] = jnp.where(_tri(CHUNK, dr == 1, False), qk[u] * decay[u], 0.0).astype(BF16)
            eg = jnp.exp(g_c)
            qh = refs[dr][0][bi, rows, hd * dk:(hd + 1) * dk].astype(F32)
            khf = refs[dr][1][bi, rows, hd * dk:(hd + 1) * dk].astype(F32)
            vh = refs[dr][2][bi, rows, hd * dv:(hd + 1) * dv].astype(F32)
            rhs[u] = jnp.concatenate([(be_c * vh).astype(BF16), ((be_c * eg) * khf).astype(BF16)], axis=1)
            q_dec[u] = qh * eg
            k_dec[u] = khf * jnp.exp(g_last - g_c)
            a_last[u] = jnp.exp(g_last)

    def inverse_start():
        for u in units:
            t[u] = jnp.where(eye, 1.0, jnp.where(_couple(CHUNK, 1, u[0] == 1), -x[u], 0.0))

    def inverse_level(s):
        a = {u: jnp.where(_couple(CHUNK, s, u[0] == 1), x[u], 0.0).astype(BF16) for u in units}
        tb16 = {u: t[u].astype(BF16) for u in units}
        ta = {u: _dot(tb16[u], a[u]).astype(BF16) for u in units}
        for u in units:
            t[u] = t[u] - _dot(ta[u], tb16[u])

    levels = []
    s = 2
    while s < CHUNK:
        levels.append(functools.partial(inverse_level, s))
        s *= 2

    def solve():
        for u in units:
            uw[u] = _dot(t[u].astype(BF16), rhs[u]).astype(BF16)

    def fold():
        for u in units:
            both = _dot(jnp.concatenate([k_dec[u].T.astype(BF16), aqk[u]], axis=0), uw[u])
            kd_uw[u], aq_uw[u] = both[:dk], both[dk:]
        for u in units:
            lhs[u] = jnp.concatenate([(-kd_uw[u][:, dv:]).astype(BF16),
                                      (q_dec[u] - aq_uw[u][:, dv:]).astype(BF16)], axis=0)

    def recurrence():
        st = {ch: st_ref[bi, ch[0], ch[1]] for ch in chains}
        for ci in range(nchunk):
            for ch in chains:
                dr, hd = ch
                c = nchunk - 1 - ci if dr == 1 else ci
                u = (dr, c, hd)
                rows = slice(c * CHUNK, (c + 1) * CHUNK)
                res = _dot(lhs[u], st[ch].astype(BF16))
                refs[dr][5][bi, rows, hd * dv:(hd + 1) * dv] = (res[dk:] + aq_uw[u][:, :dv]).astype(BF16)
                st[ch] = st[ch] * a_last[u] + res[:dk] + kd_uw[u][:, :dv]
        for ch in chains:
            st_ref[bi, ch[0], ch[1]] = st[ch]

    return [cumulative_decay, grams, solve_operands, other_operands, inverse_start, *levels,
            solve, fold, recurrence]


def _scan_kernel(af, alf, ab, alb, bf, bcf, brf, bb, bcb, brb,
                 oaf, oab, obf, obb, sta_ref, stb_ref, *, dk_a, dv_a, dk_b, dv_b, nh_b):
    @pl.when(pl.program_id(1) == 0)
    def _():
        sta_ref[...] = jnp.zeros_like(sta_ref)
        stb_ref[...] = jnp.zeros_like(stb_ref)

    qk_a, qk_b = GLA_HEADS * dk_a, nh_b * dk_b
    qkv = lambda ref, qk: (_Cols(ref, 0), _Cols(ref, qk), _Cols(ref, 2 * qk))
    refs_a = ((*qkv(af, qk_a), alf, oaf), (*qkv(ab, qk_a), alb, oab))
    refs_b = ((*qkv(bf, qk_b), bcf, brf, obf), (*qkv(bb, qk_b), bcb, brb, obb))
    stages = {}
    for bi in range(af.shape[0]):
        stages["a", bi] = _gla_stages(refs_a, sta_ref, dk_a, dv_a, (0, 1), bi)
        stages["b", bi] = _gdn_stages(refs_b, stb_ref, dk_b, dv_b, nh_b, (0, 1), bi)
    lag = SCAN_ROW_LAG
    for pos in range(len(SCAN_ORDER) + lag * (af.shape[0] - 1)):
        for bi in range(af.shape[0]):
            if 0 <= pos - lag * bi < len(SCAN_ORDER):
                name, idx = SCAN_ORDER[pos - lag * bi]
                stages[name, bi][idx]()


def _scan(gqkv, gla, dqkv, small_c, small_r, dims):
    b, s, _ = gqkv.shape
    qk_a, v_a, _, _, qk_b, v_b, _ = dims
    nb = s // TOKEN_BLOCK
    tb = TOKEN_BLOCK
    ns = small_c.shape[2]
    nr = SCAN_ROWS
    assert b % nr == 0
    fwd = lambda n, col=0: pl.BlockSpec((nr, tb, n), lambda i, j: (i, j, col))
    bwd = lambda n, col=0: pl.BlockSpec((nr, tb, n), lambda i, j: (i, _bwd_block(j, nb), col))
    rfwd = pl.BlockSpec((nr, 1, ns, tb), lambda i, j: (i, j, 0, 0))
    rbwd = pl.BlockSpec((nr, 1, ns, tb), lambda i, j: (i, _bwd_block(j, nb), 0, 0))
    dk_a, dv_a = qk_a // GLA_HEADS, v_a // GLA_HEADS
    dk_b, dv_b = qk_b // GDN_HEADS, v_b // GDN_HEADS
    out = lambda v: jax.ShapeDtypeStruct((b, s, v), BF16)
    return pl.pallas_call(
        functools.partial(_scan_kernel, dk_a=dk_a, dv_a=dv_a, dk_b=dk_b, dv_b=dv_b, nh_b=GDN_HEADS),
        out_shape=[out(v_a), out(v_a), out(v_b), out(v_b)],
        grid=(b // nr, nb),
        in_specs=[fwd(2 * qk_a + v_a), fwd(qk_a, 0), bwd(2 * qk_a + v_a), bwd(qk_a, 1),
                  fwd(2 * qk_b + v_b), fwd(ns), rfwd, bwd(2 * qk_b + v_b), bwd(ns), rbwd],
        out_specs=[fwd(v_a), bwd(v_a), fwd(v_b), bwd(v_b)],
        scratch_shapes=[pltpu.VMEM((nr, 2, qk_a // LANES, dv_a, LANES), F32),
                        pltpu.VMEM((nr, 2, GDN_HEADS, dk_b, dv_b), F32)],
        compiler_params=pltpu.CompilerParams(dimension_semantics=("parallel", "arbitrary"),
                                             vmem_limit_bytes=VMEM_LIMIT),
        name="scan",
    )(gqkv, gla, gqkv, gla, dqkv, small_c, small_r, dqkv, small_c, small_r)


def _post_kernel(x_ref, mod_ref, *refs, d, nh_a, nh_b, final, nsub):
    mix = [refs[6 * i:6 * i + 6] for i in range(nsub)]
    na_ref, nb_ref, wo_ref, g2_ref, w1_ref, w2_ref, fg_ref, o_ref = refs[6 * nsub:]
    tb = x_ref.shape[1] // nsub
    m = mod_ref[0]
    gt1 = m[:, 2 * d:3 * d]
    sh2, sc2, gt2 = m[:, 3 * d:4 * d], m[:, 4 * d:5 * d], m[:, 5 * d:6 * d]
    slab = 2 * LANES
    x1, hb, act = {}, {}, {}

    def merge_and_project(i):
        af_ref, ab_ref, bf_ref, bb_ref, ga_ref, gb_ref = mix[i]
        acc = None
        row = 0
        for (f_ref, b_ref, g_ref, n_ref, nh) in ((af_ref, ab_ref, ga_ref, na_ref, nh_a),
                                                 (bf_ref, bb_ref, gb_ref, nb_ref, nh_b)):
            width = f_ref.shape[2]
            hv = width // nh
            for c0 in range(0, width, slab):
                o = f_ref[0, :, c0:c0 + slab].astype(F32) + b_ref[0, :, c0:c0 + slab].astype(F32)
                gate = _silu(g_ref[0, :, c0:c0 + slab].astype(F32))
                parts = [(_rms(o[:, k:k + hv]) * n_ref[...] * gate[:, k:k + hv]).astype(BF16)
                         for k in range(0, slab, hv)]
                part = _dot(jnp.concatenate(parts, axis=1), wo_ref[row:row + slab, :])
                acc = part if acc is None else acc + part
                row += slab
        x1[i] = x_ref[0, i * tb:(i + 1) * tb, :] + gt1 * acc
        hb[i] = (_rms(x1[i]) * (g2_ref[...] * (1.0 + sc2)) + sh2).astype(BF16)

    def mlp_up(i):
        a = jnp.maximum(_dot(hb[i], w1_ref[...]), 0.0)
        act[i] = (a * a).astype(BF16)

    def mlp_down(i):
        x2 = x1[i] + gt2 * _dot(act[i], w2_ref[...])
        if final:
            x2 = _rms(x2) * fg_ref[...]
        o_ref[0, i * tb:(i + 1) * tb, :] = x2

    merge_and_project(0)
    for i in range(nsub):
        mlp_up(i)
        if i + 1 < nsub:
            merge_and_project(i + 1)
        mlp_down(i)


def _post(stream, part, mod, layer, final, oaf, oab, obf, obb, gg, dg, params, fg, nh_a, nh_b):
    na, nb_, wo, g2, w1, w2 = params
    ctx_arr, lat_arr = stream
    b, t, d = lat_arr.shape
    tb = TOKEN_BLOCK
    ctx = part == "ctx"
    nsub = 1 if ctx else POST_SUB
    nstep = 1 if ctx else t // (nsub * tb)
    assert ctx or t % (nsub * tb) == 0
    first = 0 if ctx else 1
    lay = lambda a: pl.BlockSpec((None,) + a.shape[1:], lambda i, j: (layer,) + (0,) * (a.ndim - 1))
    weight = lambda a: pl.BlockSpec((None,) + a.shape[1:], lambda i, j: (layer,) + (0,) * (a.ndim - 1),
                                    pipeline_mode=pl.Buffered(1))
    bsz = b
    mod_spec = pl.BlockSpec((1, 1, 6 * d), lambda i, j: (layer * MOD_ROWS + (bsz if ctx else i), 0, 0))
    va, vb = oaf.shape[2], obf.shape[2]
    mix_specs, mix_args = [], []
    for k in range(nsub):
        blk = lambda n, k=k: pl.BlockSpec((1, tb, n), lambda i, j: (i, first + j * nsub + k, 0))
        mix_specs += [blk(va), blk(va), blk(vb), blk(vb), blk(va), blk(vb)]
        mix_args += [oaf, oab, obf, obb, gg, dg]
    return pl.pallas_call(
        functools.partial(_post_kernel, d=d, nh_a=nh_a, nh_b=nh_b, final=final, nsub=nsub),
        out_shape=jax.ShapeDtypeStruct((b, nstep * nsub * tb, d), F32),
        grid=(b, nstep),
        in_specs=[pl.BlockSpec((1, nsub * tb, d), lambda i, j: (i, j, 0)), mod_spec, *mix_specs,
                  lay(na), lay(nb_), weight(wo), lay(g2), weight(w1), weight(w2),
                  pl.BlockSpec((1, d), lambda i, j: (0, 0))],
        out_specs=pl.BlockSpec((1, nsub * tb, d), lambda i, j: (i, j, 0)),
        compiler_params=pltpu.CompilerParams(dimension_semantics=("parallel", "parallel"),
                                             vmem_limit_bytes=VMEM_LIMIT),
        name="post_ctx" if ctx else "post",
    )(ctx_arr if ctx else lat_arr, mod, *mix_args, na, nb_, wo, g2, w1, w2, fg.reshape(1, d))


def kernel(x, c, ctx, c_ctx, w_ada, b_ada, norm1_g, norm2_g, w_in, gla_w_lr, gla_b_lr, gdn_conv_w,
           gdn_a_log, gdn_dt_bias, gla_norm_g, gdn_norm_g, w_out, w_ff1, w_ff2, final_norm_g):
    bsz, t, d = x.shape
    depth = w_ada.shape[0]
    assert ctx.shape[1] == TOKEN_BLOCK and t % TOKEN_BLOCK == 0 and TOKEN_BLOCK % GRID_W == 0
    assert bsz + 1 <= MOD_ROWS
    qk_a = gla_w_lr.shape[3]
    rank = gla_w_lr.shape[2]
    v_a = gla_norm_g.shape[1] * GLA_HEADS
    conv_dim = gdn_conv_w.shape[2]
    v_b = gdn_norm_g.shape[1] * GDN_HEADS
    qk_b = (conv_dim - v_b) // 2
    nd = 2 * GDN_HEADS
    dims = (qk_a, v_a, rank, conv_dim, qk_b, v_b, GDN_HEADS)

    o_r = 2 * qk_a + 2 * v_a
    o_c = o_r + rank
    o_g = o_c + conv_dim
    o_s = o_g + v_b
    assert w_in.shape[2] == o_s + 2 * nd
    w_main = jnp.concatenate([w_in[:, :, :o_r], w_in[:, :, o_c:o_s]], axis=2).astype(BF16)
    w_sm = jnp.concatenate([w_in[:, :, o_r:o_c], w_in[:, :, o_s:]], axis=2)
    w_small = jnp.pad(w_sm, ((0, 0), (0, 0), (0, LANES - w_sm.shape[2]))).astype(BF16)
    w_small_t = jnp.swapaxes(w_in[:, :, o_s:], 1, 2).astype(BF16)

    cc = jnp.concatenate([c, c_ctx[None, :], jnp.zeros((MOD_ROWS - bsz - 1, d), F32)], axis=0)
    mod = _modulation(cc, w_ada, b_ada).reshape(depth * MOD_ROWS, 1, 6 * d)

    in_params = (norm1_g.reshape(depth, 1, d), w_main, w_small, w_small_t, gla_w_lr.astype(BF16),
                 gla_b_lr.reshape(depth, 2, 1, qk_a), gdn_conv_w,
                 gdn_a_log.reshape(depth, 1, nd), gdn_dt_bias.reshape(depth, 1, nd),
                 gdn_a_log.reshape(depth, nd, 1), gdn_dt_bias.reshape(depth, nd, 1))
    post_params = (gla_norm_g.reshape(depth, 1, -1), gdn_norm_g.reshape(depth, 1, -1), w_out.astype(BF16),
                   norm2_g.reshape(depth, 1, d), w_ff1.astype(BF16), w_ff2.astype(BF16))

    stream = (ctx, x)
    for l in range(depth):
        last = l == depth - 1
        gqkv, gg, gla, dqkv, dg, small_c, small_r = _in_proj(stream, mod, l, bsz, in_params, dims)
        oaf, oab, obf, obb = _scan(gqkv, gla, dqkv, small_c, small_r, dims)
        args = (mod, l, last, oaf, oab, obf, obb, gg, dg, post_params, final_norm_g, GLA_HEADS, GDN_HEADS)
        xs = _post(stream, "latent", *args)
        if not last:
            stream = (_post(stream, "ctx", *args), xs)
    return xs
```

```python
import functools

import jax
import jax.numpy as jnp
from jax import lax
from jax.experimental import pallas as pl
from jax.experimental.pallas import tpu as pltpu

EPS = 1e-6
GRID_W = 64
GLA_HEADS = 4
GDN_HEADS = 4
GLA_TAU = 16.0
TOKEN_BLOCK = 256
CHUNK = 64
LANES = 128
SUBLANES = 8
POST_SUB = 2
SCAN_ROWS = 2
IN_ROWS = 4
SCAN_ROW_LAG = 6
MOD_ROWS = 16
VMEM_LIMIT = 56 * 1024 * 1024
SCAN_ORDER = (("b", 0), ("b", 1), ("a", 0), ("a", 1), ("b", 2), ("b", 3), ("a", 2), ("a", 3), ("b", 4),
              ("b", 5), ("b", 6), ("b", 7), ("b", 8), ("b", 9), ("a", 4), ("b", 10), ("b", 11), ("b", 12))

F32 = jnp.float32
BF16 = jnp.bfloat16


def _dot(a, b):
    return jnp.dot(a, b, preferred_element_type=F32)


def _dot_nt(a, b):
    return lax.dot_general(a, b, (((1,), (1,)), ((), ())), preferred_element_type=F32)


def _dot_tn(a, b):
    return lax.dot_general(a, b, (((0,), (0,)), ((), ())), preferred_element_type=F32)


def _silu(x):
    return x / (1.0 + jnp.exp(-x))


def _sigmoid(x):
    return 1.0 / (1.0 + jnp.exp(-x))


def _softplus(x):
    return jnp.maximum(x, 0.0) + jnp.log(1.0 + jnp.exp(-jnp.abs(x)))


def _log_sigmoid(x):
    return jnp.minimum(x, 0.0) - jnp.log(1.0 + jnp.exp(-jnp.abs(x)))


def _rms(x):
    return x * lax.rsqrt(jnp.mean(x * x, axis=-1, keepdims=True) + EPS)


def _cumsum_dot(tri_bf16, x, left):
    hi = x.astype(BF16)
    lo = (x - hi.astype(F32)).astype(BF16)
    if left:
        return _dot(tri_bf16, hi) + _dot(tri_bf16, lo)
    return _dot(hi, tri_bf16) + _dot(lo, tri_bf16)


def _tri(n, upper, strict):
    r = lax.broadcasted_iota(jnp.int32, (n, n), 0)
    c = lax.broadcasted_iota(jnp.int32, (n, n), 1)
    if upper:
        m = (r < c) if strict else (r <= c)
    else:
        m = (r > c) if strict else (r >= c)
    return m


def _couple(n, s, upper):
    r = lax.broadcasted_iota(jnp.int32, (n, n), 0)
    c = lax.broadcasted_iota(jnp.int32, (n, n), 1)
    same = (r & ~(2 * s - 1)) == (c & ~(2 * s - 1))
    r_hi, c_hi = (r & s) != 0, (c & s) != 0
    return same & ((~r_hi & c_hi) if upper else (r_hi & ~c_hi))


def _tri_blocks(n, upper):
    r = lax.broadcasted_iota(jnp.int32, (n, n), 0)
    c = lax.broadcasted_iota(jnp.int32, (n, n), 1)
    same = (r & ~(CHUNK - 1)) == (c & ~(CHUNK - 1))
    return same & ((r <= c) if upper else (r >= c))


def _mod_kernel(cc_ref, w_ref, b_ref, o_ref):
    s = _silu(cc_ref[...]).astype(BF16)
    o_ref[0] = _dot(s, w_ref[0].astype(BF16)) + b_ref[0]


def _modulation(cc, w_ada, b_ada):
    depth, d, d6 = w_ada.shape
    nblk = d6 // d
    return pl.pallas_call(
        _mod_kernel,
        out_shape=jax.ShapeDtypeStruct((depth, MOD_ROWS, d6), F32),
        grid=(depth, nblk),
        in_specs=[
            pl.BlockSpec((MOD_ROWS, d), lambda l, n: (0, 0)),
            pl.BlockSpec((1, d, d), lambda l, n: (l, 0, n)),
            pl.BlockSpec((1, 1, d), lambda l, n: (l, 0, n)),
        ],
        out_specs=pl.BlockSpec((1, MOD_ROWS, d), lambda l, n: (l, 0, n)),
        compiler_params=pltpu.CompilerParams(dimension_semantics=("parallel", "parallel")),
        name="adaln_modulation",
    )(cc, w_ada, b_ada.reshape(depth, 1, d6))


def _in_proj_kernel(c_ref, x_ref, *refs, d, qk_a, v_a, rank, conv_dim, qk_b, v_b, nh_b):
    nr = x_ref.shape[0]
    mod_refs = refs[:nr]
    (g1_ref, wm_ref, ws_ref, wst_ref, wlr_ref, blr_ref, cw_ref, arow_ref, dtrow_ref, acol_ref, dtcol_ref,
     gqkv_ref, gg_ref, gla_ref, dqkv_ref, dg_ref, sc_ref, sr_ref) = refs[nr:]
    for bi in range(nr):
        _in_proj_row(bi, c_ref, x_ref, mod_refs[bi], g1_ref, wm_ref, ws_ref, wst_ref, wlr_ref, blr_ref, cw_ref,
                     arow_ref, dtrow_ref, acol_ref, dtcol_ref, gqkv_ref, gg_ref, gla_ref, dqkv_ref, dg_ref,
                     sc_ref, sr_ref, d=d, qk_a=qk_a, v_a=v_a, rank=rank, conv_dim=conv_dim, qk_b=qk_b, v_b=v_b,
                     nh_b=nh_b)


def _in_proj_row(bi, c_ref, x_ref, mod_ref, g1_ref, wm_ref, ws_ref, wst_ref, wlr_ref, blr_ref, cw_ref,
                 arow_ref, dtrow_ref, acol_ref, dtcol_ref,
                 gqkv_ref, gg_ref, gla_ref, dqkv_ref, dg_ref, sc_ref, sr_ref, *, d, qk_a, v_a, rank, conv_dim,
                 qk_b, v_b, nh_b):
    j = pl.program_id(1)
    x = jnp.where(j == 0, c_ref[bi], x_ref[bi])
    m = mod_ref[0]
    sh1, sc1 = m[:, 0:d], m[:, d:2 * d]
    h = _rms(x) * (g1_ref[...] * (1.0 + sc1)) + sh1
    hb = h.astype(BF16)
    o_conv = 2 * qk_a + 2 * v_a
    o_gate = o_conv + conv_dim
    grp = 2 * LANES
    tb = x.shape[0]
    t = lax.broadcasted_iota(jnp.int32, (tb, 1), 0)
    seg_mask = jnp.where(j == 0, tb - 1, GRID_W - 1)
    first = (t & seg_mask) == 0
    last = (t & seg_mask) == seg_mask
    dkh = qk_b // nh_b

    def conv_dot(k):
        return _dot(hb, wm_ref[:, o_conv + k * grp:o_conv + (k + 1) * grp])

    def zero_rows(a, mask, at):
        pieces = []
        for s0 in range(0, tb, GRID_W):
            r0 = s0 + at
            pieces += [a[s0:r0], jnp.where(mask[r0:r0 + SUBLANES], 0.0, a[r0:r0 + SUBLANES]),
                       a[r0 + SUBLANES:s0 + GRID_W]]
        return jnp.concatenate([p for p in pieces if p.shape[0]], axis=0)

    def conv_group(u, k):
        up = zero_rows(pltpu.roll(u, 1, 0), first, 0)
        un = zero_rows(pltpu.roll(u, tb - 1, 0), last, GRID_W - SUBLANES)
        cw = cw_ref[:, k * grp:(k + 1) * grp]
        s = _silu(cw[0:1] * up + cw[1:2] * u + cw[2:3] * un)
        c0 = k * grp
        if c0 < 2 * qk_b:
            for hh in range(grp // dkh):
                sh = s[:, hh * dkh:(hh + 1) * dkh]
                inv = lax.rsqrt(jnp.sum(sh * sh, axis=-1, keepdims=True) + EPS)
                if c0 < qk_b:
                    inv = inv * (float(dkh) ** -0.5)
                dqkv_ref[bi, :, c0 + hh * dkh:c0 + (hh + 1) * dkh] = (sh * inv).astype(BF16)
        else:
            dqkv_ref[bi, :, c0:c0 + grp] = s.astype(BF16)

    def light_group(k):
        c0 = k * grp
        w0 = c0 if c0 < o_conv else o_gate + (c0 - o_conv)
        pa = _dot(hb, wm_ref[:, w0:w0 + grp])
        if c0 < qk_a:
            gqkv_ref[bi, :, c0:c0 + grp] = (pa * (float(qk_a // GLA_HEADS) ** -0.5)).astype(BF16)
        elif c0 < 2 * qk_a + v_a:
            gqkv_ref[bi, :, c0:c0 + grp] = pa.astype(BF16)
        elif c0 < o_conv:
            gg_ref[bi, :, c0 - 2 * qk_a - v_a:c0 - 2 * qk_a - v_a + grp] = pa.astype(BF16)
        else:
            dg_ref[bi, :, c0 - o_conv:c0 - o_conv + grp] = pa.astype(BF16)

    def small_group(ps, pst):
        r_a = ps[:, 0:rank].astype(BF16)
        for dr in range(2):
            lr = _dot(r_a, wlr_ref[dr]) + blr_ref[dr]
            gla_ref[bi, :, dr * qk_a:(dr + 1) * qk_a] = _log_sigmoid(lr) * (1.0 / GLA_TAU)
        nd = 2 * nh_b
        a_c = ps[:, rank:rank + nd]
        b_c = ps[:, rank + nd:rank + 2 * nd]
        sc_ref[bi, :, 0:nd] = -jnp.exp(arow_ref[...]) * _softplus(a_c + dtrow_ref[...])
        sc_ref[bi, :, nd:2 * nd] = _sigmoid(b_c)
        sr_ref[bi, 0, 0:nd, :] = -jnp.exp(acol_ref[...]) * _softplus(pst[0:nd] + dtcol_ref[...])
        sr_ref[bi, 0, nd:2 * nd, :] = _sigmoid(pst[nd:2 * nd])

    n_conv = conv_dim // grp
    n_light = (o_conv + v_b) // grp
    us = {0: conv_dot(0)}
    ps = _dot(hb, ws_ref[...])
    pst = _dot_nt(wst_ref[...], hb)
    if n_conv > 1:
        us[1] = conv_dot(1)
    g_next = 0
    for k in range(n_conv):
        if k + 2 < n_conv:
            us[k + 2] = conv_dot(k + 2)
        conv_group(us.pop(k), k)
        if k == 1 or n_conv == 1:
            small_group(ps, pst)
        todo = (n_light - g_next + (n_conv - k) - 1) // (n_conv - k) if k >= n_conv // 2 else 1
        for _ in range(min(todo, n_light - g_next)):
            light_group(g_next)
            g_next += 1
    while g_next < n_light:
        light_group(g_next)
        g_next += 1


def _in_proj(stream, mod, layer, bsz, params, dims):
    g1, wm, ws, wst, wlr, blr, cw, a_row, dt_row, a_col, dt_col = params
    head, body = stream
    b, _, d = body.shape
    nb = body.shape[1] // TOKEN_BLOCK + 1
    s = nb * TOKEN_BLOCK
    qk_a, v_a, rank, conv_dim, qk_b, v_b, nh_b = dims
    nd = 2 * nh_b
    tb = TOKEN_BLOCK
    lay = lambda a: pl.BlockSpec((None,) + a.shape[1:], lambda i, j: (layer,) + (0,) * (a.ndim - 1))
    nr = IN_ROWS
    assert b % nr == 0
    tok = lambda n: pl.BlockSpec((nr, tb, n), lambda i, j: (i, j, 0))
    mod_specs = [pl.BlockSpec((1, 1, 6 * d), lambda i, j, k=k: (
        layer * MOD_ROWS + jnp.where(j == 0, bsz, i * nr + k), 0, 0)) for k in range(nr)]
    out_shape = [
        jax.ShapeDtypeStruct((b, s, 2 * qk_a + v_a), BF16), jax.ShapeDtypeStruct((b, s, v_a), BF16),
        jax.ShapeDtypeStruct((b, s, 2 * qk_a), F32),
        jax.ShapeDtypeStruct((b, s, conv_dim), BF16), jax.ShapeDtypeStruct((b, s, v_b), BF16),
        jax.ShapeDtypeStruct((b, s, 2 * nd), F32),
        jax.ShapeDtypeStruct((b, nb, 2 * nd, tb), F32),
    ]
    out_specs = [tok(2 * qk_a + v_a), tok(v_a), tok(2 * qk_a), tok(conv_dim), tok(v_b), tok(2 * nd),
                 pl.BlockSpec((nr, 1, 2 * nd, tb), lambda i, j: (i, j, 0, 0))]
    kern = functools.partial(_in_proj_kernel, d=d, qk_a=qk_a, v_a=v_a, rank=rank, conv_dim=conv_dim,
                             qk_b=qk_b, v_b=v_b, nh_b=nh_b)
    return pl.pallas_call(
        kern, out_shape=out_shape, grid=(b // nr, nb),
        in_specs=[
            pl.BlockSpec((nr, tb, d), lambda i, j: (i, 0, 0)),
            pl.BlockSpec((nr, tb, d), lambda i, j: (i, jnp.maximum(j - 1, 0), 0)), *mod_specs, lay(g1),
            lay(wm), lay(ws), lay(wst), lay(wlr), lay(blr), lay(cw), lay(a_row), lay(dt_row), lay(a_col), lay(dt_col),
        ],
        out_specs=out_specs,
        compiler_params=pltpu.CompilerParams(dimension_semantics=("parallel", "parallel"),
                                             vmem_limit_bytes=VMEM_LIMIT),
        name="in_proj",
    )(head, body, *([mod] * nr), g1, wm, ws, wst, wlr, blr, cw, a_row, dt_row, a_col, dt_col)


def _bwd_block(j, nb):
    return jnp.where(j == 0, 0, nb - j)


class _Cols:
    def __init__(self, ref, off):
        self.ref, self.off = ref, off

    def __getitem__(self, idx):
        z, rows, cols = idx
        return self.ref[z, rows, self.off + cols.start:self.off + cols.stop]


def _gla_stages(refs, st_ref, dk, dv, bi):
    tb = refs[0][3].shape[1]
    nchunk = tb // CHUNK
    npair = (GLA_HEADS * dk) // LANES
    hpp = LANES // dk
    lane = lax.broadcasted_iota(jnp.int32, (1, LANES), 1)
    lms = [(lane >= hh * dk) & (lane < (hh + 1) * dk) for hh in range(hpp)]
    mid = CHUNK // 2
    chains = [(dr, pr) for dr in range(2) for pr in range(npair)]
    units = [(dr, pr, c) for (dr, pr) in chains for c in range(nchunk)]
    bcum, q_dec, k_hat, q_mid, k_mid, a_last, sc, intra, dst = ({} for _ in range(9))

    def cumulative_decay():
        for dr in range(2):
            both = _cumsum_dot(_tri_blocks(tb, dr == 1).astype(BF16), refs[dr][3][bi], True)
            for pr in range(npair):
                bcum[dr, pr] = both[:, pr * LANES:(pr + 1) * LANES]

    def decayed_operands():
        for u in units:
            dr, pr, c = u
            rows = slice(c * CHUNK, (c + 1) * CHUNK)
            lanes = slice(pr * LANES, (pr + 1) * LANES)
            last = c * CHUNK + (0 if dr == 1 else CHUNK - 1)
            b = bcum[dr, pr][rows]
            b_last = bcum[dr, pr][last:last + 1]
            b_mid = bcum[dr, pr][c * CHUNK + mid:c * CHUNK + mid + 1]
            qc = refs[dr][0][bi, rows, lanes].astype(F32)
            kc = refs[dr][1][bi, rows, lanes].astype(F32)
            q_dec[u] = qc * jnp.exp(b)
            k_hat[u] = (kc * jnp.exp(b_last - b)).astype(BF16)
            q_mid[u] = qc * jnp.exp(b - b_mid)
            k_mid[u] = (kc * jnp.exp(b_mid - b)).astype(BF16)
            a_last[u] = jnp.exp(b_last)

    def per_head(m):
        return jnp.concatenate([jnp.where(lm, m, 0.0).astype(BF16) for lm in lms], axis=0)

    def scores():
        for u in units:
            sc[u] = _dot_nt(per_head(q_mid[u]), k_mid[u])

    def intra_and_increments():
        for u in units:
            dr, pr, c = u
            rows = slice(c * CHUNK, (c + 1) * CHUNK)
            causal = _tri(CHUNK, dr == 1, False)
            v_grp = refs[dr][2][bi, rows, pr * hpp * dv:(pr + 1) * hpp * dv]
            dh = _dot_tn(v_grp, k_hat[u])
            acc = None
            for hh in range(hpp):
                pm = jnp.where(causal, sc[u][hh * CHUNK:(hh + 1) * CHUNK], 0.0).astype(BF16)
                intra[u, hh] = _dot(pm, v_grp[:, hh * dv:(hh + 1) * dv])
                part = dh[hh * dv:(hh + 1) * dv]
                acc = part if acc is None else jnp.where(lms[hh], part, acc)
            dst[u] = acc

    def recurrence():
        st = {ch: st_ref[bi, ch[0], ch[1]] for ch in chains}
        for ci in range(nchunk):
            for ch in chains:
                dr, pr = ch
                c = nchunk - 1 - ci if dr == 1 else ci
                u = (dr, pr, c)
                rows = slice(c * CHUNK, (c + 1) * CHUNK)
                inter = _dot_nt(per_head(q_dec[u]), st[ch].astype(BF16))
                for hh in range(hpp):
                    head = pr * hpp + hh
                    refs[dr][4][bi, rows, head * dv:(head + 1) * dv] = (
                        intra[u, hh] + inter[hh * CHUNK:(hh + 1) * CHUNK]).astype(BF16)
                st[ch] = st[ch] * a_last[u] + dst[u]
        for ch in chains:
            st_ref[bi, ch[0], ch[1]] = st[ch]

    return [cumulative_decay, decayed_operands, scores, intra_and_increments, recurrence]


def _gdn_stages(refs, st_ref, dk, dv, nh, bi):
    tb = refs[0][3].shape[1]
    nchunk = tb // CHUNK
    eye = _tri(CHUNK, False, False) & _tri(CHUNK, True, False)
    chains = [(dr, hd) for dr in range(2) for hd in range(nh)]
    units = [(dr, c, hd) for dr in range(2) for c in range(nchunk) for hd in range(nh)]
    g_cols, g_rows, small, kk, qk, x, aqk, rhs, q_dec, k_dec, a_last = ({} for _ in range(11))
    t, uw, kd_uw, aq_uw, lhs, decay = ({} for _ in range(6))

    def cumulative_decay():
        for dr in range(2):
            small[dr] = refs[dr][3][bi]
            g_cols[dr] = _cumsum_dot(_tri_blocks(tb, dr == 1).astype(BF16), small[dr], True)
            g_rows[dr] = _cumsum_dot(_tri_blocks(tb, dr != 1).astype(BF16), refs[dr][4][bi, 0], False)

    def grams():
        for u in units:
            dr, c, hd = u
            rows = slice(c * CHUNK, (c + 1) * CHUNK)
            kh = refs[dr][1][bi, rows, hd * dk:(hd + 1) * dk]
            qh = refs[dr][0][bi, rows, hd * dk:(hd + 1) * dk]
            kq = _dot_nt(jnp.concatenate([kh, qh], axis=0), kh)
            kk[u], qk[u] = kq[:CHUNK], kq[CHUNK:]

    def solve_operands():
        for u in units:
            dr, c, hd = u
            rows = slice(c * CHUNK, (c + 1) * CHUNK)
            ia = dr * nh + hd
            ib = 2 * nh + dr * nh + hd
            g_c = g_cols[dr][rows, ia:ia + 1]
            g_r = g_rows[dr][ia:ia + 1, rows]
            be_c = small[dr][rows, ib:ib + 1]
            decay[u] = jnp.exp(jnp.minimum(g_c - g_r, 0.0))
            x[u] = jnp.where(_tri(CHUNK, dr == 1, True), be_c * kk[u] * decay[u], 0.0)

    def other_operands():
        for u in units:
            dr, c, hd = u
            rows = slice(c * CHUNK, (c + 1) * CHUNK)
            ia = dr * nh + hd
            ib = 2 * nh + dr * nh + hd
            last = c * CHUNK + (0 if dr == 1 else CHUNK - 1)
            g_c = g_cols[dr][rows, ia:ia + 1]
            be_c = small[dr][rows, ib:ib + 1]
            g_last = g_cols[dr][last:last + 1, ia:ia + 1]
            aqk[u] = jnp.where(_tri(CHUNK, dr == 1, False), qk[u] * decay[u], 0.0).astype(BF16)
            eg = jnp.exp(g_c)
            qh = refs[dr][0][bi, rows, hd * dk:(hd + 1) * dk].astype(F32)
            khf = refs[dr][1][bi, rows, hd * dk:(hd + 1) * dk].astype(F32)
            vh = refs[dr][2][bi, rows, hd * dv:(hd + 1) * dv].astype(F32)
            rhs[u] = jnp.concatenate([(be_c * vh).astype(BF16), ((be_c * eg) * khf).astype(BF16)], axis=1)
            q_dec[u] = qh * eg
            k_dec[u] = khf * jnp.exp(g_last - g_c)
            a_last[u] = jnp.exp(g_last)

    def inverse_start():
        for u in units:
            t[u] = jnp.where(eye, 1.0, jnp.where(_couple(CHUNK, 1, u[0] == 1), -x[u], 0.0))

    def inverse_level(s):
        a = {u: jnp.where(_couple(CHUNK, s, u[0] == 1), x[u], 0.0).astype(BF16) for u in units}
        tb16 = {u: t[u].astype(BF16) for u in units}
        ta = {u: _dot(tb16[u], a[u]).astype(BF16) for u in units}
        for u in units:
            t[u] = t[u] - _dot(ta[u], tb16[u])

    levels = []
    s = 2
    while s < CHUNK:
        levels.append(functools.partial(inverse_level, s))
        s *= 2

    def solve():
        for u in units:
            uw[u] = _dot(t[u].astype(BF16), rhs[u]).astype(BF16)

    def fold():
        for u in units:
            both = _dot(jnp.concatenate([k_dec[u].T.astype(BF16), aqk[u]], axis=0), uw[u])
            kd_uw[u], aq_uw[u] = both[:dk], both[dk:]
        for u in units:
            lhs[u] = jnp.concatenate([(-kd_uw[u][:, dv:]).astype(BF16),
                                      (q_dec[u] - aq_uw[u][:, dv:]).astype(BF16)], axis=0)

    def recurrence():
        st = {ch: st_ref[bi, ch[0], ch[1]] for ch in chains}
        for ci in range(nchunk):
            for ch in chains:
                dr, hd = ch
                c = nchunk - 1 - ci if dr == 1 else ci
                u = (dr, c, hd)
                rows = slice(c * CHUNK, (c + 1) * CHUNK)
                res = _dot(lhs[u], st[ch].astype(BF16))
                refs[dr][5][bi, rows, hd * dv:(hd + 1) * dv] = (res[dk:] + aq_uw[u][:, :dv]).astype(BF16)
                st[ch] = st[ch] * a_last[u] + res[:dk] + kd_uw[u][:, :dv]
        for ch in chains:
            st_ref[bi, ch[0], ch[1]] = st[ch]

    return [cumulative_decay, grams, solve_operands, other_operands, inverse_start, *levels,
            solve, fold, recurrence]


def _scan_kernel(af, alf, ab, alb, bf, bcf, brf, bb, bcb, brb,
                 oaf, oab, obf, obb, sta_ref, stb_ref, *, dk_a, dv_a, dk_b, dv_b, nh_b):
    @pl.when(pl.program_id(1) == 0)
    def _():
        sta_ref[...] = jnp.zeros_like(sta_ref)
        stb_ref[...] = jnp.zeros_like(stb_ref)

    qk_a, qk_b = GLA_HEADS * dk_a, nh_b * dk_b
    qkv = lambda ref, qk: (_Cols(ref, 0), _Cols(ref, qk), _Cols(ref, 2 * qk))
    refs_a = ((*qkv(af, qk_a), alf, oaf), (*qkv(ab, qk_a), alb, oab))
    refs_b = ((*qkv(bf, qk_b), bcf, brf, obf), (*qkv(bb, qk_b), bcb, brb, obb))
    stages = {}
    for bi in range(af.shape[0]):
        stages["a", bi] = _gla_stages(refs_a, sta_ref, dk_a, dv_a, bi)
        stages["b", bi] = _gdn_stages(refs_b, stb_ref, dk_b, dv_b, nh_b, bi)
    lag = SCAN_ROW_LAG
    for pos in range(len(SCAN_ORDER) + lag * (af.shape[0] - 1)):
        for bi in range(af.shape[0]):
            if 0 <= pos - lag * bi < len(SCAN_ORDER):
                name, idx = SCAN_ORDER[pos - lag * bi]
                stages[name, bi][idx]()


def _scan(gqkv, gla, dqkv, small_c, small_r, dims):
    b, s, _ = gqkv.shape
    qk_a, v_a, _, _, qk_b, v_b, _ = dims
    nb = s // TOKEN_BLOCK
    tb = TOKEN_BLOCK
    ns = small_c.shape[2]
    nr = SCAN_ROWS
    assert b % nr == 0
    fwd = lambda n, col=0: pl.BlockSpec((nr, tb, n), lambda i, j: (i, j, col))
    bwd = lambda n, col=0: pl.BlockSpec((nr, tb, n), lambda i, j: (i, _bwd_block(j, nb), col))
    rfwd = pl.BlockSpec((nr, 1, ns, tb), lambda i, j: (i, j, 0, 0))
    rbwd = pl.BlockSpec((nr, 1, ns, tb), lambda i, j: (i, _bwd_block(j, nb), 0, 0))
    dk_a, dv_a = qk_a // GLA_HEADS, v_a // GLA_HEADS
    dk_b, dv_b = qk_b // GDN_HEADS, v_b // GDN_HEADS
    out = lambda v: jax.ShapeDtypeStruct((b, s, v), BF16)
    return pl.pallas_call(
        functools.partial(_scan_kernel, dk_a=dk_a, dv_a=dv_a, dk_b=dk_b, dv_b=dv_b, nh_b=GDN_HEADS),
        out_shape=[out(v_a), out(v_a), out(v_b), out(v_b)],
        grid=(b // nr, nb),
        in_specs=[fwd(2 * qk_a + v_a), fwd(qk_a, 0), bwd(2 * qk_a + v_a), bwd(qk_a, 1),
                  fwd(2 * qk_b + v_b), fwd(ns), rfwd, bwd(2 * qk_b + v_b), bwd(ns), rbwd],
        out_specs=[fwd(v_a), bwd(v_a), fwd(v_b), bwd(v_b)],
        scratch_shapes=[pltpu.VMEM((nr, 2, qk_a // LANES, dv_a, LANES), F32),
                        pltpu.VMEM((nr, 2, GDN_HEADS, dk_b, dv_b), F32)],
        compiler_params=pltpu.CompilerParams(dimension_semantics=("parallel", "arbitrary"),
                                             vmem_limit_bytes=VMEM_LIMIT),
        name="scan",
    )(gqkv, gla, gqkv, gla, dqkv, small_c, small_r, dqkv, small_c, small_r)


def _post_kernel(x_ref, mod_ref, *refs, d, nh_a, nh_b, final, nsub):
    mix = [refs[6 * i:6 * i + 6] for i in range(nsub)]
    na_ref, nb_ref, wo_ref, g2_ref, w1_ref, w2_ref, fg_ref, o_ref = refs[6 * nsub:]
    tb = x_ref.shape[1] // nsub
    m = mod_ref[0]
    gt1 = m[:, 2 * d:3 * d]
    sh2, sc2, gt2 = m[:, 3 * d:4 * d], m[:, 4 * d:5 * d], m[:, 5 * d:6 * d]
    slab = 4 * LANES
    x1, hb, act = {}, {}, {}

    def merge_and_project(i):
        af_ref, ab_ref, bf_ref, bb_ref, ga_ref, gb_ref = mix[i]
        acc = None
        row = 0
        for (f_ref, b_ref, g_ref, n_ref, nh) in ((af_ref, ab_ref, ga_ref, na_ref, nh_a),
                                                 (bf_ref, bb_ref, gb_ref, nb_ref, nh_b)):
            width = f_ref.shape[2]
            hv = width // nh
            for c0 in range(0, width, slab):
                o = f_ref[0, :, c0:c0 + slab].astype(F32) + b_ref[0, :, c0:c0 + slab].astype(F32)
                gate = _silu(g_ref[0, :, c0:c0 + slab].astype(F32))
                parts = [(_rms(o[:, k:k + hv]) * n_ref[...] * gate[:, k:k + hv]).astype(BF16)
                         for k in range(0, slab, hv)]
                part = _dot(jnp.concatenate(parts, axis=1), wo_ref[row:row + slab, :])
                acc = part if acc is None else acc + part
                row += slab
        x1[i] = x_ref[0, i * tb:(i + 1) * tb, :] + gt1 * acc
        hb[i] = (_rms(x1[i]) * (g2_ref[...] * (1.0 + sc2)) + sh2).astype(BF16)

    def mlp_up(i):
        a = jnp.maximum(_dot(hb[i], w1_ref[...]), 0.0)
        act[i] = (a * a).astype(BF16)

    def mlp_down(i):
        x2 = x1[i] + gt2 * _dot(act[i], w2_ref[...])
        if final:
            x2 = _rms(x2) * fg_ref[...]
        o_ref[0, i * tb:(i + 1) * tb, :] = x2

    merge_and_project(0)
    for i in range(nsub):
        mlp_up(i)
        if i + 1 < nsub:
            merge_and_project(i + 1)
        mlp_down(i)


def _post(stream, part, mod, layer, final, oaf, oab, obf, obb, gg, dg, params, fg, nh_a, nh_b):
    na, nb_, wo, g2, w1, w2 = params
    ctx_arr, lat_arr = stream
    b, t, d = lat_arr.shape
    tb = TOKEN_BLOCK
    ctx = part == "ctx"
    nsub = 1 if ctx else POST_SUB
    nstep = 1 if ctx else t // (nsub * tb)
    assert ctx or t % (nsub * tb) == 0
    first = 0 if ctx else 1
    lay = lambda a: pl.BlockSpec((None,) + a.shape[1:], lambda i, j: (layer,) + (0,) * (a.ndim - 1))
    weight = lambda a: pl.BlockSpec((None,) + a.shape[1:], lambda i, j: (layer,) + (0,) * (a.ndim - 1),
                                    pipeline_mode=pl.Buffered(1))
    bsz = b
    mod_spec = pl.BlockSpec((1, 1, 6 * d), lambda i, j: (layer * MOD_ROWS + (bsz if ctx else i), 0, 0))
    va, vb = oaf.shape[2], obf.shape[2]
    mix_specs, mix_args = [], []
    for k in range(nsub):
        blk = lambda n, k=k: pl.BlockSpec((1, tb, n), lambda i, j: (i, first + j * nsub + k, 0))
        mix_specs += [blk(va), blk(va), blk(vb), blk(vb), blk(va), blk(vb)]
        mix_args += [oaf, oab, obf, obb, gg, dg]
    return pl.pallas_call(
        functools.partial(_post_kernel, d=d, nh_a=nh_a, nh_b=nh_b, final=final, nsub=nsub),
        out_shape=jax.ShapeDtypeStruct((b, nstep * nsub * tb, d), F32),
        grid=(b, nstep),
        in_specs=[pl.BlockSpec((1, nsub * tb, d), lambda i, j: (i, j, 0)), mod_spec, *mix_specs,
                  lay(na), lay(nb_), weight(wo), lay(g2), weight(w1), weight(w2),
                  pl.BlockSpec((1, d), lambda i, j: (0, 0))],
        out_specs=pl.BlockSpec((1, nsub * tb, d), lambda i, j: (i, j, 0)),
        compiler_params=pltpu.CompilerParams(dimension_semantics=("parallel", "parallel"),
                                             vmem_limit_bytes=VMEM_LIMIT),
        name="post_ctx" if ctx else "post",
    )(ctx_arr if ctx else lat_arr, mod, *mix_args, na, nb_, wo, g2, w1, w2, fg.reshape(1, d))


def kernel(x, c, ctx, c_ctx, w_ada, b_ada, norm1_g, norm2_g, w_in, gla_w_lr, gla_b_lr, gdn_conv_w,
           gdn_a_log, gdn_dt_bias, gla_norm_g, gdn_norm_g, w_out, w_ff1, w_ff2, final_norm_g):
    bsz, t, d = x.shape
    depth = w_ada.shape[0]
    assert ctx.shape[1] == TOKEN_BLOCK and t % TOKEN_BLOCK == 0 and TOKEN_BLOCK % GRID_W == 0
    assert bsz + 1 <= MOD_ROWS
    qk_a = gla_w_lr.shape[3]
    rank = gla_w_lr.shape[2]
    v_a = gla_norm_g.shape[1] * GLA_HEADS
    conv_dim = gdn_conv_w.shape[2]
    v_b = gdn_norm_g.shape[1] * GDN_HEADS
    qk_b = (conv_dim - v_b) // 2
    nd = 2 * GDN_HEADS
    dims = (qk_a, v_a, rank, conv_dim, qk_b, v_b, GDN_HEADS)

    o_r = 2 * qk_a + 2 * v_a
    o_c = o_r + rank
    o_g = o_c + conv_dim
    o_s = o_g + v_b
    assert w_in.shape[2] == o_s + 2 * nd
    w_main = jnp.concatenate([w_in[:, :, :o_r], w_in[:, :, o_c:o_s]], axis=2).astype(BF16)
    w_sm = jnp.concatenate([w_in[:, :, o_r:o_c], w_in[:, :, o_s:]], axis=2)
    w_small = jnp.pad(w_sm, ((0, 0), (0, 0), (0, LANES - w_sm.shape[2]))).astype(BF16)
    w_small_t = jnp.swapaxes(w_in[:, :, o_s:], 1, 2).astype(BF16)

    cc = jnp.concatenate([c, c_ctx[None, :], jnp.zeros((MOD_ROWS - bsz - 1, d), F32)], axis=0)
    mod = _modulation(cc, w_ada, b_ada).reshape(depth * MOD_ROWS, 1, 6 * d)

    in_params = (norm1_g.reshape(depth, 1, d), w_main, w_small, w_small_t, gla_w_lr.astype(BF16),
                 gla_b_lr.reshape(depth, 2, 1, qk_a), gdn_conv_w,
                 gdn_a_log.reshape(depth, 1, nd), gdn_dt_bias.reshape(depth, 1, nd),
                 gdn_a_log.reshape(depth, nd, 1), gdn_dt_bias.reshape(depth, nd, 1))
    post_params = (gla_norm_g.reshape(depth, 1, -1), gdn_norm_g.reshape(depth, 1, -1), w_out.astype(BF16),
                   norm2_g.reshape(depth, 1, d), w_ff1.astype(BF16), w_ff2.astype(BF16))

    stream = (ctx, x)
    for l in range(depth):
        last = l == depth - 1
        gqkv, gg, gla, dqkv, dg, small_c, small_r = _in_proj(stream, mod, l, bsz, in_params, dims)
        oaf, oab, obf, obb = _scan(gqkv, gla, dqkv, small_c, small_r, dims)
        args = (mod, l, last, oaf, oab, obf, obb, gg, dg, post_params, final_norm_g, GLA_HEADS, GDN_HEADS)
        xs = _post(stream, "latent", *args)
        if not last:
            stream = (_post(stream, "ctx", *args), xs)
    return xs
```

```python
import functools

import jax
import jax.numpy as jnp
from jax import lax
from jax.experimental import pallas as pl
from jax.experimental.pallas import tpu as pltpu

EPS = 1e-6
GRID_W = 64
GLA_HEADS = 4
GDN_HEADS = 4
GLA_TAU = 16.0
TOKEN_BLOCK = 256
CHUNK = 64
LANES = 128
SUBLANES = 8
POST_SUB = 2
SCAN_ROWS = 2
IN_ROWS = 4
SCAN_ROW_LAG = 4
MOD_ROWS = 16
VMEM_LIMIT = 56 * 1024 * 1024
SCAN_ORDER = (("b", 0), ("b", 1), ("a", 0), ("a", 1), ("b", 2), ("b", 3), ("a", 2), ("a", 3), ("b", 4),
              ("b", 5), ("b", 6), ("b", 7), ("b", 8), ("b", 9), ("a", 4), ("b", 10), ("b", 11), ("b", 12))

F32 = jnp.float32
BF16 = jnp.bfloat16


def _dot(a, b):
    return jnp.dot(a, b, preferred_element_type=F32)


def _dot_nt(a, b):
    return lax.dot_general(a, b, (((1,), (1,)), ((), ())), preferred_element_type=F32)


def _dot_tn(a, b):
    return lax.dot_general(a, b, (((0,), (0,)), ((), ())), preferred_element_type=F32)


def _silu(x):
    return x / (1.0 + jnp.exp(-x))


def _sigmoid(x):
    return 1.0 / (1.0 + jnp.exp(-x))


def _softplus(x):
    return jnp.maximum(x, 0.0) + jnp.log(1.0 + jnp.exp(-jnp.abs(x)))


def _log_sigmoid(x):
    return jnp.minimum(x, 0.0) - jnp.log(1.0 + jnp.exp(-jnp.abs(x)))


def _rms(x):
    return x * lax.rsqrt(jnp.mean(x * x, axis=-1, keepdims=True) + EPS)


def _cumsum_dot(tri_bf16, x, left):
    hi = x.astype(BF16)
    lo = (x - hi.astype(F32)).astype(BF16)
    if left:
        return _dot(tri_bf16, hi) + _dot(tri_bf16, lo)
    return _dot(hi, tri_bf16) + _dot(lo, tri_bf16)


def _tri(n, upper, strict):
    r = lax.broadcasted_iota(jnp.int32, (n, n), 0)
    c = lax.broadcasted_iota(jnp.int32, (n, n), 1)
    if upper:
        m = (r < c) if strict else (r <= c)
    else:
        m = (r > c) if strict else (r >= c)
    return m


def _couple(n, s, upper):
    r = lax.broadcasted_iota(jnp.int32, (n, n), 0)
    c = lax.broadcasted_iota(jnp.int32, (n, n), 1)
    same = (r & ~(2 * s - 1)) == (c & ~(2 * s - 1))
    r_hi, c_hi = (r & s) != 0, (c & s) != 0
    return same & ((~r_hi & c_hi) if upper else (r_hi & ~c_hi))


def _tri_blocks(n, upper):
    r = lax.broadcasted_iota(jnp.int32, (n, n), 0)
    c = lax.broadcasted_iota(jnp.int32, (n, n), 1)
    same = (r & ~(CHUNK - 1)) == (c & ~(CHUNK - 1))
    return same & ((r <= c) if upper else (r >= c))


def _mod_kernel(cc_ref, w_ref, b_ref, o_ref):
    s = _silu(cc_ref[...]).astype(BF16)
    o_ref[0] = _dot(s, w_ref[0].astype(BF16)) + b_ref[0]


def _modulation(cc, w_ada, b_ada):
    depth, d, d6 = w_ada.shape
    nblk = d6 // d
    return pl.pallas_call(
        _mod_kernel,
        out_shape=jax.ShapeDtypeStruct((depth, MOD_ROWS, d6), F32),
        grid=(depth, nblk),
        in_specs=[
            pl.BlockSpec((MOD_ROWS, d), lambda l, n: (0, 0)),
            pl.BlockSpec((1, d, d), lambda l, n: (l, 0, n)),
            pl.BlockSpec((1, 1, d), lambda l, n: (l, 0, n)),
        ],
        out_specs=pl.BlockSpec((1, MOD_ROWS, d), lambda l, n: (l, 0, n)),
        compiler_params=pltpu.CompilerParams(dimension_semantics=("parallel", "parallel")),
        name="adaln_modulation",
    )(cc, w_ada, b_ada.reshape(depth, 1, d6))


def _in_proj_kernel(c_ref, x_ref, *refs, d, qk_a, v_a, rank, conv_dim, qk_b, v_b, nh_b):
    nr = x_ref.shape[0]
    mod_refs = refs[:nr]
    (g1_ref, wm_ref, ws_ref, wst_ref, wlr_ref, blr_ref, cw_ref, arow_ref, dtrow_ref, acol_ref, dtcol_ref,
     gqkv_ref, gg_ref, gla_ref, dqkv_ref, dg_ref, sc_ref, sr_ref) = refs[nr:]
    for bi in range(nr):
        _in_proj_row(bi, c_ref, x_ref, mod_refs[bi], g1_ref, wm_ref, ws_ref, wst_ref, wlr_ref, blr_ref, cw_ref,
                     arow_ref, dtrow_ref, acol_ref, dtcol_ref, gqkv_ref, gg_ref, gla_ref, dqkv_ref, dg_ref,
                     sc_ref, sr_ref, d=d, qk_a=qk_a, v_a=v_a, rank=rank, conv_dim=conv_dim, qk_b=qk_b, v_b=v_b,
                     nh_b=nh_b)


def _in_proj_row(bi, c_ref, x_ref, mod_ref, g1_ref, wm_ref, ws_ref, wst_ref, wlr_ref, blr_ref, cw_ref,
                 arow_ref, dtrow_ref, acol_ref, dtcol_ref,
                 gqkv_ref, gg_ref, gla_ref, dqkv_ref, dg_ref, sc_ref, sr_ref, *, d, qk_a, v_a, rank, conv_dim,
                 qk_b, v_b, nh_b):
    j = pl.program_id(1)
    x = jnp.where(j == 0, c_ref[bi], x_ref[bi])
    m = mod_ref[0]
    sh1, sc1 = m[:, 0:d], m[:, d:2 * d]
    h = _rms(x) * (g1_ref[...] * (1.0 + sc1)) + sh1
    hb = h.astype(BF16)
    o_conv = 2 * qk_a + 2 * v_a
    o_gate = o_conv + conv_dim
    grp = 2 * LANES
    tb = x.shape[0]
    t = lax.broadcasted_iota(jnp.int32, (tb, 1), 0)
    seg_mask = jnp.where(j == 0, tb - 1, GRID_W - 1)
    first = (t & seg_mask) == 0
    last = (t & seg_mask) == seg_mask
    dkh = qk_b // nh_b

    def conv_dot(k):
        return _dot(hb, wm_ref[:, o_conv + k * grp:o_conv + (k + 1) * grp])

    def zero_rows(a, mask, at):
        pieces = []
        for s0 in range(0, tb, GRID_W):
            r0 = s0 + at
            pieces += [a[s0:r0], jnp.where(mask[r0:r0 + SUBLANES], 0.0, a[r0:r0 + SUBLANES]),
                       a[r0 + SUBLANES:s0 + GRID_W]]
        return jnp.concatenate([p for p in pieces if p.shape[0]], axis=0)

    def conv_group(u, k):
        up = zero_rows(pltpu.roll(u, 1, 0), first, 0)
        un = zero_rows(pltpu.roll(u, tb - 1, 0), last, GRID_W - SUBLANES)
        cw = cw_ref[:, k * grp:(k + 1) * grp]
        s = _silu(cw[0:1] * up + cw[1:2] * u + cw[2:3] * un)
        c0 = k * grp
        if c0 < 2 * qk_b:
            for hh in range(grp // dkh):
                sh = s[:, hh * dkh:(hh + 1) * dkh]
                inv = lax.rsqrt(jnp.sum(sh * sh, axis=-1, keepdims=True) + EPS)
                if c0 < qk_b:
                    inv = inv * (float(dkh) ** -0.5)
                dqkv_ref[bi, :, c0 + hh * dkh:c0 + (hh + 1) * dkh] = (sh * inv).astype(BF16)
        else:
            dqkv_ref[bi, :, c0:c0 + grp] = s.astype(BF16)

    def light_group(k):
        c0 = k * grp
        w0 = c0 if c0 < o_conv else o_gate + (c0 - o_conv)
        pa = _dot(hb, wm_ref[:, w0:w0 + grp])
        if c0 < qk_a:
            gqkv_ref[bi, :, c0:c0 + grp] = (pa * (float(qk_a // GLA_HEADS) ** -0.5)).astype(BF16)
        elif c0 < 2 * qk_a + v_a:
            gqkv_ref[bi, :, c0:c0 + grp] = pa.astype(BF16)
        elif c0 < o_conv:
            gg_ref[bi, :, c0 - 2 * qk_a - v_a:c0 - 2 * qk_a - v_a + grp] = pa.astype(BF16)
        else:
            dg_ref[bi, :, c0 - o_conv:c0 - o_conv + grp] = pa.astype(BF16)

    def small_group(ps, pst):
        r_a = ps[:, 0:rank].astype(BF16)
        for dr in range(2):
            lr = _dot(r_a, wlr_ref[dr]) + blr_ref[dr]
            gla_ref[bi, :, dr * qk_a:(dr + 1) * qk_a] = _log_sigmoid(lr) * (1.0 / GLA_TAU)
        nd = 2 * nh_b
        a_c = ps[:, rank:rank + nd]
        b_c = ps[:, rank + nd:rank + 2 * nd]
        sc_ref[bi, :, 0:nd] = -jnp.exp(arow_ref[...]) * _softplus(a_c + dtrow_ref[...])
        sc_ref[bi, :, nd:2 * nd] = _sigmoid(b_c)
        sr_ref[bi, 0, 0:nd, :] = -jnp.exp(acol_ref[...]) * _softplus(pst[0:nd] + dtcol_ref[...])
        sr_ref[bi, 0, nd:2 * nd, :] = _sigmoid(pst[nd:2 * nd])

    n_conv = conv_dim // grp
    n_light = (o_conv + v_b) // grp
    us = {0: conv_dot(0)}
    ps = _dot(hb, ws_ref[...])
    pst = _dot_nt(wst_ref[...], hb)
    if n_conv > 1:
        us[1] = conv_dot(1)
    g_next = 0
    for k in range(n_conv):
        if k + 2 < n_conv:
            us[k + 2] = conv_dot(k + 2)
        conv_group(us.pop(k), k)
        if k == 1 or n_conv == 1:
            small_group(ps, pst)
        todo = (n_light - g_next + (n_conv - k) - 1) // (n_conv - k) if k >= n_conv // 2 else 1
        for _ in range(min(todo, n_light - g_next)):
            light_group(g_next)
            g_next += 1
    while g_next < n_light:
        light_group(g_next)
        g_next += 1


def _in_proj(stream, mod, layer, bsz, params, dims):
    g1, wm, ws, wst, wlr, blr, cw, a_row, dt_row, a_col, dt_col = params
    head, body = stream
    b, _, d = body.shape
    nb = body.shape[1] // TOKEN_BLOCK + 1
    s = nb * TOKEN_BLOCK
    qk_a, v_a, rank, conv_dim, qk_b, v_b, nh_b = dims
    nd = 2 * nh_b
    tb = TOKEN_BLOCK
    lay = lambda a: pl.BlockSpec((None,) + a.shape[1:], lambda i, j: (layer,) + (0,) * (a.ndim - 1))
    nr = IN_ROWS
    assert b % nr == 0
    tok = lambda n: pl.BlockSpec((nr, tb, n), lambda i, j: (i, j, 0))
    mod_specs = [pl.BlockSpec((1, 1, 6 * d), lambda i, j, k=k: (
        layer * MOD_ROWS + jnp.where(j == 0, bsz, i * nr + k), 0, 0)) for k in range(nr)]
    out_shape = [
        jax.ShapeDtypeStruct((b, s, 2 * qk_a + v_a), BF16), jax.ShapeDtypeStruct((b, s, v_a), BF16),
        jax.ShapeDtypeStruct((b, s, 2 * qk_a), F32),
        jax.ShapeDtypeStruct((b, s, conv_dim), BF16), jax.ShapeDtypeStruct((b, s, v_b), BF16),
        jax.ShapeDtypeStruct((b, s, 2 * nd), F32),
        jax.ShapeDtypeStruct((b, nb, 2 * nd, tb), F32),
    ]
    out_specs = [tok(2 * qk_a + v_a), tok(v_a), tok(2 * qk_a), tok(conv_dim), tok(v_b), tok(2 * nd),
                 pl.BlockSpec((nr, 1, 2 * nd, tb), lambda i, j: (i, j, 0, 0))]
    kern = functools.partial(_in_proj_kernel, d=d, qk_a=qk_a, v_a=v_a, rank=rank, conv_dim=conv_dim,
                             qk_b=qk_b, v_b=v_b, nh_b=nh_b)
    return pl.pallas_call(
        kern, out_shape=out_shape, grid=(b // nr, nb),
        in_specs=[
            pl.BlockSpec((nr, tb, d), lambda i, j: (i, 0, 0)),
            pl.BlockSpec((nr, tb, d), lambda i, j: (i, jnp.maximum(j - 1, 0), 0)), *mod_specs, lay(g1),
            lay(wm), lay(ws), lay(wst), lay(wlr), lay(blr), lay(cw), lay(a_row), lay(dt_row), lay(a_col), lay(dt_col),
        ],
        out_specs=out_specs,
        compiler_params=pltpu.CompilerParams(dimension_semantics=("parallel", "parallel"),
                                             vmem_limit_bytes=VMEM_LIMIT),
        name="in_proj",
    )(head, body, *([mod] * nr), g1, wm, ws, wst, wlr, blr, cw, a_row, dt_row, a_col, dt_col)


def _bwd_block(j, nb):
    return jnp.where(j == 0, 0, nb - j)


class _Cols:
    def __init__(self, ref, off):
        self.ref, self.off = ref, off

    def __getitem__(self, idx):
        z, rows, cols = idx
        return self.ref[z, rows, self.off + cols.start:self.off + cols.stop]


def _gla_stages(refs, st_ref, dk, dv, bi):
    tb = refs[0][3].shape[1]
    nchunk = tb // CHUNK
    npair = (GLA_HEADS * dk) // LANES
    hpp = LANES // dk
    lane = lax.broadcasted_iota(jnp.int32, (1, LANES), 1)
    lms = [(lane >= hh * dk) & (lane < (hh + 1) * dk) for hh in range(hpp)]
    mid = CHUNK // 2
    chains = [(dr, pr) for dr in range(2) for pr in range(npair)]
    units = [(dr, pr, c) for (dr, pr) in chains for c in range(nchunk)]
    bcum, q_dec, k_hat, q_mid, k_mid, a_last, sc, intra, dst = ({} for _ in range(9))

    def cumulative_decay():
        for dr in range(2):
            both = _cumsum_dot(_tri_blocks(tb, dr == 1).astype(BF16), refs[dr][3][bi], True)
            for pr in range(npair):
                bcum[dr, pr] = both[:, pr * LANES:(pr + 1) * LANES]

    def decayed_operands():
        for u in units:
            dr, pr, c = u
            rows = slice(c * CHUNK, (c + 1) * CHUNK)
            lanes = slice(pr * LANES, (pr + 1) * LANES)
            last = c * CHUNK + (0 if dr == 1 else CHUNK - 1)
            b = bcum[dr, pr][rows]
            b_last = bcum[dr, pr][last:last + 1]
            b_mid = bcum[dr, pr][c * CHUNK + mid:c * CHUNK + mid + 1]
            qc = refs[dr][0][bi, rows, lanes].astype(F32)
            kc = refs[dr][1][bi, rows, lanes].astype(F32)
            q_dec[u] = qc * jnp.exp(b)
            k_hat[u] = (kc * jnp.exp(b_last - b)).astype(BF16)
            q_mid[u] = qc * jnp.exp(b - b_mid)
            k_mid[u] = (kc * jnp.exp(b_mid - b)).astype(BF16)
            a_last[u] = jnp.exp(b_last)

    def per_head(m):
        return jnp.concatenate([jnp.where(lm, m, 0.0).astype(BF16) for lm in lms], axis=0)

    def scores():
        for u in units:
            sc[u] = _dot_nt(per_head(q_mid[u]), k_mid[u])

    def intra_and_increments():
        for u in units:
            dr, pr, c = u
            rows = slice(c * CHUNK, (c + 1) * CHUNK)
            causal = _tri(CHUNK, dr == 1, False)
            v_grp = refs[dr][2][bi, rows, pr * hpp * dv:(pr + 1) * hpp * dv]
            dh = _dot_tn(v_grp, k_hat[u])
            acc = None
            for hh in range(hpp):
                pm = jnp.where(causal, sc[u][hh * CHUNK:(hh + 1) * CHUNK], 0.0).astype(BF16)
                intra[u, hh] = _dot(pm, v_grp[:, hh * dv:(hh + 1) * dv])
                part = dh[hh * dv:(hh + 1) * dv]
                acc = part if acc is None else jnp.where(lms[hh], part, acc)
            dst[u] = acc

    def recurrence():
        st = {ch: st_ref[bi, ch[0], ch[1]] for ch in chains}
        for ci in range(nchunk):
            for ch in chains:
                dr, pr = ch
                c = nchunk - 1 - ci if dr == 1 else ci
                u = (dr, pr, c)
                rows = slice(c * CHUNK, (c + 1) * CHUNK)
                inter = _dot_nt(per_head(q_dec[u]), st[ch].astype(BF16))
                for hh in range(hpp):
                    head = pr * hpp + hh
                    refs[dr][4][bi, rows, head * dv:(head + 1) * dv] = (
                        intra[u, hh] + inter[hh * CHUNK:(hh + 1) * CHUNK]).astype(BF16)
                st[ch] = st[ch] * a_last[u] + dst[u]
        for ch in chains:
            st_ref[bi, ch[0], ch[1]] = st[ch]

    return [cumulative_decay, decayed_operands, scores, intra_and_increments, recurrence]


def _gdn_stages(refs, st_ref, dk, dv, nh, bi):
    tb = refs[0][3].shape[1]
    nchunk = tb // CHUNK
    eye = _tri(CHUNK, False, False) & _tri(CHUNK, True, False)
    chains = [(dr, hd) for dr in range(2) for hd in range(nh)]
    units = [(dr, c, hd) for dr in range(2) for c in range(nchunk) for hd in range(nh)]
    g_cols, g_rows, small, kk, qk, x, aqk, rhs, q_dec, k_dec, a_last = ({} for _ in range(11))
    t, uw, kd_uw, aq_uw, lhs, decay = ({} for _ in range(6))

    def cumulative_decay():
        for dr in range(2):
            small[dr] = refs[dr][3][bi]
            g_cols[dr] = _cumsum_dot(_tri_blocks(tb, dr == 1).astype(BF16), small[dr], True)
            g_rows[dr] = _cumsum_dot(_tri_blocks(tb, dr != 1).astype(BF16), refs[dr][4][bi, 0], False)

    def grams():
        for u in units:
            dr, c, hd = u
            rows = slice(c * CHUNK, (c + 1) * CHUNK)
            kh = refs[dr][1][bi, rows, hd * dk:(hd + 1) * dk]
            qh = refs[dr][0][bi, rows, hd * dk:(hd + 1) * dk]
            kq = _dot_nt(jnp.concatenate([kh, qh], axis=0), kh)
            kk[u], qk[u] = kq[:CHUNK], kq[CHUNK:]

    def solve_operands():
        for u in units:
            dr, c, hd = u
            rows = slice(c * CHUNK, (c + 1) * CHUNK)
            ia = dr * nh + hd
            ib = 2 * nh + dr * nh + hd
            g_c = g_cols[dr][rows, ia:ia + 1]
            g_r = g_rows[dr][ia:ia + 1, rows]
            be_c = small[dr][rows, ib:ib + 1]
            decay[u] = jnp.exp(jnp.minimum(g_c - g_r, 0.0))
            x[u] = jnp.where(_tri(CHUNK, dr == 1, True), be_c * kk[u] * decay[u], 0.0)

    def other_operands():
        for u in units:
            dr, c, hd = u
            rows = slice(c * CHUNK, (c + 1) * CHUNK)
            ia = dr * nh + hd
            ib = 2 * nh + dr * nh + hd
            last = c * CHUNK + (0 if dr == 1 else CHUNK - 1)
            g_c = g_cols[dr][rows, ia:ia + 1]
            be_c = small[dr][rows, ib:ib + 1]
            g_last = g_cols[dr][last:last + 1, ia:ia + 1]
            aqk[u] = jnp.where(_tri(CHUNK, dr == 1, False), qk[u] * decay[u], 0.0).astype(BF16)
            eg = jnp.exp(g_c)
            qh = refs[dr][0][bi, rows, hd * dk:(hd + 1) * dk].astype(F32)
            khf = refs[dr][1][bi, rows, hd * dk:(hd + 1) * dk].astype(F32)
            vh = refs[dr][2][bi, rows, hd * dv:(hd + 1) * dv].astype(F32)
            rhs[u] = jnp.concatenate([(be_c * vh).astype(BF16), ((be_c * eg) * khf).astype(BF16)], axis=1)
            q_dec[u] = qh * eg
            k_dec[u] = khf * jnp.exp(g_last - g_c)
            a_last[u] = jnp.exp(g_last)

    def inverse_start():
        for u in units:
            t[u] = jnp.where(eye, 1.0, jnp.where(_couple(CHUNK, 1, u[0] == 1), -x[u], 0.0))

    def inverse_level(s):
        a = {u: jnp.where(_couple(CHUNK, s, u[0] == 1), x[u], 0.0).astype(BF16) for u in units}
        tb16 = {u: t[u].astype(BF16) for u in units}
        ta = {u: _dot(tb16[u], a[u]).astype(BF16) for u in units}
        for u in units:
            t[u] = t[u] - _dot(ta[u], tb16[u])

    levels = []
    s = 2
    while s < CHUNK:
        levels.append(functools.partial(inverse_level, s))
        s *= 2

    def solve():
        for u in units:
            uw[u] = _dot(t[u].astype(BF16), rhs[u]).astype(BF16)

    def fold():
        for u in units:
            both = _dot(jnp.concatenate([k_dec[u].T.astype(BF16), aqk[u]], axis=0), uw[u])
            kd_uw[u], aq_uw[u] = both[:dk], both[dk:]
        for u in units:
            lhs[u] = jnp.concatenate([(-kd_uw[u][:, dv:]).astype(BF16),
                                      (q_dec[u] - aq_uw[u][:, dv:]).astype(BF16)], axis=0)

    def recurrence():
        st = {ch: st_ref[bi, ch[0], ch[1]] for ch in chains}
        for ci in range(nchunk):
            for ch in chains:
                dr, hd = ch
                c = nchunk - 1 - ci if dr == 1 else ci
                u = (dr, c, hd)
                rows = slice(c * CHUNK, (c + 1) * CHUNK)
                res = _dot(lhs[u], st[ch].astype(BF16))
                refs[dr][5][bi, rows, hd * dv:(hd + 1) * dv] = (res[dk:] + aq_uw[u][:, :dv]).astype(BF16)
                st[ch] = st[ch] * a_last[u] + res[:dk] + kd_uw[u][:, :dv]
        for ch in chains:
            st_ref[bi, ch[0], ch[1]] = st[ch]

    return [cumulative_decay, grams, solve_operands, other_operands, inverse_start, *levels,
            solve, fold, recurrence]


def _scan_kernel(af, alf, ab, alb, bf, bcf, brf, bb, bcb, brb,
                 oaf, oab, obf, obb, sta_ref, stb_ref, *, dk_a, dv_a, dk_b, dv_b, nh_b):
    @pl.when(pl.program_id(1) == 0)
    def _():
        sta_ref[...] = jnp.zeros_like(sta_ref)
        stb_ref[...] = jnp.zeros_like(stb_ref)

    qk_a, qk_b = GLA_HEADS * dk_a, nh_b * dk_b
    qkv = lambda ref, qk: (_Cols(ref, 0), _Cols(ref, qk), _Cols(ref, 2 * qk))
    refs_a = ((*qkv(af, qk_a), alf, oaf), (*qkv(ab, qk_a), alb, oab))
    refs_b = ((*qkv(bf, qk_b), bcf, brf, obf), (*qkv(bb, qk_b), bcb, brb, obb))
    stages = {}
    for bi in range(af.shape[0]):
        stages["a", bi] = _gla_stages(refs_a, sta_ref, dk_a, dv_a, bi)
        stages["b", bi] = _gdn_stages(refs_b, stb_ref, dk_b, dv_b, nh_b, bi)
    lag = SCAN_ROW_LAG
    for pos in range(len(SCAN_ORDER) + lag * (af.shape[0] - 1)):
        for bi in range(af.shape[0]):
            if 0 <= pos - lag * bi < len(SCAN_ORDER):
                name, idx = SCAN_ORDER[pos - lag * bi]
                stages[name, bi][idx]()


def _scan(gqkv, gla, dqkv, small_c, small_r, dims):
    b, s, _ = gqkv.shape
    qk_a, v_a, _, _, qk_b, v_b, _ = dims
    nb = s // TOKEN_BLOCK
    tb = TOKEN_BLOCK
    ns = small_c.shape[2]
    nr = SCAN_ROWS
    assert b % nr == 0
    fwd = lambda n, col=0: pl.BlockSpec((nr, tb, n), lambda i, j: (i, j, col))
    bwd = lambda n, col=0: pl.BlockSpec((nr, tb, n), lambda i, j: (i, _bwd_block(j, nb), col))
    rfwd = pl.BlockSpec((nr, 1, ns, tb), lambda i, j: (i, j, 0, 0))
    rbwd = pl.BlockSpec((nr, 1, ns, tb), lambda i, j: (i, _bwd_block(j, nb), 0, 0))
    dk_a, dv_a = qk_a // GLA_HEADS, v_a // GLA_HEADS
    dk_b, dv_b = qk_b // GDN_HEADS, v_b // GDN_HEADS
    out = lambda v: jax.ShapeDtypeStruct((b, s, v), BF16)
    return pl.pallas_call(
        functools.partial(_scan_kernel, dk_a=dk_a, dv_a=dv_a, dk_b=dk_b, dv_b=dv_b, nh_b=GDN_HEADS),
        out_shape=[out(v_a), out(v_a), out(v_b), out(v_b)],
        grid=(b // nr, nb),
        in_specs=[fwd(2 * qk_a + v_a), fwd(qk_a, 0), bwd(2 * qk_a + v_a), bwd(qk_a, 1),
                  fwd(2 * qk_b + v_b), fwd(ns), rfwd, bwd(2 * qk_b + v_b), bwd(ns), rbwd],
        out_specs=[fwd(v_a), bwd(v_a), fwd(v_b), bwd(v_b)],
        scratch_shapes=[pltpu.VMEM((nr, 2, qk_a // LANES, dv_a, LANES), F32),
                        pltpu.VMEM((nr, 2, GDN_HEADS, dk_b, dv_b), F32)],
        compiler_params=pltpu.CompilerParams(dimension_semantics=("parallel", "arbitrary"),
                                             vmem_limit_bytes=VMEM_LIMIT),
        name="scan",
    )(gqkv, gla, gqkv, gla, dqkv, small_c, small_r, dqkv, small_c, small_r)


def _post_kernel(x_ref, mod_ref, *refs, d, nh_a, nh_b, final, nsub):
    mix = [refs[6 * i:6 * i + 6] for i in range(nsub)]
    na_ref, nb_ref, wo_ref, g2_ref, w1_ref, w2_ref, fg_ref, o_ref = refs[6 * nsub:]
    tb = x_ref.shape[1] // nsub
    m = mod_ref[0]
    gt1 = m[:, 2 * d:3 * d]
    sh2, sc2, gt2 = m[:, 3 * d:4 * d], m[:, 4 * d:5 * d], m[:, 5 * d:6 * d]
    slab = 2 * LANES
    x1, hb, act = {}, {}, {}

    def merge_and_project(i):
        af_ref, ab_ref, bf_ref, bb_ref, ga_ref, gb_ref = mix[i]
        acc = None
        row = 0
        for (f_ref, b_ref, g_ref, n_ref, nh) in ((af_ref, ab_ref, ga_ref, na_ref, nh_a),
                                                 (bf_ref, bb_ref, gb_ref, nb_ref, nh_b)):
            width = f_ref.shape[2]
            hv = width // nh
            for c0 in range(0, width, slab):
                o = f_ref[0, :, c0:c0 + slab].astype(F32) + b_ref[0, :, c0:c0 + slab].astype(F32)
                gate = _silu(g_ref[0, :, c0:c0 + slab].astype(F32))
                parts = [(_rms(o[:, k:k + hv]) * n_ref[...] * gate[:, k:k + hv]).astype(BF16)
                         for k in range(0, slab, hv)]
                part = _dot(jnp.concatenate(parts, axis=1), wo_ref[row:row + slab, :])
                acc = part if acc is None else acc + part
                row += slab
        x1[i] = x_ref[0, i * tb:(i + 1) * tb, :] + gt1 * acc
        hb[i] = (_rms(x1[i]) * (g2_ref[...] * (1.0 + sc2)) + sh2).astype(BF16)

    def mlp_up(i):
        a = jnp.maximum(_dot(hb[i], w1_ref[...]), 0.0)
        act[i] = (a * a).astype(BF16)

    def mlp_down(i):
        x2 = x1[i] + gt2 * _dot(act[i], w2_ref[...])
        if final:
            x2 = _rms(x2) * fg_ref[...]
        o_ref[0, i * tb:(i + 1) * tb, :] = x2

    merge_and_project(0)
    for i in range(nsub):
        if i + 1 < nsub:
            merge_and_project(i + 1)
        mlp_up(i)
        mlp_down(i)


def _post(stream, part, mod, layer, final, oaf, oab, obf, obb, gg, dg, params, fg, nh_a, nh_b):
    na, nb_, wo, g2, w1, w2 = params
    ctx_arr, lat_arr = stream
    b, t, d = lat_arr.shape
    tb = TOKEN_BLOCK
    ctx = part == "ctx"
    nsub = 1 if ctx else POST_SUB
    nstep = 1 if ctx else t // (nsub * tb)
    assert ctx or t % (nsub * tb) == 0
    first = 0 if ctx else 1
    lay = lambda a: pl.BlockSpec((None,) + a.shape[1:], lambda i, j: (layer,) + (0,) * (a.ndim - 1))
    weight = lambda a: pl.BlockSpec((None,) + a.shape[1:], lambda i, j: (layer,) + (0,) * (a.ndim - 1),
                                    pipeline_mode=pl.Buffered(1))
    bsz = b
    mod_spec = pl.BlockSpec((1, 1, 6 * d), lambda i, j: (layer * MOD_ROWS + (bsz if ctx else i), 0, 0))
    va, vb = oaf.shape[2], obf.shape[2]
    mix_specs, mix_args = [], []
    for k in range(nsub):
        blk = lambda n, k=k: pl.BlockSpec((1, tb, n), lambda i, j: (i, first + j * nsub + k, 0))
        mix_specs += [blk(va), blk(va), blk(vb), blk(vb), blk(va), blk(vb)]
        mix_args += [oaf, oab, obf, obb, gg, dg]
    return pl.pallas_call(
        functools.partial(_post_kernel, d=d, nh_a=nh_a, nh_b=nh_b, final=final, nsub=nsub),
        out_shape=jax.ShapeDtypeStruct((b, nstep * nsub * tb, d), F32),
        grid=(b, nstep),
        in_specs=[pl.BlockSpec((1, nsub * tb, d), lambda i, j: (i, j, 0)), mod_spec, *mix_specs,
                  lay(na), lay(nb_), weight(wo), lay(g2), weight(w1), weight(w2),
                  pl.BlockSpec((1, d), lambda i, j: (0, 0))],
        out_specs=pl.BlockSpec((1, nsub * tb, d), lambda i, j: (i, j, 0)),
        compiler_params=pltpu.CompilerParams(dimension_semantics=("parallel", "parallel"),
                                             vmem_limit_bytes=VMEM_LIMIT),
        name="post_ctx" if ctx else "post",
    )(ctx_arr if ctx else lat_arr, mod, *mix_args, na, nb_, wo, g2, w1, w2, fg.reshape(1, d))


def kernel(x, c, ctx, c_ctx, w_ada, b_ada, norm1_g, norm2_g, w_in, gla_w_lr, gla_b_lr, gdn_conv_w,
           gdn_a_log, gdn_dt_bias, gla_norm_g, gdn_norm_g, w_out, w_ff1, w_ff2, final_norm_g):
    bsz, t, d = x.shape
    depth = w_ada.shape[0]
    assert ctx.shape[1] == TOKEN_BLOCK and t % TOKEN_BLOCK == 0 and TOKEN_BLOCK % GRID_W == 0
    assert bsz + 1 <= MOD_ROWS
    qk_a = gla_w_lr.shape[3]
    rank = gla_w_lr.shape[2]
    v_a = gla_norm_g.shape[1] * GLA_HEADS
    conv_dim = gdn_conv_w.shape[2]
    v_b = gdn_norm_g.shape[1] * GDN_HEADS
    qk_b = (conv_dim - v_b) // 2
    nd = 2 * GDN_HEADS
    dims = (qk_a, v_a, rank, conv_dim, qk_b, v_b, GDN_HEADS)

    o_r = 2 * qk_a + 2 * v_a
    o_c = o_r + rank
    o_g = o_c + conv_dim
    o_s = o_g + v_b
    assert w_in.shape[2] == o_s + 2 * nd
    w_main = jnp.concatenate([w_in[:, :, :o_r], w_in[:, :, o_c:o_s]], axis=2).astype(BF16)
    w_sm = jnp.concatenate([w_in[:, :, o_r:o_c], w_in[:, :, o_s:]], axis=2)
    w_small = jnp.pad(w_sm, ((0, 0), (0, 0), (0, LANES - w_sm.shape[2]))).astype(BF16)
    w_small_t = jnp.swapaxes(w_in[:, :, o_s:], 1, 2).astype(BF16)

    cc = jnp.concatenate([c, c_ctx[None, :], jnp.zeros((MOD_ROWS - bsz - 1, d), F32)], axis=0)
    mod = _modulation(cc, w_ada, b_ada).reshape(depth * MOD_ROWS, 1, 6 * d)

    in_params = (norm1_g.reshape(depth, 1, d), w_main, w_small, w_small_t, gla_w_lr.astype(BF16),
                 gla_b_lr.reshape(depth, 2, 1, qk_a), gdn_conv_w,
                 gdn_a_log.reshape(depth, 1, nd), gdn_dt_bias.reshape(depth, 1, nd),
                 gdn_a_log.reshape(depth, nd, 1), gdn_dt_bias.reshape(depth, nd, 1))
    post_params = (gla_norm_g.reshape(depth, 1, -1), gdn_norm_g.reshape(depth, 1, -1), w_out.astype(BF16),
                   norm2_g.reshape(depth, 1, d), w_ff1.astype(BF16), w_ff2.astype(BF16))

    stream = (ctx, x)
    for l in range(depth):
        last = l == depth - 1
        gqkv, gg, gla, dqkv, dg, small_c, small_r = _in_proj(stream, mod, l, bsz, in_params, dims)
        oaf, oab, obf, obb = _scan(gqkv, gla, dqkv, small_c, small_r, dims)
        args = (mod, l, last, oaf, oab, obf, obb, gg, dg, post_params, final_norm_g, GLA_HEADS, GDN_HEADS)
        xs = _post(stream, "latent", *args)
        if not last:
            stream = (_post(stream, "ctx", *args), xs)
    return xs
```

```python
import functools
import math

import jax
import jax.numpy as jnp
from jax import lax
from jax.experimental import pallas as pl
from jax.experimental.pallas import tpu as pltpu

EPS = 1e-6
GRID_W = 64
GLA_HEADS = 4
GDN_HEADS = 4
GLA_TAU = 16.0
TOKEN_BLOCK = 256
CHUNK = 64
LANES = 128
SUBLANES = 8
POST_SUB = 2
SCAN_ROWS = 2
IN_ROWS = 4
SCAN_ROW_LAG = 6
MOD_ROWS = 16
VMEM_LIMIT = 56 * 1024 * 1024
SCAN_ORDER = (("b", 0), ("b", 1), ("a", 0), ("a", 1), ("a", 2), ("a", 3), ("b", 2), ("b", 3), ("b", 4),
              ("b", 5), ("b", 6), ("b", 7), ("b", 8), ("b", 9), ("a", 4), ("b", 10), ("b", 11), ("b", 12))

F32 = jnp.float32
BF16 = jnp.bfloat16


def _dot(a, b):
    return jnp.dot(a, b, preferred_element_type=F32)


def _dot_nt(a, b):
    return lax.dot_general(a, b, (((1,), (1,)), ((), ())), preferred_element_type=F32)


def _dot_tn(a, b):
    return lax.dot_general(a, b, (((0,), (0,)), ((), ())), preferred_element_type=F32)


def _silu(x):
    return x / (1.0 + jnp.exp(-x))


def _sigmoid(x):
    return 1.0 / (1.0 + jnp.exp(-x))


def _softplus(x):
    return jnp.maximum(x, 0.0) + jnp.log(1.0 + jnp.exp(-jnp.abs(x)))


def _log_sigmoid(x):
    return jnp.minimum(x, 0.0) - jnp.log(1.0 + jnp.exp(-jnp.abs(x)))


def _rms(x):
    return x * lax.rsqrt(jnp.mean(x * x, axis=-1, keepdims=True) + EPS)


def _cumsum_dot(tri_bf16, x, left):
    hi = x.astype(BF16)
    lo = (x - hi.astype(F32)).astype(BF16)
    if left:
        return _dot(tri_bf16, hi) + _dot(tri_bf16, lo)
    return _dot(hi, tri_bf16) + _dot(lo, tri_bf16)


def _tri(n, upper, strict):
    r = lax.broadcasted_iota(jnp.int32, (n, n), 0)
    c = lax.broadcasted_iota(jnp.int32, (n, n), 1)
    if upper:
        m = (r < c) if strict else (r <= c)
    else:
        m = (r > c) if strict else (r >= c)
    return m


def _couple(n, s, upper):
    r = lax.broadcasted_iota(jnp.int32, (n, n), 0)
    c = lax.broadcasted_iota(jnp.int32, (n, n), 1)
    same = (r & ~(2 * s - 1)) == (c & ~(2 * s - 1))
    r_hi, c_hi = (r & s) != 0, (c & s) != 0
    return same & ((~r_hi & c_hi) if upper else (r_hi & ~c_hi))


def _tri_blocks(n, upper):
    r = lax.broadcasted_iota(jnp.int32, (n, n), 0)
    c = lax.broadcasted_iota(jnp.int32, (n, n), 1)
    same = (r & ~(CHUNK - 1)) == (c & ~(CHUNK - 1))
    return same & ((r <= c) if upper else (r >= c))


def _mod_kernel(cc_ref, w_ref, b_ref, o_ref):
    s = _silu(cc_ref[...]).astype(BF16)
    o_ref[0] = _dot(s, w_ref[0].astype(BF16)) + b_ref[0]


def _modulation(cc, w_ada, b_ada):
    depth, d, d6 = w_ada.shape
    nblk = d6 // d
    return pl.pallas_call(
        _mod_kernel,
        out_shape=jax.ShapeDtypeStruct((depth, MOD_ROWS, d6), F32),
        grid=(depth, nblk),
        in_specs=[
            pl.BlockSpec((MOD_ROWS, d), lambda l, n: (0, 0)),
            pl.BlockSpec((1, d, d), lambda l, n: (l, 0, n)),
            pl.BlockSpec((1, 1, d), lambda l, n: (l, 0, n)),
        ],
        out_specs=pl.BlockSpec((1, MOD_ROWS, d), lambda l, n: (l, 0, n)),
        compiler_params=pltpu.CompilerParams(dimension_semantics=("parallel", "parallel")),
        name="adaln_modulation",
    )(cc, w_ada, b_ada.reshape(depth, 1, d6))


def _in_proj_kernel(c_ref, x_ref, *refs, d, qk_a, v_a, rank, conv_dim, qk_b, v_b, nh_b):
    nr = x_ref.shape[0]
    mod_refs = refs[:nr]
    (g1_ref, wm_ref, ws_ref, wst_ref, wlr_ref, blr_ref, cw_ref, arow_ref, dtrow_ref, acol_ref, dtcol_ref,
     gqkv_ref, gg_ref, gla_ref, dqkv_ref, dg_ref, sc_ref, sr_ref) = refs[nr:]
    for bi in range(nr):
        _in_proj_row(bi, c_ref, x_ref, mod_refs[bi], g1_ref, wm_ref, ws_ref, wst_ref, wlr_ref, blr_ref, cw_ref,
                     arow_ref, dtrow_ref, acol_ref, dtcol_ref, gqkv_ref, gg_ref, gla_ref, dqkv_ref, dg_ref,
                     sc_ref, sr_ref, d=d, qk_a=qk_a, v_a=v_a, rank=rank, conv_dim=conv_dim, qk_b=qk_b, v_b=v_b,
                     nh_b=nh_b)


def _in_proj_row(bi, c_ref, x_ref, mod_ref, g1_ref, wm_ref, ws_ref, wst_ref, wlr_ref, blr_ref, cw_ref,
                 arow_ref, dtrow_ref, acol_ref, dtcol_ref,
                 gqkv_ref, gg_ref, gla_ref, dqkv_ref, dg_ref, sc_ref, sr_ref, *, d, qk_a, v_a, rank, conv_dim,
                 qk_b, v_b, nh_b):
    j = pl.program_id(1)
    x = jnp.where(j == 0, c_ref[bi], x_ref[bi])
    m = mod_ref[0]
    sh1, sc1 = m[:, 0:d], m[:, d:2 * d]
    h = _rms(x) * (g1_ref[...] * (1.0 + sc1)) + sh1
    hb = h.astype(BF16)
    o_conv = 2 * qk_a + 2 * v_a
    o_gate = o_conv + conv_dim
    grp = 2 * LANES
    tb = x.shape[0]
    t = lax.broadcasted_iota(jnp.int32, (tb, 1), 0)
    seg_mask = jnp.where(j == 0, tb - 1, GRID_W - 1)
    first = (t & seg_mask) == 0
    last = (t & seg_mask) == seg_mask
    dkh = qk_b // nh_b

    def conv_dot(k):
        return _dot(hb, wm_ref[:, o_conv + k * grp:o_conv + (k + 1) * grp])

    def zero_rows(a, mask, at):
        pieces = []
        for s0 in range(0, tb, GRID_W):
            r0 = s0 + at
            pieces += [a[s0:r0], jnp.where(mask[r0:r0 + SUBLANES], 0.0, a[r0:r0 + SUBLANES]),
                       a[r0 + SUBLANES:s0 + GRID_W]]
        return jnp.concatenate([p for p in pieces if p.shape[0]], axis=0)

    def conv_group(u, k):
        up = zero_rows(pltpu.roll(u, 1, 0), first, 0)
        un = zero_rows(pltpu.roll(u, tb - 1, 0), last, GRID_W - SUBLANES)
        cw = cw_ref[:, k * grp:(k + 1) * grp]
        s = _silu(cw[0:1] * up + cw[1:2] * u + cw[2:3] * un)
        c0 = k * grp
        if c0 < 2 * qk_b:
            for hh in range(grp // dkh):
                sh = s[:, hh * dkh:(hh + 1) * dkh]
                inv = lax.rsqrt(jnp.sum(sh * sh, axis=-1, keepdims=True) + EPS)
                if c0 < qk_b:
                    inv = inv * (float(dkh) ** -0.5)
                dqkv_ref[bi, :, c0 + hh * dkh:c0 + (hh + 1) * dkh] = (sh * inv).astype(BF16)
        else:
            dqkv_ref[bi, :, c0:c0 + grp] = s.astype(BF16)

    def light_group(k):
        c0 = k * grp
        w0 = c0 if c0 < o_conv else o_gate + (c0 - o_conv)
        pa = _dot(hb, wm_ref[:, w0:w0 + grp])
        if c0 < qk_a:
            gqkv_ref[bi, :, c0:c0 + grp] = (pa * (float(qk_a // GLA_HEADS) ** -0.5)).astype(BF16)
        elif c0 < 2 * qk_a + v_a:
            gqkv_ref[bi, :, c0:c0 + grp] = pa.astype(BF16)
        elif c0 < o_conv:
            gg_ref[bi, :, c0 - 2 * qk_a - v_a:c0 - 2 * qk_a - v_a + grp] = pa.astype(BF16)
        else:
            dg_ref[bi, :, c0 - o_conv:c0 - o_conv + grp] = pa.astype(BF16)

    def small_group(ps, pst):
        r_a = ps[:, 0:rank].astype(BF16)
        for dr in range(2):
            lr = _dot(r_a, wlr_ref[dr]) + blr_ref[dr]
            gla_ref[bi, :, dr * qk_a:(dr + 1) * qk_a] = _log_sigmoid(lr) * (1.0 / GLA_TAU)
        nd = 2 * nh_b
        a_c = ps[:, rank:rank + nd]
        b_c = ps[:, rank + nd:rank + 2 * nd]
        sc_ref[bi, :, 0:nd] = -jnp.exp(arow_ref[...]) * _softplus(a_c + dtrow_ref[...])
        sc_ref[bi, :, nd:2 * nd] = _sigmoid(b_c)
        sr_ref[bi, 0, 0:nd, :] = -jnp.exp(acol_ref[...]) * _softplus(pst[0:nd] + dtcol_ref[...])
        sr_ref[bi, 0, nd:2 * nd, :] = _sigmoid(pst[nd:2 * nd])

    n_conv = conv_dim // grp
    n_light = (o_conv + v_b) // grp
    us = {0: conv_dot(0)}
    ps = _dot(hb, ws_ref[...])
    pst = _dot_nt(wst_ref[...], hb)
    if n_conv > 1:
        us[1] = conv_dot(1)
    g_next = 0
    for k in range(n_conv):
        if k + 2 < n_conv:
            us[k + 2] = conv_dot(k + 2)
        conv_group(us.pop(k), k)
        if k == 1 or n_conv == 1:
            small_group(ps, pst)
        todo = (n_light - g_next + (n_conv - k) - 1) // (n_conv - k) if k >= n_conv // 2 else 1
        for _ in range(min(todo, n_light - g_next)):
            light_group(g_next)
            g_next += 1
    while g_next < n_light:
        light_group(g_next)
        g_next += 1


def _in_proj(stream, mod, layer, bsz, params, dims):
    g1, wm, ws, wst, wlr, blr, cw, a_row, dt_row, a_col, dt_col = params
    head, body = stream
    b, _, d = body.shape
    nb = body.shape[1] // TOKEN_BLOCK + 1
    s = nb * TOKEN_BLOCK
    qk_a, v_a, rank, conv_dim, qk_b, v_b, nh_b = dims
    nd = 2 * nh_b
    tb = TOKEN_BLOCK
    lay = lambda a: pl.BlockSpec((None,) + a.shape[1:], lambda i, j: (layer,) + (0,) * (a.ndim - 1))
    nr = math.gcd(b, IN_ROWS)
    tok = lambda n: pl.BlockSpec((nr, tb, n), lambda i, j: (i, j, 0))
    mod_specs = [pl.BlockSpec((1, 1, 6 * d), lambda i, j, k=k: (
        layer * MOD_ROWS + jnp.where(j == 0, bsz, i * nr + k), 0, 0)) for k in range(nr)]
    out_shape = [
        jax.ShapeDtypeStruct((b, s, 2 * qk_a + v_a), BF16), jax.ShapeDtypeStruct((b, s, v_a), BF16),
        jax.ShapeDtypeStruct((b, s, 2 * qk_a), F32),
        jax.ShapeDtypeStruct((b, s, conv_dim), BF16), jax.ShapeDtypeStruct((b, s, v_b), BF16),
        jax.ShapeDtypeStruct((b, s, 2 * nd), F32),
        jax.ShapeDtypeStruct((b, nb, 2 * nd, tb), F32),
    ]
    out_specs = [tok(2 * qk_a + v_a), tok(v_a), tok(2 * qk_a), tok(conv_dim), tok(v_b), tok(2 * nd),
                 pl.BlockSpec((nr, 1, 2 * nd, tb), lambda i, j: (i, j, 0, 0))]
    kern = functools.partial(_in_proj_kernel, d=d, qk_a=qk_a, v_a=v_a, rank=rank, conv_dim=conv_dim,
                             qk_b=qk_b, v_b=v_b, nh_b=nh_b)
    return pl.pallas_call(
        kern, out_shape=out_shape, grid=(b // nr, nb),
        in_specs=[
            pl.BlockSpec((nr, tb, d), lambda i, j: (i, 0, 0)),
            pl.BlockSpec((nr, tb, d), lambda i, j: (i, jnp.maximum(j - 1, 0), 0)), *mod_specs, lay(g1),
            lay(wm), lay(ws), lay(wst), lay(wlr), lay(blr), lay(cw), lay(a_row), lay(dt_row), lay(a_col), lay(dt_col),
        ],
        out_specs=out_specs,
        compiler_params=pltpu.CompilerParams(dimension_semantics=("parallel", "parallel"),
                                             vmem_limit_bytes=VMEM_LIMIT),
        name="in_proj",
    )(head, body, *([mod] * nr), g1, wm, ws, wst, wlr, blr, cw, a_row, dt_row, a_col, dt_col)


def _bwd_block(j, nb):
    return jnp.where(j == 0, 0, nb - j)


class _Cols:
    def __init__(self, ref, off):
        self.ref, self.off = ref, off

    def __getitem__(self, idx):
        z, rows, cols = idx
        return self.ref[z, rows, self.off + cols.start:self.off + cols.stop]


def _gla_stages(refs, st_ref, dk, dv, bi):
    tb = refs[0][3].shape[1]
    nchunk = tb // CHUNK
    npair = (GLA_HEADS * dk) // LANES
    hpp = LANES // dk
    lane = lax.broadcasted_iota(jnp.int32, (1, LANES), 1)
    lms = [(lane >= hh * dk) & (lane < (hh + 1) * dk) for hh in range(hpp)]
    mid = CHUNK // 2
    chains = [(dr, pr) for dr in range(2) for pr in range(npair)]
    units = [(dr, pr, c) for (dr, pr) in chains for c in range(nchunk)]
    bcum, q_dec, k_hat, q_mid, k_mid, a_last, sc, intra, dst = ({} for _ in range(9))

    def cumulative_decay():
        for dr in range(2):
            both = _cumsum_dot(_tri_blocks(tb, dr == 1).astype(BF16), refs[dr][3][bi], True)
            for pr in range(npair):
                bcum[dr, pr] = both[:, pr * LANES:(pr + 1) * LANES]

    def decayed_operands():
        for u in units:
            dr, pr, c = u
            rows = slice(c * CHUNK, (c + 1) * CHUNK)
            lanes = slice(pr * LANES, (pr + 1) * LANES)
            last = c * CHUNK + (0 if dr == 1 else CHUNK - 1)
            b = bcum[dr, pr][rows]
            b_last = bcum[dr, pr][last:last + 1]
            b_mid = bcum[dr, pr][c * CHUNK + mid:c * CHUNK + mid + 1]
            qc = refs[dr][0][bi, rows, lanes].astype(F32)
            kc = refs[dr][1][bi, rows, lanes].astype(F32)
            q_dec[u] = qc * jnp.exp(b)
            k_hat[u] = (kc * jnp.exp(b_last - b)).astype(BF16)
            q_mid[u] = qc * jnp.exp(b - b_mid)
            k_mid[u] = (kc * jnp.exp(b_mid - b)).astype(BF16)
            a_last[u] = jnp.exp(b_last)

    def per_head(m):
        return jnp.concatenate([jnp.where(lm, m, 0.0).astype(BF16) for lm in lms], axis=0)

    def scores():
        for u in units:
            sc[u] = _dot_nt(per_head(q_mid[u]), k_mid[u])

    def intra_and_increments():
        for u in units:
            dr, pr, c = u
            rows = slice(c * CHUNK, (c + 1) * CHUNK)
            causal = _tri(CHUNK, dr == 1, False)
            v_grp = refs[dr][2][bi, rows, pr * hpp * dv:(pr + 1) * hpp * dv]
            dh = _dot_tn(v_grp, k_hat[u])
            acc = None
            for hh in range(hpp):
                pm = jnp.where(causal, sc[u][hh * CHUNK:(hh + 1) * CHUNK], 0.0).astype(BF16)
                intra[u, hh] = _dot(pm, v_grp[:, hh * dv:(hh + 1) * dv])
                part = dh[hh * dv:(hh + 1) * dv]
                acc = part if acc is None else jnp.where(lms[hh], part, acc)
            dst[u] = acc

    def recurrence():
        st = {ch: st_ref[bi, ch[0], ch[1]] for ch in chains}
        for ci in range(nchunk):
            for ch in chains:
                dr, pr = ch
                c = nchunk - 1 - ci if dr == 1 else ci
                u = (dr, pr, c)
                rows = slice(c * CHUNK, (c + 1) * CHUNK)
                inter = _dot_nt(per_head(q_dec[u]), st[ch].astype(BF16))
                for hh in range(hpp):
                    head = pr * hpp + hh
                    refs[dr][4][bi, rows, head * dv:(head + 1) * dv] = (
                        intra[u, hh] + inter[hh * CHUNK:(hh + 1) * CHUNK]).astype(BF16)
                st[ch] = st[ch] * a_last[u] + dst[u]
        for ch in chains:
            st_ref[bi, ch[0], ch[1]] = st[ch]

    return [cumulative_decay, decayed_operands, scores, intra_and_increments, recurrence]


def _gdn_stages(refs, st_ref, dk, dv, nh, bi):
    tb = refs[0][3].shape[1]
    nchunk = tb // CHUNK
    eye = _tri(CHUNK, False, False) & _tri(CHUNK, True, False)
    chains = [(dr, hd) for dr in range(2) for hd in range(nh)]
    units = [(dr, c, hd) for dr in range(2) for c in range(nchunk) for hd in range(nh)]
    g_cols, g_rows, small, kk, qk, x, aqk, rhs, q_dec, k_dec, a_last = ({} for _ in range(11))
    t, uw, kd_uw, aq_uw, lhs, decay = ({} for _ in range(6))

    def cumulative_decay():
        for dr in range(2):
            small[dr] = refs[dr][3][bi]
            g_cols[dr] = _cumsum_dot(_tri_blocks(tb, dr == 1).astype(BF16), small[dr], True)
            g_rows[dr] = _cumsum_dot(_tri_blocks(tb, dr != 1).astype(BF16), refs[dr][4][bi, 0], False)

    def grams():
        for u in units:
            dr, c, hd = u
            rows = slice(c * CHUNK, (c + 1) * CHUNK)
            kh = refs[dr][1][bi, rows, hd * dk:(hd + 1) * dk]
            qh = refs[dr][0][bi, rows, hd * dk:(hd + 1) * dk]
            kq = _dot_nt(jnp.concatenate([kh, qh], axis=0), kh)
            kk[u], qk[u] = kq[:CHUNK], kq[CHUNK:]

    def solve_operands():
        for u in units:
            dr, c, hd = u
            rows = slice(c * CHUNK, (c + 1) * CHUNK)
            ia = dr * nh + hd
            ib = 2 * nh + dr * nh + hd
            g_c = g_cols[dr][rows, ia:ia + 1]
            g_r = g_rows[dr][ia:ia + 1, rows]
            be_c = small[dr][rows, ib:ib + 1]
            decay[u] = jnp.exp(jnp.minimum(g_c - g_r, 0.0))
            x[u] = jnp.where(_tri(CHUNK, dr == 1, True), be_c * kk[u] * decay[u], 0.0)

    def other_operands():
        for u in units:
            dr, c, hd = u
            rows = slice(c * CHUNK, (c + 1) * CHUNK)
            ia = dr * nh + hd
            ib = 2 * nh + dr * nh + hd
            last = c * CHUNK + (0 if dr == 1 else CHUNK - 1)
            g_c = g_cols[dr][rows, ia:ia + 1]
            be_c = small[dr][rows, ib:ib + 1]
            g_last = g_cols[dr][last:last + 1, ia:ia + 1]
            aqk[u] = jnp.where(_tri(CHUNK, dr == 1, False), qk[u] * decay[u], 0.0).astype(BF16)
            eg = jnp.exp(g_c)
            qh = refs[dr][0][bi, rows, hd * dk:(hd + 1) * dk].astype(F32)
            khf = refs[dr][1][bi, rows, hd * dk:(hd + 1) * dk].astype(F32)
            vh = refs[dr][2][bi, rows, hd * dv:(hd + 1) * dv].astype(F32)
            rhs[u] = jnp.concatenate([(be_c * vh).astype(BF16), ((be_c * eg) * khf).astype(BF16)], axis=1)
            q_dec[u] = qh * eg
            k_dec[u] = khf * jnp.exp(g_last - g_c)
            a_last[u] = jnp.exp(g_last)

    def inverse_start():
        for u in units:
            t[u] = jnp.where(eye, 1.0, jnp.where(_couple(CHUNK, 1, u[0] == 1), -x[u], 0.0))

    def inverse_level(s):
        a = {u: jnp.where(_couple(CHUNK, s, u[0] == 1), x[u], 0.0).astype(BF16) for u in units}
        tb16 = {u: t[u].astype(BF16) for u in units}
        ta = {u: _dot(tb16[u], a[u]).astype(BF16) for u in units}
        for u in units:
            t[u] = t[u] - _dot(ta[u], tb16[u])

    levels = []
    s = 2
    while s < CHUNK:
        levels.append(functools.partial(inverse_level, s))
        s *= 2

    def solve():
        for u in units:
            uw[u] = _dot(t[u].astype(BF16), rhs[u]).astype(BF16)

    def fold():
        for u in units:
            both = _dot(jnp.concatenate([k_dec[u].T.astype(BF16), aqk[u]], axis=0), uw[u])
            kd_uw[u], aq_uw[u] = both[:dk], both[dk:]
        for u in units:
            lhs[u] = jnp.concatenate([(-kd_uw[u][:, dv:]).astype(BF16),
                                      (q_dec[u] - aq_uw[u][:, dv:]).astype(BF16)], axis=0)

    def recurrence():
        st = {ch: st_ref[bi, ch[0], ch[1]] for ch in chains}
        for ci in range(nchunk):
            for ch in chains:
                dr, hd = ch
                c = nchunk - 1 - ci if dr == 1 else ci
                u = (dr, c, hd)
                rows = slice(c * CHUNK, (c + 1) * CHUNK)
                res = _dot(lhs[u], st[ch].astype(BF16))
                refs[dr][5][bi, rows, hd * dv:(hd + 1) * dv] = (res[dk:] + aq_uw[u][:, :dv]).astype(BF16)
                st[ch] = st[ch] * a_last[u] + res[:dk] + kd_uw[u][:, :dv]
        for ch in chains:
            st_ref[bi, ch[0], ch[1]] = st[ch]

    return [cumulative_decay, grams, solve_operands, other_operands, inverse_start, *levels,
            solve, fold, recurrence]


def _scan_kernel(af, alf, ab, alb, bf, bcf, brf, bb, bcb, brb,
                 oaf, oab, obf, obb, sta_ref, stb_ref, *, dk_a, dv_a, dk_b, dv_b, nh_b):
    @pl.when(pl.program_id(1) == 0)
    def _():
        sta_ref[...] = jnp.zeros_like(sta_ref)
        stb_ref[...] = jnp.zeros_like(stb_ref)

    qk_a, qk_b = GLA_HEADS * dk_a, nh_b * dk_b
    qkv = lambda ref, qk: (_Cols(ref, 0), _Cols(ref, qk), _Cols(ref, 2 * qk))
    refs_a = ((*qkv(af, qk_a), alf, oaf), (*qkv(ab, qk_a), alb, oab))
    refs_b = ((*qkv(bf, qk_b), bcf, brf, obf), (*qkv(bb, qk_b), bcb, brb, obb))
    stages = {}
    for bi in range(af.shape[0]):
        stages["a", bi] = _gla_stages(refs_a, sta_ref, dk_a, dv_a, bi)
        stages["b", bi] = _gdn_stages(refs_b, stb_ref, dk_b, dv_b, nh_b, bi)
    lag = SCAN_ROW_LAG
    for pos in range(len(SCAN_ORDER) + lag * (af.shape[0] - 1)):
        for bi in range(af.shape[0]):
            if 0 <= pos - lag * bi < len(SCAN_ORDER):
                name, idx = SCAN_ORDER[pos - lag * bi]
                stages[name, bi][idx]()


def _scan(gqkv, gla, dqkv, small_c, small_r, dims):
    b, s, _ = gqkv.shape
    qk_a, v_a, _, _, qk_b, v_b, _ = dims
    nb = s // TOKEN_BLOCK
    tb = TOKEN_BLOCK
    ns = small_c.shape[2]
    nr = math.gcd(b, SCAN_ROWS)
    fwd = lambda n, col=0: pl.BlockSpec((nr, tb, n), lambda i, j: (i, j, col))
    bwd = lambda n, col=0: pl.BlockSpec((nr, tb, n), lambda i, j: (i, _bwd_block(j, nb), col))
    rfwd = pl.BlockSpec((nr, 1, ns, tb), lambda i, j: (i, j, 0, 0))
    rbwd = pl.BlockSpec((nr, 1, ns, tb), lambda i, j: (i, _bwd_block(j, nb), 0, 0))
    dk_a, dv_a = qk_a // GLA_HEADS, v_a // GLA_HEADS
    dk_b, dv_b = qk_b // GDN_HEADS, v_b // GDN_HEADS
    out = lambda v: jax.ShapeDtypeStruct((b, s, v), BF16)
    return pl.pallas_call(
        functools.partial(_scan_kernel, dk_a=dk_a, dv_a=dv_a, dk_b=dk_b, dv_b=dv_b, nh_b=GDN_HEADS),
        out_shape=[out(v_a), out(v_a), out(v_b), out(v_b)],
        grid=(b // nr, nb),
        in_specs=[fwd(2 * qk_a + v_a), fwd(qk_a, 0), bwd(2 * qk_a + v_a), bwd(qk_a, 1),
                  fwd(2 * qk_b + v_b), fwd(ns), rfwd, bwd(2 * qk_b + v_b), bwd(ns), rbwd],
        out_specs=[fwd(v_a), bwd(v_a), fwd(v_b), bwd(v_b)],
        scratch_shapes=[pltpu.VMEM((nr, 2, qk_a // LANES, dv_a, LANES), F32),
                        pltpu.VMEM((nr, 2, GDN_HEADS, dk_b, dv_b), F32)],
        compiler_params=pltpu.CompilerParams(dimension_semantics=("parallel", "arbitrary"),
                                             vmem_limit_bytes=VMEM_LIMIT),
        name="scan",
    )(gqkv, gla, gqkv, gla, dqkv, small_c, small_r, dqkv, small_c, small_r)


def _post_kernel(x_ref, mod_ref, *refs, d, nh_a, nh_b, final, nsub):
    mix = [refs[6 * i:6 * i + 6] for i in range(nsub)]
    na_ref, nb_ref, wo_ref, g2_ref, w1_ref, w2_ref, fg_ref, o_ref = refs[6 * nsub:]
    tb = x_ref.shape[1] // nsub
    m = mod_ref[0]
    gt1 = m[:, 2 * d:3 * d]
    sh2, sc2, gt2 = m[:, 3 * d:4 * d], m[:, 4 * d:5 * d], m[:, 5 * d:6 * d]
    slab = 2 * LANES
    x1, hb, act = {}, {}, {}

    def merge_and_project(i):
        af_ref, ab_ref, bf_ref, bb_ref, ga_ref, gb_ref = mix[i]
        acc = None
        row = 0
        for (f_ref, b_ref, g_ref, n_ref, nh) in ((af_ref, ab_ref, ga_ref, na_ref, nh_a),
                                                 (bf_ref, bb_ref, gb_ref, nb_ref, nh_b)):
            width = f_ref.shape[2]
            hv = width // nh
            for c0 in range(0, width, slab):
                o = f_ref[0, :, c0:c0 + slab].astype(F32) + b_ref[0, :, c0:c0 + slab].astype(F32)
                gate = _silu(g_ref[0, :, c0:c0 + slab].astype(F32))
                parts = [(_rms(o[:, k:k + hv]) * n_ref[...] * gate[:, k:k + hv]).astype(BF16)
                         for k in range(0, slab, hv)]
                part = _dot(jnp.concatenate(parts, axis=1), wo_ref[row:row + slab, :])
                acc = part if acc is None else acc + part
                row += slab
        x1[i] = x_ref[0, i * tb:(i + 1) * tb, :] + gt1 * acc
        hb[i] = (_rms(x1[i]) * (g2_ref[...] * (1.0 + sc2)) + sh2).astype(BF16)

    def mlp_up(i):
        a = jnp.maximum(_dot(hb[i], w1_ref[...]), 0.0)
        act[i] = (a * a).astype(BF16)

    def mlp_down(i):
        x2 = x1[i] + gt2 * _dot(act[i], w2_ref[...])
        if final:
            x2 = _rms(x2) * fg_ref[...]
        o_ref[0, i * tb:(i + 1) * tb, :] = x2

    merge_and_project(0)
    for i in range(nsub):
        mlp_up(i)
        if i + 1 < nsub:
            merge_and_project(i + 1)
        mlp_down(i)


def _post(stream, part, mod, layer, final, oaf, oab, obf, obb, gg, dg, params, fg, nh_a, nh_b):
    na, nb_, wo, g2, w1, w2 = params
    ctx_arr, lat_arr = stream
    b, t, d = lat_arr.shape
    tb = TOKEN_BLOCK
    ctx = part == "ctx"
    nsub = 1 if ctx else math.gcd(t // tb, POST_SUB)
    nstep = 1 if ctx else t // (nsub * tb)
    first = 0 if ctx else 1
    lay = lambda a: pl.BlockSpec((None,) + a.shape[1:], lambda i, j: (layer,) + (0,) * (a.ndim - 1))
    weight = lambda a: pl.BlockSpec((None,) + a.shape[1:], lambda i, j: (layer,) + (0,) * (a.ndim - 1),
                                    pipeline_mode=pl.Buffered(1))
    bsz = b
    mod_spec = pl.BlockSpec((1, 1, 6 * d), lambda i, j: (layer * MOD_ROWS + (bsz if ctx else i), 0, 0))
    va, vb = oaf.shape[2], obf.shape[2]
    mix_specs, mix_args = [], []
    for k in range(nsub):
        blk = lambda n, k=k: pl.BlockSpec((1, tb, n), lambda i, j: (i, first + j * nsub + k, 0))
        mix_specs += [blk(va), blk(va), blk(vb), blk(vb), blk(va), blk(vb)]
        mix_args += [oaf, oab, obf, obb, gg, dg]
    return pl.pallas_call(
        functools.partial(_post_kernel, d=d, nh_a=nh_a, nh_b=nh_b, final=final, nsub=nsub),
        out_shape=jax.ShapeDtypeStruct((b, nstep * nsub * tb, d), F32),
        grid=(b, nstep),
        in_specs=[pl.BlockSpec((1, nsub * tb, d), lambda i, j: (i, j, 0)), mod_spec, *mix_specs,
                  lay(na), lay(nb_), weight(wo), lay(g2), weight(w1), weight(w2),
                  pl.BlockSpec((1, d), lambda i, j: (0, 0))],
        out_specs=pl.BlockSpec((1, nsub * tb, d), lambda i, j: (i, j, 0)),
        compiler_params=pltpu.CompilerParams(dimension_semantics=("parallel", "parallel"),
                                             vmem_limit_bytes=VMEM_LIMIT),
        name="post_ctx" if ctx else "post",
    )(ctx_arr if ctx else lat_arr, mod, *mix_args, na, nb_, wo, g2, w1, w2, fg.reshape(1, d))


def kernel(x, c, ctx, c_ctx, w_ada, b_ada, norm1_g, norm2_g, w_in, gla_w_lr, gla_b_lr, gdn_conv_w,
           gdn_a_log, gdn_dt_bias, gla_norm_g, gdn_norm_g, w_out, w_ff1, w_ff2, final_norm_g):
    bsz, t, d = x.shape
    depth = w_ada.shape[0]
    assert ctx.shape[1] == TOKEN_BLOCK and t % TOKEN_BLOCK == 0 and TOKEN_BLOCK % GRID_W == 0
    assert bsz + 1 <= MOD_ROWS
    qk_a = gla_w_lr.shape[3]
    rank = gla_w_lr.shape[2]
    v_a = gla_norm_g.shape[1] * GLA_HEADS
    conv_dim = gdn_conv_w.shape[2]
    v_b = gdn_norm_g.shape[1] * GDN_HEADS
    qk_b = (conv_dim - v_b) // 2
    nd = 2 * GDN_HEADS
    dims = (qk_a, v_a, rank, conv_dim, qk_b, v_b, GDN_HEADS)

    o_r = 2 * qk_a + 2 * v_a
    o_c = o_r + rank
    o_g = o_c + conv_dim
    o_s = o_g + v_b
    assert w_in.shape[2] == o_s + 2 * nd
    w_main = jnp.concatenate([w_in[:, :, :o_r], w_in[:, :, o_c:o_s]], axis=2).astype(BF16)
    w_sm = jnp.concatenate([w_in[:, :, o_r:o_c], w_in[:, :, o_s:]], axis=2)
    w_small = jnp.pad(w_sm, ((0, 0), (0, 0), (0, LANES - w_sm.shape[2]))).astype(BF16)
    w_small_t = jnp.swapaxes(w_in[:, :, o_s:], 1, 2).astype(BF16)

    cc = jnp.concatenate([c, c_ctx[None, :], jnp.zeros((MOD_ROWS - bsz - 1, d), F32)], axis=0)
    mod = _modulation(cc, w_ada, b_ada).reshape(depth * MOD_ROWS, 1, 6 * d)

    in_params = (norm1_g.reshape(depth, 1, d), w_main, w_small, w_small_t, gla_w_lr.astype(BF16),
                 gla_b_lr.reshape(depth, 2, 1, qk_a), gdn_conv_w,
                 gdn_a_log.reshape(depth, 1, nd), gdn_dt_bias.reshape(depth, 1, nd),
                 gdn_a_log.reshape(depth, nd, 1), gdn_dt_bias.reshape(depth, nd, 1))
    post_params = (gla_norm_g.reshape(depth, 1, -1), gdn_norm_g.reshape(depth, 1, -1), w_out.astype(BF16),
                   norm2_g.reshape(depth, 1, d), w_ff1.astype(BF16), w_ff2.astype(BF16))

    stream = (ctx, x)
    for l in range(depth):
        last = l == depth - 1
        gqkv, gg, gla, dqkv, dg, small_c, small_r = _in_proj(stream, mod, l, bsz, in_params, dims)
        oaf, oab, obf, obb = _scan(gqkv, gla, dqkv, small_c, small_r, dims)
        args = (mod, l, last, oaf, oab, obf, obb, gg, dg, post_params, final_norm_g, GLA_HEADS, GDN_HEADS)
        xs = _post(stream, "latent", *args)
        if not last:
            stream = (_post(stream, "ctx", *args), xs)
    return xs
```

```python
import functools
import math

import jax
import jax.numpy as jnp
from jax import lax
from jax.experimental import pallas as pl
from jax.experimental.pallas import tpu as pltpu

EPS = 1e-6
GRID_W = 64
GLA_HEADS = 4
GDN_HEADS = 4
GLA_TAU = 16.0
TOKEN_BLOCK = 256
CHUNK = 64
LANES = 128
SUBLANES = 8
POST_SUB = 2
SCAN_ROWS = 2
IN_ROWS = 4
SCAN_ROW_LAG = 6
MOD_ROWS = 16
VMEM_LIMIT = 56 * 1024 * 1024
SCAN_ORDER = (("b", 0), ("b", 1), ("a", 0), ("a", 1), ("a", 2), ("a", 3), ("b", 2), ("b", 3), ("b", 4),
              ("b", 5), ("b", 6), ("a", 4), ("b", 7), ("b", 8), ("b", 9), ("b", 10), ("b", 11), ("b", 12))

F32 = jnp.float32
BF16 = jnp.bfloat16


def _dot(a, b):
    return jnp.dot(a, b, preferred_element_type=F32)


def _dot_nt(a, b):
    return lax.dot_general(a, b, (((1,), (1,)), ((), ())), preferred_element_type=F32)


def _dot_tn(a, b):
    return lax.dot_general(a, b, (((0,), (0,)), ((), ())), preferred_element_type=F32)


def _silu(x):
    return x / (1.0 + jnp.exp(-x))


def _sigmoid(x):
    return 1.0 / (1.0 + jnp.exp(-x))


def _softplus(x):
    return jnp.maximum(x, 0.0) + jnp.log(1.0 + jnp.exp(-jnp.abs(x)))


def _log_sigmoid(x):
    return jnp.minimum(x, 0.0) - jnp.log(1.0 + jnp.exp(-jnp.abs(x)))


def _rms(x):
    return x * lax.rsqrt(jnp.mean(x * x, axis=-1, keepdims=True) + EPS)


def _cumsum_dot(tri_bf16, x, left):
    hi = x.astype(BF16)
    lo = (x - hi.astype(F32)).astype(BF16)
    if left:
        return _dot(tri_bf16, hi) + _dot(tri_bf16, lo)
    return _dot(hi, tri_bf16) + _dot(lo, tri_bf16)


def _tri(n, upper, strict):
    r = lax.broadcasted_iota(jnp.int32, (n, n), 0)
    c = lax.broadcasted_iota(jnp.int32, (n, n), 1)
    if upper:
        m = (r < c) if strict else (r <= c)
    else:
        m = (r > c) if strict else (r >= c)
    return m


def _couple(n, s, upper):
    r = lax.broadcasted_iota(jnp.int32, (n, n), 0)
    c = lax.broadcasted_iota(jnp.int32, (n, n), 1)
    same = (r & ~(2 * s - 1)) == (c & ~(2 * s - 1))
    r_hi, c_hi = (r & s) != 0, (c & s) != 0
    return same & ((~r_hi & c_hi) if upper else (r_hi & ~c_hi))


def _tri_blocks(n, upper):
    r = lax.broadcasted_iota(jnp.int32, (n, n), 0)
    c = lax.broadcasted_iota(jnp.int32, (n, n), 1)
    same = (r & ~(CHUNK - 1)) == (c & ~(CHUNK - 1))
    return same & ((r <= c) if upper else (r >= c))


def _mod_kernel(cc_ref, w_ref, b_ref, o_ref):
    s = _silu(cc_ref[...]).astype(BF16)
    o_ref[0] = _dot(s, w_ref[0].astype(BF16)) + b_ref[0]


def _modulation(cc, w_ada, b_ada):
    depth, d, d6 = w_ada.shape
    nblk = d6 // d
    return pl.pallas_call(
        _mod_kernel,
        out_shape=jax.ShapeDtypeStruct((depth, MOD_ROWS, d6), F32),
        grid=(depth, nblk),
        in_specs=[
            pl.BlockSpec((MOD_ROWS, d), lambda l, n: (0, 0)),
            pl.BlockSpec((1, d, d), lambda l, n: (l, 0, n)),
            pl.BlockSpec((1, 1, d), lambda l, n: (l, 0, n)),
        ],
        out_specs=pl.BlockSpec((1, MOD_ROWS, d), lambda l, n: (l, 0, n)),
        compiler_params=pltpu.CompilerParams(dimension_semantics=("parallel", "parallel")),
        name="adaln_modulation",
    )(cc, w_ada, b_ada.reshape(depth, 1, d6))


def _in_proj_kernel(c_ref, x_ref, *refs, d, qk_a, v_a, rank, conv_dim, qk_b, v_b, nh_b):
    nr = x_ref.shape[0]
    mod_refs = refs[:nr]
    (g1_ref, wm_ref, ws_ref, wst_ref, wlr_ref, blr_ref, cw_ref, arow_ref, dtrow_ref, acol_ref, dtcol_ref,
     gqkv_ref, gg_ref, gla_ref, dqkv_ref, dg_ref, sc_ref, sr_ref) = refs[nr:]
    for bi in range(nr):
        _in_proj_row(bi, c_ref, x_ref, mod_refs[bi], g1_ref, wm_ref, ws_ref, wst_ref, wlr_ref, blr_ref, cw_ref,
                     arow_ref, dtrow_ref, acol_ref, dtcol_ref, gqkv_ref, gg_ref, gla_ref, dqkv_ref, dg_ref,
                     sc_ref, sr_ref, d=d, qk_a=qk_a, v_a=v_a, rank=rank, conv_dim=conv_dim, qk_b=qk_b, v_b=v_b,
                     nh_b=nh_b)


def _in_proj_row(bi, c_ref, x_ref, mod_ref, g1_ref, wm_ref, ws_ref, wst_ref, wlr_ref, blr_ref, cw_ref,
                 arow_ref, dtrow_ref, acol_ref, dtcol_ref,
                 gqkv_ref, gg_ref, gla_ref, dqkv_ref, dg_ref, sc_ref, sr_ref, *, d, qk_a, v_a, rank, conv_dim,
                 qk_b, v_b, nh_b):
    j = pl.program_id(1)
    x = jnp.where(j == 0, c_ref[bi], x_ref[bi])
    m = mod_ref[0]
    sh1, sc1 = m[:, 0:d], m[:, d:2 * d]
    h = _rms(x) * (g1_ref[...] * (1.0 + sc1)) + sh1
    hb = h.astype(BF16)
    o_conv = 2 * qk_a + 2 * v_a
    o_gate = o_conv + conv_dim
    grp = 2 * LANES
    tb = x.shape[0]
    t = lax.broadcasted_iota(jnp.int32, (tb, 1), 0)
    seg_mask = jnp.where(j == 0, tb - 1, GRID_W - 1)
    first = (t & seg_mask) == 0
    last = (t & seg_mask) == seg_mask
    dkh = qk_b // nh_b

    def conv_dot(k):
        return _dot(hb, wm_ref[:, o_conv + k * grp:o_conv + (k + 1) * grp])

    def zero_rows(a, mask, at):
        pieces = []
        for s0 in range(0, tb, GRID_W):
            r0 = s0 + at
            pieces += [a[s0:r0], jnp.where(mask[r0:r0 + SUBLANES], 0.0, a[r0:r0 + SUBLANES]),
                       a[r0 + SUBLANES:s0 + GRID_W]]
        return jnp.concatenate([p for p in pieces if p.shape[0]], axis=0)

    def conv_group(u, k):
        up = zero_rows(pltpu.roll(u, 1, 0), first, 0)
        un = zero_rows(pltpu.roll(u, tb - 1, 0), last, GRID_W - SUBLANES)
        cw = cw_ref[:, k * grp:(k + 1) * grp]
        s = _silu(cw[0:1] * up + cw[1:2] * u + cw[2:3] * un)
        c0 = k * grp
        if c0 < 2 * qk_b:
            for hh in range(grp // dkh):
                sh = s[:, hh * dkh:(hh + 1) * dkh]
                inv = lax.rsqrt(jnp.sum(sh * sh, axis=-1, keepdims=True) + EPS)
                if c0 < qk_b:
                    inv = inv * (float(dkh) ** -0.5)
                dqkv_ref[bi, :, c0 + hh * dkh:c0 + (hh + 1) * dkh] = (sh * inv).astype(BF16)
        else:
            dqkv_ref[bi, :, c0:c0 + grp] = s.astype(BF16)

    def light_group(k):
        c0 = k * grp
        w0 = c0 if c0 < o_conv else o_gate + (c0 - o_conv)
        pa = _dot(hb, wm_ref[:, w0:w0 + grp])
        if c0 < qk_a:
            gqkv_ref[bi, :, c0:c0 + grp] = (pa * (float(qk_a // GLA_HEADS) ** -0.5)).astype(BF16)
        elif c0 < 2 * qk_a + v_a:
            gqkv_ref[bi, :, c0:c0 + grp] = pa.astype(BF16)
        elif c0 < o_conv:
            gg_ref[bi, :, c0 - 2 * qk_a - v_a:c0 - 2 * qk_a - v_a + grp] = pa.astype(BF16)
        else:
            dg_ref[bi, :, c0 - o_conv:c0 - o_conv + grp] = pa.astype(BF16)

    def small_group(ps, pst):
        r_a = ps[:, 0:rank].astype(BF16)
        for dr in range(2):
            lr = _dot(r_a, wlr_ref[dr]) + blr_ref[dr]
            gla_ref[bi, :, dr * qk_a:(dr + 1) * qk_a] = _log_sigmoid(lr) * (1.0 / GLA_TAU)
        nd = 2 * nh_b
        a_c = ps[:, rank:rank + nd]
        b_c = ps[:, rank + nd:rank + 2 * nd]
        sc_ref[bi, :, 0:nd] = -jnp.exp(arow_ref[...]) * _softplus(a_c + dtrow_ref[...])
        sc_ref[bi, :, nd:2 * nd] = _sigmoid(b_c)
        sr_ref[bi, 0, 0:nd, :] = -jnp.exp(acol_ref[...]) * _softplus(pst[0:nd] + dtcol_ref[...])
        sr_ref[bi, 0, nd:2 * nd, :] = _sigmoid(pst[nd:2 * nd])

    n_conv = conv_dim // grp
    n_light = (o_conv + v_b) // grp
    us = {0: conv_dot(0)}
    ps = _dot(hb, ws_ref[...])
    pst = _dot_nt(wst_ref[...], hb)
    if n_conv > 1:
        us[1] = conv_dot(1)
    g_next = 0
    for k in range(n_conv):
        if k + 2 < n_conv:
            us[k + 2] = conv_dot(k + 2)
        conv_group(us.pop(k), k)
        if k == 1 or n_conv == 1:
            small_group(ps, pst)
        todo = (n_light - g_next + (n_conv - k) - 1) // (n_conv - k) if k >= n_conv // 2 else 1
        for _ in range(min(todo, n_light - g_next)):
            light_group(g_next)
            g_next += 1
    while g_next < n_light:
        light_group(g_next)
        g_next += 1


def _in_proj(stream, mod, layer, bsz, params, dims):
    g1, wm, ws, wst, wlr, blr, cw, a_row, dt_row, a_col, dt_col = params
    head, body = stream
    b, _, d = body.shape
    nb = body.shape[1] // TOKEN_BLOCK + 1
    s = nb * TOKEN_BLOCK
    qk_a, v_a, rank, conv_dim, qk_b, v_b, nh_b = dims
    nd = 2 * nh_b
    tb = TOKEN_BLOCK
    lay = lambda a: pl.BlockSpec((None,) + a.shape[1:], lambda i, j: (layer,) + (0,) * (a.ndim - 1))
    nr = math.gcd(b, IN_ROWS)
    tok = lambda n: pl.BlockSpec((nr, tb, n), lambda i, j: (i, j, 0))
    mod_specs = [pl.BlockSpec((1, 1, 6 * d), lambda i, j, k=k: (
        layer * MOD_ROWS + jnp.where(j == 0, bsz, i * nr + k), 0, 0)) for k in range(nr)]
    out_shape = [
        jax.ShapeDtypeStruct((b, s, 2 * qk_a + v_a), BF16), jax.ShapeDtypeStruct((b, s, v_a), BF16),
        jax.ShapeDtypeStruct((b, s, 2 * qk_a), F32),
        jax.ShapeDtypeStruct((b, s, conv_dim), BF16), jax.ShapeDtypeStruct((b, s, v_b), BF16),
        jax.ShapeDtypeStruct((b, s, 2 * nd), F32),
        jax.ShapeDtypeStruct((b, nb, 2 * nd, tb), F32),
    ]
    out_specs = [tok(2 * qk_a + v_a), tok(v_a), tok(2 * qk_a), tok(conv_dim), tok(v_b), tok(2 * nd),
                 pl.BlockSpec((nr, 1, 2 * nd, tb), lambda i, j: (i, j, 0, 0))]
    kern = functools.partial(_in_proj_kernel, d=d, qk_a=qk_a, v_a=v_a, rank=rank, conv_dim=conv_dim,
                             qk_b=qk_b, v_b=v_b, nh_b=nh_b)
    return pl.pallas_call(
        kern, out_shape=out_shape, grid=(b // nr, nb),
        in_specs=[
            pl.BlockSpec((nr, tb, d), lambda i, j: (i, 0, 0)),
            pl.BlockSpec((nr, tb, d), lambda i, j: (i, jnp.maximum(j - 1, 0), 0)), *mod_specs, lay(g1),
            lay(wm), lay(ws), lay(wst), lay(wlr), lay(blr), lay(cw), lay(a_row), lay(dt_row), lay(a_col), lay(dt_col),
        ],
        out_specs=out_specs,
        compiler_params=pltpu.CompilerParams(dimension_semantics=("parallel", "parallel"),
                                             vmem_limit_bytes=VMEM_LIMIT),
        name="in_proj",
    )(head, body, *([mod] * nr), g1, wm, ws, wst, wlr, blr, cw, a_row, dt_row, a_col, dt_col)


def _bwd_block(j, nb):
    return jnp.where(j == 0, 0, nb - j)


class _Cols:
    def __init__(self, ref, off):
        self.ref, self.off = ref, off

    def __getitem__(self, idx):
        z, rows, cols = idx
        return self.ref[z, rows, self.off + cols.start:self.off + cols.stop]


def _gla_stages(refs, st_ref, dk, dv, bi):
    tb = refs[0][3].shape[1]
    nchunk = tb // CHUNK
    npair = (GLA_HEADS * dk) // LANES
    hpp = LANES // dk
    lane = lax.broadcasted_iota(jnp.int32, (1, LANES), 1)
    lms = [(lane >= hh * dk) & (lane < (hh + 1) * dk) for hh in range(hpp)]
    mid = CHUNK // 2
    chains = [(dr, pr) for dr in range(2) for pr in range(npair)]
    units = [(dr, pr, c) for (dr, pr) in chains for c in range(nchunk)]
    bcum, q_dec, k_hat, q_mid, k_mid, a_last, sc, intra, dst = ({} for _ in range(9))

    def cumulative_decay():
        for dr in range(2):
            both = _cumsum_dot(_tri_blocks(tb, dr == 1).astype(BF16), refs[dr][3][bi], True)
            for pr in range(npair):
                bcum[dr, pr] = both[:, pr * LANES:(pr + 1) * LANES]

    def decayed_operands():
        for u in units:
            dr, pr, c = u
            rows = slice(c * CHUNK, (c + 1) * CHUNK)
            lanes = slice(pr * LANES, (pr + 1) * LANES)
            last = c * CHUNK + (0 if dr == 1 else CHUNK - 1)
            b = bcum[dr, pr][rows]
            b_last = bcum[dr, pr][last:last + 1]
            b_mid = bcum[dr, pr][c * CHUNK + mid:c * CHUNK + mid + 1]
            qc = refs[dr][0][bi, rows, lanes].astype(F32)
            kc = refs[dr][1][bi, rows, lanes].astype(F32)
            q_dec[u] = qc * jnp.exp(b)
            k_hat[u] = (kc * jnp.exp(b_last - b)).astype(BF16)
            q_mid[u] = qc * jnp.exp(b - b_mid)
            k_mid[u] = (kc * jnp.exp(b_mid - b)).astype(BF16)
            a_last[u] = jnp.exp(b_last)

    def per_head(m):
        return jnp.concatenate([jnp.where(lm, m, 0.0).astype(BF16) for lm in lms], axis=0)

    def scores():
        for u in units:
            sc[u] = _dot_nt(per_head(q_mid[u]), k_mid[u])

    def intra_and_increments():
        for u in units:
            dr, pr, c = u
            rows = slice(c * CHUNK, (c + 1) * CHUNK)
            causal = _tri(CHUNK, dr == 1, False)
            v_grp = refs[dr][2][bi, rows, pr * hpp * dv:(pr + 1) * hpp * dv]
            dh = _dot_tn(v_grp, k_hat[u])
            acc = None
            for hh in range(hpp):
                pm = jnp.where(causal, sc[u][hh * CHUNK:(hh + 1) * CHUNK], 0.0).astype(BF16)
                intra[u, hh] = _dot(pm, v_grp[:, hh * dv:(hh + 1) * dv])
                part = dh[hh * dv:(hh + 1) * dv]
                acc = part if acc is None else jnp.where(lms[hh], part, acc)
            dst[u] = acc

    def recurrence():
        st = {ch: st_ref[bi, ch[0], ch[1]] for ch in chains}
        for ci in range(nchunk):
            for ch in chains:
                dr, pr = ch
                c = nchunk - 1 - ci if dr == 1 else ci
                u = (dr, pr, c)
                rows = slice(c * CHUNK, (c + 1) * CHUNK)
                inter = _dot_nt(per_head(q_dec[u]), st[ch].astype(BF16))
                for hh in range(hpp):
                    head = pr * hpp + hh
                    refs[dr][4][bi, rows, head * dv:(head + 1) * dv] = (
                        intra[u, hh] + inter[hh * CHUNK:(hh + 1) * CHUNK]).astype(BF16)
                st[ch] = st[ch] * a_last[u] + dst[u]
        for ch in chains:
            st_ref[bi, ch[0], ch[1]] = st[ch]

    return [cumulative_decay, decayed_operands, scores, intra_and_increments, recurrence]


def _gdn_stages(refs, st_ref, dk, dv, nh, bi):
    tb = refs[0][3].shape[1]
    nchunk = tb // CHUNK
    eye = _tri(CHUNK, False, False) & _tri(CHUNK, True, False)
    chains = [(dr, hd) for dr in range(2) for hd in range(nh)]
    units = [(dr, c, hd) for dr in range(2) for c in range(nchunk) for hd in range(nh)]
    g_cols, g_rows, small, kk, qk, x, aqk, rhs, q_dec, k_dec, a_last = ({} for _ in range(11))
    t, uw, kd_uw, aq_uw, lhs, decay = ({} for _ in range(6))

    def cumulative_decay():
        for dr in range(2):
            small[dr] = refs[dr][3][bi]
            g_cols[dr] = _cumsum_dot(_tri_blocks(tb, dr == 1).astype(BF16), small[dr], True)
            g_rows[dr] = _cumsum_dot(_tri_blocks(tb, dr != 1).astype(BF16), refs[dr][4][bi, 0], False)

    def grams():
        for u in units:
            dr, c, hd = u
            rows = slice(c * CHUNK, (c + 1) * CHUNK)
            kh = refs[dr][1][bi, rows, hd * dk:(hd + 1) * dk]
            qh = refs[dr][0][bi, rows, hd * dk:(hd + 1) * dk]
            kq = _dot_nt(jnp.concatenate([kh, qh], axis=0), kh)
            kk[u], qk[u] = kq[:CHUNK], kq[CHUNK:]

    def solve_operands():
        for u in units:
            dr, c, hd = u
            rows = slice(c * CHUNK, (c + 1) * CHUNK)
            ia = dr * nh + hd
            ib = 2 * nh + dr * nh + hd
            g_c = g_cols[dr][rows, ia:ia + 1]
            g_r = g_rows[dr][ia:ia + 1, rows]
            be_c = small[dr][rows, ib:ib + 1]
            decay[u] = jnp.exp(jnp.minimum(g_c - g_r, 0.0))
            x[u] = jnp.where(_tri(CHUNK, dr == 1, True), be_c * kk[u] * decay[u], 0.0)

    def other_operands():
        for u in units:
            dr, c, hd = u
            rows = slice(c * CHUNK, (c + 1) * CHUNK)
            ia = dr * nh + hd
            ib = 2 * nh + dr * nh + hd
            last = c * CHUNK + (0 if dr == 1 else CHUNK - 1)
            g_c = g_cols[dr][rows, ia:ia + 1]
            be_c = small[dr][rows, ib:ib + 1]
            g_last = g_cols[dr][last:last + 1, ia:ia + 1]
            aqk[u] = jnp.where(_tri(CHUNK, dr == 1, False), qk[u] * decay[u], 0.0).astype(BF16)
            eg = jnp.exp(g_c)
            qh = refs[dr][0][bi, rows, hd * dk:(hd + 1) * dk].astype(F32)
            khf = refs[dr][1][bi, rows, hd * dk:(hd + 1) * dk].astype(F32)
            vh = refs[dr][2][bi, rows, hd * dv:(hd + 1) * dv].astype(F32)
            rhs[u] = jnp.concatenate([(be_c * vh).astype(BF16), ((be_c * eg) * khf).astype(BF16)], axis=1)
            q_dec[u] = qh * eg
            k_dec[u] = khf * jnp.exp(g_last - g_c)
            a_last[u] = jnp.exp(g_last)

    def inverse_start():
        for u in units:
            t[u] = jnp.where(eye, 1.0, jnp.where(_couple(CHUNK, 1, u[0] == 1), -x[u], 0.0))

    def inverse_level(s):
        a = {u: jnp.where(_couple(CHUNK, s, u[0] == 1), x[u], 0.0).astype(BF16) for u in units}
        tb16 = {u: t[u].astype(BF16) for u in units}
        ta = {u: _dot(tb16[u], a[u]).astype(BF16) for u in units}
        for u in units:
            t[u] = t[u] - _dot(ta[u], tb16[u])

    levels = []
    s = 2
    while s < CHUNK:
        levels.append(functools.partial(inverse_level, s))
        s *= 2

    def solve():
        for u in units:
            uw[u] = _dot(t[u].astype(BF16), rhs[u]).astype(BF16)

    def fold():
        for u in units:
            both = _dot(jnp.concatenate([k_dec[u].T.astype(BF16), aqk[u]], axis=0), uw[u])
            kd_uw[u], aq_uw[u] = both[:dk], both[dk:]
        for u in units:
            lhs[u] = jnp.concatenate([(-kd_uw[u][:, dv:]).astype(BF16),
                                      (q_dec[u] - aq_uw[u][:, dv:]).astype(BF16)], axis=0)

    def recurrence():
        st = {ch: st_ref[bi, ch[0], ch[1]] for ch in chains}
        for ci in range(nchunk):
            for ch in chains:
                dr, hd = ch
                c = nchunk - 1 - ci if dr == 1 else ci
                u = (dr, c, hd)
                rows = slice(c * CHUNK, (c + 1) * CHUNK)
                res = _dot(lhs[u], st[ch].astype(BF16))
                refs[dr][5][bi, rows, hd * dv:(hd + 1) * dv] = (res[dk:] + aq_uw[u][:, :dv]).astype(BF16)
                st[ch] = st[ch] * a_last[u] + res[:dk] + kd_uw[u][:, :dv]
        for ch in chains:
            st_ref[bi, ch[0], ch[1]] = st[ch]

    return [cumulative_decay, grams, solve_operands, other_operands, inverse_start, *levels,
            solve, fold, recurrence]


def _scan_kernel(af, alf, ab, alb, bf, bcf, brf, bb, bcb, brb,
                 oaf, oab, obf, obb, sta_ref, stb_ref, *, dk_a, dv_a, dk_b, dv_b, nh_b):
    @pl.when(pl.program_id(1) == 0)
    def _():
        sta_ref[...] = jnp.zeros_like(sta_ref)
        stb_ref[...] = jnp.zeros_like(stb_ref)

    qk_a, qk_b = GLA_HEADS * dk_a, nh_b * dk_b
    qkv = lambda ref, qk: (_Cols(ref, 0), _Cols(ref, qk), _Cols(ref, 2 * qk))
    refs_a = ((*qkv(af, qk_a), alf, oaf), (*qkv(ab, qk_a), alb, oab))
    refs_b = ((*qkv(bf, qk_b), bcf, brf, obf), (*qkv(bb, qk_b), bcb, brb, obb))
    stages = {}
    for bi in range(af.shape[0]):
        stages["a", bi] = _gla_stages(refs_a, sta_ref, dk_a, dv_a, bi)
        stages["b", bi] = _gdn_stages(refs_b, stb_ref, dk_b, dv_b, nh_b, bi)
    lag = SCAN_ROW_LAG
    for pos in range(len(SCAN_ORDER) + lag * (af.shape[0] - 1)):
        for bi in range(af.shape[0]):
            if 0 <= pos - lag * bi < len(SCAN_ORDER):
                name, idx = SCAN_ORDER[pos - lag * bi]
                stages[name, bi][idx]()


def _scan(gqkv, gla, dqkv, small_c, small_r, dims):
    b, s, _ = gqkv.shape
    qk_a, v_a, _, _, qk_b, v_b, _ = dims
    nb = s // TOKEN_BLOCK
    tb = TOKEN_BLOCK
    ns = small_c.shape[2]
    nr = math.gcd(b, SCAN_ROWS)
    fwd = lambda n, col=0: pl.BlockSpec((nr, tb, n), lambda i, j: (i, j, col))
    bwd = lambda n, col=0: pl.BlockSpec((nr, tb, n), lambda i, j: (i, _bwd_block(j, nb), col))
    rfwd = pl.BlockSpec((nr, 1, ns, tb), lambda i, j: (i, j, 0, 0))
    rbwd = pl.BlockSpec((nr, 1, ns, tb), lambda i, j: (i, _bwd_block(j, nb), 0, 0))
    dk_a, dv_a = qk_a // GLA_HEADS, v_a // GLA_HEADS
    dk_b, dv_b = qk_b // GDN_HEADS, v_b // GDN_HEADS
    out = lambda v: jax.ShapeDtypeStruct((b, s, v), BF16)
    return pl.pallas_call(
        functools.partial(_scan_kernel, dk_a=dk_a, dv_a=dv_a, dk_b=dk_b, dv_b=dv_b, nh_b=GDN_HEADS),
        out_shape=[out(v_a), out(v_a), out(v_b), out(v_b)],
        grid=(b // nr, nb),
        in_specs=[fwd(2 * qk_a + v_a), fwd(qk_a, 0), bwd(2 * qk_a + v_a), bwd(qk_a, 1),
                  fwd(2 * qk_b + v_b), fwd(ns), rfwd, bwd(2 * qk_b + v_b), bwd(ns), rbwd],
        out_specs=[fwd(v_a), bwd(v_a), fwd(v_b), bwd(v_b)],
        scratch_shapes=[pltpu.VMEM((nr, 2, qk_a // LANES, dv_a, LANES), F32),
                        pltpu.VMEM((nr, 2, GDN_HEADS, dk_b, dv_b), F32)],
        compiler_params=pltpu.CompilerParams(dimension_semantics=("parallel", "arbitrary"),
                                             vmem_limit_bytes=VMEM_LIMIT),
        name="scan",
    )(gqkv, gla, gqkv, gla, dqkv, small_c, small_r, dqkv, small_c, small_r)


def _post_kernel(x_ref, mod_ref, *refs, d, nh_a, nh_b, final, nsub):
    mix = [refs[6 * i:6 * i + 6] for i in range(nsub)]
    na_ref, nb_ref, wo_ref, g2_ref, w1_ref, w2_ref, fg_ref, o_ref = refs[6 * nsub:]
    tb = x_ref.shape[1] // nsub
    m = mod_ref[0]
    gt1 = m[:, 2 * d:3 * d]
    sh2, sc2, gt2 = m[:, 3 * d:4 * d], m[:, 4 * d:5 * d], m[:, 5 * d:6 * d]
    slab = 2 * LANES
    x1, hb, act = {}, {}, {}

    def merge_and_project(i):
        af_ref, ab_ref, bf_ref, bb_ref, ga_ref, gb_ref = mix[i]
        acc = None
        row = 0
        for (f_ref, b_ref, g_ref, n_ref, nh) in ((af_ref, ab_ref, ga_ref, na_ref, nh_a),
                                                 (bf_ref, bb_ref, gb_ref, nb_ref, nh_b)):
            width = f_ref.shape[2]
            hv = width // nh
            for c0 in range(0, width, slab):
                o = f_ref[0, :, c0:c0 + slab].astype(F32) + b_ref[0, :, c0:c0 + slab].astype(F32)
                gate = _silu(g_ref[0, :, c0:c0 + slab].astype(F32))
                parts = [(_rms(o[:, k:k + hv]) * n_ref[...] * gate[:, k:k + hv]).astype(BF16)
                         for k in range(0, slab, hv)]
                part = _dot(jnp.concatenate(parts, axis=1), wo_ref[row:row + slab, :])
                acc = part if acc is None else acc + part
                row += slab
        x1[i] = x_ref[0, i * tb:(i + 1) * tb, :] + gt1 * acc
        hb[i] = (_rms(x1[i]) * (g2_ref[...] * (1.0 + sc2)) + sh2).astype(BF16)

    def mlp_up(i):
        a = jnp.maximum(_dot(hb[i], w1_ref[...]), 0.0)
        act[i] = (a * a).astype(BF16)

    def mlp_down(i):
        x2 = x1[i] + gt2 * _dot(act[i], w2_ref[...])
        if final:
            x2 = _rms(x2) * fg_ref[...]
        o_ref[0, i * tb:(i + 1) * tb, :] = x2

    merge_and_project(0)
    for i in range(nsub):
        mlp_up(i)
        if i + 1 < nsub:
            merge_and_project(i + 1)
        mlp_down(i)


def _post(stream, part, mod, layer, final, oaf, oab, obf, obb, gg, dg, params, fg, nh_a, nh_b):
    na, nb_, wo, g2, w1, w2 = params
    ctx_arr, lat_arr = stream
    b, t, d = lat_arr.shape
    tb = TOKEN_BLOCK
    ctx = part == "ctx"
    nsub = 1 if ctx else math.gcd(t // tb, POST_SUB)
    nstep = 1 if ctx else t // (nsub * tb)
    first = 0 if ctx else 1
    lay = lambda a: pl.BlockSpec((None,) + a.shape[1:], lambda i, j: (layer,) + (0,) * (a.ndim - 1))
    weight = lambda a: pl.BlockSpec((None,) + a.shape[1:], lambda i, j: (layer,) + (0,) * (a.ndim - 1),
                                    pipeline_mode=pl.Buffered(1))
    bsz = b
    mod_spec = pl.BlockSpec((1, 1, 6 * d), lambda i, j: (layer * MOD_ROWS + (bsz if ctx else i), 0, 0))
    va, vb = oaf.shape[2], obf.shape[2]
    mix_specs, mix_args = [], []
    for k in range(nsub):
        blk = lambda n, k=k: pl.BlockSpec((1, tb, n), lambda i, j: (i, first + j * nsub + k, 0))
        mix_specs += [blk(va), blk(va), blk(vb), blk(vb), blk(va), blk(vb)]
        mix_args += [oaf, oab, obf, obb, gg, dg]
    return pl.pallas_call(
        functools.partial(_post_kernel, d=d, nh_a=nh_a, nh_b=nh_b, final=final, nsub=nsub),
        out_shape=jax.ShapeDtypeStruct((b, nstep * nsub * tb, d), F32),
        grid=(b, nstep),
        in_specs=[pl.BlockSpec((1, nsub * tb, d), lambda i, j: (i, j, 0)), mod_spec, *mix_specs,
                  lay(na), lay(nb_), weight(wo), lay(g2), weight(w1), weight(w2),
                  pl.BlockSpec((1, d), lambda i, j: (0, 0))],
        out_specs=pl.BlockSpec((1, nsub * tb, d), lambda i, j: (i, j, 0)),
        compiler_params=pltpu.CompilerParams(dimension_semantics=("parallel", "parallel"),
                                             vmem_limit_bytes=VMEM_LIMIT),
        name="post_ctx" if ctx else "post",
    )(ctx_arr if ctx else lat_arr, mod, *mix_args, na, nb_, wo, g2, w1, w2, fg.reshape(1, d))


def kernel(x, c, ctx, c_ctx, w_ada, b_ada, norm1_g, norm2_g, w_in, gla_w_lr, gla_b_lr, gdn_conv_w,
           gdn_a_log, gdn_dt_bias, gla_norm_g, gdn_norm_g, w_out, w_ff1, w_ff2, final_norm_g):
    bsz, t, d = x.shape
    depth = w_ada.shape[0]
    assert ctx.shape[1] == TOKEN_BLOCK and t % TOKEN_BLOCK == 0 and TOKEN_BLOCK % GRID_W == 0
    assert bsz + 1 <= MOD_ROWS
    qk_a = gla_w_lr.shape[3]
    rank = gla_w_lr.shape[2]
    v_a = gla_norm_g.shape[1] * GLA_HEADS
    conv_dim = gdn_conv_w.shape[2]
    v_b = gdn_norm_g.shape[1] * GDN_HEADS
    qk_b = (conv_dim - v_b) // 2
    nd = 2 * GDN_HEADS
    dims = (qk_a, v_a, rank, conv_dim, qk_b, v_b, GDN_HEADS)

    o_r = 2 * qk_a + 2 * v_a
    o_c = o_r + rank
    o_g = o_c + conv_dim
    o_s = o_g + v_b
    assert w_in.shape[2] == o_s + 2 * nd
    w_main = jnp.concatenate([w_in[:, :, :o_r], w_in[:, :, o_c:o_s]], axis=2).astype(BF16)
    w_sm = jnp.concatenate([w_in[:, :, o_r:o_c], w_in[:, :, o_s:]], axis=2)
    w_small = jnp.pad(w_sm, ((0, 0), (0, 0), (0, LANES - w_sm.shape[2]))).astype(BF16)
    w_small_t = jnp.swapaxes(w_in[:, :, o_s:], 1, 2).astype(BF16)

    cc = jnp.concatenate([c, c_ctx[None, :], jnp.zeros((MOD_ROWS - bsz - 1, d), F32)], axis=0)
    mod = _modulation(cc, w_ada, b_ada).reshape(depth * MOD_ROWS, 1, 6 * d)

    in_params = (norm1_g.reshape(depth, 1, d), w_main, w_small, w_small_t, gla_w_lr.astype(BF16),
                 gla_b_lr.reshape(depth, 2, 1, qk_a), gdn_conv_w,
                 gdn_a_log.reshape(depth, 1, nd), gdn_dt_bias.reshape(depth, 1, nd),
                 gdn_a_log.reshape(depth, nd, 1), gdn_dt_bias.reshape(depth, nd, 1))
    post_params = (gla_norm_g.reshape(depth, 1, -1), gdn_norm_g.reshape(depth, 1, -1), w_out.astype(BF16),
                   norm2_g.reshape(depth, 1, d), w_ff1.astype(BF16), w_ff2.astype(BF16))

    stream = (ctx, x)
    for l in range(depth):
        last = l == depth - 1
        gqkv, gg, gla, dqkv, dg, small_c, small_r = _in_proj(stream, mod, l, bsz, in_params, dims)
        oaf, oab, obf, obb = _scan(gqkv, gla, dqkv, small_c, small_r, dims)
        args = (mod, l, last, oaf, oab, obf, obb, gg, dg, post_params, final_norm_g, GLA_HEADS, GDN_HEADS)
        xs = _post(stream, "latent", *args)
        if not last:
            stream = (_post(stream, "ctx", *args), xs)
    return xs
```

```python
import functools
import math

import jax
import jax.numpy as jnp
from jax import lax
from jax.experimental import pallas as pl
from jax.experimental.pallas import tpu as pltpu

EPS = 1e-6
GRID_W = 64
GLA_HEADS = 4
GDN_HEADS = 4
GLA_TAU = 16.0
TOKEN_BLOCK = 256
CHUNK = 64
LANES = 128
SUBLANES = 8
POST_SUB = 2
SCAN_ROWS = 2
IN_ROWS = 4
SCAN_ROW_LAG = 6
MOD_ROWS = 16
VMEM_LIMIT = 56 * 1024 * 1024
SCAN_ORDER = (("b", 0), ("b", 1), ("a", 0), ("a", 1), ("a", 2), ("a", 3), ("b", 2), ("b", 3), ("b", 4),
              ("a", 4), ("b", 5), ("b", 6), ("b", 7), ("b", 8), ("b", 9), ("b", 10), ("b", 11), ("b", 12))

F32 = jnp.float32
BF16 = jnp.bfloat16


def _dot(a, b):
    return jnp.dot(a, b, preferred_element_type=F32)


def _dot_nt(a, b):
    return lax.dot_general(a, b, (((1,), (1,)), ((), ())), preferred_element_type=F32)


def _dot_tn(a, b):
    return lax.dot_general(a, b, (((0,), (0,)), ((), ())), preferred_element_type=F32)


def _silu(x):
    return x / (1.0 + jnp.exp(-x))


def _sigmoid(x):
    return 1.0 / (1.0 + jnp.exp(-x))


def _softplus(x):
    return jnp.maximum(x, 0.0) + jnp.log(1.0 + jnp.exp(-jnp.abs(x)))


def _log_sigmoid(x):
    return jnp.minimum(x, 0.0) - jnp.log(1.0 + jnp.exp(-jnp.abs(x)))


def _rms(x):
    return x * lax.rsqrt(jnp.mean(x * x, axis=-1, keepdims=True) + EPS)


def _cumsum_dot(tri_bf16, x, left):
    hi = x.astype(BF16)
    lo = (x - hi.astype(F32)).astype(BF16)
    if left:
        return _dot(tri_bf16, hi) + _dot(tri_bf16, lo)
    return _dot(hi, tri_bf16) + _dot(lo, tri_bf16)


def _tri(n, upper, strict):
    r = lax.broadcasted_iota(jnp.int32, (n, n), 0)
    c = lax.broadcasted_iota(jnp.int32, (n, n), 1)
    if upper:
        m = (r < c) if strict else (r <= c)
    else:
        m = (r > c) if strict else (r >= c)
    return m


def _couple(n, s, upper):
    r = lax.broadcasted_iota(jnp.int32, (n, n), 0)
    c = lax.broadcasted_iota(jnp.int32, (n, n), 1)
    same = (r & ~(2 * s - 1)) == (c & ~(2 * s - 1))
    r_hi, c_hi = (r & s) != 0, (c & s) != 0
    return same & ((~r_hi & c_hi) if upper else (r_hi & ~c_hi))


def _tri_blocks(n, upper):
    r = lax.broadcasted_iota(jnp.int32, (n, n), 0)
    c = lax.broadcasted_iota(jnp.int32, (n, n), 1)
    same = (r & ~(CHUNK - 1)) == (c & ~(CHUNK - 1))
    return same & ((r <= c) if upper else (r >= c))


def _mod_kernel(cc_ref, w_ref, b_ref, o_ref):
    s = _silu(cc_ref[...]).astype(BF16)
    o_ref[0] = _dot(s, w_ref[0].astype(BF16)) + b_ref[0]


def _modulation(cc, w_ada, b_ada):
    depth, d, d6 = w_ada.shape
    nblk = d6 // d
    return pl.pallas_call(
        _mod_kernel,
        out_shape=jax.ShapeDtypeStruct((depth, MOD_ROWS, d6), F32),
        grid=(depth, nblk),
        in_specs=[
            pl.BlockSpec((MOD_ROWS, d), lambda l, n: (0, 0)),
            pl.BlockSpec((1, d, d), lambda l, n: (l, 0, n)),
            pl.BlockSpec((1, 1, d), lambda l, n: (l, 0, n)),
        ],
        out_specs=pl.BlockSpec((1, MOD_ROWS, d), lambda l, n: (l, 0, n)),
        compiler_params=pltpu.CompilerParams(dimension_semantics=("parallel", "parallel")),
        name="adaln_modulation",
    )(cc, w_ada, b_ada.reshape(depth, 1, d6))


def _in_proj_kernel(c_ref, x_ref, *refs, d, qk_a, v_a, rank, conv_dim, qk_b, v_b, nh_b):
    nr = x_ref.shape[0]
    mod_refs = refs[:nr]
    (g1_ref, wm_ref, ws_ref, wst_ref, wlr_ref, blr_ref, cw_ref, arow_ref, dtrow_ref, acol_ref, dtcol_ref,
     gqkv_ref, gg_ref, gla_ref, dqkv_ref, dg_ref, sc_ref, sr_ref) = refs[nr:]
    for bi in range(nr):
        _in_proj_row(bi, c_ref, x_ref, mod_refs[bi], g1_ref, wm_ref, ws_ref, wst_ref, wlr_ref, blr_ref, cw_ref,
                     arow_ref, dtrow_ref, acol_ref, dtcol_ref, gqkv_ref, gg_ref, gla_ref, dqkv_ref, dg_ref,
                     sc_ref, sr_ref, d=d, qk_a=qk_a, v_a=v_a, rank=rank, conv_dim=conv_dim, qk_b=qk_b, v_b=v_b,
                     nh_b=nh_b)


def _in_proj_row(bi, c_ref, x_ref, mod_ref, g1_ref, wm_ref, ws_ref, wst_ref, wlr_ref, blr_ref, cw_ref,
                 arow_ref, dtrow_ref, acol_ref, dtcol_ref,
                 gqkv_ref, gg_ref, gla_ref, dqkv_ref, dg_ref, sc_ref, sr_ref, *, d, qk_a, v_a, rank, conv_dim,
                 qk_b, v_b, nh_b):
    j = pl.program_id(1)
    x = jnp.where(j == 0, c_ref[bi], x_ref[bi])
    m = mod_ref[0]
    sh1, sc1 = m[:, 0:d], m[:, d:2 * d]
    h = _rms(x) * (g1_ref[...] * (1.0 + sc1)) + sh1
    hb = h.astype(BF16)
    o_conv = 2 * qk_a + 2 * v_a
    o_gate = o_conv + conv_dim
    grp = 2 * LANES
    tb = x.shape[0]
    t = lax.broadcasted_iota(jnp.int32, (tb, 1), 0)
    seg_mask = jnp.where(j == 0, tb - 1, GRID_W - 1)
    first = (t & seg_mask) == 0
    last = (t & seg_mask) == seg_mask
    dkh = qk_b // nh_b

    def conv_dot(k):
        return _dot(hb, wm_ref[:, o_conv + k * grp:o_conv + (k + 1) * grp])

    def zero_rows(a, mask, at):
        pieces = []
        for s0 in range(0, tb, GRID_W):
            r0 = s0 + at
            pieces += [a[s0:r0], jnp.where(mask[r0:r0 + SUBLANES], 0.0, a[r0:r0 + SUBLANES]),
                       a[r0 + SUBLANES:s0 + GRID_W]]
        return jnp.concatenate([p for p in pieces if p.shape[0]], axis=0)

    def conv_group(u, k):
        up = zero_rows(pltpu.roll(u, 1, 0), first, 0)
        un = zero_rows(pltpu.roll(u, tb - 1, 0), last, GRID_W - SUBLANES)
        cw = cw_ref[:, k * grp:(k + 1) * grp]
        s = _silu(cw[0:1] * up + cw[1:2] * u + cw[2:3] * un)
        c0 = k * grp
        if c0 < 2 * qk_b:
            for hh in range(grp // dkh):
                sh = s[:, hh * dkh:(hh + 1) * dkh]
                inv = lax.rsqrt(jnp.sum(sh * sh, axis=-1, keepdims=True) + EPS)
                if c0 < qk_b:
                    inv = inv * (float(dkh) ** -0.5)
                dqkv_ref[bi, :, c0 + hh * dkh:c0 + (hh + 1) * dkh] = (sh * inv).astype(BF16)
        else:
            dqkv_ref[bi, :, c0:c0 + grp] = s.astype(BF16)

    def light_group(k):
        c0 = k * grp
        w0 = c0 if c0 < o_conv else o_gate + (c0 - o_conv)
        pa = _dot(hb, wm_ref[:, w0:w0 + grp])
        if c0 < qk_a:
            gqkv_ref[bi, :, c0:c0 + grp] = (pa * (float(qk_a // GLA_HEADS) ** -0.5)).astype(BF16)
        elif c0 < 2 * qk_a + v_a:
            gqkv_ref[bi, :, c0:c0 + grp] = pa.astype(BF16)
        elif c0 < o_conv:
            gg_ref[bi, :, c0 - 2 * qk_a - v_a:c0 - 2 * qk_a - v_a + grp] = pa.astype(BF16)
        else:
            dg_ref[bi, :, c0 - o_conv:c0 - o_conv + grp] = pa.astype(BF16)

    def small_group(ps, pst):
        r_a = ps[:, 0:rank].astype(BF16)
        for dr in range(2):
            lr = _dot(r_a, wlr_ref[dr]) + blr_ref[dr]
            gla_ref[bi, :, dr * qk_a:(dr + 1) * qk_a] = _log_sigmoid(lr) * (1.0 / GLA_TAU)
        nd = 2 * nh_b
        a_c = ps[:, rank:rank + nd]
        b_c = ps[:, rank + nd:rank + 2 * nd]
        sc_ref[bi, :, 0:nd] = -jnp.exp(arow_ref[...]) * _softplus(a_c + dtrow_ref[...])
        sc_ref[bi, :, nd:2 * nd] = _sigmoid(b_c)
        sr_ref[bi, 0, 0:nd, :] = -jnp.exp(acol_ref[...]) * _softplus(pst[0:nd] + dtcol_ref[...])
        sr_ref[bi, 0, nd:2 * nd, :] = _sigmoid(pst[nd:2 * nd])

    n_conv = conv_dim // grp
    n_light = (o_conv + v_b) // grp
    us = {0: conv_dot(0)}
    ps = _dot(hb, ws_ref[...])
    pst = _dot_nt(wst_ref[...], hb)
    if n_conv > 1:
        us[1] = conv_dot(1)
    g_next = 0
    for k in range(n_conv):
        if k + 2 < n_conv:
            us[k + 2] = conv_dot(k + 2)
        conv_group(us.pop(k), k)
        if k == 1 or n_conv == 1:
            small_group(ps, pst)
        todo = (n_light - g_next + (n_conv - k) - 1) // (n_conv - k) if k >= n_conv // 2 else 1
        for _ in range(min(todo, n_light - g_next)):
            light_group(g_next)
            g_next += 1
    while g_next < n_light:
        light_group(g_next)
        g_next += 1


def _in_proj(stream, mod, layer, bsz, params, dims):
    g1, wm, ws, wst, wlr, blr, cw, a_row, dt_row, a_col, dt_col = params
    head, body = stream
    b, _, d = body.shape
    nb = body.shape[1] // TOKEN_BLOCK + 1
    s = nb * TOKEN_BLOCK
    qk_a, v_a, rank, conv_dim, qk_b, v_b, nh_b = dims
    nd = 2 * nh_b
    tb = TOKEN_BLOCK
    lay = lambda a: pl.BlockSpec((None,) + a.shape[1:], lambda i, j: (layer,) + (0,) * (a.ndim - 1))
    nr = math.gcd(b, IN_ROWS)
    tok = lambda n: pl.BlockSpec((nr, tb, n), lambda i, j: (i, j, 0))
    mod_specs = [pl.BlockSpec((1, 1, 6 * d), lambda i, j, k=k: (
        layer * MOD_ROWS + jnp.where(j == 0, bsz, i * nr + k), 0, 0)) for k in range(nr)]
    out_shape = [
        jax.ShapeDtypeStruct((b, s, 2 * qk_a + v_a), BF16), jax.ShapeDtypeStruct((b, s, v_a), BF16),
        jax.ShapeDtypeStruct((b, s, 2 * qk_a), F32),
        jax.ShapeDtypeStruct((b, s, conv_dim), BF16), jax.ShapeDtypeStruct((b, s, v_b), BF16),
        jax.ShapeDtypeStruct((b, s, 2 * nd), F32),
        jax.ShapeDtypeStruct((b, nb, 2 * nd, tb), F32),
    ]
    out_specs = [tok(2 * qk_a + v_a), tok(v_a), tok(2 * qk_a), tok(conv_dim), tok(v_b), tok(2 * nd),
                 pl.BlockSpec((nr, 1, 2 * nd, tb), lambda i, j: (i, j, 0, 0))]
    kern = functools.partial(_in_proj_kernel, d=d, qk_a=qk_a, v_a=v_a, rank=rank, conv_dim=conv_dim,
                             qk_b=qk_b, v_b=v_b, nh_b=nh_b)
    return pl.pallas_call(
        kern, out_shape=out_shape, grid=(b // nr, nb),
        in_specs=[
            pl.BlockSpec((nr, tb, d), lambda i, j: (i, 0, 0)),
            pl.BlockSpec((nr, tb, d), lambda i, j: (i, jnp.maximum(j - 1, 0), 0)), *mod_specs, lay(g1),
            lay(wm), lay(ws), lay(wst), lay(wlr), lay(blr), lay(cw), lay(a_row), lay(dt_row), lay(a_col), lay(dt_col),
        ],
        out_specs=out_specs,
        compiler_params=pltpu.CompilerParams(dimension_semantics=("parallel", "parallel"),
                                             vmem_limit_bytes=VMEM_LIMIT),
        name="in_proj",
    )(head, body, *([mod] * nr), g1, wm, ws, wst, wlr, blr, cw, a_row, dt_row, a_col, dt_col)


def _bwd_block(j, nb):
    return jnp.where(j == 0, 0, nb - j)


class _Cols:
    def __init__(self, ref, off):
        self.ref, self.off = ref, off

    def __getitem__(self, idx):
        z, rows, cols = idx
        return self.ref[z, rows, self.off + cols.start:self.off + cols.stop]


def _gla_stages(refs, st_ref, dk, dv, bi):
    tb = refs[0][3].shape[1]
    nchunk = tb // CHUNK
    npair = (GLA_HEADS * dk) // LANES
    hpp = LANES // dk
    lane = lax.broadcasted_iota(jnp.int32, (1, LANES), 1)
    lms = [(lane >= hh * dk) & (lane < (hh + 1) * dk) for hh in range(hpp)]
    mid = CHUNK // 2
    chains = [(dr, pr) for dr in range(2) for pr in range(npair)]
    units = [(dr, pr, c) for (dr, pr) in chains for c in range(nchunk)]
    bcum, q_dec, k_hat, q_mid, k_mid, a_last, sc, intra, dst = ({} for _ in range(9))

    def cumulative_decay():
        for dr in range(2):
            both = _cumsum_dot(_tri_blocks(tb, dr == 1).astype(BF16), refs[dr][3][bi], True)
            for pr in range(npair):
                bcum[dr, pr] = both[:, pr * LANES:(pr + 1) * LANES]

    def decayed_operands():
        for u in units:
            dr, pr, c = u
            rows = slice(c * CHUNK, (c + 1) * CHUNK)
            lanes = slice(pr * LANES, (pr + 1) * LANES)
            last = c * CHUNK + (0 if dr == 1 else CHUNK - 1)
            b = bcum[dr, pr][rows]
            b_last = bcum[dr, pr][last:last + 1]
            b_mid = bcum[dr, pr][c * CHUNK + mid:c * CHUNK + mid + 1]
            qc = refs[dr][0][bi, rows, lanes].astype(F32)
            kc = refs[dr][1][bi, rows, lanes].astype(F32)
            q_dec[u] = qc * jnp.exp(b)
            k_hat[u] = (kc * jnp.exp(b_last - b)).astype(BF16)
            q_mid[u] = qc * jnp.exp(b - b_mid)
            k_mid[u] = (kc * jnp.exp(b_mid - b)).astype(BF16)
            a_last[u] = jnp.exp(b_last)

    def per_head(m):
        return jnp.concatenate([jnp.where(lm, m, 0.0).astype(BF16) for lm in lms], axis=0)

    def scores():
        for u in units:
            sc[u] = _dot_nt(per_head(q_mid[u]), k_mid[u])

    def intra_and_increments():
        for u in units:
            dr, pr, c = u
            rows = slice(c * CHUNK, (c + 1) * CHUNK)
            causal = _tri(CHUNK, dr == 1, False)
            v_grp = refs[dr][2][bi, rows, pr * hpp * dv:(pr + 1) * hpp * dv]
            dh = _dot_tn(v_grp, k_hat[u])
            acc = None
            for hh in range(hpp):
                pm = jnp.where(causal, sc[u][hh * CHUNK:(hh + 1) * CHUNK], 0.0).astype(BF16)
                intra[u, hh] = _dot(pm, v_grp[:, hh * dv:(hh + 1) * dv])
                part = dh[hh * dv:(hh + 1) * dv]
                acc = part if acc is None else jnp.where(lms[hh], part, acc)
            dst[u] = acc

    def recurrence():
        st = {ch: st_ref[bi, ch[0], ch[1]] for ch in chains}
        for ci in range(nchunk):
            for ch in chains:
                dr, pr = ch
                c = nchunk - 1 - ci if dr == 1 else ci
                u = (dr, pr, c)
                rows = slice(c * CHUNK, (c + 1) * CHUNK)
                inter = _dot_nt(per_head(q_dec[u]), st[ch].astype(BF16))
                for hh in range(hpp):
                    head = pr * hpp + hh
                    refs[dr][4][bi, rows, head * dv:(head + 1) * dv] = (
                        intra[u, hh] + inter[hh * CHUNK:(hh + 1) * CHUNK]).astype(BF16)
                st[ch] = st[ch] * a_last[u] + dst[u]
        for ch in chains:
            st_ref[bi, ch[0], ch[1]] = st[ch]

    return [cumulative_decay, decayed_operands, scores, intra_and_increments, recurrence]


def _gdn_stages(refs, st_ref, dk, dv, nh, bi):
    tb = refs[0][3].shape[1]
    nchunk = tb // CHUNK
    eye = _tri(CHUNK, False, False) & _tri(CHUNK, True, False)
    chains = [(dr, hd) for dr in range(2) for hd in range(nh)]
    units = [(dr, c, hd) for dr in range(2) for c in range(nchunk) for hd in range(nh)]
    g_cols, g_rows, small, kk, qk, x, aqk, rhs, q_dec, k_dec, a_last = ({} for _ in range(11))
    t, uw, kd_uw, aq_uw, lhs, decay = ({} for _ in range(6))

    def cumulative_decay():
        for dr in range(2):
            small[dr] = refs[dr][3][bi]
            g_cols[dr] = _cumsum_dot(_tri_blocks(tb, dr == 1).astype(BF16), small[dr], True)
            g_rows[dr] = _cumsum_dot(_tri_blocks(tb, dr != 1).astype(BF16), refs[dr][4][bi, 0], False)

    def grams():
        for u in units:
            dr, c, hd = u
            rows = slice(c * CHUNK, (c + 1) * CHUNK)
            kh = refs[dr][1][bi, rows, hd * dk:(hd + 1) * dk]
            qh = refs[dr][0][bi, rows, hd * dk:(hd + 1) * dk]
            kq = _dot_nt(jnp.concatenate([kh, qh], axis=0), kh)
            kk[u], qk[u] = kq[:CHUNK], kq[CHUNK:]

    def solve_operands():
        for u in units:
            dr, c, hd = u
            rows = slice(c * CHUNK, (c + 1) * CHUNK)
            ia = dr * nh + hd
            ib = 2 * nh + dr * nh + hd
            g_c = g_cols[dr][rows, ia:ia + 1]
            g_r = g_rows[dr][ia:ia + 1, rows]
            be_c = small[dr][rows, ib:ib + 1]
            decay[u] = jnp.exp(jnp.minimum(g_c - g_r, 0.0))
            x[u] = jnp.where(_tri(CHUNK, dr == 1, True), be_c * kk[u] * decay[u], 0.0)

    def other_operands():
        for u in units:
            dr, c, hd = u
            rows = slice(c * CHUNK, (c + 1) * CHUNK)
            ia = dr * nh + hd
            ib = 2 * nh + dr * nh + hd
            last = c * CHUNK + (0 if dr == 1 else CHUNK - 1)
            g_c = g_cols[dr][rows, ia:ia + 1]
            be_c = small[dr][rows, ib:ib + 1]
            g_last = g_cols[dr][last:last + 1, ia:ia + 1]
            aqk[u] = jnp.where(_tri(CHUNK, dr == 1, False), qk[u] * decay[u], 0.0).astype(BF16)
            eg = jnp.exp(g_c)
            qh = refs[dr][0][bi, rows, hd * dk:(hd + 1) * dk].astype(F32)
            khf = refs[dr][1][bi, rows, hd * dk:(hd + 1) * dk].astype(F32)
            vh = refs[dr][2][bi, rows, hd * dv:(hd + 1) * dv].astype(F32)
            rhs[u] = jnp.concatenate([(be_c * vh).astype(BF16), ((be_c * eg) * khf).astype(BF16)], axis=1)
            q_dec[u] = qh * eg
            k_dec[u] = khf * jnp.exp(g_last - g_c)
            a_last[u] = jnp.exp(g_last)

    def inverse_start():
        for u in units:
            t[u] = jnp.where(eye, 1.0, jnp.where(_couple(CHUNK, 1, u[0] == 1), -x[u], 0.0))

    def inverse_level(s):
        a = {u: jnp.where(_couple(CHUNK, s, u[0] == 1), x[u], 0.0).astype(BF16) for u in units}
        tb16 = {u: t[u].astype(BF16) for u in units}
        ta = {u: _dot(tb16[u], a[u]).astype(BF16) for u in units}
        for u in units:
            t[u] = t[u] - _dot(ta[u], tb16[u])

    levels = []
    s = 2
    while s < CHUNK:
        levels.append(functools.partial(inverse_level, s))
        s *= 2

    def solve():
        for u in units:
            uw[u] = _dot(t[u].astype(BF16), rhs[u]).astype(BF16)

    def fold():
        for u in units:
            both = _dot(jnp.concatenate([k_dec[u].T.astype(BF16), aqk[u]], axis=0), uw[u])
            kd_uw[u], aq_uw[u] = both[:dk], both[dk:]
        for u in units:
            lhs[u] = jnp.concatenate([(-kd_uw[u][:, dv:]).astype(BF16),
                                      (q_dec[u] - aq_uw[u][:, dv:]).astype(BF16)], axis=0)

    def recurrence():
        st = {ch: st_ref[bi, ch[0], ch[1]] for ch in chains}
        for ci in range(nchunk):
            for ch in chains:
                dr, hd = ch
                c = nchunk - 1 - ci if dr == 1 else ci
                u = (dr, c, hd)
                rows = slice(c * CHUNK, (c + 1) * CHUNK)
                res = _dot(lhs[u], st[ch].astype(BF16))
                refs[dr][5][bi, rows, hd * dv:(hd + 1) * dv] = (res[dk:] + aq_uw[u][:, :dv]).astype(BF16)
                st[ch] = st[ch] * a_last[u] + res[:dk] + kd_uw[u][:, :dv]
        for ch in chains:
            st_ref[bi, ch[0], ch[1]] = st[ch]

    return [cumulative_decay, grams, solve_operands, other_operands, inverse_start, *levels,
            solve, fold, recurrence]


def _scan_kernel(af, alf, ab, alb, bf, bcf, brf, bb, bcb, brb,
                 oaf, oab, obf, obb, sta_ref, stb_ref, *, dk_a, dv_a, dk_b, dv_b, nh_b):
    @pl.when(pl.program_id(1) == 0)
    def _():
        sta_ref[...] = jnp.zeros_like(sta_ref)
        stb_ref[...] = jnp.zeros_like(stb_ref)

    qk_a, qk_b = GLA_HEADS * dk_a, nh_b * dk_b
    qkv = lambda ref, qk: (_Cols(ref, 0), _Cols(ref, qk), _Cols(ref, 2 * qk))
    refs_a = ((*qkv(af, qk_a), alf, oaf), (*qkv(ab, qk_a), alb, oab))
    refs_b = ((*qkv(bf, qk_b), bcf, brf, obf), (*qkv(bb, qk_b), bcb, brb, obb))
    stages = {}
    for bi in range(af.shape[0]):
        stages["a", bi] = _gla_stages(refs_a, sta_ref, dk_a, dv_a, bi)
        stages["b", bi] = _gdn_stages(refs_b, stb_ref, dk_b, dv_b, nh_b, bi)
    lag = SCAN_ROW_LAG
    for pos in range(len(SCAN_ORDER) + lag * (af.shape[0] - 1)):
        for bi in range(af.shape[0]):
            if 0 <= pos - lag * bi < len(SCAN_ORDER):
                name, idx = SCAN_ORDER[pos - lag * bi]
                stages[name, bi][idx]()


def _scan(gqkv, gla, dqkv, small_c, small_r, dims):
    b, s, _ = gqkv.shape
    qk_a, v_a, _, _, qk_b, v_b, _ = dims
    nb = s // TOKEN_BLOCK
    tb = TOKEN_BLOCK
    ns = small_c.shape[2]
    nr = math.gcd(b, SCAN_ROWS)
    fwd = lambda n, col=0: pl.BlockSpec((nr, tb, n), lambda i, j: (i, j, col))
    bwd = lambda n, col=0: pl.BlockSpec((nr, tb, n), lambda i, j: (i, _bwd_block(j, nb), col))
    rfwd = pl.BlockSpec((nr, 1, ns, tb), lambda i, j: (i, j, 0, 0))
    rbwd = pl.BlockSpec((nr, 1, ns, tb), lambda i, j: (i, _bwd_block(j, nb), 0, 0))
    dk_a, dv_a = qk_a // GLA_HEADS, v_a // GLA_HEADS
    dk_b, dv_b = qk_b // GDN_HEADS, v_b // GDN_HEADS
    out = lambda v: jax.ShapeDtypeStruct((b, s, v), BF16)
    return pl.pallas_call(
        functools.partial(_scan_kernel, dk_a=dk_a, dv_a=dv_a, dk_b=dk_b, dv_b=dv_b, nh_b=GDN_HEADS),
        out_shape=[out(v_a), out(v_a), out(v_b), out(v_b)],
        grid=(b // nr, nb),
        in_specs=[fwd(2 * qk_a + v_a), fwd(qk_a, 0), bwd(2 * qk_a + v_a), bwd(qk_a, 1),
                  fwd(2 * qk_b + v_b), fwd(ns), rfwd, bwd(2 * qk_b + v_b), bwd(ns), rbwd],
        out_specs=[fwd(v_a), bwd(v_a), fwd(v_b), bwd(v_b)],
        scratch_shapes=[pltpu.VMEM((nr, 2, qk_a // LANES, dv_a, LANES), F32),
                        pltpu.VMEM((nr, 2, GDN_HEADS, dk_b, dv_b), F32)],
        compiler_params=pltpu.CompilerParams(dimension_semantics=("parallel", "arbitrary"),
                                             vmem_limit_bytes=VMEM_LIMIT),
        name="scan",
    )(gqkv, gla, gqkv, gla, dqkv, small_c, small_r, dqkv, small_c, small_r)


def _post_kernel(x_ref, mod_ref, *refs, d, nh_a, nh_b, final, nsub):
    mix = [refs[6 * i:6 * i + 6] for i in range(nsub)]
    na_ref, nb_ref, wo_ref, g2_ref, w1_ref, w2_ref, fg_ref, o_ref = refs[6 * nsub:]
    tb = x_ref.shape[1] // nsub
    m = mod_ref[0]
    gt1 = m[:, 2 * d:3 * d]
    sh2, sc2, gt2 = m[:, 3 * d:4 * d], m[:, 4 * d:5 * d], m[:, 5 * d:6 * d]
    slab = 2 * LANES
    x1, hb, act = {}, {}, {}

    def merge_and_project(i):
        af_ref, ab_ref, bf_ref, bb_ref, ga_ref, gb_ref = mix[i]
        acc = None
        row = 0
        for (f_ref, b_ref, g_ref, n_ref, nh) in ((af_ref, ab_ref, ga_ref, na_ref, nh_a),
                                                 (bf_ref, bb_ref, gb_ref, nb_ref, nh_b)):
            width = f_ref.shape[2]
            hv = width // nh
            for c0 in range(0, width, slab):
                o = f_ref[0, :, c0:c0 + slab].astype(F32) + b_ref[0, :, c0:c0 + slab].astype(F32)
                gate = _silu(g_ref[0, :, c0:c0 + slab].astype(F32))
                parts = [(_rms(o[:, k:k + hv]) * n_ref[...] * gate[:, k:k + hv]).astype(BF16)
                         for k in range(0, slab, hv)]
                part = _dot(jnp.concatenate(parts, axis=1), wo_ref[row:row + slab, :])
                acc = part if acc is None else acc + part
                row += slab
        x1[i] = x_ref[0, i * tb:(i + 1) * tb, :] + gt1 * acc
        hb[i] = (_rms(x1[i]) * (g2_ref[...] * (1.0 + sc2)) + sh2).astype(BF16)

    def mlp_up(i):
        a = jnp.maximum(_dot(hb[i], w1_ref[...]), 0.0)
        act[i] = (a * a).astype(BF16)

    def mlp_down(i):
        x2 = x1[i] + gt2 * _dot(act[i], w2_ref[...])
        if final:
            x2 = _rms(x2) * fg_ref[...]
        o_ref[0, i * tb:(i + 1) * tb, :] = x2

    merge_and_project(0)
    for i in range(nsub):
        mlp_up(i)
        if i + 1 < nsub:
            merge_and_project(i + 1)
        mlp_down(i)


def _post(stream, part, mod, layer, final, oaf, oab, obf, obb, gg, dg, params, fg, nh_a, nh_b):
    na, nb_, wo, g2, w1, w2 = params
    ctx_arr, lat_arr = stream
    b, t, d = lat_arr.shape
    tb = TOKEN_BLOCK
    ctx = part == "ctx"
    nsub = 1 if ctx else math.gcd(t // tb, POST_SUB)
    nstep = 1 if ctx else t // (nsub * tb)
    first = 0 if ctx else 1
    lay = lambda a: pl.BlockSpec((None,) + a.shape[1:], lambda i, j: (layer,) + (0,) * (a.ndim - 1))
    weight = lambda a: pl.BlockSpec((None,) + a.shape[1:], lambda i, j: (layer,) + (0,) * (a.ndim - 1),
                                    pipeline_mode=pl.Buffered(1))
    bsz = b
    mod_spec = pl.BlockSpec((1, 1, 6 * d), lambda i, j: (layer * MOD_ROWS + (bsz if ctx else i), 0, 0))
    va, vb = oaf.shape[2], obf.shape[2]
    mix_specs, mix_args = [], []
    for k in range(nsub):
        blk = lambda n, k=k: pl.BlockSpec((1, tb, n), lambda i, j: (i, first + j * nsub + k, 0))
        mix_specs += [blk(va), blk(va), blk(vb), blk(vb), blk(va), blk(vb)]
        mix_args += [oaf, oab, obf, obb, gg, dg]
    return pl.pallas_call(
        functools.partial(_post_kernel, d=d, nh_a=nh_a, nh_b=nh_b, final=final, nsub=nsub),
        out_shape=jax.ShapeDtypeStruct((b, nstep * nsub * tb, d), F32),
        grid=(b, nstep),
        in_specs=[pl.BlockSpec((1, nsub * tb, d), lambda i, j: (i, j, 0)), mod_spec, *mix_specs,
                  lay(na), lay(nb_), weight(wo), lay(g2), weight(w1), weight(w2),
                  pl.BlockSpec((1, d), lambda i, j: (0, 0))],
        out_specs=pl.BlockSpec((1, nsub * tb, d), lambda i, j: (i, j, 0)),
        compiler_params=pltpu.CompilerParams(dimension_semantics=("parallel", "parallel"),
                                             vmem_limit_bytes=VMEM_LIMIT),
        name="post_ctx" if ctx else "post",
    )(ctx_arr if ctx else lat_arr, mod, *mix_args, na, nb_, wo, g2, w1, w2, fg.reshape(1, d))


def kernel(x, c, ctx, c_ctx, w_ada, b_ada, norm1_g, norm2_g, w_in, gla_w_lr, gla_b_lr, gdn_conv_w,
           gdn_a_log, gdn_dt_bias, gla_norm_g, gdn_norm_g, w_out, w_ff1, w_ff2, final_norm_g):
    bsz, t, d = x.shape
    depth = w_ada.shape[0]
    assert ctx.shape[1] == TOKEN_BLOCK and t % TOKEN_BLOCK == 0 and TOKEN_BLOCK % GRID_W == 0
    assert bsz + 1 <= MOD_ROWS
    qk_a = gla_w_lr.shape[3]
    rank = gla_w_lr.shape[2]
    v_a = gla_norm_g.shape[1] * GLA_HEADS
    conv_dim = gdn_conv_w.shape[2]
    v_b = gdn_norm_g.shape[1] * GDN_HEADS
    qk_b = (conv_dim - v_b) // 2
    nd = 2 * GDN_HEADS
    dims = (qk_a, v_a, rank, conv_dim, qk_b, v_b, GDN_HEADS)

    o_r = 2 * qk_a + 2 * v_a
    o_c = o_r + rank
    o_g = o_c + conv_dim
    o_s = o_g + v_b
    assert w_in.shape[2] == o_s + 2 * nd
    w_main = jnp.concatenate([w_in[:, :, :o_r], w_in[:, :, o_c:o_s]], axis=2).astype(BF16)
    w_sm = jnp.concatenate([w_in[:, :, o_r:o_c], w_in[:, :, o_s:]], axis=2)
    w_small = jnp.pad(w_sm, ((0, 0), (0, 0), (0, LANES - w_sm.shape[2]))).astype(BF16)
    w_small_t = jnp.swapaxes(w_in[:, :, o_s:], 1, 2).astype(BF16)

    cc = jnp.concatenate([c, c_ctx[None, :], jnp.zeros((MOD_ROWS - bsz - 1, d), F32)], axis=0)
    mod = _modulation(cc, w_ada, b_ada).reshape(depth * MOD_ROWS, 1, 6 * d)

    in_params = (norm1_g.reshape(depth, 1, d), w_main, w_small, w_small_t, gla_w_lr.astype(BF16),
                 gla_b_lr.reshape(depth, 2, 1, qk_a), gdn_conv_w,
                 gdn_a_log.reshape(depth, 1, nd), gdn_dt_bias.reshape(depth, 1, nd),
                 gdn_a_log.reshape(depth, nd, 1), gdn_dt_bias.reshape(depth, nd, 1))
    post_params = (gla_norm_g.reshape(depth, 1, -1), gdn_norm_g.reshape(depth, 1, -1), w_out.astype(BF16),
                   norm2_g.reshape(depth, 1, d), w_ff1.astype(BF16), w_ff2.astype(BF16))

    stream = (ctx, x)
    for l in range(depth):
        last = l == depth - 1
        gqkv, gg, gla, dqkv, dg, small_c, small_r = _in_proj(stream, mod, l, bsz, in_params, dims)
        oaf, oab, obf, obb = _scan(gqkv, gla, dqkv, small_c, small_r, dims)
        args = (mod, l, last, oaf, oab, obf, obb, gg, dg, post_params, final_norm_g, GLA_HEADS, GDN_HEADS)
        xs = _post(stream, "latent", *args)
        if not last:
            stream = (_post(stream, "ctx", *args), xs)
    return xs
```

```python
import functools
import math

import jax
import jax.numpy as jnp
from jax import lax
from jax.experimental import pallas as pl
from jax.experimental.pallas import tpu as pltpu

EPS = 1e-6
GRID_W = 64
GLA_HEADS = 4
GDN_HEADS = 4
GLA_TAU = 16.0
TOKEN_BLOCK = 256
CHUNK = 64
LANES = 128
SUBLANES = 8
POST_SUB = 2
SCAN_ROWS = 2
IN_ROWS = 4
SCAN_ROW_LAG = 6
MOD_ROWS = 16
VMEM_LIMIT = 56 * 1024 * 1024
SCAN_ORDER = (("b", 0), ("b", 1), ("a", 0), ("a", 1), ("a", 2), ("a", 3), ("b", 2), ("b", 3), ("b", 4),
              ("b", 5), ("b", 6), ("b", 7), ("a", 4), ("b", 8), ("b", 9), ("b", 10), ("b", 11), ("b", 12))

F32 = jnp.float32
BF16 = jnp.bfloat16


def _dot(a, b):
    return jnp.dot(a, b, preferred_element_type=F32)


def _dot_nt(a, b):
    return lax.dot_general(a, b, (((1,), (1,)), ((), ())), preferred_element_type=F32)


def _dot_tn(a, b):
    return lax.dot_general(a, b, (((0,), (0,)), ((), ())), preferred_element_type=F32)


def _silu(x):
    return x / (1.0 + jnp.exp(-x))


def _sigmoid(x):
    return 1.0 / (1.0 + jnp.exp(-x))


def _softplus(x):
    return jnp.maximum(x, 0.0) + jnp.log(1.0 + jnp.exp(-jnp.abs(x)))


def _log_sigmoid(x):
    return jnp.minimum(x, 0.0) - jnp.log(1.0 + jnp.exp(-jnp.abs(x)))


def _rms(x):
    return x * lax.rsqrt(jnp.mean(x * x, axis=-1, keepdims=True) + EPS)


def _cumsum_dot(tri_bf16, x, left):
    hi = x.astype(BF16)
    lo = (x - hi.astype(F32)).astype(BF16)
    if left:
        return _dot(tri_bf16, hi) + _dot(tri_bf16, lo)
    return _dot(hi, tri_bf16) + _dot(lo, tri_bf16)


def _tri(n, upper, strict):
    r = lax.broadcasted_iota(jnp.int32, (n, n), 0)
    c = lax.broadcasted_iota(jnp.int32, (n, n), 1)
    if upper:
        m = (r < c) if strict else (r <= c)
    else:
        m = (r > c) if strict else (r >= c)
    return m


def _couple(n, s, upper):
    r = lax.broadcasted_iota(jnp.int32, (n, n), 0)
    c = lax.broadcasted_iota(jnp.int32, (n, n), 1)
    same = (r & ~(2 * s - 1)) == (c & ~(2 * s - 1))
    r_hi, c_hi = (r & s) != 0, (c & s) != 0
    return same & ((~r_hi & c_hi) if upper else (r_hi & ~c_hi))


def _tri_blocks(n, upper):
    r = lax.broadcasted_iota(jnp.int32, (n, n), 0)
    c = lax.broadcasted_iota(jnp.int32, (n, n), 1)
    same = (r & ~(CHUNK - 1)) == (c & ~(CHUNK - 1))
    return same & ((r <= c) if upper else (r >= c))


def _mod_kernel(cc_ref, w_ref, b_ref, o_ref):
    s = _silu(cc_ref[...]).astype(BF16)
    o_ref[0] = _dot(s, w_ref[0].astype(BF16)) + b_ref[0]


def _modulation(cc, w_ada, b_ada):
    depth, d, d6 = w_ada.shape
    nblk = d6 // d
    return pl.pallas_call(
        _mod_kernel,
        out_shape=jax.ShapeDtypeStruct((depth, MOD_ROWS, d6), F32),
        grid=(depth, nblk),
        in_specs=[
            pl.BlockSpec((MOD_ROWS, d), lambda l, n: (0, 0)),
            pl.BlockSpec((1, d, d), lambda l, n: (l, 0, n)),
            pl.BlockSpec((1, 1, d), lambda l, n: (l, 0, n)),
        ],
        out_specs=pl.BlockSpec((1, MOD_ROWS, d), lambda l, n: (l, 0, n)),
        compiler_params=pltpu.CompilerParams(dimension_semantics=("parallel", "parallel")),
        name="adaln_modulation",
    )(cc, w_ada, b_ada.reshape(depth, 1, d6))


def _in_proj_kernel(c_ref, x_ref, *refs, d, qk_a, v_a, rank, conv_dim, qk_b, v_b, nh_b):
    nr = x_ref.shape[0]
    mod_refs = refs[:nr]
    (g1_ref, wm_ref, ws_ref, wst_ref, wlr_ref, blr_ref, cw_ref, arow_ref, dtrow_ref, acol_ref, dtcol_ref,
     gqkv_ref, gg_ref, gla_ref, dqkv_ref, dg_ref, sc_ref, sr_ref) = refs[nr:]
    for bi in range(nr):
        _in_proj_row(bi, c_ref, x_ref, mod_refs[bi], g1_ref, wm_ref, ws_ref, wst_ref, wlr_ref, blr_ref, cw_ref,
                     arow_ref, dtrow_ref, acol_ref, dtcol_ref, gqkv_ref, gg_ref, gla_ref, dqkv_ref, dg_ref,
                     sc_ref, sr_ref, d=d, qk_a=qk_a, v_a=v_a, rank=rank, conv_dim=conv_dim, qk_b=qk_b, v_b=v_b,
                     nh_b=nh_b)


def _in_proj_row(bi, c_ref, x_ref, mod_ref, g1_ref, wm_ref, ws_ref, wst_ref, wlr_ref, blr_ref, cw_ref,
                 arow_ref, dtrow_ref, acol_ref, dtcol_ref,
                 gqkv_ref, gg_ref, gla_ref, dqkv_ref, dg_ref, sc_ref, sr_ref, *, d, qk_a, v_a, rank, conv_dim,
                 qk_b, v_b, nh_b):
    j = pl.program_id(1)
    x = jnp.where(j == 0, c_ref[bi], x_ref[bi])
    m = mod_ref[0]
    sh1, sc1 = m[:, 0:d], m[:, d:2 * d]
    h = _rms(x) * (g1_ref[...] * (1.0 + sc1)) + sh1
    hb = h.astype(BF16)
    o_conv = 2 * qk_a + 2 * v_a
    o_gate = o_conv + conv_dim
    grp = 2 * LANES
    tb = x.shape[0]
    t = lax.broadcasted_iota(jnp.int32, (tb, 1), 0)
    seg_mask = jnp.where(j == 0, tb - 1, GRID_W - 1)
    first = (t & seg_mask) == 0
    last = (t & seg_mask) == seg_mask
    dkh = qk_b // nh_b

    def conv_dot(k):
        return _dot(hb, wm_ref[:, o_conv + k * grp:o_conv + (k + 1) * grp])

    def zero_rows(a, mask, at):
        pieces = []
        for s0 in range(0, tb, GRID_W):
            r0 = s0 + at
            pieces += [a[s0:r0], jnp.where(mask[r0:r0 + SUBLANES], 0.0, a[r0:r0 + SUBLANES]),
                       a[r0 + SUBLANES:s0 + GRID_W]]
        return jnp.concatenate([p for p in pieces if p.shape[0]], axis=0)

    def conv_group(u, k):
        up = zero_rows(pltpu.roll(u, 1, 0), first, 0)
        un = zero_rows(pltpu.roll(u, tb - 1, 0), last, GRID_W - SUBLANES)
        cw = cw_ref[:, k * grp:(k + 1) * grp]
        s = _silu(cw[0:1] * up + cw[1:2] * u + cw[2:3] * un)
        c0 = k * grp
        if c0 < 2 * qk_b:
            for hh in range(grp // dkh):
                sh = s[:, hh * dkh:(hh + 1) * dkh]
                inv = lax.rsqrt(jnp.sum(sh * sh, axis=-1, keepdims=True) + EPS)
                if c0 < qk_b:
                    inv = inv * (float(dkh) ** -0.5)
                dqkv_ref[bi, :, c0 + hh * dkh:c0 + (hh + 1) * dkh] = (sh * inv).astype(BF16)
        else:
            dqkv_ref[bi, :, c0:c0 + grp] = s.astype(BF16)

    def light_group(k):
        c0 = k * grp
        w0 = c0 if c0 < o_conv else o_gate + (c0 - o_conv)
        pa = _dot(hb, wm_ref[:, w0:w0 + grp])
        if c0 < qk_a:
            gqkv_ref[bi, :, c0:c0 + grp] = (pa * (float(qk_a // GLA_HEADS) ** -0.5)).astype(BF16)
        elif c0 < 2 * qk_a + v_a:
            gqkv_ref[bi, :, c0:c0 + grp] = pa.astype(BF16)
        elif c0 < o_conv:
            gg_ref[bi, :, c0 - 2 * qk_a - v_a:c0 - 2 * qk_a - v_a + grp] = pa.astype(BF16)
        else:
            dg_ref[bi, :, c0 - o_conv:c0 - o_conv + grp] = pa.astype(BF16)

    def small_group(ps, pst):
        r_a = ps[:, 0:rank].astype(BF16)
        for dr in range(2):
            lr = _dot(r_a, wlr_ref[dr]) + blr_ref[dr]
            gla_ref[bi, :, dr * qk_a:(dr + 1) * qk_a] = _log_sigmoid(lr) * (1.0 / GLA_TAU)
        nd = 2 * nh_b
        a_c = ps[:, rank:rank + nd]
        b_c = ps[:, rank + nd:rank + 2 * nd]
        sc_ref[bi, :, 0:nd] = -jnp.exp(arow_ref[...]) * _softplus(a_c + dtrow_ref[...])
        sc_ref[bi, :, nd:2 * nd] = _sigmoid(b_c)
        sr_ref[bi, 0, 0:nd, :] = -jnp.exp(acol_ref[...]) * _softplus(pst[0:nd] + dtcol_ref[...])
        sr_ref[bi, 0, nd:2 * nd, :] = _sigmoid(pst[nd:2 * nd])

    n_conv = conv_dim // grp
    n_light = (o_conv + v_b) // grp
    us = {0: conv_dot(0)}
    ps = _dot(hb, ws_ref[...])
    pst = _dot_nt(wst_ref[...], hb)
    if n_conv > 1:
        us[1] = conv_dot(1)
    g_next = 0
    for k in range(n_conv):
        if k + 2 < n_conv:
            us[k + 2] = conv_dot(k + 2)
        conv_group(us.pop(k), k)
        if k == 1 or n_conv == 1:
            small_group(ps, pst)
        todo = (n_light - g_next + (n_conv - k) - 1) // (n_conv - k) if k >= n_conv // 2 else 1
        for _ in range(min(todo, n_light - g_next)):
            light_group(g_next)
            g_next += 1
    while g_next < n_light:
        light_group(g_next)
        g_next += 1


def _in_proj(stream, mod, layer, bsz, params, dims):
    g1, wm, ws, wst, wlr, blr, cw, a_row, dt_row, a_col, dt_col = params
    head, body = stream
    b, _, d = body.shape
    nb = body.shape[1] // TOKEN_BLOCK + 1
    s = nb * TOKEN_BLOCK
    qk_a, v_a, rank, conv_dim, qk_b, v_b, nh_b = dims
    nd = 2 * nh_b
    tb = TOKEN_BLOCK
    lay = lambda a: pl.BlockSpec((None,) + a.shape[1:], lambda i, j: (layer,) + (0,) * (a.ndim - 1))
    nr = math.gcd(b, IN_ROWS)
    tok = lambda n: pl.BlockSpec((nr, tb, n), lambda i, j: (i, j, 0))
    mod_specs = [pl.BlockSpec((1, 1, 6 * d), lambda i, j, k=k: (
        layer * MOD_ROWS + jnp.where(j == 0, bsz, i * nr + k), 0, 0)) for k in range(nr)]
    out_shape = [
        jax.ShapeDtypeStruct((b, s, 2 * qk_a + v_a), BF16), jax.ShapeDtypeStruct((b, s, v_a), BF16),
        jax.ShapeDtypeStruct((b, s, 2 * qk_a), F32),
        jax.ShapeDtypeStruct((b, s, conv_dim), BF16), jax.ShapeDtypeStruct((b, s, v_b), BF16),
        jax.ShapeDtypeStruct((b, s, 2 * nd), F32),
        jax.ShapeDtypeStruct((b, nb, 2 * nd, tb), F32),
    ]
    out_specs = [tok(2 * qk_a + v_a), tok(v_a), tok(2 * qk_a), tok(conv_dim), tok(v_b), tok(2 * nd),
                 pl.BlockSpec((nr, 1, 2 * nd, tb), lambda i, j: (i, j, 0, 0))]
    kern = functools.partial(_in_proj_kernel, d=d, qk_a=qk_a, v_a=v_a, rank=rank, conv_dim=conv_dim,
                             qk_b=qk_b, v_b=v_b, nh_b=nh_b)
    return pl.pallas_call(
        kern, out_shape=out_shape, grid=(b // nr, nb),
        in_specs=[
            pl.BlockSpec((nr, tb, d), lambda i, j: (i, 0, 0)),
            pl.BlockSpec((nr, tb, d), lambda i, j: (i, jnp.maximum(j - 1, 0), 0)), *mod_specs, lay(g1),
            lay(wm), lay(ws), lay(wst), lay(wlr), lay(blr), lay(cw), lay(a_row), lay(dt_row), lay(a_col), lay(dt_col),
        ],
        out_specs=out_specs,
        compiler_params=pltpu.CompilerParams(dimension_semantics=("parallel", "parallel"),
                                             vmem_limit_bytes=VMEM_LIMIT),
        name="in_proj",
    )(head, body, *([mod] * nr), g1, wm, ws, wst, wlr, blr, cw, a_row, dt_row, a_col, dt_col)


def _bwd_block(j, nb):
    return jnp.where(j == 0, 0, nb - j)


class _Cols:
    def __init__(self, ref, off):
        self.ref, self.off = ref, off

    def __getitem__(self, idx):
        z, rows, cols = idx
        return self.ref[z, rows, self.off + cols.start:self.off + cols.stop]


def _gla_stages(refs, st_ref, dk, dv, bi):
    tb = refs[0][3].shape[1]
    nchunk = tb // CHUNK
    npair = (GLA_HEADS * dk) // LANES
    hpp = LANES // dk
    lane = lax.broadcasted_iota(jnp.int32, (1, LANES), 1)
    lms = [(lane >= hh * dk) & (lane < (hh + 1) * dk) for hh in range(hpp)]
    mid = CHUNK // 2
    chains = [(dr, pr) for dr in range(2) for pr in range(npair)]
    units = [(dr, pr, c) for (dr, pr) in chains for c in range(nchunk)]
    bcum, q_dec, k_hat, q_mid, k_mid, a_last, sc, intra, dst = ({} for _ in range(9))

    def cumulative_decay():
        for dr in range(2):
            both = _cumsum_dot(_tri_blocks(tb, dr == 1).astype(BF16), refs[dr][3][bi], True)
            for pr in range(npair):
                bcum[dr, pr] = both[:, pr * LANES:(pr + 1) * LANES]

    def decayed_operands():
        for u in units:
            dr, pr, c = u
            rows = slice(c * CHUNK, (c + 1) * CHUNK)
            lanes = slice(pr * LANES, (pr + 1) * LANES)
            last = c * CHUNK + (0 if dr == 1 else CHUNK - 1)
            b = bcum[dr, pr][rows]
            b_last = bcum[dr, pr][last:last + 1]
            b_mid = bcum[dr, pr][c * CHUNK + mid:c * CHUNK + mid + 1]
            qc = refs[dr][0][bi, rows, lanes].astype(F32)
            kc = refs[dr][1][bi, rows, lanes].astype(F32)
            q_dec[u] = qc * jnp.exp(b)
            k_hat[u] = (kc * jnp.exp(b_last - b)).astype(BF16)
            q_mid[u] = qc * jnp.exp(b - b_mid)
            k_mid[u] = (kc * jnp.exp(b_mid - b)).astype(BF16)
            a_last[u] = jnp.exp(b_last)

    def per_head(m):
        return jnp.concatenate([jnp.where(lm, m, 0.0).astype(BF16) for lm in lms], axis=0)

    def scores():
        for u in units:
            sc[u] = _dot_nt(per_head(q_mid[u]), k_mid[u])

    def intra_and_increments():
        for u in units:
            dr, pr, c = u
            rows = slice(c * CHUNK, (c + 1) * CHUNK)
            causal = _tri(CHUNK, dr == 1, False)
            v_grp = refs[dr][2][bi, rows, pr * hpp * dv:(pr + 1) * hpp * dv]
            dh = _dot_tn(v_grp, k_hat[u])
            acc = None
            for hh in range(hpp):
                pm = jnp.where(causal, sc[u][hh * CHUNK:(hh + 1) * CHUNK], 0.0).astype(BF16)
                intra[u, hh] = _dot(pm, v_grp[:, hh * dv:(hh + 1) * dv])
                part = dh[hh * dv:(hh + 1) * dv]
                acc = part if acc is None else jnp.where(lms[hh], part, acc)
            dst[u] = acc

    def recurrence():
        st = {ch: st_ref[bi, ch[0], ch[1]] for ch in chains}
        for ci in range(nchunk):
            for ch in chains:
                dr, pr = ch
                c = nchunk - 1 - ci if dr == 1 else ci
                u = (dr, pr, c)
                rows = slice(c * CHUNK, (c + 1) * CHUNK)
                inter = _dot_nt(per_head(q_dec[u]), st[ch].astype(BF16))
                for hh in range(hpp):
                    head = pr * hpp + hh
                    refs[dr][4][bi, rows, head * dv:(head + 1) * dv] = (
                        intra[u, hh] + inter[hh * CHUNK:(hh + 1) * CHUNK]).astype(BF16)
                st[ch] = st[ch] * a_last[u] + dst[u]
        for ch in chains:
            st_ref[bi, ch[0], ch[1]] = st[ch]

    return [cumulative_decay, decayed_operands, scores, intra_and_increments, recurrence]


def _gdn_stages(refs, st_ref, dk, dv, nh, bi):
    tb = refs[0][3].shape[1]
    nchunk = tb // CHUNK
    eye = _tri(CHUNK, False, False) & _tri(CHUNK, True, False)
    chains = [(dr, hd) for dr in range(2) for hd in range(nh)]
    units = [(dr, c, hd) for dr in range(2) for c in range(nchunk) for hd in range(nh)]
    g_cols, g_rows, small, kk, qk, x, aqk, rhs, q_dec, k_dec, a_last = ({} for _ in range(11))
    t, uw, kd_uw, aq_uw, lhs, decay = ({} for _ in range(6))

    def cumulative_decay():
        for dr in range(2):
            small[dr] = refs[dr][3][bi]
            g_cols[dr] = _cumsum_dot(_tri_blocks(tb, dr == 1).astype(BF16), small[dr], True)
            g_rows[dr] = _cumsum_dot(_tri_blocks(tb, dr != 1).astype(BF16), refs[dr][4][bi, 0], False)

    def grams():
        for u in units:
            dr, c, hd = u
            rows = slice(c * CHUNK, (c + 1) * CHUNK)
            kh = refs[dr][1][bi, rows, hd * dk:(hd + 1) * dk]
            qh = refs[dr][0][bi, rows, hd * dk:(hd + 1) * dk]
            kq = _dot_nt(jnp.concatenate([kh, qh], axis=0), kh)
            kk[u], qk[u] = kq[:CHUNK], kq[CHUNK:]

    def solve_operands():
        for u in units:
            dr, c, hd = u
            rows = slice(c * CHUNK, (c + 1) * CHUNK)
            ia = dr * nh + hd
            ib = 2 * nh + dr * nh + hd
            g_c = g_cols[dr][rows, ia:ia + 1]
            g_r = g_rows[dr][ia:ia + 1, rows]
            be_c = small[dr][rows, ib:ib + 1]
            decay[u] = jnp.exp(jnp.minimum(g_c - g_r, 0.0))
            x[u] = jnp.where(_tri(CHUNK, dr == 1, True), be_c * kk[u] * decay[u], 0.0)

    def other_operands():
        for u in units:
            dr, c, hd = u
            rows = slice(c * CHUNK, (c + 1) * CHUNK)
            ia = dr * nh + hd
            ib = 2 * nh + dr * nh + hd
            last = c * CHUNK + (0 if dr == 1 else CHUNK - 1)
            g_c = g_cols[dr][rows, ia:ia + 1]
            be_c = small[dr][rows, ib:ib + 1]
            g_last = g_cols[dr][last:last + 1, ia:ia + 1]
            aqk[u] = jnp.where(_tri(CHUNK, dr == 1, False), qk[u] * decay[u], 0.0).astype(BF16)
            eg = jnp.exp(g_c)
            qh = refs[dr][0][bi, rows, hd * dk:(hd + 1) * dk].astype(F32)
            khf = refs[dr][1][bi, rows, hd * dk:(hd + 1) * dk].astype(F32)
            vh = refs[dr][2][bi, rows, hd * dv:(hd + 1) * dv].astype(F32)
            rhs[u] = jnp.concatenate([(be_c * vh).astype(BF16), ((be_c * eg) * khf).astype(BF16)], axis=1)
            q_dec[u] = qh * eg
            k_dec[u] = khf * jnp.exp(g_last - g_c)
            a_last[u] = jnp.exp(g_last)

    def inverse_start():
        for u in units:
            t[u] = jnp.where(eye, 1.0, jnp.where(_couple(CHUNK, 1, u[0] == 1), -x[u], 0.0))

    def inverse_level(s):
        a = {u: jnp.where(_couple(CHUNK, s, u[0] == 1), x[u], 0.0).astype(BF16) for u in units}
        tb16 = {u: t[u].astype(BF16) for u in units}
        ta = {u: _dot(tb16[u], a[u]).astype(BF16) for u in units}
        for u in units:
            t[u] = t[u] - _dot(ta[u], tb16[u])

    levels = []
    s = 2
    while s < CHUNK:
        levels.append(functools.partial(inverse_level, s))
        s *= 2

    def solve():
        for u in units:
            uw[u] = _dot(t[u].astype(BF16), rhs[u]).astype(BF16)

    def fold():
        for u in units:
            both = _dot(jnp.concatenate([k_dec[u].T.astype(BF16), aqk[u]], axis=0), uw[u])
            kd_uw[u], aq_uw[u] = both[:dk], both[dk:]
        for u in units:
            lhs[u] = jnp.concatenate([(-kd_uw[u][:, dv:]).astype(BF16),
                                      (q_dec[u] - aq_uw[u][:, dv:]).astype(BF16)], axis=0)

    def recurrence():
        st = {ch: st_ref[bi, ch[0], ch[1]] for ch in chains}
        for ci in range(nchunk):
            for ch in chains:
                dr, hd = ch
                c = nchunk - 1 - ci if dr == 1 else ci
                u = (dr, c, hd)
                rows = slice(c * CHUNK, (c + 1) * CHUNK)
                res = _dot(lhs[u], st[ch].astype(BF16))
                refs[dr][5][bi, rows, hd * dv:(hd + 1) * dv] = (res[dk:] + aq_uw[u][:, :dv]).astype(BF16)
                st[ch] = st[ch] * a_last[u] + res[:dk] + kd_uw[u][:, :dv]
        for ch in chains:
            st_ref[bi, ch[0], ch[1]] = st[ch]

    return [cumulative_decay, grams, solve_operands, other_operands, inverse_start, *levels,
            solve, fold, recurrence]


def _scan_kernel(af, alf, ab, alb, bf, bcf, brf, bb, bcb, brb,
                 oaf, oab, obf, obb, sta_ref, stb_ref, *, dk_a, dv_a, dk_b, dv_b, nh_b):
    @pl.when(pl.program_id(1) == 0)
    def _():
        sta_ref[...] = jnp.zeros_like(sta_ref)
        stb_ref[...] = jnp.zeros_like(stb_ref)

    qk_a, qk_b = GLA_HEADS * dk_a, nh_b * dk_b
    qkv = lambda ref, qk: (_Cols(ref, 0), _Cols(ref, qk), _Cols(ref, 2 * qk))
    refs_a = ((*qkv(af, qk_a), alf, oaf), (*qkv(ab, qk_a), alb, oab))
    refs_b = ((*qkv(bf, qk_b), bcf, brf, obf), (*qkv(bb, qk_b), bcb, brb, obb))
    stages = {}
    for bi in range(af.shape[0]):
        stages["a", bi] = _gla_stages(refs_a, sta_ref, dk_a, dv_a, bi)
        stages["b", bi] = _gdn_stages(refs_b, stb_ref, dk_b, dv_b, nh_b, bi)
    lag = SCAN_ROW_LAG
    for pos in range(len(SCAN_ORDER) + lag * (af.shape[0] - 1)):
        for bi in range(af.shape[0]):
            if 0 <= pos - lag * bi < len(SCAN_ORDER):
                name, idx = SCAN_ORDER[pos - lag * bi]
                stages[name, bi][idx]()


def _scan(gqkv, gla, dqkv, small_c, small_r, dims):
    b, s, _ = gqkv.shape
    qk_a, v_a, _, _, qk_b, v_b, _ = dims
    nb = s // TOKEN_BLOCK
    tb = TOKEN_BLOCK
    ns = small_c.shape[2]
    nr = math.gcd(b, SCAN_ROWS)
    fwd = lambda n, col=0: pl.BlockSpec((nr, tb, n), lambda i, j: (i, j, col))
    bwd = lambda n, col=0: pl.BlockSpec((nr, tb, n), lambda i, j: (i, _bwd_block(j, nb), col))
    rfwd = pl.BlockSpec((nr, 1, ns, tb), lambda i, j: (i, j, 0, 0))
    rbwd = pl.BlockSpec((nr, 1, ns, tb), lambda i, j: (i, _bwd_block(j, nb), 0, 0))
    dk_a, dv_a = qk_a // GLA_HEADS, v_a // GLA_HEADS
    dk_b, dv_b = qk_b // GDN_HEADS, v_b // GDN_HEADS
    out = lambda v: jax.ShapeDtypeStruct((b, s, v), BF16)
    return pl.pallas_call(
        functools.partial(_scan_kernel, dk_a=dk_a, dv_a=dv_a, dk_b=dk_b, dv_b=dv_b, nh_b=GDN_HEADS),
        out_shape=[out(v_a), out(v_a), out(v_b), out(v_b)],
        grid=(b // nr, nb),
        in_specs=[fwd(2 * qk_a + v_a), fwd(qk_a, 0), bwd(2 * qk_a + v_a), bwd(qk_a, 1),
                  fwd(2 * qk_b + v_b), fwd(ns), rfwd, bwd(2 * qk_b + v_b), bwd(ns), rbwd],
        out_specs=[fwd(v_a), bwd(v_a), fwd(v_b), bwd(v_b)],
        scratch_shapes=[pltpu.VMEM((nr, 2, qk_a // LANES, dv_a, LANES), F32),
                        pltpu.VMEM((nr, 2, GDN_HEADS, dk_b, dv_b), F32)],
        compiler_params=pltpu.CompilerParams(dimension_semantics=("parallel", "arbitrary"),
                                             vmem_limit_bytes=VMEM_LIMIT),
        name="scan",
    )(gqkv, gla, gqkv, gla, dqkv, small_c, small_r, dqkv, small_c, small_r)


def _post_kernel(x_ref, mod_ref, *refs, d, nh_a, nh_b, final, nsub):
    mix = [refs[6 * i:6 * i + 6] for i in range(nsub)]
    na_ref, nb_ref, wo_ref, g2_ref, w1_ref, w2_ref, fg_ref, o_ref = refs[6 * nsub:]
    tb = x_ref.shape[1] // nsub
    m = mod_ref[0]
    gt1 = m[:, 2 * d:3 * d]
    sh2, sc2, gt2 = m[:, 3 * d:4 * d], m[:, 4 * d:5 * d], m[:, 5 * d:6 * d]
    slab = 2 * LANES
    x1, hb, act = {}, {}, {}

    def merge_and_project(i):
        af_ref, ab_ref, bf_ref, bb_ref, ga_ref, gb_ref = mix[i]
        acc = None
        row = 0
        for (f_ref, b_ref, g_ref, n_ref, nh) in ((af_ref, ab_ref, ga_ref, na_ref, nh_a),
                                                 (bf_ref, bb_ref, gb_ref, nb_ref, nh_b)):
            width = f_ref.shape[2]
            hv = width // nh
            for c0 in range(0, width, slab):
                o = f_ref[0, :, c0:c0 + slab].astype(F32) + b_ref[0, :, c0:c0 + slab].astype(F32)
                gate = _silu(g_ref[0, :, c0:c0 + slab].astype(F32))
                parts = [(_rms(o[:, k:k + hv]) * n_ref[...] * gate[:, k:k + hv]).astype(BF16)
                         for k in range(0, slab, hv)]
                part = _dot(jnp.concatenate(parts, axis=1), wo_ref[row:row + slab, :])
                acc = part if acc is None else acc + part
                row += slab
        x1[i] = x_ref[0, i * tb:(i + 1) * tb, :] + gt1 * acc
        hb[i] = (_rms(x1[i]) * (g2_ref[...] * (1.0 + sc2)) + sh2).astype(BF16)

    def mlp_up(i):
        a = jnp.maximum(_dot(hb[i], w1_ref[...]), 0.0)
        act[i] = (a * a).astype(BF16)

    def mlp_down(i):
        x2 = x1[i] + gt2 * _dot(act[i], w2_ref[...])
        if final:
            x2 = _rms(x2) * fg_ref[...]
        o_ref[0, i * tb:(i + 1) * tb, :] = x2

    merge_and_project(0)
    for i in range(nsub):
        mlp_up(i)
        if i + 1 < nsub:
            merge_and_project(i + 1)
        mlp_down(i)


def _post(stream, part, mod, layer, final, oaf, oab, obf, obb, gg, dg, params, fg, nh_a, nh_b):
    na, nb_, wo, g2, w1, w2 = params
    ctx_arr, lat_arr = stream
    b, t, d = lat_arr.shape
    tb = TOKEN_BLOCK
    ctx = part == "ctx"
    nsub = 1 if ctx else math.gcd(t // tb, POST_SUB)
    nstep = 1 if ctx else t // (nsub * tb)
    first = 0 if ctx else 1
    lay = lambda a: pl.BlockSpec((None,) + a.shape[1:], lambda i, j: (layer,) + (0,) * (a.ndim - 1))
    weight = lambda a: pl.BlockSpec((None,) + a.shape[1:], lambda i, j: (layer,) + (0,) * (a.ndim - 1),
                                    pipeline_mode=pl.Buffered(1))
    bsz = b
    mod_spec = pl.BlockSpec((1, 1, 6 * d), lambda i, j: (layer * MOD_ROWS + (bsz if ctx else i), 0, 0))
    va, vb = oaf.shape[2], obf.shape[2]
    mix_specs, mix_args = [], []
    for k in range(nsub):
        blk = lambda n, k=k: pl.BlockSpec((1, tb, n), lambda i, j: (i, first + j * nsub + k, 0))
        mix_specs += [blk(va), blk(va), blk(vb), blk(vb), blk(va), blk(vb)]
        mix_args += [oaf, oab, obf, obb, gg, dg]
    return pl.pallas_call(
        functools.partial(_post_kernel, d=d, nh_a=nh_a, nh_b=nh_b, final=final, nsub=nsub),
        out_shape=jax.ShapeDtypeStruct((b, nstep * nsub * tb, d), F32),
        grid=(b, nstep),
        in_specs=[pl.BlockSpec((1, nsub * tb, d), lambda i, j: (i, j, 0)), mod_spec, *mix_specs,
                  lay(na), lay(nb_), weight(wo), lay(g2), weight(w1), weight(w2),
                  pl.BlockSpec((1, d), lambda i, j: (0, 0))],
        out_specs=pl.BlockSpec((1, nsub * tb, d), lambda i, j: (i, j, 0)),
        compiler_params=pltpu.CompilerParams(dimension_semantics=("parallel", "parallel"),
                                             vmem_limit_bytes=VMEM_LIMIT),
        name="post_ctx" if ctx else "post",
    )(ctx_arr if ctx else lat_arr, mod, *mix_args, na, nb_, wo, g2, w1, w2, fg.reshape(1, d))


def kernel(x, c, ctx, c_ctx, w_ada, b_ada, norm1_g, norm2_g, w_in, gla_w_lr, gla_b_lr, gdn_conv_w,
           gdn_a_log, gdn_dt_bias, gla_norm_g, gdn_norm_g, w_out, w_ff1, w_ff2, final_norm_g):
    bsz, t, d = x.shape
    depth = w_ada.shape[0]
    assert ctx.shape[1] == TOKEN_BLOCK and t % TOKEN_BLOCK == 0 and TOKEN_BLOCK % GRID_W == 0
    assert bsz + 1 <= MOD_ROWS
    qk_a = gla_w_lr.shape[3]
    rank = gla_w_lr.shape[2]
    v_a = gla_norm_g.shape[1] * GLA_HEADS
    conv_dim = gdn_conv_w.shape[2]
    v_b = gdn_norm_g.shape[1] * GDN_HEADS
    qk_b = (conv_dim - v_b) // 2
    nd = 2 * GDN_HEADS
    dims = (qk_a, v_a, rank, conv_dim, qk_b, v_b, GDN_HEADS)

    o_r = 2 * qk_a + 2 * v_a
    o_c = o_r + rank
    o_g = o_c + conv_dim
    o_s = o_g + v_b
    assert w_in.shape[2] == o_s + 2 * nd
    w_main = jnp.concatenate([w_in[:, :, :o_r], w_in[:, :, o_c:o_s]], axis=2).astype(BF16)
    w_sm = jnp.concatenate([w_in[:, :, o_r:o_c], w_in[:, :, o_s:]], axis=2)
    w_small = jnp.pad(w_sm, ((0, 0), (0, 0), (0, LANES - w_sm.shape[2]))).astype(BF16)
    w_small_t = jnp.swapaxes(w_in[:, :, o_s:], 1, 2).astype(BF16)

    cc = jnp.concatenate([c, c_ctx[None, :], jnp.zeros((MOD_ROWS - bsz - 1, d), F32)], axis=0)
    mod = _modulation(cc, w_ada, b_ada).reshape(depth * MOD_ROWS, 1, 6 * d)

    in_params = (norm1_g.reshape(depth, 1, d), w_main, w_small, w_small_t, gla_w_lr.astype(BF16),
                 gla_b_lr.reshape(depth, 2, 1, qk_a), gdn_conv_w,
                 gdn_a_log.reshape(depth, 1, nd), gdn_dt_bias.reshape(depth, 1, nd),
                 gdn_a_log.reshape(depth, nd, 1), gdn_dt_bias.reshape(depth, nd, 1))
    post_params = (gla_norm_g.reshape(depth, 1, -1), gdn_norm_g.reshape(depth, 1, -1), w_out.astype(BF16),
                   norm2_g.reshape(depth, 1, d), w_ff1.astype(BF16), w_ff2.astype(BF16))

    stream = (ctx, x)
    for l in range(depth):
        last = l == depth - 1
        gqkv, gg, gla, dqkv, dg, small_c, small_r = _in_proj(stream, mod, l, bsz, in_params, dims)
        oaf, oab, obf, obb = _scan(gqkv, gla, dqkv, small_c, small_r, dims)
        args = (mod, l, last, oaf, oab, obf, obb, gg, dg, post_params, final_norm_g, GLA_HEADS, GDN_HEADS)
        xs = _post(stream, "latent", *args)
        if not last:
            stream = (_post(stream, "ctx", *args), xs)
    return xs
```

```python
import functools
import math

import jax
import jax.numpy as jnp
from jax import lax
from jax.experimental import pallas as pl
from jax.experimental.pallas import tpu as pltpu

EPS = 1e-6
GRID_W = 64
GLA_HEADS = 4
GDN_HEADS = 4
GLA_TAU = 16.0
TOKEN_BLOCK = 256
CHUNK = 64
LANES = 128
SUBLANES = 8
POST_SUB = 2
SCAN_ROWS = 2
IN_ROWS = 4
SCAN_ROW_LAG = 8
MOD_ROWS = 16
VMEM_LIMIT = 56 * 1024 * 1024
SCAN_ORDER = (("b", 0), ("b", 1), ("a", 0), ("a", 1), ("a", 2), ("a", 3), ("b", 2), ("b", 3), ("b", 4),
              ("b", 5), ("b", 6), ("a", 4), ("b", 7), ("b", 8), ("b", 9), ("b", 10), ("b", 11), ("b", 12))

F32 = jnp.float32
BF16 = jnp.bfloat16


def _dot(a, b):
    return jnp.dot(a, b, preferred_element_type=F32)


def _dot_nt(a, b):
    return lax.dot_general(a, b, (((1,), (1,)), ((), ())), preferred_element_type=F32)


def _dot_tn(a, b):
    return lax.dot_general(a, b, (((0,), (0,)), ((), ())), preferred_element_type=F32)


def _silu(x):
    return x / (1.0 + jnp.exp(-x))


def _sigmoid(x):
    return 1.0 / (1.0 + jnp.exp(-x))


def _softplus(x):
    return jnp.maximum(x, 0.0) + jnp.log(1.0 + jnp.exp(-jnp.abs(x)))


def _log_sigmoid(x):
    return jnp.minimum(x, 0.0) - jnp.log(1.0 + jnp.exp(-jnp.abs(x)))


def _rms(x):
    return x * lax.rsqrt(jnp.mean(x * x, axis=-1, keepdims=True) + EPS)


def _cumsum_dot(tri_bf16, x, left):
    hi = x.astype(BF16)
    lo = (x - hi.astype(F32)).astype(BF16)
    if left:
        return _dot(tri_bf16, hi) + _dot(tri_bf16, lo)
    return _dot(hi, tri_bf16) + _dot(lo, tri_bf16)


def _tri(n, upper, strict):
    r = lax.broadcasted_iota(jnp.int32, (n, n), 0)
    c = lax.broadcasted_iota(jnp.int32, (n, n), 1)
    if upper:
        m = (r < c) if strict else (r <= c)
    else:
        m = (r > c) if strict else (r >= c)
    return m


def _couple(n, s, upper):
    r = lax.broadcasted_iota(jnp.int32, (n, n), 0)
    c = lax.broadcasted_iota(jnp.int32, (n, n), 1)
    same = (r & ~(2 * s - 1)) == (c & ~(2 * s - 1))
    r_hi, c_hi = (r & s) != 0, (c & s) != 0
    return same & ((~r_hi & c_hi) if upper else (r_hi & ~c_hi))


def _tri_blocks(n, upper):
    r = lax.broadcasted_iota(jnp.int32, (n, n), 0)
    c = lax.broadcasted_iota(jnp.int32, (n, n), 1)
    same = (r & ~(CHUNK - 1)) == (c & ~(CHUNK - 1))
    return same & ((r <= c) if upper else (r >= c))


def _mod_kernel(cc_ref, w_ref, b_ref, o_ref):
    s = _silu(cc_ref[...]).astype(BF16)
    o_ref[0] = _dot(s, w_ref[0].astype(BF16)) + b_ref[0]


def _modulation(cc, w_ada, b_ada):
    depth, d, d6 = w_ada.shape
    nblk = d6 // d
    return pl.pallas_call(
        _mod_kernel,
        out_shape=jax.ShapeDtypeStruct((depth, MOD_ROWS, d6), F32),
        grid=(depth, nblk),
        in_specs=[
            pl.BlockSpec((MOD_ROWS, d), lambda l, n: (0, 0)),
            pl.BlockSpec((1, d, d), lambda l, n: (l, 0, n)),
            pl.BlockSpec((1, 1, d), lambda l, n: (l, 0, n)),
        ],
        out_specs=pl.BlockSpec((1, MOD_ROWS, d), lambda l, n: (l, 0, n)),
        compiler_params=pltpu.CompilerParams(dimension_semantics=("parallel", "parallel")),
        name="adaln_modulation",
    )(cc, w_ada, b_ada.reshape(depth, 1, d6))


def _in_proj_kernel(c_ref, x_ref, *refs, d, qk_a, v_a, rank, conv_dim, qk_b, v_b, nh_b):
    nr = x_ref.shape[0]
    mod_refs = refs[:nr]
    (g1_ref, wm_ref, ws_ref, wst_ref, wlr_ref, blr_ref, cw_ref, arow_ref, dtrow_ref, acol_ref, dtcol_ref,
     gqkv_ref, gg_ref, gla_ref, dqkv_ref, dg_ref, sc_ref, sr_ref) = refs[nr:]
    for bi in range(nr):
        _in_proj_row(bi, c_ref, x_ref, mod_refs[bi], g1_ref, wm_ref, ws_ref, wst_ref, wlr_ref, blr_ref, cw_ref,
                     arow_ref, dtrow_ref, acol_ref, dtcol_ref, gqkv_ref, gg_ref, gla_ref, dqkv_ref, dg_ref,
                     sc_ref, sr_ref, d=d, qk_a=qk_a, v_a=v_a, rank=rank, conv_dim=conv_dim, qk_b=qk_b, v_b=v_b,
                     nh_b=nh_b)


def _in_proj_row(bi, c_ref, x_ref, mod_ref, g1_ref, wm_ref, ws_ref, wst_ref, wlr_ref, blr_ref, cw_ref,
                 arow_ref, dtrow_ref, acol_ref, dtcol_ref,
                 gqkv_ref, gg_ref, gla_ref, dqkv_ref, dg_ref, sc_ref, sr_ref, *, d, qk_a, v_a, rank, conv_dim,
                 qk_b, v_b, nh_b):
    j = pl.program_id(1)
    x = jnp.where(j == 0, c_ref[bi], x_ref[bi])
    m = mod_ref[0]
    sh1, sc1 = m[:, 0:d], m[:, d:2 * d]
    h = _rms(x) * (g1_ref[...] * (1.0 + sc1)) + sh1
    hb = h.astype(BF16)
    o_conv = 2 * qk_a + 2 * v_a
    o_gate = o_conv + conv_dim
    grp = 2 * LANES
    tb = x.shape[0]
    t = lax.broadcasted_iota(jnp.int32, (tb, 1), 0)
    seg_mask = jnp.where(j == 0, tb - 1, GRID_W - 1)
    first = (t & seg_mask) == 0
    last = (t & seg_mask) == seg_mask
    dkh = qk_b // nh_b

    def conv_dot(k):
        return _dot(hb, wm_ref[:, o_conv + k * grp:o_conv + (k + 1) * grp])

    def zero_rows(a, mask, at):
        pieces = []
        for s0 in range(0, tb, GRID_W):
            r0 = s0 + at
            pieces += [a[s0:r0], jnp.where(mask[r0:r0 + SUBLANES], 0.0, a[r0:r0 + SUBLANES]),
                       a[r0 + SUBLANES:s0 + GRID_W]]
        return jnp.concatenate([p for p in pieces if p.shape[0]], axis=0)

    def conv_group(u, k):
        up = zero_rows(pltpu.roll(u, 1, 0), first, 0)
        un = zero_rows(pltpu.roll(u, tb - 1, 0), last, GRID_W - SUBLANES)
        cw = cw_ref[:, k * grp:(k + 1) * grp]
        s = _silu(cw[0:1] * up + cw[1:2] * u + cw[2:3] * un)
        c0 = k * grp
        if c0 < 2 * qk_b:
            for hh in range(grp // dkh):
                sh = s[:, hh * dkh:(hh + 1) * dkh]
                inv = lax.rsqrt(jnp.sum(sh * sh, axis=-1, keepdims=True) + EPS)
                if c0 < qk_b:
                    inv = inv * (float(dkh) ** -0.5)
                dqkv_ref[bi, :, c0 + hh * dkh:c0 + (hh + 1) * dkh] = (sh * inv).astype(BF16)
        else:
            dqkv_ref[bi, :, c0:c0 + grp] = s.astype(BF16)

    def light_group(k):
        c0 = k * grp
        w0 = c0 if c0 < o_conv else o_gate + (c0 - o_conv)
        pa = _dot(hb, wm_ref[:, w0:w0 + grp])
        if c0 < qk_a:
            gqkv_ref[bi, :, c0:c0 + grp] = (pa * (float(qk_a // GLA_HEADS) ** -0.5)).astype(BF16)
        elif c0 < 2 * qk_a + v_a:
            gqkv_ref[bi, :, c0:c0 + grp] = pa.astype(BF16)
        elif c0 < o_conv:
            gg_ref[bi, :, c0 - 2 * qk_a - v_a:c0 - 2 * qk_a - v_a + grp] = pa.astype(BF16)
        else:
            dg_ref[bi, :, c0 - o_conv:c0 - o_conv + grp] = pa.astype(BF16)

    def small_group(ps, pst):
        r_a = ps[:, 0:rank].astype(BF16)
        for dr in range(2):
            lr = _dot(r_a, wlr_ref[dr]) + blr_ref[dr]
            gla_ref[bi, :, dr * qk_a:(dr + 1) * qk_a] = _log_sigmoid(lr) * (1.0 / GLA_TAU)
        nd = 2 * nh_b
        a_c = ps[:, rank:rank + nd]
        b_c = ps[:, rank + nd:rank + 2 * nd]
        sc_ref[bi, :, 0:nd] = -jnp.exp(arow_ref[...]) * _softplus(a_c + dtrow_ref[...])
        sc_ref[bi, :, nd:2 * nd] = _sigmoid(b_c)
        sr_ref[bi, 0, 0:nd, :] = -jnp.exp(acol_ref[...]) * _softplus(pst[0:nd] + dtcol_ref[...])
        sr_ref[bi, 0, nd:2 * nd, :] = _sigmoid(pst[nd:2 * nd])

    n_conv = conv_dim // grp
    n_light = (o_conv + v_b) // grp
    us = {0: conv_dot(0)}
    ps = _dot(hb, ws_ref[...])
    pst = _dot_nt(wst_ref[...], hb)
    g_next = 0
    for k in range(n_conv):
        if k + 1 < n_conv:
            us[k + 1] = conv_dot(k + 1)
        conv_group(us.pop(k), k)
        if k == 1 or n_conv == 1:
            small_group(ps, pst)
        todo = (n_light - g_next + (n_conv - k) - 1) // (n_conv - k) if k >= n_conv // 2 else 1
        for _ in range(min(todo, n_light - g_next)):
            light_group(g_next)
            g_next += 1
    while g_next < n_light:
        light_group(g_next)
        g_next += 1


def _in_proj(stream, mod, layer, bsz, params, dims):
    g1, wm, ws, wst, wlr, blr, cw, a_row, dt_row, a_col, dt_col = params
    head, body = stream
    b, _, d = body.shape
    nb = body.shape[1] // TOKEN_BLOCK + 1
    s = nb * TOKEN_BLOCK
    qk_a, v_a, rank, conv_dim, qk_b, v_b, nh_b = dims
    nd = 2 * nh_b
    tb = TOKEN_BLOCK
    lay = lambda a: pl.BlockSpec((None,) + a.shape[1:], lambda i, j: (layer,) + (0,) * (a.ndim - 1))
    nr = math.gcd(b, IN_ROWS)
    tok = lambda n: pl.BlockSpec((nr, tb, n), lambda i, j: (i, j, 0))
    mod_specs = [pl.BlockSpec((1, 1, 6 * d), lambda i, j, k=k: (
        layer * MOD_ROWS + jnp.where(j == 0, bsz, i * nr + k), 0, 0)) for k in range(nr)]
    out_shape = [
        jax.ShapeDtypeStruct((b, s, 2 * qk_a + v_a), BF16), jax.ShapeDtypeStruct((b, s, v_a), BF16),
        jax.ShapeDtypeStruct((b, s, 2 * qk_a), F32),
        jax.ShapeDtypeStruct((b, s, conv_dim), BF16), jax.ShapeDtypeStruct((b, s, v_b), BF16),
        jax.ShapeDtypeStruct((b, s, 2 * nd), F32),
        jax.ShapeDtypeStruct((b, nb, 2 * nd, tb), F32),
    ]
    out_specs = [tok(2 * qk_a + v_a), tok(v_a), tok(2 * qk_a), tok(conv_dim), tok(v_b), tok(2 * nd),
                 pl.BlockSpec((nr, 1, 2 * nd, tb), lambda i, j: (i, j, 0, 0))]
    kern = functools.partial(_in_proj_kernel, d=d, qk_a=qk_a, v_a=v_a, rank=rank, conv_dim=conv_dim,
                             qk_b=qk_b, v_b=v_b, nh_b=nh_b)
    return pl.pallas_call(
        kern, out_shape=out_shape, grid=(b // nr, nb),
        in_specs=[
            pl.BlockSpec((nr, tb, d), lambda i, j: (i, 0, 0)),
            pl.BlockSpec((nr, tb, d), lambda i, j: (i, jnp.maximum(j - 1, 0), 0)), *mod_specs, lay(g1),
            lay(wm), lay(ws), lay(wst), lay(wlr), lay(blr), lay(cw), lay(a_row), lay(dt_row), lay(a_col), lay(dt_col),
        ],
        out_specs=out_specs,
        compiler_params=pltpu.CompilerParams(dimension_semantics=("parallel", "parallel"),
                                             vmem_limit_bytes=VMEM_LIMIT),
        name="in_proj",
    )(head, body, *([mod] * nr), g1, wm, ws, wst, wlr, blr, cw, a_row, dt_row, a_col, dt_col)


def _bwd_block(j, nb):
    return jnp.where(j == 0, 0, nb - j)


class _Cols:
    def __init__(self, ref, off):
        self.ref, self.off = ref, off

    def __getitem__(self, idx):
        z, rows, cols = idx
        return self.ref[z, rows, self.off + cols.start:self.off + cols.stop]


def _gla_stages(refs, st_ref, dk, dv, bi):
    tb = refs[0][3].shape[1]
    nchunk = tb // CHUNK
    npair = (GLA_HEADS * dk) // LANES
    hpp = LANES // dk
    lane = lax.broadcasted_iota(jnp.int32, (1, LANES), 1)
    lms = [(lane >= hh * dk) & (lane < (hh + 1) * dk) for hh in range(hpp)]
    mid = CHUNK // 2
    chains = [(dr, pr) for dr in range(2) for pr in range(npair)]
    units = [(dr, pr, c) for (dr, pr) in chains for c in range(nchunk)]
    bcum, q_dec, k_hat, q_mid, k_mid, a_last, sc, intra, dst = ({} for _ in range(9))

    def cumulative_decay():
        for dr in range(2):
            both = _cumsum_dot(_tri_blocks(tb, dr == 1).astype(BF16), refs[dr][3][bi], True)
            for pr in range(npair):
                bcum[dr, pr] = both[:, pr * LANES:(pr + 1) * LANES]

    def decayed_operands():
        for u in units:
            dr, pr, c = u
            rows = slice(c * CHUNK, (c + 1) * CHUNK)
            lanes = slice(pr * LANES, (pr + 1) * LANES)
            last = c * CHUNK + (0 if dr == 1 else CHUNK - 1)
            b = bcum[dr, pr][rows]
            b_last = bcum[dr, pr][last:last + 1]
            b_mid = bcum[dr, pr][c * CHUNK + mid:c * CHUNK + mid + 1]
            qc = refs[dr][0][bi, rows, lanes].astype(F32)
            kc = refs[dr][1][bi, rows, lanes].astype(F32)
            q_dec[u] = qc * jnp.exp(b)
            k_hat[u] = (kc * jnp.exp(b_last - b)).astype(BF16)
            q_mid[u] = qc * jnp.exp(b - b_mid)
            k_mid[u] = (kc * jnp.exp(b_mid - b)).astype(BF16)
            a_last[u] = jnp.exp(b_last)

    def per_head(m):
        return jnp.concatenate([jnp.where(lm, m, 0.0).astype(BF16) for lm in lms], axis=0)

    def scores():
        for u in units:
            sc[u] = _dot_nt(per_head(q_mid[u]), k_mid[u])

    def intra_and_increments():
        for u in units:
            dr, pr, c = u
            rows = slice(c * CHUNK, (c + 1) * CHUNK)
            causal = _tri(CHUNK, dr == 1, False)
            v_grp = refs[dr][2][bi, rows, pr * hpp * dv:(pr + 1) * hpp * dv]
            dh = _dot_tn(v_grp, k_hat[u])
            acc = None
            for hh in range(hpp):
                pm = jnp.where(causal, sc[u][hh * CHUNK:(hh + 1) * CHUNK], 0.0).astype(BF16)
                intra[u, hh] = _dot(pm, v_grp[:, hh * dv:(hh + 1) * dv])
                part = dh[hh * dv:(hh + 1) * dv]
                acc = part if acc is None else jnp.where(lms[hh], part, acc)
            dst[u] = acc

    def recurrence():
        st = {ch: st_ref[bi, ch[0], ch[1]] for ch in chains}
        for ci in range(nchunk):
            for ch in chains:
                dr, pr = ch
                c = nchunk - 1 - ci if dr == 1 else ci
                u = (dr, pr, c)
                rows = slice(c * CHUNK, (c + 1) * CHUNK)
                inter = _dot_nt(per_head(q_dec[u]), st[ch].astype(BF16))
                for hh in range(hpp):
                    head = pr * hpp + hh
                    refs[dr][4][bi, rows, head * dv:(head + 1) * dv] = (
                        intra[u, hh] + inter[hh * CHUNK:(hh + 1) * CHUNK]).astype(BF16)
                st[ch] = st[ch] * a_last[u] + dst[u]
        for ch in chains:
            st_ref[bi, ch[0], ch[1]] = st[ch]

    return [cumulative_decay, decayed_operands, scores, intra_and_increments, recurrence]


def _gdn_stages(refs, st_ref, dk, dv, nh, bi):
    tb = refs[0][3].shape[1]
    nchunk = tb // CHUNK
    eye = _tri(CHUNK, False, False) & _tri(CHUNK, True, False)
    chains = [(dr, hd) for dr in range(2) for hd in range(nh)]
    units = [(dr, c, hd) for dr in range(2) for c in range(nchunk) for hd in range(nh)]
    g_cols, g_rows, small, kk, qk, x, aqk, rhs, q_dec, k_dec, a_last = ({} for _ in range(11))
    t, uw, kd_uw, aq_uw, lhs, decay = ({} for _ in range(6))

    def cumulative_decay():
        for dr in range(2):
            small[dr] = refs[dr][3][bi]
            g_cols[dr] = _cumsum_dot(_tri_blocks(tb, dr == 1).astype(BF16), small[dr], True)
            g_rows[dr] = _cumsum_dot(_tri_blocks(tb, dr != 1).astype(BF16), refs[dr][4][bi, 0], False)

    def grams():
        for u in units:
            dr, c, hd = u
            rows = slice(c * CHUNK, (c + 1) * CHUNK)
            kh = refs[dr][1][bi, rows, hd * dk:(hd + 1) * dk]
            qh = refs[dr][0][bi, rows, hd * dk:(hd + 1) * dk]
            kq = _dot_nt(jnp.concatenate([kh, qh], axis=0), kh)
            kk[u], qk[u] = kq[:CHUNK], kq[CHUNK:]

    def solve_operands():
        for u in units:
            dr, c, hd = u
            rows = slice(c * CHUNK, (c + 1) * CHUNK)
            ia = dr * nh + hd
            ib = 2 * nh + dr * nh + hd
            g_c = g_cols[dr][rows, ia:ia + 1]
            g_r = g_rows[dr][ia:ia + 1, rows]
            be_c = small[dr][rows, ib:ib + 1]
            decay[u] = jnp.exp(jnp.minimum(g_c - g_r, 0.0))
            x[u] = jnp.where(_tri(CHUNK, dr == 1, True), be_c * kk[u] * decay[u], 0.0)

    def other_operands():
        for u in units:
            dr, c, hd = u
            rows = slice(c * CHUNK, (c + 1) * CHUNK)
            ia = dr * nh + hd
            ib = 2 * nh + dr * nh + hd
            last = c * CHUNK + (0 if dr == 1 else CHUNK - 1)
            g_c = g_cols[dr][rows, ia:ia + 1]
            be_c = small[dr][rows, ib:ib + 1]
            g_last = g_cols[dr][last:last + 1, ia:ia + 1]
            aqk[u] = jnp.where(_tri(CHUNK, dr == 1, False), qk[u] * decay[u], 0.0).astype(BF16)
            eg = jnp.exp(g_c)
            qh = refs[dr][0][bi, rows, hd * dk:(hd + 1) * dk].astype(F32)
            khf = refs[dr][1][bi, rows, hd * dk:(hd + 1) * dk].astype(F32)
            vh = refs[dr][2][bi, rows, hd * dv:(hd + 1) * dv].astype(F32)
            rhs[u] = jnp.concatenate([(be_c * vh).astype(BF16), ((be_c * eg) * khf).astype(BF16)], axis=1)
            q_dec[u] = qh * eg
            k_dec[u] = khf * jnp.exp(g_last - g_c)
            a_last[u] = jnp.exp(g_last)

    def inverse_start():
        for u in units:
            t[u] = jnp.where(eye, 1.0, jnp.where(_couple(CHUNK, 1, u[0] == 1), -x[u], 0.0))

    def inverse_level(s):
        a = {u: jnp.where(_couple(CHUNK, s, u[0] == 1), x[u], 0.0).astype(BF16) for u in units}
        tb16 = {u: t[u].astype(BF16) for u in units}
        ta = {u: _dot(tb16[u], a[u]).astype(BF16) for u in units}
        for u in units:
            t[u] = t[u] - _dot(ta[u], tb16[u])

    levels = []
    s = 2
    while s < CHUNK:
        levels.append(functools.partial(inverse_level, s))
        s *= 2

    def solve():
        for u in units:
            uw[u] = _dot(t[u].astype(BF16), rhs[u]).astype(BF16)

    def fold():
        for u in units:
            both = _dot(jnp.concatenate([k_dec[u].T.astype(BF16), aqk[u]], axis=0), uw[u])
            kd_uw[u], aq_uw[u] = both[:dk], both[dk:]
        for u in units:
            lhs[u] = jnp.concatenate([(-kd_uw[u][:, dv:]).astype(BF16),
                                      (q_dec[u] - aq_uw[u][:, dv:]).astype(BF16)], axis=0)

    def recurrence():
        st = {ch: st_ref[bi, ch[0], ch[1]] for ch in chains}
        for ci in range(nchunk):
            for ch in chains:
                dr, hd = ch
                c = nchunk - 1 - ci if dr == 1 else ci
                u = (dr, c, hd)
                rows = slice(c * CHUNK, (c + 1) * CHUNK)
                res = _dot(lhs[u], st[ch].astype(BF16))
                refs[dr][5][bi, rows, hd * dv:(hd + 1) * dv] = (res[dk:] + aq_uw[u][:, :dv]).astype(BF16)
                st[ch] = st[ch] * a_last[u] + res[:dk] + kd_uw[u][:, :dv]
        for ch in chains:
            st_ref[bi, ch[0], ch[1]] = st[ch]

    return [cumulative_decay, grams, solve_operands, other_operands, inverse_start, *levels,
            solve, fold, recurrence]


def _scan_kernel(af, alf, ab, alb, bf, bcf, brf, bb, bcb, brb,
                 oaf, oab, obf, obb, sta_ref, stb_ref, *, dk_a, dv_a, dk_b, dv_b, nh_b):
    @pl.when(pl.program_id(1) == 0)
    def _():
        sta_ref[...] = jnp.zeros_like(sta_ref)
        stb_ref[...] = jnp.zeros_like(stb_ref)

    qk_a, qk_b = GLA_HEADS * dk_a, nh_b * dk_b
    qkv = lambda ref, qk: (_Cols(ref, 0), _Cols(ref, qk), _Cols(ref, 2 * qk))
    refs_a = ((*qkv(af, qk_a), alf, oaf), (*qkv(ab, qk_a), alb, oab))
    refs_b = ((*qkv(bf, qk_b), bcf, brf, obf), (*qkv(bb, qk_b), bcb, brb, obb))
    stages = {}
    for bi in range(af.shape[0]):
        stages["a", bi] = _gla_stages(refs_a, sta_ref, dk_a, dv_a, bi)
        stages["b", bi] = _gdn_stages(refs_b, stb_ref, dk_b, dv_b, nh_b, bi)
    lag = SCAN_ROW_LAG
    for pos in range(len(SCAN_ORDER) + lag * (af.shape[0] - 1)):
        for bi in range(af.shape[0]):
            if 0 <= pos - lag * bi < len(SCAN_ORDER):
                name, idx = SCAN_ORDER[pos - lag * bi]
                stages[name, bi][idx]()


def _scan(gqkv, gla, dqkv, small_c, small_r, dims):
    b, s, _ = gqkv.shape
    qk_a, v_a, _, _, qk_b, v_b, _ = dims
    nb = s // TOKEN_BLOCK
    tb = TOKEN_BLOCK
    ns = small_c.shape[2]
    nr = math.gcd(b, SCAN_ROWS)
    fwd = lambda n, col=0: pl.BlockSpec((nr, tb, n), lambda i, j: (i, j, col))
    bwd = lambda n, col=0: pl.BlockSpec((nr, tb, n), lambda i, j: (i, _bwd_block(j, nb), col))
    rfwd = pl.BlockSpec((nr, 1, ns, tb), lambda i, j: (i, j, 0, 0))
    rbwd = pl.BlockSpec((nr, 1, ns, tb), lambda i, j: (i, _bwd_block(j, nb), 0, 0))
    dk_a, dv_a = qk_a // GLA_HEADS, v_a // GLA_HEADS
    dk_b, dv_b = qk_b // GDN_HEADS, v_b // GDN_HEADS
    out = lambda v: jax.ShapeDtypeStruct((b, s, v), BF16)
    return pl.pallas_call(
        functools.partial(_scan_kernel, dk_a=dk_a, dv_a=dv_a, dk_b=dk_b, dv_b=dv_b, nh_b=GDN_HEADS),
        out_shape=[out(v_a), out(v_a), out(v_b), out(v_b)],
        grid=(b // nr, nb),
        in_specs=[fwd(2 * qk_a + v_a), fwd(qk_a, 0), bwd(2 * qk_a + v_a), bwd(qk_a, 1),
                  fwd(2 * qk_b + v_b), fwd(ns), rfwd, bwd(2 * qk_b + v_b), bwd(ns), rbwd],
        out_specs=[fwd(v_a), bwd(v_a), fwd(v_b), bwd(v_b)],
        scratch_shapes=[pltpu.VMEM((nr, 2, qk_a // LANES, dv_a, LANES), F32),
                        pltpu.VMEM((nr, 2, GDN_HEADS, dk_b, dv_b), F32)],
        compiler_params=pltpu.CompilerParams(dimension_semantics=("parallel", "arbitrary"),
                                             vmem_limit_bytes=VMEM_LIMIT),
        name="scan",
    )(gqkv, gla, gqkv, gla, dqkv, small_c, small_r, dqkv, small_c, small_r)


def _post_kernel(x_ref, mod_ref, *refs, d, nh_a, nh_b, final, nsub):
    mix = [refs[6 * i:6 * i + 6] for i in range(nsub)]
    na_ref, nb_ref, wo_ref, g2_ref, w1_ref, w2_ref, fg_ref, o_ref = refs[6 * nsub:]
    tb = x_ref.shape[1] // nsub
    m = mod_ref[0]
    gt1 = m[:, 2 * d:3 * d]
    sh2, sc2, gt2 = m[:, 3 * d:4 * d], m[:, 4 * d:5 * d], m[:, 5 * d:6 * d]
    slab = 2 * LANES
    x1, hb, act = {}, {}, {}

    def merge_and_project(i):
        af_ref, ab_ref, bf_ref, bb_ref, ga_ref, gb_ref = mix[i]
        acc = None
        row = 0
        for (f_ref, b_ref, g_ref, n_ref, nh) in ((af_ref, ab_ref, ga_ref, na_ref, nh_a),
                                                 (bf_ref, bb_ref, gb_ref, nb_ref, nh_b)):
            width = f_ref.shape[2]
            hv = width // nh
            for c0 in range(0, width, slab):
                o = f_ref[0, :, c0:c0 + slab].astype(F32) + b_ref[0, :, c0:c0 + slab].astype(F32)
                gate = _silu(g_ref[0, :, c0:c0 + slab].astype(F32))
                parts = [(_rms(o[:, k:k + hv]) * n_ref[...] * gate[:, k:k + hv]).astype(BF16)
                         for k in range(0, slab, hv)]
                part = _dot(jnp.concatenate(parts, axis=1), wo_ref[row:row + slab, :])
                acc = part if acc is None else acc + part
                row += slab
        x1[i] = x_ref[0, i * tb:(i + 1) * tb, :] + gt1 * acc
        hb[i] = (_rms(x1[i]) * (g2_ref[...] * (1.0 + sc2)) + sh2).astype(BF16)

    def mlp_up(i):
        a = jnp.maximum(_dot(hb[i], w1_ref[...]), 0.0)
        act[i] = (a * a).astype(BF16)

    def mlp_down(i):
        x2 = x1[i] + gt2 * _dot(act[i], w2_ref[...])
        if final:
            x2 = _rms(x2) * fg_ref[...]
        o_ref[0, i * tb:(i + 1) * tb, :] = x2

    merge_and_project(0)
    for i in range(nsub):
        mlp_up(i)
        if i + 1 < nsub:
            merge_and_project(i + 1)
        mlp_down(i)


def _post(stream, part, mod, layer, final, oaf, oab, obf, obb, gg, dg, params, fg, nh_a, nh_b):
    na, nb_, wo, g2, w1, w2 = params
    ctx_arr, lat_arr = stream
    b, t, d = lat_arr.shape
    tb = TOKEN_BLOCK
    ctx = part == "ctx"
    nsub = 1 if ctx else math.gcd(t // tb, POST_SUB)
    nstep = 1 if ctx else t // (nsub * tb)
    first = 0 if ctx else 1
    lay = lambda a: pl.BlockSpec((None,) + a.shape[1:], lambda i, j: (layer,) + (0,) * (a.ndim - 1))
    weight = lambda a: pl.BlockSpec((None,) + a.shape[1:], lambda i, j: (layer,) + (0,) * (a.ndim - 1),
                                    pipeline_mode=pl.Buffered(1))
    bsz = b
    mod_spec = pl.BlockSpec((1, 1, 6 * d), lambda i, j: (layer * MOD_ROWS + (bsz if ctx else i), 0, 0))
    va, vb = oaf.shape[2], obf.shape[2]
    mix_specs, mix_args = [], []
    for k in range(nsub):
        blk = lambda n, k=k: pl.BlockSpec((1, tb, n), lambda i, j: (i, first + j * nsub + k, 0))
        mix_specs += [blk(va), blk(va), blk(vb), blk(vb), blk(va), blk(vb)]
        mix_args += [oaf, oab, obf, obb, gg, dg]
    return pl.pallas_call(
        functools.partial(_post_kernel, d=d, nh_a=nh_a, nh_b=nh_b, final=final, nsub=nsub),
        out_shape=jax.ShapeDtypeStruct((b, nstep * nsub * tb, d), F32),
        grid=(b, nstep),
        in_specs=[pl.BlockSpec((1, nsub * tb, d), lambda i, j: (i, j, 0)), mod_spec, *mix_specs,
                  lay(na), lay(nb_), weight(wo), lay(g2), weight(w1), weight(w2),
                  pl.BlockSpec((1, d), lambda i, j: (0, 0))],
        out_specs=pl.BlockSpec((1, nsub * tb, d), lambda i, j: (i, j, 0)),
        compiler_params=pltpu.CompilerParams(dimension_semantics=("parallel", "parallel"),
                                             vmem_limit_bytes=VMEM_LIMIT),
        name="post_ctx" if ctx else "post",
    )(ctx_arr if ctx else lat_arr, mod, *mix_args, na, nb_, wo, g2, w1, w2, fg.reshape(1, d))


def kernel(x, c, ctx, c_ctx, w_ada, b_ada, norm1_g, norm2_g, w_in, gla_w_lr, gla_b_lr, gdn_conv_w,
           gdn_a_log, gdn_dt_bias, gla_norm_g, gdn_norm_g, w_out, w_ff1, w_ff2, final_norm_g):
    bsz, t, d = x.shape
    depth = w_ada.shape[0]
    assert ctx.shape[1] == TOKEN_BLOCK and t % TOKEN_BLOCK == 0 and TOKEN_BLOCK % GRID_W == 0
    assert bsz + 1 <= MOD_ROWS
    qk_a = gla_w_lr.shape[3]
    rank = gla_w_lr.shape[2]
    v_a = gla_norm_g.shape[1] * GLA_HEADS
    conv_dim = gdn_conv_w.shape[2]
    v_b = gdn_norm_g.shape[1] * GDN_HEADS
    qk_b = (conv_dim - v_b) // 2
    nd = 2 * GDN_HEADS
    dims = (qk_a, v_a, rank, conv_dim, qk_b, v_b, GDN_HEADS)

    o_r = 2 * qk_a + 2 * v_a
    o_c = o_r + rank
    o_g = o_c + conv_dim
    o_s = o_g + v_b
    assert w_in.shape[2] == o_s + 2 * nd
    w_main = jnp.concatenate([w_in[:, :, :o_r], w_in[:, :, o_c:o_s]], axis=2).astype(BF16)
    w_sm = jnp.concatenate([w_in[:, :, o_r:o_c], w_in[:, :, o_s:]], axis=2)
    w_small = jnp.pad(w_sm, ((0, 0), (0, 0), (0, LANES - w_sm.shape[2]))).astype(BF16)
    w_small_t = jnp.swapaxes(w_in[:, :, o_s:], 1, 2).astype(BF16)

    cc = jnp.concatenate([c, c_ctx[None, :], jnp.zeros((MOD_ROWS - bsz - 1, d), F32)], axis=0)
    mod = _modulation(cc, w_ada, b_ada).reshape(depth * MOD_ROWS, 1, 6 * d)

    in_params = (norm1_g.reshape(depth, 1, d), w_main, w_small, w_small_t, gla_w_lr.astype(BF16),
                 gla_b_lr.reshape(depth, 2, 1, qk_a), gdn_conv_w,
                 gdn_a_log.reshape(depth, 1, nd), gdn_dt_bias.reshape(depth, 1, nd),
                 gdn_a_log.reshape(depth, nd, 1), gdn_dt_bias.reshape(depth, nd, 1))
    post_params = (gla_norm_g.reshape(depth, 1, -1), gdn_norm_g.reshape(depth, 1, -1), w_out.astype(BF16),
                   norm2_g.reshape(depth, 1, d), w_ff1.astype(BF16), w_ff2.astype(BF16))

    stream = (ctx, x)
    for l in range(depth):
        last = l == depth - 1
        gqkv, gg, gla, dqkv, dg, small_c, small_r = _in_proj(stream, mod, l, bsz, in_params, dims)
        oaf, oab, obf, obb = _scan(gqkv, gla, dqkv, small_c, small_r, dims)
        args = (mod, l, last, oaf, oab, obf, obb, gg, dg, post_params, final_norm_g, GLA_HEADS, GDN_HEADS)
        xs = _post(stream, "latent", *args)
        if not last:
            stream = (_post(stream, "ctx", *args), xs)
    return xs
```

```python
import functools
import math

import jax
import jax.numpy as jnp
from jax import lax
from jax.experimental import pallas as pl
from jax.experimental.pallas import tpu as pltpu

EPS = 1e-6
GRID_W = 64
GLA_HEADS = 4
GDN_HEADS = 4
GLA_TAU = 16.0
TOKEN_BLOCK = 256
CHUNK = 64
LANES = 128
SUBLANES = 8
POST_SUB = 2
SCAN_ROWS = 2
IN_ROWS = 4
SCAN_ROW_LAG = 6
MOD_ROWS = 16
VMEM_LIMIT = 56 * 1024 * 1024
SCAN_ORDER = (("b", 0), ("a", 0), ("a", 1), ("b", 1), ("a", 2), ("a", 3), ("b", 2), ("b", 3), ("b", 4),
              ("b", 5), ("b", 6), ("a", 4), ("b", 7), ("b", 8), ("b", 9), ("b", 10), ("b", 11), ("b", 12))

F32 = jnp.float32
BF16 = jnp.bfloat16


def _dot(a, b):
    return jnp.dot(a, b, preferred_element_type=F32)


def _dot_nt(a, b):
    return lax.dot_general(a, b, (((1,), (1,)), ((), ())), preferred_element_type=F32)


def _dot_tn(a, b):
    return lax.dot_general(a, b, (((0,), (0,)), ((), ())), preferred_element_type=F32)


def _silu(x):
    return x / (1.0 + jnp.exp(-x))


def _sigmoid(x):
    return 1.0 / (1.0 + jnp.exp(-x))


def _softplus(x):
    return jnp.maximum(x, 0.0) + jnp.log(1.0 + jnp.exp(-jnp.abs(x)))


def _log_sigmoid(x):
    return jnp.minimum(x, 0.0) - jnp.log(1.0 + jnp.exp(-jnp.abs(x)))


def _rms(x):
    return x * lax.rsqrt(jnp.mean(x * x, axis=-1, keepdims=True) + EPS)


def _cumsum_dot(tri_bf16, x, left):
    hi = x.astype(BF16)
    lo = (x - hi.astype(F32)).astype(BF16)
    if left:
        return _dot(tri_bf16, hi) + _dot(tri_bf16, lo)
    return _dot(hi, tri_bf16) + _dot(lo, tri_bf16)


def _tri(n, upper, strict):
    r = lax.broadcasted_iota(jnp.int32, (n, n), 0)
    c = lax.broadcasted_iota(jnp.int32, (n, n), 1)
    if upper:
        m = (r < c) if strict else (r <= c)
    else:
        m = (r > c) if strict else (r >= c)
    return m


def _couple(n, s, upper):
    r = lax.broadcasted_iota(jnp.int32, (n, n), 0)
    c = lax.broadcasted_iota(jnp.int32, (n, n), 1)
    same = (r & ~(2 * s - 1)) == (c & ~(2 * s - 1))
    r_hi, c_hi = (r & s) != 0, (c & s) != 0
    return same & ((~r_hi & c_hi) if upper else (r_hi & ~c_hi))


def _tri_blocks(n, upper):
    r = lax.broadcasted_iota(jnp.int32, (n, n), 0)
    c = lax.broadcasted_iota(jnp.int32, (n, n), 1)
    same = (r & ~(CHUNK - 1)) == (c & ~(CHUNK - 1))
    return same & ((r <= c) if upper else (r >= c))


def _mod_kernel(cc_ref, w_ref, b_ref, o_ref):
    s = _silu(cc_ref[...]).astype(BF16)
    o_ref[0] = _dot(s, w_ref[0].astype(BF16)) + b_ref[0]


def _modulation(cc, w_ada, b_ada):
    depth, d, d6 = w_ada.shape
    nblk = d6 // d
    return pl.pallas_call(
        _mod_kernel,
        out_shape=jax.ShapeDtypeStruct((depth, MOD_ROWS, d6), F32),
        grid=(depth, nblk),
        in_specs=[
            pl.BlockSpec((MOD_ROWS, d), lambda l, n: (0, 0)),
            pl.BlockSpec((1, d, d), lambda l, n: (l, 0, n)),
            pl.BlockSpec((1, 1, d), lambda l, n: (l, 0, n)),
        ],
        out_specs=pl.BlockSpec((1, MOD_ROWS, d), lambda l, n: (l, 0, n)),
        compiler_params=pltpu.CompilerParams(dimension_semantics=("parallel", "parallel")),
        name="adaln_modulation",
    )(cc, w_ada, b_ada.reshape(depth, 1, d6))


def _in_proj_kernel(c_ref, x_ref, *refs, d, qk_a, v_a, rank, conv_dim, qk_b, v_b, nh_b):
    nr = x_ref.shape[0]
    mod_refs = refs[:nr]
    (g1_ref, wm_ref, ws_ref, wst_ref, wlr_ref, blr_ref, cw_ref, arow_ref, dtrow_ref, acol_ref, dtcol_ref,
     gqkv_ref, gg_ref, gla_ref, dqkv_ref, dg_ref, sc_ref, sr_ref) = refs[nr:]
    for bi in range(nr):
        _in_proj_row(bi, c_ref, x_ref, mod_refs[bi], g1_ref, wm_ref, ws_ref, wst_ref, wlr_ref, blr_ref, cw_ref,
                     arow_ref, dtrow_ref, acol_ref, dtcol_ref, gqkv_ref, gg_ref, gla_ref, dqkv_ref, dg_ref,
                     sc_ref, sr_ref, d=d, qk_a=qk_a, v_a=v_a, rank=rank, conv_dim=conv_dim, qk_b=qk_b, v_b=v_b,
                     nh_b=nh_b)


def _in_proj_row(bi, c_ref, x_ref, mod_ref, g1_ref, wm_ref, ws_ref, wst_ref, wlr_ref, blr_ref, cw_ref,
                 arow_ref, dtrow_ref, acol_ref, dtcol_ref,
                 gqkv_ref, gg_ref, gla_ref, dqkv_ref, dg_ref, sc_ref, sr_ref, *, d, qk_a, v_a, rank, conv_dim,
                 qk_b, v_b, nh_b):
    j = pl.program_id(1)
    x = jnp.where(j == 0, c_ref[bi], x_ref[bi])
    m = mod_ref[0]
    sh1, sc1 = m[:, 0:d], m[:, d:2 * d]
    h = _rms(x) * (g1_ref[...] * (1.0 + sc1)) + sh1
    hb = h.astype(BF16)
    o_conv = 2 * qk_a + 2 * v_a
    o_gate = o_conv + conv_dim
    grp = 2 * LANES
    tb = x.shape[0]
    t = lax.broadcasted_iota(jnp.int32, (tb, 1), 0)
    seg_mask = jnp.where(j == 0, tb - 1, GRID_W - 1)
    first = (t & seg_mask) == 0
    last = (t & seg_mask) == seg_mask
    dkh = qk_b // nh_b

    def conv_dot(k):
        return _dot(hb, wm_ref[:, o_conv + k * grp:o_conv + (k + 1) * grp])

    def zero_rows(a, mask, at):
        pieces = []
        for s0 in range(0, tb, GRID_W):
            r0 = s0 + at
            pieces += [a[s0:r0], jnp.where(mask[r0:r0 + SUBLANES], 0.0, a[r0:r0 + SUBLANES]),
                       a[r0 + SUBLANES:s0 + GRID_W]]
        return jnp.concatenate([p for p in pieces if p.shape[0]], axis=0)

    def conv_group(u, k):
        up = zero_rows(pltpu.roll(u, 1, 0), first, 0)
        un = zero_rows(pltpu.roll(u, tb - 1, 0), last, GRID_W - SUBLANES)
        cw = cw_ref[:, k * grp:(k + 1) * grp]
        s = _silu(cw[0:1] * up + cw[1:2] * u + cw[2:3] * un)
        c0 = k * grp
        if c0 < 2 * qk_b:
            for hh in range(grp // dkh):
                sh = s[:, hh * dkh:(hh + 1) * dkh]
                inv = lax.rsqrt(jnp.sum(sh * sh, axis=-1, keepdims=True) + EPS)
                if c0 < qk_b:
                    inv = inv * (float(dkh) ** -0.5)
                dqkv_ref[bi, :, c0 + hh * dkh:c0 + (hh + 1) * dkh] = (sh * inv).astype(BF16)
        else:
            dqkv_ref[bi, :, c0:c0 + grp] = s.astype(BF16)

    def light_group(k):
        c0 = k * grp
        w0 = c0 if c0 < o_conv else o_gate + (c0 - o_conv)
        pa = _dot(hb, wm_ref[:, w0:w0 + grp])
        if c0 < qk_a:
            gqkv_ref[bi, :, c0:c0 + grp] = (pa * (float(qk_a // GLA_HEADS) ** -0.5)).astype(BF16)
        elif c0 < 2 * qk_a + v_a:
            gqkv_ref[bi, :, c0:c0 + grp] = pa.astype(BF16)
        elif c0 < o_conv:
            gg_ref[bi, :, c0 - 2 * qk_a - v_a:c0 - 2 * qk_a - v_a + grp] = pa.astype(BF16)
        else:
            dg_ref[bi, :, c0 - o_conv:c0 - o_conv + grp] = pa.astype(BF16)

    def small_group(ps, pst):
        r_a = ps[:, 0:rank].astype(BF16)
        for dr in range(2):
            lr = _dot(r_a, wlr_ref[dr]) + blr_ref[dr]
            gla_ref[bi, :, dr * qk_a:(dr + 1) * qk_a] = _log_sigmoid(lr) * (1.0 / GLA_TAU)
        nd = 2 * nh_b
        a_c = ps[:, rank:rank + nd]
        b_c = ps[:, rank + nd:rank + 2 * nd]
        sc_ref[bi, :, 0:nd] = -jnp.exp(arow_ref[...]) * _softplus(a_c + dtrow_ref[...])
        sc_ref[bi, :, nd:2 * nd] = _sigmoid(b_c)
        sr_ref[bi, 0, 0:nd, :] = -jnp.exp(acol_ref[...]) * _softplus(pst[0:nd] + dtcol_ref[...])
        sr_ref[bi, 0, nd:2 * nd, :] = _sigmoid(pst[nd:2 * nd])

    n_conv = conv_dim // grp
    n_light = (o_conv + v_b) // grp
    us = {0: conv_dot(0)}
    ps = _dot(hb, ws_ref[...])
    pst = _dot_nt(wst_ref[...], hb)
    for k in range(1, min(3, n_conv)):
        us[k] = conv_dot(k)
    g_next = 0
    for k in range(n_conv):
        if k + 3 < n_conv:
            us[k + 3] = conv_dot(k + 3)
        conv_group(us.pop(k), k)
        if k == 1 or n_conv == 1:
            small_group(ps, pst)
        todo = (n_light - g_next + (n_conv - k) - 1) // (n_conv - k) if k >= n_conv // 2 else 1
        for _ in range(min(todo, n_light - g_next)):
            light_group(g_next)
            g_next += 1
    while g_next < n_light:
        light_group(g_next)
        g_next += 1


def _in_proj(stream, mod, layer, bsz, params, dims):
    g1, wm, ws, wst, wlr, blr, cw, a_row, dt_row, a_col, dt_col = params
    head, body = stream
    b, _, d = body.shape
    nb = body.shape[1] // TOKEN_BLOCK + 1
    s = nb * TOKEN_BLOCK
    qk_a, v_a, rank, conv_dim, qk_b, v_b, nh_b = dims
    nd = 2 * nh_b
    tb = TOKEN_BLOCK
    lay = lambda a: pl.BlockSpec((None,) + a.shape[1:], lambda i, j: (layer,) + (0,) * (a.ndim - 1))
    nr = math.gcd(b, IN_ROWS)
    tok = lambda n: pl.BlockSpec((nr, tb, n), lambda i, j: (i, j, 0))
    mod_specs = [pl.BlockSpec((1, 1, 6 * d), lambda i, j, k=k: (
        layer * MOD_ROWS + jnp.where(j == 0, bsz, i * nr + k), 0, 0)) for k in range(nr)]
    out_shape = [
        jax.ShapeDtypeStruct((b, s, 2 * qk_a + v_a), BF16), jax.ShapeDtypeStruct((b, s, v_a), BF16),
        jax.ShapeDtypeStruct((b, s, 2 * qk_a), F32),
        jax.ShapeDtypeStruct((b, s, conv_dim), BF16), jax.ShapeDtypeStruct((b, s, v_b), BF16),
        jax.ShapeDtypeStruct((b, s, 2 * nd), F32),
        jax.ShapeDtypeStruct((b, nb, 2 * nd, tb), F32),
    ]
    out_specs = [tok(2 * qk_a + v_a), tok(v_a), tok(2 * qk_a), tok(conv_dim), tok(v_b), tok(2 * nd),
                 pl.BlockSpec((nr, 1, 2 * nd, tb), lambda i, j: (i, j, 0, 0))]
    kern = functools.partial(_in_proj_kernel, d=d, qk_a=qk_a, v_a=v_a, rank=rank, conv_dim=conv_dim,
                             qk_b=qk_b, v_b=v_b, nh_b=nh_b)
    return pl.pallas_call(
        kern, out_shape=out_shape, grid=(b // nr, nb),
        in_specs=[
            pl.BlockSpec((nr, tb, d), lambda i, j: (i, 0, 0)),
            pl.BlockSpec((nr, tb, d), lambda i, j: (i, jnp.maximum(j - 1, 0), 0)), *mod_specs, lay(g1),
            lay(wm), lay(ws), lay(wst), lay(wlr), lay(blr), lay(cw), lay(a_row), lay(dt_row), lay(a_col), lay(dt_col),
        ],
        out_specs=out_specs,
        compiler_params=pltpu.CompilerParams(dimension_semantics=("parallel", "parallel"),
                                             vmem_limit_bytes=VMEM_LIMIT),
        name="in_proj",
    )(head, body, *([mod] * nr), g1, wm, ws, wst, wlr, blr, cw, a_row, dt_row, a_col, dt_col)


def _bwd_block(j, nb):
    return jnp.where(j == 0, 0, nb - j)


class _Cols:
    def __init__(self, ref, off):
        self.ref, self.off = ref, off

    def __getitem__(self, idx):
        z, rows, cols = idx
        return self.ref[z, rows, self.off + cols.start:self.off + cols.stop]


def _gla_stages(refs, st_ref, dk, dv, bi):
    tb = refs[0][3].shape[1]
    nchunk = tb // CHUNK
    npair = (GLA_HEADS * dk) // LANES
    hpp = LANES // dk
    lane = lax.broadcasted_iota(jnp.int32, (1, LANES), 1)
    lms = [(lane >= hh * dk) & (lane < (hh + 1) * dk) for hh in range(hpp)]
    mid = CHUNK // 2
    chains = [(dr, pr) for dr in range(2) for pr in range(npair)]
    units = [(dr, pr, c) for (dr, pr) in chains for c in range(nchunk)]
    bcum, q_dec, k_hat, q_mid, k_mid, a_last, sc, intra, dst = ({} for _ in range(9))

    def cumulative_decay():
        for dr in range(2):
            both = _cumsum_dot(_tri_blocks(tb, dr == 1).astype(BF16), refs[dr][3][bi], True)
            for pr in range(npair):
                bcum[dr, pr] = both[:, pr * LANES:(pr + 1) * LANES]

    def decayed_operands():
        for u in units:
            dr, pr, c = u
            rows = slice(c * CHUNK, (c + 1) * CHUNK)
            lanes = slice(pr * LANES, (pr + 1) * LANES)
            last = c * CHUNK + (0 if dr == 1 else CHUNK - 1)
            b = bcum[dr, pr][rows]
            b_last = bcum[dr, pr][last:last + 1]
            b_mid = bcum[dr, pr][c * CHUNK + mid:c * CHUNK + mid + 1]
            qc = refs[dr][0][bi, rows, lanes].astype(F32)
            kc = refs[dr][1][bi, rows, lanes].astype(F32)
            q_dec[u] = qc * jnp.exp(b)
            k_hat[u] = (kc * jnp.exp(b_last - b)).astype(BF16)
            q_mid[u] = qc * jnp.exp(b - b_mid)
            k_mid[u] = (kc * jnp.exp(b_mid - b)).astype(BF16)
            a_last[u] = jnp.exp(b_last)

    def per_head(m):
        return jnp.concatenate([jnp.where(lm, m, 0.0).astype(BF16) for lm in lms], axis=0)

    def scores():
        for u in units:
            sc[u] = _dot_nt(per_head(q_mid[u]), k_mid[u])

    def intra_and_increments():
        for u in units:
            dr, pr, c = u
            rows = slice(c * CHUNK, (c + 1) * CHUNK)
            causal = _tri(CHUNK, dr == 1, False)
            v_grp = refs[dr][2][bi, rows, pr * hpp * dv:(pr + 1) * hpp * dv]
            dh = _dot_tn(v_grp, k_hat[u])
            acc = None
            for hh in range(hpp):
                pm = jnp.where(causal, sc[u][hh * CHUNK:(hh + 1) * CHUNK], 0.0).astype(BF16)
                intra[u, hh] = _dot(pm, v_grp[:, hh * dv:(hh + 1) * dv])
                part = dh[hh * dv:(hh + 1) * dv]
                acc = part if acc is None else jnp.where(lms[hh], part, acc)
            dst[u] = acc

    def recurrence():
        st = {ch: st_ref[bi, ch[0], ch[1]] for ch in chains}
        for ci in range(nchunk):
            for ch in chains:
                dr, pr = ch
                c = nchunk - 1 - ci if dr == 1 else ci
                u = (dr, pr, c)
                rows = slice(c * CHUNK, (c + 1) * CHUNK)
                inter = _dot_nt(per_head(q_dec[u]), st[ch].astype(BF16))
                for hh in range(hpp):
                    head = pr * hpp + hh
                    refs[dr][4][bi, rows, head * dv:(head + 1) * dv] = (
                        intra[u, hh] + inter[hh * CHUNK:(hh + 1) * CHUNK]).astype(BF16)
                st[ch] = st[ch] * a_last[u] + dst[u]
        for ch in chains:
            st_ref[bi, ch[0], ch[1]] = st[ch]

    return [cumulative_decay, decayed_operands, scores, intra_and_increments, recurrence]


def _gdn_stages(refs, st_ref, dk, dv, nh, bi):
    tb = refs[0][3].shape[1]
    nchunk = tb // CHUNK
    eye = _tri(CHUNK, False, False) & _tri(CHUNK, True, False)
    chains = [(dr, hd) for dr in range(2) for hd in range(nh)]
    units = [(dr, c, hd) for dr in range(2) for c in range(nchunk) for hd in range(nh)]
    g_cols, g_rows, small, kk, qk, x, aqk, rhs, q_dec, k_dec, a_last = ({} for _ in range(11))
    t, uw, kd_uw, aq_uw, lhs, decay = ({} for _ in range(6))

    def cumulative_decay():
        for dr in range(2):
            small[dr] = refs[dr][3][bi]
            g_cols[dr] = _cumsum_dot(_tri_blocks(tb, dr == 1).astype(BF16), small[dr], True)
            g_rows[dr] = _cumsum_dot(_tri_blocks(tb, dr != 1).astype(BF16), refs[dr][4][bi, 0], False)

    def grams():
        for u in units:
            dr, c, hd = u
            rows = slice(c * CHUNK, (c + 1) * CHUNK)
            kh = refs[dr][1][bi, rows, hd * dk:(hd + 1) * dk]
            qh = refs[dr][0][bi, rows, hd * dk:(hd + 1) * dk]
            kq = _dot_nt(jnp.concatenate([kh, qh], axis=0), kh)
            kk[u], qk[u] = kq[:CHUNK], kq[CHUNK:]

    def solve_operands():
        for u in units:
            dr, c, hd = u
            rows = slice(c * CHUNK, (c + 1) * CHUNK)
            ia = dr * nh + hd
            ib = 2 * nh + dr * nh + hd
            g_c = g_cols[dr][rows, ia:ia + 1]
            g_r = g_rows[dr][ia:ia + 1, rows]
            be_c = small[dr][rows, ib:ib + 1]
            decay[u] = jnp.exp(jnp.minimum(g_c - g_r, 0.0))
            x[u] = jnp.where(_tri(CHUNK, dr == 1, True), be_c * kk[u] * decay[u], 0.0)

    def other_operands():
        for u in units:
            dr, c, hd = u
            rows = slice(c * CHUNK, (c + 1) * CHUNK)
            ia = dr * nh + hd
            ib = 2 * nh + dr * nh + hd
            last = c * CHUNK + (0 if dr == 1 else CHUNK - 1)
            g_c = g_cols[dr][rows, ia:ia + 1]
            be_c = small[dr][rows, ib:ib + 1]
            g_last = g_cols[dr][last:last + 1, ia:ia + 1]
            aqk[u] = jnp.where(_tri(CHUNK, dr == 1, False), qk[u] * decay[u], 0.0).astype(BF16)
            eg = jnp.exp(g_c)
            qh = refs[dr][0][bi, rows, hd * dk:(hd + 1) * dk].astype(F32)
            khf = refs[dr][1][bi, rows, hd * dk:(hd + 1) * dk].astype(F32)
            vh = refs[dr][2][bi, rows, hd * dv:(hd + 1) * dv].astype(F32)
            rhs[u] = jnp.concatenate([(be_c * vh).astype(BF16), ((be_c * eg) * khf).astype(BF16)], axis=1)
            q_dec[u] = qh * eg
            k_dec[u] = khf * jnp.exp(g_last - g_c)
            a_last[u] = jnp.exp(g_last)

    def inverse_start():
        for u in units:
            t[u] = jnp.where(eye, 1.0, jnp.where(_couple(CHUNK, 1, u[0] == 1), -x[u], 0.0))

    def inverse_level(s):
        a = {u: jnp.where(_couple(CHUNK, s, u[0] == 1), x[u], 0.0).astype(BF16) for u in units}
        tb16 = {u: t[u].astype(BF16) for u in units}
        ta = {u: _dot(tb16[u], a[u]).astype(BF16) for u in units}
        for u in units:
            t[u] = t[u] - _dot(ta[u], tb16[u])

    levels = []
    s = 2
    while s < CHUNK:
        levels.append(functools.partial(inverse_level, s))
        s *= 2

    def solve():
        for u in units:
            uw[u] = _dot(t[u].astype(BF16), rhs[u]).astype(BF16)

    def fold():
        for u in units:
            both = _dot(jnp.concatenate([k_dec[u].T.astype(BF16), aqk[u]], axis=0), uw[u])
            kd_uw[u], aq_uw[u] = both[:dk], both[dk:]
        for u in units:
            lhs[u] = jnp.concatenate([(-kd_uw[u][:, dv:]).astype(BF16),
                                      (q_dec[u] - aq_uw[u][:, dv:]).astype(BF16)], axis=0)

    def recurrence():
        st = {ch: st_ref[bi, ch[0], ch[1]] for ch in chains}
        for ci in range(nchunk):
            for ch in chains:
                dr, hd = ch
                c = nchunk - 1 - ci if dr == 1 else ci
                u = (dr, c, hd)
                rows = slice(c * CHUNK, (c + 1) * CHUNK)
                res = _dot(lhs[u], st[ch].astype(BF16))
                refs[dr][5][bi, rows, hd * dv:(hd + 1) * dv] = (res[dk:] + aq_uw[u][:, :dv]).astype(BF16)
                st[ch] = st[ch] * a_last[u] + res[:dk] + kd_uw[u][:, :dv]
        for ch in chains:
            st_ref[bi, ch[0], ch[1]] = st[ch]

    return [cumulative_decay, grams, solve_operands, other_operands, inverse_start, *levels,
            solve, fold, recurrence]


def _scan_kernel(af, alf, ab, alb, bf, bcf, brf, bb, bcb, brb,
                 oaf, oab, obf, obb, sta_ref, stb_ref, *, dk_a, dv_a, dk_b, dv_b, nh_b):
    @pl.when(pl.program_id(1) == 0)
    def _():
        sta_ref[...] = jnp.zeros_like(sta_ref)
        stb_ref[...] = jnp.zeros_like(stb_ref)

    qk_a, qk_b = GLA_HEADS * dk_a, nh_b * dk_b
    qkv = lambda ref, qk: (_Cols(ref, 0), _Cols(ref, qk), _Cols(ref, 2 * qk))
    refs_a = ((*qkv(af, qk_a), alf, oaf), (*qkv(ab, qk_a), alb, oab))
    refs_b = ((*qkv(bf, qk_b), bcf, brf, obf), (*qkv(bb, qk_b), bcb, brb, obb))
    stages = {}
    for bi in range(af.shape[0]):
        stages["a", bi] = _gla_stages(refs_a, sta_ref, dk_a, dv_a, bi)
        stages["b", bi] = _gdn_stages(refs_b, stb_ref, dk_b, dv_b, nh_b, bi)
    lag = SCAN_ROW_LAG
    for pos in range(len(SCAN_ORDER) + lag * (af.shape[0] - 1)):
        for bi in range(af.shape[0]):
            if 0 <= pos - lag * bi < len(SCAN_ORDER):
                name, idx = SCAN_ORDER[pos - lag * bi]
                stages[name, bi][idx]()


def _scan(gqkv, gla, dqkv, small_c, small_r, dims):
    b, s, _ = gqkv.shape
    qk_a, v_a, _, _, qk_b, v_b, _ = dims
    nb = s // TOKEN_BLOCK
    tb = TOKEN_BLOCK
    ns = small_c.shape[2]
    nr = math.gcd(b, SCAN_ROWS)
    fwd = lambda n, col=0: pl.BlockSpec((nr, tb, n), lambda i, j: (i, j, col))
    bwd = lambda n, col=0: pl.BlockSpec((nr, tb, n), lambda i, j: (i, _bwd_block(j, nb), col))
    rfwd = pl.BlockSpec((nr, 1, ns, tb), lambda i, j: (i, j, 0, 0))
    rbwd = pl.BlockSpec((nr, 1, ns, tb), lambda i, j: (i, _bwd_block(j, nb), 0, 0))
    dk_a, dv_a = qk_a // GLA_HEADS, v_a // GLA_HEADS
    dk_b, dv_b = qk_b // GDN_HEADS, v_b // GDN_HEADS
    out = lambda v: jax.ShapeDtypeStruct((b, s, v), BF16)
    return pl.pallas_call(
        functools.partial(_scan_kernel, dk_a=dk_a, dv_a=dv_a, dk_b=dk_b, dv_b=dv_b, nh_b=GDN_HEADS),
        out_shape=[out(v_a), out(v_a), out(v_b), out(v_b)],
        grid=(b // nr, nb),
        in_specs=[fwd(2 * qk_a + v_a), fwd(qk_a, 0), bwd(2 * qk_a + v_a), bwd(qk_a, 1),
                  fwd(2 * qk_b + v_b), fwd(ns), rfwd, bwd(2 * qk_b + v_b), bwd(ns), rbwd],
        out_specs=[fwd(v_a), bwd(v_a), fwd(v_b), bwd(v_b)],
        scratch_shapes=[pltpu.VMEM((nr, 2, qk_a // LANES, dv_a, LANES), F32),
                        pltpu.VMEM((nr, 2, GDN_HEADS, dk_b, dv_b), F32)],
        compiler_params=pltpu.CompilerParams(dimension_semantics=("parallel", "arbitrary"),
                                             vmem_limit_bytes=VMEM_LIMIT),
        name="scan",
    )(gqkv, gla, gqkv, gla, dqkv, small_c, small_r, dqkv, small_c, small_r)


def _post_kernel(x_ref, mod_ref, *refs, d, nh_a, nh_b, final, nsub):
    mix = [refs[6 * i:6 * i + 6] for i in range(nsub)]
    na_ref, nb_ref, wo_ref, g2_ref, w1_ref, w2_ref, fg_ref, o_ref = refs[6 * nsub:]
    tb = x_ref.shape[1] // nsub
    m = mod_ref[0]
    gt1 = m[:, 2 * d:3 * d]
    sh2, sc2, gt2 = m[:, 3 * d:4 * d], m[:, 4 * d:5 * d], m[:, 5 * d:6 * d]
    slab = 2 * LANES
    x1, hb, act = {}, {}, {}

    def merge_and_project(i):
        af_ref, ab_ref, bf_ref, bb_ref, ga_ref, gb_ref = mix[i]
        acc = None
        row = 0
        for (f_ref, b_ref, g_ref, n_ref, nh) in ((af_ref, ab_ref, ga_ref, na_ref, nh_a),
                                                 (bf_ref, bb_ref, gb_ref, nb_ref, nh_b)):
            width = f_ref.shape[2]
            hv = width // nh
            for c0 in range(0, width, slab):
                o = f_ref[0, :, c0:c0 + slab].astype(F32) + b_ref[0, :, c0:c0 + slab].astype(F32)
                gate = _silu(g_ref[0, :, c0:c0 + slab].astype(F32))
                parts = [(_rms(o[:, k:k + hv]) * n_ref[...] * gate[:, k:k + hv]).astype(BF16)
                         for k in range(0, slab, hv)]
                part = _dot(jnp.concatenate(parts, axis=1), wo_ref[row:row + slab, :])
                acc = part if acc is None else acc + part
                row += slab
        x1[i] = x_ref[0, i * tb:(i + 1) * tb, :] + gt1 * acc
        hb[i] = (_rms(x1[i]) * (g2_ref[...] * (1.0 + sc2)) + sh2).astype(BF16)

    def mlp_up(i):
        a = jnp.maximum(_dot(hb[i], w1_ref[...]), 0.0)
        act[i] = (a * a).astype(BF16)

    def mlp_down(i):
        x2 = x1[i] + gt2 * _dot(act[i], w2_ref[...])
        if final:
            x2 = _rms(x2) * fg_ref[...]
        o_ref[0, i * tb:(i + 1) * tb, :] = x2

    merge_and_project(0)
    for i in range(nsub):
        mlp_up(i)
        if i + 1 < nsub:
            merge_and_project(i + 1)
        mlp_down(i)


def _post(stream, part, mod, layer, final, oaf, oab, obf, obb, gg, dg, params, fg, nh_a, nh_b):
    na, nb_, wo, g2, w1, w2 = params
    ctx_arr, lat_arr = stream
    b, t, d = lat_arr.shape
    tb = TOKEN_BLOCK
    ctx = part == "ctx"
    nsub = 1 if ctx else math.gcd(t // tb, POST_SUB)
    nstep = 1 if ctx else t // (nsub * tb)
    first = 0 if ctx else 1
    lay = lambda a: pl.BlockSpec((None,) + a.shape[1:], lambda i, j: (layer,) + (0,) * (a.ndim - 1))
    weight = lambda a: pl.BlockSpec((None,) + a.shape[1:], lambda i, j: (layer,) + (0,) * (a.ndim - 1),
                                    pipeline_mode=pl.Buffered(1))
    bsz = b
    mod_spec = pl.BlockSpec((1, 1, 6 * d), lambda i, j: (layer * MOD_ROWS + (bsz if ctx else i), 0, 0))
    va, vb = oaf.shape[2], obf.shape[2]
    mix_specs, mix_args = [], []
    for k in range(nsub):
        blk = lambda n, k=k: pl.BlockSpec((1, tb, n), lambda i, j: (i, first + j * nsub + k, 0))
        mix_specs += [blk(va), blk(va), blk(vb), blk(vb), blk(va), blk(vb)]
        mix_args += [oaf, oab, obf, obb, gg, dg]
    return pl.pallas_call(
        functools.partial(_post_kernel, d=d, nh_a=nh_a, nh_b=nh_b, final=final, nsub=nsub),
        out_shape=jax.ShapeDtypeStruct((b, nstep * nsub * tb, d), F32),
        grid=(b, nstep),
        in_specs=[pl.BlockSpec((1, nsub * tb, d), lambda i, j: (i, j, 0)), mod_spec, *mix_specs,
                  lay(na), lay(nb_), weight(wo), lay(g2), weight(w1), weight(w2),
                  pl.BlockSpec((1, d), lambda i, j: (0, 0))],
        out_specs=pl.BlockSpec((1, nsub * tb, d), lambda i, j: (i, j, 0)),
        compiler_params=pltpu.CompilerParams(dimension_semantics=("parallel", "parallel"),
                                             vmem_limit_bytes=VMEM_LIMIT),
        name="post_ctx" if ctx else "post",
    )(ctx_arr if ctx else lat_arr, mod, *mix_args, na, nb_, wo, g2, w1, w2, fg.reshape(1, d))


def kernel(x, c, ctx, c_ctx, w_ada, b_ada, norm1_g, norm2_g, w_in, gla_w_lr, gla_b_lr, gdn_conv_w,
           gdn_a_log, gdn_dt_bias, gla_norm_g, gdn_norm_g, w_out, w_ff1, w_ff2, final_norm_g):
    bsz, t, d = x.shape
    depth = w_ada.shape[0]
    assert ctx.shape[1] == TOKEN_BLOCK and t % TOKEN_BLOCK == 0 and TOKEN_BLOCK % GRID_W == 0
    assert bsz + 1 <= MOD_ROWS
    qk_a = gla_w_lr.shape[3]
    rank = gla_w_lr.shape[2]
    v_a = gla_norm_g.shape[1] * GLA_HEADS
    conv_dim = gdn_conv_w.shape[2]
    v_b = gdn_norm_g.shape[1] * GDN_HEADS
    qk_b = (conv_dim - v_b) // 2
    nd = 2 * GDN_HEADS
    dims = (qk_a, v_a, rank, conv_dim, qk_b, v_b, GDN_HEADS)

    o_r = 2 * qk_a + 2 * v_a
    o_c = o_r + rank
    o_g = o_c + conv_dim
    o_s = o_g + v_b
    assert w_in.shape[2] == o_s + 2 * nd
    w_main = jnp.concatenate([w_in[:, :, :o_r], w_in[:, :, o_c:o_s]], axis=2).astype(BF16)
    w_sm = jnp.concatenate([w_in[:, :, o_r:o_c], w_in[:, :, o_s:]], axis=2)
    w_small = jnp.pad(w_sm, ((0, 0), (0, 0), (0, LANES - w_sm.shape[2]))).astype(BF16)
    w_small_t = jnp.swapaxes(w_in[:, :, o_s:], 1, 2).astype(BF16)

    cc = jnp.concatenate([c, c_ctx[None, :], jnp.zeros((MOD_ROWS - bsz - 1, d), F32)], axis=0)
    mod = _modulation(cc, w_ada, b_ada).reshape(depth * MOD_ROWS, 1, 6 * d)

    in_params = (norm1_g.reshape(depth, 1, d), w_main, w_small, w_small_t, gla_w_lr.astype(BF16),
                 gla_b_lr.reshape(depth, 2, 1, qk_a), gdn_conv_w,
                 gdn_a_log.reshape(depth, 1, nd), gdn_dt_bias.reshape(depth, 1, nd),
                 gdn_a_log.reshape(depth, nd, 1), gdn_dt_bias.reshape(depth, nd, 1))
    post_params = (gla_norm_g.reshape(depth, 1, -1), gdn_norm_g.reshape(depth, 1, -1), w_out.astype(BF16),
                   norm2_g.reshape(depth, 1, d), w_ff1.astype(BF16), w_ff2.astype(BF16))

    stream = (ctx, x)
    for l in range(depth):
        last = l == depth - 1
        gqkv, gg, gla, dqkv, dg, small_c, small_r = _in_proj(stream, mod, l, bsz, in_params, dims)
        oaf, oab, obf, obb = _scan(gqkv, gla, dqkv, small_c, small_r, dims)
        args = (mod, l, last, oaf, oab, obf, obb, gg, dg, post_params, final_norm_g, GLA_HEADS, GDN_HEADS)
        xs = _post(stream, "latent", *args)
        if not last:
            stream = (_post(stream, "ctx", *args), xs)
    return xs
```

```python
import functools
import math

import jax
import jax.numpy as jnp
from jax import lax
from jax.experimental import pallas as pl
from jax.experimental.pallas import tpu as pltpu

EPS = 1e-6
GRID_W = 64
GLA_HEADS = 4
GDN_HEADS = 4
GLA_TAU = 16.0
TOKEN_BLOCK = 256
CHUNK = 64
LANES = 128
SUBLANES = 8
POST_SUB = 2
SCAN_ROWS = 2
IN_ROWS = 4
SCAN_ROW_LAG = 6
MOD_ROWS = 16
VMEM_LIMIT = 56 * 1024 * 1024
SCAN_ORDER = (("b", 0), ("b", 1), ("b", 2), ("a", 0), ("a", 1), ("a", 2), ("a", 3), ("b", 3), ("b", 4),
              ("b", 5), ("b", 6), ("a", 4), ("b", 7), ("b", 8), ("b", 9), ("b", 10), ("b", 11), ("b", 12))

F32 = jnp.float32
BF16 = jnp.bfloat16


def _dot(a, b):
    return jnp.dot(a, b, preferred_element_type=F32)


def _dot_nt(a, b):
    return lax.dot_general(a, b, (((1,), (1,)), ((), ())), preferred_element_type=F32)


def _dot_tn(a, b):
    return lax.dot_general(a, b, (((0,), (0,)), ((), ())), preferred_element_type=F32)


def _silu(x):
    return x / (1.0 + jnp.exp(-x))


def _sigmoid(x):
    return 1.0 / (1.0 + jnp.exp(-x))


def _softplus(x):
    return jnp.maximum(x, 0.0) + jnp.log(1.0 + jnp.exp(-jnp.abs(x)))


def _log_sigmoid(x):
    return jnp.minimum(x, 0.0) - jnp.log(1.0 + jnp.exp(-jnp.abs(x)))


def _rms(x):
    return x * lax.rsqrt(jnp.mean(x * x, axis=-1, keepdims=True) + EPS)


def _cumsum_dot(tri_bf16, x, left):
    hi = x.astype(BF16)
    lo = (x - hi.astype(F32)).astype(BF16)
    if left:
        return _dot(tri_bf16, hi) + _dot(tri_bf16, lo)
    return _dot(hi, tri_bf16) + _dot(lo, tri_bf16)


def _tri(n, upper, strict):
    r = lax.broadcasted_iota(jnp.int32, (n, n), 0)
    c = lax.broadcasted_iota(jnp.int32, (n, n), 1)
    if upper:
        m = (r < c) if strict else (r <= c)
    else:
        m = (r > c) if strict else (r >= c)
    return m


def _couple(n, s, upper):
    r = lax.broadcasted_iota(jnp.int32, (n, n), 0)
    c = lax.broadcasted_iota(jnp.int32, (n, n), 1)
    same = (r & ~(2 * s - 1)) == (c & ~(2 * s - 1))
    r_hi, c_hi = (r & s) != 0, (c & s) != 0
    return same & ((~r_hi & c_hi) if upper else (r_hi & ~c_hi))


def _tri_blocks(n, upper):
    r = lax.broadcasted_iota(jnp.int32, (n, n), 0)
    c = lax.broadcasted_iota(jnp.int32, (n, n), 1)
    same = (r & ~(CHUNK - 1)) == (c & ~(CHUNK - 1))
    return same & ((r <= c) if upper else (r >= c))


def _mod_kernel(cc_ref, w_ref, b_ref, o_ref):
    s = _silu(cc_ref[...]).astype(BF16)
    o_ref[0] = _dot(s, w_ref[0].astype(BF16)) + b_ref[0]


def _modulation(cc, w_ada, b_ada):
    depth, d, d6 = w_ada.shape
    nblk = d6 // d
    return pl.pallas_call(
        _mod_kernel,
        out_shape=jax.ShapeDtypeStruct((depth, MOD_ROWS, d6), F32),
        grid=(depth, nblk),
        in_specs=[
            pl.BlockSpec((MOD_ROWS, d), lambda l, n: (0, 0)),
            pl.BlockSpec((1, d, d), lambda l, n: (l, 0, n)),
            pl.BlockSpec((1, 1, d), lambda l, n: (l, 0, n)),
        ],
        out_specs=pl.BlockSpec((1, MOD_ROWS, d), lambda l, n: (l, 0, n)),
        compiler_params=pltpu.CompilerParams(dimension_semantics=("parallel", "parallel")),
        name="adaln_modulation",
    )(cc, w_ada, b_ada.reshape(depth, 1, d6))


def _in_proj_kernel(c_ref, x_ref, *refs, d, qk_a, v_a, rank, conv_dim, qk_b, v_b, nh_b):
    nr = x_ref.shape[0]
    mod_refs = refs[:nr]
    (g1_ref, wm_ref, ws_ref, wst_ref, wlr_ref, blr_ref, cw_ref, arow_ref, dtrow_ref, acol_ref, dtcol_ref,
     gqkv_ref, gg_ref, gla_ref, dqkv_ref, dg_ref, sc_ref, sr_ref) = refs[nr:]
    for bi in range(nr):
        _in_proj_row(bi, c_ref, x_ref, mod_refs[bi], g1_ref, wm_ref, ws_ref, wst_ref, wlr_ref, blr_ref, cw_ref,
                     arow_ref, dtrow_ref, acol_ref, dtcol_ref, gqkv_ref, gg_ref, gla_ref, dqkv_ref, dg_ref,
                     sc_ref, sr_ref, d=d, qk_a=qk_a, v_a=v_a, rank=rank, conv_dim=conv_dim, qk_b=qk_b, v_b=v_b,
                     nh_b=nh_b)


def _in_proj_row(bi, c_ref, x_ref, mod_ref, g1_ref, wm_ref, ws_ref, wst_ref, wlr_ref, blr_ref, cw_ref,
                 arow_ref, dtrow_ref, acol_ref, dtcol_ref,
                 gqkv_ref, gg_ref, gla_ref, dqkv_ref, dg_ref, sc_ref, sr_ref, *, d, qk_a, v_a, rank, conv_dim,
                 qk_b, v_b, nh_b):
    j = pl.program_id(1)
    x = jnp.where(j == 0, c_ref[bi], x_ref[bi])
    m = mod_ref[0]
    sh1, sc1 = m[:, 0:d], m[:, d:2 * d]
    h = _rms(x) * (g1_ref[...] * (1.0 + sc1)) + sh1
    hb = h.astype(BF16)
    o_conv = 2 * qk_a + 2 * v_a
    o_gate = o_conv + conv_dim
    grp = 2 * LANES
    tb = x.shape[0]
    t = lax.broadcasted_iota(jnp.int32, (tb, 1), 0)
    seg_mask = jnp.where(j == 0, tb - 1, GRID_W - 1)
    first = (t & seg_mask) == 0
    last = (t & seg_mask) == seg_mask
    dkh = qk_b // nh_b

    def conv_dot(k):
        return _dot(hb, wm_ref[:, o_conv + k * grp:o_conv + (k + 1) * grp])

    def zero_rows(a, mask, at):
        pieces = []
        for s0 in range(0, tb, GRID_W):
            r0 = s0 + at
            pieces += [a[s0:r0], jnp.where(mask[r0:r0 + SUBLANES], 0.0, a[r0:r0 + SUBLANES]),
                       a[r0 + SUBLANES:s0 + GRID_W]]
        return jnp.concatenate([p for p in pieces if p.shape[0]], axis=0)

    def conv_group(u, k):
        up = zero_rows(pltpu.roll(u, 1, 0), first, 0)
        un = zero_rows(pltpu.roll(u, tb - 1, 0), last, GRID_W - SUBLANES)
        cw = cw_ref[:, k * grp:(k + 1) * grp]
        s = _silu(cw[0:1] * up + cw[1:2] * u + cw[2:3] * un)
        c0 = k * grp
        if c0 < 2 * qk_b:
            for hh in range(grp // dkh):
                sh = s[:, hh * dkh:(hh + 1) * dkh]
                inv = lax.rsqrt(jnp.sum(sh * sh, axis=-1, keepdims=True) + EPS)
                if c0 < qk_b:
                    inv = inv * (float(dkh) ** -0.5)
                dqkv_ref[bi, :, c0 + hh * dkh:c0 + (hh + 1) * dkh] = (sh * inv).astype(BF16)
        else:
            dqkv_ref[bi, :, c0:c0 + grp] = s.astype(BF16)

    def light_group(k):
        c0 = k * grp
        w0 = c0 if c0 < o_conv else o_gate + (c0 - o_conv)
        pa = _dot(hb, wm_ref[:, w0:w0 + grp])
        if c0 < qk_a:
            gqkv_ref[bi, :, c0:c0 + grp] = (pa * (float(qk_a // GLA_HEADS) ** -0.5)).astype(BF16)
        elif c0 < 2 * qk_a + v_a:
            gqkv_ref[bi, :, c0:c0 + grp] = pa.astype(BF16)
        elif c0 < o_conv:
            gg_ref[bi, :, c0 - 2 * qk_a - v_a:c0 - 2 * qk_a - v_a + grp] = pa.astype(BF16)
        else:
            dg_ref[bi, :, c0 - o_conv:c0 - o_conv + grp] = pa.astype(BF16)

    def small_group(ps, pst):
        r_a = ps[:, 0:rank].astype(BF16)
        for dr in range(2):
            lr = _dot(r_a, wlr_ref[dr]) + blr_ref[dr]
            gla_ref[bi, :, dr * qk_a:(dr + 1) * qk_a] = _log_sigmoid(lr) * (1.0 / GLA_TAU)
        nd = 2 * nh_b
        a_c = ps[:, rank:rank + nd]
        b_c = ps[:, rank + nd:rank + 2 * nd]
        sc_ref[bi, :, 0:nd] = -jnp.exp(arow_ref[...]) * _softplus(a_c + dtrow_ref[...])
        sc_ref[bi, :, nd:2 * nd] = _sigmoid(b_c)
        sr_ref[bi, 0, 0:nd, :] = -jnp.exp(acol_ref[...]) * _softplus(pst[0:nd] + dtcol_ref[...])
        sr_ref[bi, 0, nd:2 * nd, :] = _sigmoid(pst[nd:2 * nd])

    n_conv = conv_dim // grp
    n_light = (o_conv + v_b) // grp
    us = {0: conv_dot(0)}
    ps = _dot(hb, ws_ref[...])
    pst = _dot_nt(wst_ref[...], hb)
    if n_conv > 1:
        us[1] = conv_dot(1)
    g_next = 0
    for k in range(n_conv):
        if k + 2 < n_conv:
            us[k + 2] = conv_dot(k + 2)
        conv_group(us.pop(k), k)
        if k == 1 or n_conv == 1:
            small_group(ps, pst)
        todo = (n_light - g_next + (n_conv - k) - 1) // (n_conv - k) if k >= n_conv // 2 else 1
        for _ in range(min(todo, n_light - g_next)):
            light_group(g_next)
            g_next += 1
    while g_next < n_light:
        light_group(g_next)
        g_next += 1


def _in_proj(stream, mod, layer, bsz, params, dims):
    g1, wm, ws, wst, wlr, blr, cw, a_row, dt_row, a_col, dt_col = params
    head, body = stream
    b, _, d = body.shape
    nb = body.shape[1] // TOKEN_BLOCK + 1
    s = nb * TOKEN_BLOCK
    qk_a, v_a, rank, conv_dim, qk_b, v_b, nh_b = dims
    nd = 2 * nh_b
    tb = TOKEN_BLOCK
    lay = lambda a: pl.BlockSpec((None,) + a.shape[1:], lambda i, j: (layer,) + (0,) * (a.ndim - 1))
    nr = math.gcd(b, IN_ROWS)
    tok = lambda n: pl.BlockSpec((nr, tb, n), lambda i, j: (i, j, 0))
    mod_specs = [pl.BlockSpec((1, 1, 6 * d), lambda i, j, k=k: (
        layer * MOD_ROWS + jnp.where(j == 0, bsz, i * nr + k), 0, 0)) for k in range(nr)]
    out_shape = [
        jax.ShapeDtypeStruct((b, s, 2 * qk_a + v_a), BF16), jax.ShapeDtypeStruct((b, s, v_a), BF16),
        jax.ShapeDtypeStruct((b, s, 2 * qk_a), F32),
        jax.ShapeDtypeStruct((b, s, conv_dim), BF16), jax.ShapeDtypeStruct((b, s, v_b), BF16),
        jax.ShapeDtypeStruct((b, s, 2 * nd), F32),
        jax.ShapeDtypeStruct((b, nb, 2 * nd, tb), F32),
    ]
    out_specs = [tok(2 * qk_a + v_a), tok(v_a), tok(2 * qk_a), tok(conv_dim), tok(v_b), tok(2 * nd),
                 pl.BlockSpec((nr, 1, 2 * nd, tb), lambda i, j: (i, j, 0, 0))]
    kern = functools.partial(_in_proj_kernel, d=d, qk_a=qk_a, v_a=v_a, rank=rank, conv_dim=conv_dim,
                             qk_b=qk_b, v_b=v_b, nh_b=nh_b)
    return pl.pallas_call(
        kern, out_shape=out_shape, grid=(b // nr, nb),
        in_specs=[
            pl.BlockSpec((nr, tb, d), lambda i, j: (i, 0, 0)),
            pl.BlockSpec((nr, tb, d), lambda i, j: (i, jnp.maximum(j - 1, 0), 0)), *mod_specs, lay(g1),
            lay(wm), lay(ws), lay(wst), lay(wlr), lay(blr), lay(cw), lay(a_row), lay(dt_row), lay(a_col), lay(dt_col),
        ],
        out_specs=out_specs,
        compiler_params=pltpu.CompilerParams(dimension_semantics=("parallel", "parallel"),
                                             vmem_limit_bytes=VMEM_LIMIT),
        name="in_proj",
    )(head, body, *([mod] * nr), g1, wm, ws, wst, wlr, blr, cw, a_row, dt_row, a_col, dt_col)


def _bwd_block(j, nb):
    return jnp.where(j == 0, 0, nb - j)


class _Cols:
    def __init__(self, ref, off):
        self.ref, self.off = ref, off

    def __getitem__(self, idx):
        z, rows, cols = idx
        return self.ref[z, rows, self.off + cols.start:self.off + cols.stop]


def _gla_stages(refs, st_ref, dk, dv, bi):
    tb = refs[0][3].shape[1]
    nchunk = tb // CHUNK
    npair = (GLA_HEADS * dk) // LANES
    hpp = LANES // dk
    lane = lax.broadcasted_iota(jnp.int32, (1, LANES), 1)
    lms = [(lane >= hh * dk) & (lane < (hh + 1) * dk) for hh in range(hpp)]
    mid = CHUNK // 2
    chains = [(dr, pr) for dr in range(2) for pr in range(npair)]
    units = [(dr, pr, c) for (dr, pr) in chains for c in range(nchunk)]
    bcum, q_dec, k_hat, q_mid, k_mid, a_last, sc, intra, dst = ({} for _ in range(9))

    def cumulative_decay():
        for dr in range(2):
            both = _cumsum_dot(_tri_blocks(tb, dr == 1).astype(BF16), refs[dr][3][bi], True)
            for pr in range(npair):
                bcum[dr, pr] = both[:, pr * LANES:(pr + 1) * LANES]

    def decayed_operands():
        for u in units:
            dr, pr, c = u
            rows = slice(c * CHUNK, (c + 1) * CHUNK)
            lanes = slice(pr * LANES, (pr + 1) * LANES)
            last = c * CHUNK + (0 if dr == 1 else CHUNK - 1)
            b = bcum[dr, pr][rows]
            b_last = bcum[dr, pr][last:last + 1]
            b_mid = bcum[dr, pr][c * CHUNK + mid:c * CHUNK + mid + 1]
            qc = refs[dr][0][bi, rows, lanes].astype(F32)
            kc = refs[dr][1][bi, rows, lanes].astype(F32)
            q_dec[u] = qc * jnp.exp(b)
            k_hat[u] = (kc * jnp.exp(b_last - b)).astype(BF16)
            q_mid[u] = qc * jnp.exp(b - b_mid)
            k_mid[u] = (kc * jnp.exp(b_mid - b)).astype(BF16)
            a_last[u] = jnp.exp(b_last)

    def per_head(m):
        return jnp.concatenate([jnp.where(lm, m, 0.0).astype(BF16) for lm in lms], axis=0)

    def scores():
        for u in units:
            sc[u] = _dot_nt(per_head(q_mid[u]), k_mid[u])

    def intra_and_increments():
        for u in units:
            dr, pr, c = u
            rows = slice(c * CHUNK, (c + 1) * CHUNK)
            causal = _tri(CHUNK, dr == 1, False)
            v_grp = refs[dr][2][bi, rows, pr * hpp * dv:(pr + 1) * hpp * dv]
            dh = _dot_tn(v_grp, k_hat[u])
            acc = None
            for hh in range(hpp):
                pm = jnp.where(causal, sc[u][hh * CHUNK:(hh + 1) * CHUNK], 0.0).astype(BF16)
                intra[u, hh] = _dot(pm, v_grp[:, hh * dv:(hh + 1) * dv])
                part = dh[hh * dv:(hh + 1) * dv]
                acc = part if acc is None else jnp.where(lms[hh], part, acc)
            dst[u] = acc

    def recurrence():
        st = {ch: st_ref[bi, ch[0], ch[1]] for ch in chains}
        for ci in range(nchunk):
            for ch in chains:
                dr, pr = ch
                c = nchunk - 1 - ci if dr == 1 else ci
                u = (dr, pr, c)
                rows = slice(c * CHUNK, (c + 1) * CHUNK)
                inter = _dot_nt(per_head(q_dec[u]), st[ch].astype(BF16))
                for hh in range(hpp):
                    head = pr * hpp + hh
                    refs[dr][4][bi, rows, head * dv:(head + 1) * dv] = (
                        intra[u, hh] + inter[hh * CHUNK:(hh + 1) * CHUNK]).astype(BF16)
                st[ch] = st[ch] * a_last[u] + dst[u]
        for ch in chains:
            st_ref[bi, ch[0], ch[1]] = st[ch]

    return [cumulative_decay, decayed_operands, scores, intra_and_increments, recurrence]


def _gdn_stages(refs, st_ref, dk, dv, nh, bi):
    tb = refs[0][3].shape[1]
    nchunk = tb // CHUNK
    eye = _tri(CHUNK, False, False) & _tri(CHUNK, True, False)
    chains = [(dr, hd) for dr in range(2) for hd in range(nh)]
    units = [(dr, c, hd) for dr in range(2) for c in range(nchunk) for hd in range(nh)]
    g_cols, g_rows, small, kk, qk, x, aqk, rhs, q_dec, k_dec, a_last = ({} for _ in range(11))
    t, uw, kd_uw, aq_uw, lhs, decay = ({} for _ in range(6))

    def cumulative_decay():
        for dr in range(2):
            small[dr] = refs[dr][3][bi]
            g_cols[dr] = _cumsum_dot(_tri_blocks(tb, dr == 1).astype(BF16), small[dr], True)
            g_rows[dr] = _cumsum_dot(_tri_blocks(tb, dr != 1).astype(BF16), refs[dr][4][bi, 0], False)

    def grams():
        for u in units:
            dr, c, hd = u
            rows = slice(c * CHUNK, (c + 1) * CHUNK)
            kh = refs[dr][1][bi, rows, hd * dk:(hd + 1) * dk]
            qh = refs[dr][0][bi, rows, hd * dk:(hd + 1) * dk]
            kq = _dot_nt(jnp.concatenate([kh, qh], axis=0), kh)
            kk[u], qk[u] = kq[:CHUNK], kq[CHUNK:]

    def solve_operands():
        for u in units:
            dr, c, hd = u
            rows = slice(c * CHUNK, (c + 1) * CHUNK)
            ia = dr * nh + hd
            ib = 2 * nh + dr * nh + hd
            g_c = g_cols[dr][rows, ia:ia + 1]
            g_r = g_rows[dr][ia:ia + 1, rows]
            be_c = small[dr][rows, ib:ib + 1]
            decay[u] = jnp.exp(jnp.minimum(g_c - g_r, 0.0))
            x[u] = jnp.where(_tri(CHUNK, dr == 1, True), be_c * kk[u] * decay[u], 0.0)

    def other_operands():
        for u in units:
            dr, c, hd = u
            rows = slice(c * CHUNK, (c + 1) * CHUNK)
            ia = dr * nh + hd
            ib = 2 * nh + dr * nh + hd
            last = c * CHUNK + (0 if dr == 1 else CHUNK - 1)
            g_c = g_cols[dr][rows, ia:ia + 1]
            be_c = small[dr][rows, ib:ib + 1]
            g_last = g_cols[dr][last:last + 1, ia:ia + 1]
            aqk[u] = jnp.where(_tri(CHUNK, dr == 1, False), qk[u] * decay[u], 0.0).astype(BF16)
            eg = jnp.exp(g_c)
            qh = refs[dr][0][bi, rows, hd * dk:(hd + 1) * dk].astype(F32)
            khf = refs[dr][1][bi, rows, hd * dk:(hd + 1) * dk].astype(F32)
            vh = refs[dr][2][bi, rows, hd * dv:(hd + 1) * dv].astype(F32)
            rhs[u] = jnp.concatenate([(be_c * vh).astype(BF16), ((be_c * eg) * khf).astype(BF16)], axis=1)
            q_dec[u] = qh * eg
            k_dec[u] = khf * jnp.exp(g_last - g_c)
            a_last[u] = jnp.exp(g_last)

    def inverse_start():
        for u in units:
            t[u] = jnp.where(eye, 1.0, jnp.where(_couple(CHUNK, 1, u[0] == 1), -x[u], 0.0))

    def inverse_level(s):
        a = {u: jnp.where(_couple(CHUNK, s, u[0] == 1), x[u], 0.0).astype(BF16) for u in units}
        tb16 = {u: t[u].astype(BF16) for u in units}
        ta = {u: _dot(tb16[u], a[u]).astype(BF16) for u in units}
        for u in units:
            t[u] = t[u] - _dot(ta[u], tb16[u])

    levels = []
    s = 2
    while s < CHUNK:
        levels.append(functools.partial(inverse_level, s))
        s *= 2

    def solve():
        for u in units:
            uw[u] = _dot(t[u].astype(BF16), rhs[u]).astype(BF16)

    def fold():
        for u in units:
            both = _dot(jnp.concatenate([k_dec[u].T.astype(BF16), aqk[u]], axis=0), uw[u])
            kd_uw[u], aq_uw[u] = both[:dk], both[dk:]
        for u in units:
            lhs[u] = jnp.concatenate([(-kd_uw[u][:, dv:]).astype(BF16),
                                      (q_dec[u] - aq_uw[u][:, dv:]).astype(BF16)], axis=0)

    def recurrence():
        st = {ch: st_ref[bi, ch[0], ch[1]] for ch in chains}
        for ci in range(nchunk):
            for ch in chains:
                dr, hd = ch
                c = nchunk - 1 - ci if dr == 1 else ci
                u = (dr, c, hd)
                rows = slice(c * CHUNK, (c + 1) * CHUNK)
                res = _dot(lhs[u], st[ch].astype(BF16))
                refs[dr][5][bi, rows, hd * dv:(hd + 1) * dv] = (res[dk:] + aq_uw[u][:, :dv]).astype(BF16)
                st[ch] = st[ch] * a_last[u] + res[:dk] + kd_uw[u][:, :dv]
        for ch in chains:
            st_ref[bi, ch[0], ch[1]] = st[ch]

    return [cumulative_decay, grams, solve_operands, other_operands, inverse_start, *levels,
            solve, fold, recurrence]


def _scan_kernel(af, alf, ab, alb, bf, bcf, brf, bb, bcb, brb,
                 oaf, oab, obf, obb, sta_ref, stb_ref, *, dk_a, dv_a, dk_b, dv_b, nh_b):
    @pl.when(pl.program_id(1) == 0)
    def _():
        sta_ref[...] = jnp.zeros_like(sta_ref)
        stb_ref[...] = jnp.zeros_like(stb_ref)

    qk_a, qk_b = GLA_HEADS * dk_a, nh_b * dk_b
    qkv = lambda ref, qk: (_Cols(ref, 0), _Cols(ref, qk), _Cols(ref, 2 * qk))
    refs_a = ((*qkv(af, qk_a), alf, oaf), (*qkv(ab, qk_a), alb, oab))
    refs_b = ((*qkv(bf, qk_b), bcf, brf, obf), (*qkv(bb, qk_b), bcb, brb, obb))
    stages = {}
    for bi in range(af.shape[0]):
        stages["a", bi] = _gla_stages(refs_a, sta_ref, dk_a, dv_a, bi)
        stages["b", bi] = _gdn_stages(refs_b, stb_ref, dk_b, dv_b, nh_b, bi)
    lag = SCAN_ROW_LAG
    for pos in range(len(SCAN_ORDER) + lag * (af.shape[0] - 1)):
        for bi in range(af.shape[0]):
            if 0 <= pos - lag * bi < len(SCAN_ORDER):
                name, idx = SCAN_ORDER[pos - lag * bi]
                stages[name, bi][idx]()


def _scan(gqkv, gla, dqkv, small_c, small_r, dims):
    b, s, _ = gqkv.shape
    qk_a, v_a, _, _, qk_b, v_b, _ = dims
    nb = s // TOKEN_BLOCK
    tb = TOKEN_BLOCK
    ns = small_c.shape[2]
    nr = math.gcd(b, SCAN_ROWS)
    fwd = lambda n, col=0: pl.BlockSpec((nr, tb, n), lambda i, j: (i, j, col))
    bwd = lambda n, col=0: pl.BlockSpec((nr, tb, n), lambda i, j: (i, _bwd_block(j, nb), col))
    rfwd = pl.BlockSpec((nr, 1, ns, tb), lambda i, j: (i, j, 0, 0))
    rbwd = pl.BlockSpec((nr, 1, ns, tb), lambda i, j: (i, _bwd_block(j, nb), 0, 0))
    dk_a, dv_a = qk_a // GLA_HEADS, v_a // GLA_HEADS
    dk_b, dv_b = qk_b // GDN_HEADS, v_b // GDN_HEADS
    out = lambda v: jax.ShapeDtypeStruct((b, s, v), BF16)
    return pl.pallas_call(
        functools.partial(_scan_kernel, dk_a=dk_a, dv_a=dv_a, dk_b=dk_b, dv_b=dv_b, nh_b=GDN_HEADS),
        out_shape=[out(v_a), out(v_a), out(v_b), out(v_b)],
        grid=(b // nr, nb),
        in_specs=[fwd(2 * qk_a + v_a), fwd(qk_a, 0), bwd(2 * qk_a + v_a), bwd(qk_a, 1),
                  fwd(2 * qk_b + v_b), fwd(ns), rfwd, bwd(2 * qk_b + v_b), bwd(ns), rbwd],
        out_specs=[fwd(v_a), bwd(v_a), fwd(v_b), bwd(v_b)],
        scratch_shapes=[pltpu.VMEM((nr, 2, qk_a // LANES, dv_a, LANES), F32),
                        pltpu.VMEM((nr, 2, GDN_HEADS, dk_b, dv_b), F32)],
        compiler_params=pltpu.CompilerParams(dimension_semantics=("parallel", "arbitrary"),
                                             vmem_limit_bytes=VMEM_LIMIT),
        name="scan",
    )(gqkv, gla, gqkv, gla, dqkv, small_c, small_r, dqkv, small_c, small_r)


def _post_kernel(x_ref, mod_ref, *refs, d, nh_a, nh_b, final, nsub):
    mix = [refs[6 * i:6 * i + 6] for i in range(nsub)]
    na_ref, nb_ref, wo_ref, g2_ref, w1_ref, w2_ref, fg_ref, o_ref = refs[6 * nsub:]
    tb = x_ref.shape[1] // nsub
    m = mod_ref[0]
    gt1 = m[:, 2 * d:3 * d]
    sh2, sc2, gt2 = m[:, 3 * d:4 * d], m[:, 4 * d:5 * d], m[:, 5 * d:6 * d]
    slab = 2 * LANES
    x1, hb, act = {}, {}, {}

    def merge_and_project(i):
        af_ref, ab_ref, bf_ref, bb_ref, ga_ref, gb_ref = mix[i]
        acc = None
        row = 0
        for (f_ref, b_ref, g_ref, n_ref, nh) in ((af_ref, ab_ref, ga_ref, na_ref, nh_a),
                                                 (bf_ref, bb_ref, gb_ref, nb_ref, nh_b)):
            width = f_ref.shape[2]
            hv = width // nh
            for c0 in range(0, width, slab):
                o = f_ref[0, :, c0:c0 + slab].astype(F32) + b_ref[0, :, c0:c0 + slab].astype(F32)
                gate = _silu(g_ref[0, :, c0:c0 + slab].astype(F32))
                parts = [(_rms(o[:, k:k + hv]) * n_ref[...] * gate[:, k:k + hv]).astype(BF16)
                         for k in range(0, slab, hv)]
                part = _dot(jnp.concatenate(parts, axis=1), wo_ref[row:row + slab, :])
                acc = part if acc is None else acc + part
                row += slab
        x1[i] = x_ref[0, i * tb:(i + 1) * tb, :] + gt1 * acc
        hb[i] = (_rms(x1[i]) * (g2_ref[...] * (1.0 + sc2)) + sh2).astype(BF16)

    def mlp_up(i):
        a = jnp.maximum(_dot(hb[i], w1_ref[...]), 0.0)
        act[i] = (a * a).astype(BF16)

    def mlp_down(i):
        x2 = x1[i] + gt2 * _dot(act[i], w2_ref[...])
        if final:
            x2 = _rms(x2) * fg_ref[...]
        o_ref[0, i * tb:(i + 1) * tb, :] = x2

    merge_and_project(0)
    for i in range(nsub):
        mlp_up(i)
        if i + 1 < nsub:
            merge_and_project(i + 1)
        mlp_down(i)


def _post(stream, part, mod, layer, final, oaf, oab, obf, obb, gg, dg, params, fg, nh_a, nh_b):
    na, nb_, wo, g2, w1, w2 = params
    ctx_arr, lat_arr = stream
    b, t, d = lat_arr.shape
    tb = TOKEN_BLOCK
    ctx = part == "ctx"
    nsub = 1 if ctx else math.gcd(t // tb, POST_SUB)
    nstep = 1 if ctx else t // (nsub * tb)
    first = 0 if ctx else 1
    lay = lambda a: pl.BlockSpec((None,) + a.shape[1:], lambda i, j: (layer,) + (0,) * (a.ndim - 1))
    weight = lambda a: pl.BlockSpec((None,) + a.shape[1:], lambda i, j: (layer,) + (0,) * (a.ndim - 1),
                                    pipeline_mode=pl.Buffered(1))
    bsz = b
    mod_spec = pl.BlockSpec((1, 1, 6 * d), lambda i, j: (layer * MOD_ROWS + (bsz if ctx else i), 0, 0))
    va, vb = oaf.shape[2], obf.shape[2]
    mix_specs, mix_args = [], []
    for k in range(nsub):
        blk = lambda n, k=k: pl.BlockSpec((1, tb, n), lambda i, j: (i, first + j * nsub + k, 0))
        mix_specs += [blk(va), blk(va), blk(vb), blk(vb), blk(va), blk(vb)]
        mix_args += [oaf, oab, obf, obb, gg, dg]
    return pl.pallas_call(
        functools.partial(_post_kernel, d=d, nh_a=nh_a, nh_b=nh_b, final=final, nsub=nsub),
        out_shape=jax.ShapeDtypeStruct((b, nstep * nsub * tb, d), F32),
        grid=(b, nstep),
        in_specs=[pl.BlockSpec((1, nsub * tb, d), lambda i, j: (i, j, 0)), mod_spec, *mix_specs,
                  lay(na), lay(nb_), weight(wo), lay(g2), weight(w1), weight(w2),
                  pl.BlockSpec((1, d), lambda i, j: (0, 0))],
        out_specs=pl.BlockSpec((1, nsub * tb, d), lambda i, j: (i, j, 0)),
        compiler_params=pltpu.CompilerParams(dimension_semantics=("parallel", "parallel"),
                                             vmem_limit_bytes=VMEM_LIMIT),
        name="post_ctx" if ctx else "post",
    )(ctx_arr if ctx else lat_arr, mod, *mix_args, na, nb_, wo, g2, w1, w2, fg.reshape(1, d))


def kernel(x, c, ctx, c_ctx, w_ada, b_ada, norm1_g, norm2_g, w_in, gla_w_lr, gla_b_lr, gdn_conv_w,
           gdn_a_log, gdn_dt_bias, gla_norm_g, gdn_norm_g, w_out, w_ff1, w_ff2, final_norm_g):
    bsz, t, d = x.shape
    depth = w_ada.shape[0]
    assert ctx.shape[1] == TOKEN_BLOCK and t % TOKEN_BLOCK == 0 and TOKEN_BLOCK % GRID_W == 0
    assert bsz + 1 <= MOD_ROWS
    qk_a = gla_w_lr.shape[3]
    rank = gla_w_lr.shape[2]
    v_a = gla_norm_g.shape[1] * GLA_HEADS
    conv_dim = gdn_conv_w.shape[2]
    v_b = gdn_norm_g.shape[1] * GDN_HEADS
    qk_b = (conv_dim - v_b) // 2
    nd = 2 * GDN_HEADS
    dims = (qk_a, v_a, rank, conv_dim, qk_b, v_b, GDN_HEADS)

    o_r = 2 * qk_a + 2 * v_a
    o_c = o_r + rank
    o_g = o_c + conv_dim
    o_s = o_g + v_b
    assert w_in.shape[2] == o_s + 2 * nd
    w_main = jnp.concatenate([w_in[:, :, :o_r], w_in[:, :, o_c:o_s]], axis=2).astype(BF16)
    w_sm = jnp.concatenate([w_in[:, :, o_r:o_c], w_in[:, :, o_s:]], axis=2)
    w_small = jnp.pad(w_sm, ((0, 0), (0, 0), (0, LANES - w_sm.shape[2]))).astype(BF16)
    w_small_t = jnp.swapaxes(w_in[:, :, o_s:], 1, 2).astype(BF16)

    cc = jnp.concatenate([c, c_ctx[None, :], jnp.zeros((MOD_ROWS - bsz - 1, d), F32)], axis=0)
    mod = _modulation(cc, w_ada, b_ada).reshape(depth * MOD_ROWS, 1, 6 * d)

    in_params = (norm1_g.reshape(depth, 1, d), w_main, w_small, w_small_t, gla_w_lr.astype(BF16),
                 gla_b_lr.reshape(depth, 2, 1, qk_a), gdn_conv_w,
                 gdn_a_log.reshape(depth, 1, nd), gdn_dt_bias.reshape(depth, 1, nd),
                 gdn_a_log.reshape(depth, nd, 1), gdn_dt_bias.reshape(depth, nd, 1))
    post_params = (gla_norm_g.reshape(depth, 1, -1), gdn_norm_g.reshape(depth, 1, -1), w_out.astype(BF16),
                   norm2_g.reshape(depth, 1, d), w_ff1.astype(BF16), w_ff2.astype(BF16))

    stream = (ctx, x)
    for l in range(depth):
        last = l == depth - 1
        gqkv, gg, gla, dqkv, dg, small_c, small_r = _in_proj(stream, mod, l, bsz, in_params, dims)
        oaf, oab, obf, obb = _scan(gqkv, gla, dqkv, small_c, small_r, dims)
        args = (mod, l, last, oaf, oab, obf, obb, gg, dg, post_params, final_norm_g, GLA_HEADS, GDN_HEADS)
        xs = _post(stream, "latent", *args)
        if not last:
            stream = (_post(stream, "ctx", *args), xs)
    return xs
```

```python
import functools
import math

import jax
import jax.numpy as jnp
from jax import lax
from jax.experimental import pallas as pl
from jax.experimental.pallas import tpu as pltpu

EPS = 1e-6
GRID_W = 64
GLA_HEADS = 4
GDN_HEADS = 4
GLA_TAU = 16.0
TOKEN_BLOCK = 256
CHUNK = 64
LANES = 128
SUBLANES = 8
POST_SUB = 2
SCAN_ROWS = 2
IN_ROWS = 4
SCAN_ROW_LAG = 6
MOD_ROWS = 16
VMEM_LIMIT = 56 * 1024 * 1024
SCAN_ORDER = (("b", 0), ("b", 1), ("a", 0), ("a", 1), ("a", 2), ("b", 2), ("a", 3), ("b", 3), ("b", 4),
              ("b", 5), ("b", 6), ("a", 4), ("b", 7), ("b", 8), ("b", 9), ("b", 10), ("b", 11), ("b", 12))

F32 = jnp.float32
BF16 = jnp.bfloat16


def _dot(a, b):
    return jnp.dot(a, b, preferred_element_type=F32)


def _dot_nt(a, b):
    return lax.dot_general(a, b, (((1,), (1,)), ((), ())), preferred_element_type=F32)


def _dot_tn(a, b):
    return lax.dot_general(a, b, (((0,), (0,)), ((), ())), preferred_element_type=F32)


def _silu(x):
    return x / (1.0 + jnp.exp(-x))


def _sigmoid(x):
    return 1.0 / (1.0 + jnp.exp(-x))


def _softplus(x):
    return jnp.maximum(x, 0.0) + jnp.log(1.0 + jnp.exp(-jnp.abs(x)))


def _log_sigmoid(x):
    return jnp.minimum(x, 0.0) - jnp.log(1.0 + jnp.exp(-jnp.abs(x)))


def _rms(x):
    return x * lax.rsqrt(jnp.mean(x * x, axis=-1, keepdims=True) + EPS)


def _cumsum_dot(tri_bf16, x, left):
    hi = x.astype(BF16)
    lo = (x - hi.astype(F32)).astype(BF16)
    if left:
        return _dot(tri_bf16, hi) + _dot(tri_bf16, lo)
    return _dot(hi, tri_bf16) + _dot(lo, tri_bf16)


def _tri(n, upper, strict):
    r = lax.broadcasted_iota(jnp.int32, (n, n), 0)
    c = lax.broadcasted_iota(jnp.int32, (n, n), 1)
    if upper:
        m = (r < c) if strict else (r <= c)
    else:
        m = (r > c) if strict else (r >= c)
    return m


def _couple(n, s, upper):
    r = lax.broadcasted_iota(jnp.int32, (n, n), 0)
    c = lax.broadcasted_iota(jnp.int32, (n, n), 1)
    same = (r & ~(2 * s - 1)) == (c & ~(2 * s - 1))
    r_hi, c_hi = (r & s) != 0, (c & s) != 0
    return same & ((~r_hi & c_hi) if upper else (r_hi & ~c_hi))


def _tri_blocks(n, upper):
    r = lax.broadcasted_iota(jnp.int32, (n, n), 0)
    c = lax.broadcasted_iota(jnp.int32, (n, n), 1)
    same = (r & ~(CHUNK - 1)) == (c & ~(CHUNK - 1))
    return same & ((r <= c) if upper else (r >= c))


def _mod_kernel(cc_ref, w_ref, b_ref, o_ref):
    s = _silu(cc_ref[...]).astype(BF16)
    o_ref[0] = _dot(s, w_ref[0].astype(BF16)) + b_ref[0]


def _modulation(cc, w_ada, b_ada):
    depth, d, d6 = w_ada.shape
    nblk = d6 // d
    return pl.pallas_call(
        _mod_kernel,
        out_shape=jax.ShapeDtypeStruct((depth, MOD_ROWS, d6), F32),
        grid=(depth, nblk),
        in_specs=[
            pl.BlockSpec((MOD_ROWS, d), lambda l, n: (0, 0)),
            pl.BlockSpec((1, d, d), lambda l, n: (l, 0, n)),
            pl.BlockSpec((1, 1, d), lambda l, n: (l, 0, n)),
        ],
        out_specs=pl.BlockSpec((1, MOD_ROWS, d), lambda l, n: (l, 0, n)),
        compiler_params=pltpu.CompilerParams(dimension_semantics=("parallel", "parallel")),
        name="adaln_modulation",
    )(cc, w_ada, b_ada.reshape(depth, 1, d6))


def _in_proj_kernel(c_ref, x_ref, *refs, d, qk_a, v_a, rank, conv_dim, qk_b, v_b, nh_b):
    nr = x_ref.shape[0]
    mod_refs = refs[:nr]
    (g1_ref, wm_ref, ws_ref, wst_ref, wlr_ref, blr_ref, cw_ref, arow_ref, dtrow_ref, acol_ref, dtcol_ref,
     gqkv_ref, gg_ref, gla_ref, dqkv_ref, dg_ref, sc_ref, sr_ref) = refs[nr:]
    for bi in range(nr):
        _in_proj_row(bi, c_ref, x_ref, mod_refs[bi], g1_ref, wm_ref, ws_ref, wst_ref, wlr_ref, blr_ref, cw_ref,
                     arow_ref, dtrow_ref, acol_ref, dtcol_ref, gqkv_ref, gg_ref, gla_ref, dqkv_ref, dg_ref,
                     sc_ref, sr_ref, d=d, qk_a=qk_a, v_a=v_a, rank=rank, conv_dim=conv_dim, qk_b=qk_b, v_b=v_b,
                     nh_b=nh_b)


def _in_proj_row(bi, c_ref, x_ref, mod_ref, g1_ref, wm_ref, ws_ref, wst_ref, wlr_ref, blr_ref, cw_ref,
                 arow_ref, dtrow_ref, acol_ref, dtcol_ref,
                 gqkv_ref, gg_ref, gla_ref, dqkv_ref, dg_ref, sc_ref, sr_ref, *, d, qk_a, v_a, rank, conv_dim,
                 qk_b, v_b, nh_b):
    j = pl.program_id(1)
    x = jnp.where(j == 0, c_ref[bi], x_ref[bi])
    m = mod_ref[0]
    sh1, sc1 = m[:, 0:d], m[:, d:2 * d]
    h = _rms(x) * (g1_ref[...] * (1.0 + sc1)) + sh1
    hb = h.astype(BF16)
    o_conv = 2 * qk_a + 2 * v_a
    o_gate = o_conv + conv_dim
    grp = 2 * LANES
    tb = x.shape[0]
    t = lax.broadcasted_iota(jnp.int32, (tb, 1), 0)
    seg_mask = jnp.where(j == 0, tb - 1, GRID_W - 1)
    first = (t & seg_mask) == 0
    last = (t & seg_mask) == seg_mask
    dkh = qk_b // nh_b

    def conv_dot(k):
        return _dot(hb, wm_ref[:, o_conv + k * grp:o_conv + (k + 1) * grp])

    def zero_rows(a, mask, at):
        pieces = []
        for s0 in range(0, tb, GRID_W):
            r0 = s0 + at
            pieces += [a[s0:r0], jnp.where(mask[r0:r0 + SUBLANES], 0.0, a[r0:r0 + SUBLANES]),
                       a[r0 + SUBLANES:s0 + GRID_W]]
        return jnp.concatenate([p for p in pieces if p.shape[0]], axis=0)

    def conv_group(u, k):
        up = zero_rows(pltpu.roll(u, 1, 0), first, 0)
        un = zero_rows(pltpu.roll(u, tb - 1, 0), last, GRID_W - SUBLANES)
        cw = cw_ref[:, k * grp:(k + 1) * grp]
        s = _silu(cw[0:1] * up + cw[1:2] * u + cw[2:3] * un)
        c0 = k * grp
        if c0 < 2 * qk_b:
            for hh in range(grp // dkh):
                sh = s[:, hh * dkh:(hh + 1) * dkh]
                inv = lax.rsqrt(jnp.sum(sh * sh, axis=-1, keepdims=True) + EPS)
                if c0 < qk_b:
                    inv = inv * (float(dkh) ** -0.5)
                dqkv_ref[bi, :, c0 + hh * dkh:c0 + (hh + 1) * dkh] = (sh * inv).astype(BF16)
        else:
            dqkv_ref[bi, :, c0:c0 + grp] = s.astype(BF16)

    def light_group(k):
        c0 = k * grp
        w0 = c0 if c0 < o_conv else o_gate + (c0 - o_conv)
        pa = _dot(hb, wm_ref[:, w0:w0 + grp])
        if c0 < qk_a:
            gqkv_ref[bi, :, c0:c0 + grp] = (pa * (float(qk_a // GLA_HEADS) ** -0.5)).astype(BF16)
        elif c0 < 2 * qk_a + v_a:
            gqkv_ref[bi, :, c0:c0 + grp] = pa.astype(BF16)
        elif c0 < o_conv:
            gg_ref[bi, :, c0 - 2 * qk_a - v_a:c0 - 2 * qk_a - v_a + grp] = pa.astype(BF16)
        else:
            dg_ref[bi, :, c0 - o_conv:c0 - o_conv + grp] = pa.astype(BF16)

    def small_group(ps, pst):
        r_a = ps[:, 0:rank].astype(BF16)
        for dr in range(2):
            lr = _dot(r_a, wlr_ref[dr]) + blr_ref[dr]
            gla_ref[bi, :, dr * qk_a:(dr + 1) * qk_a] = _log_sigmoid(lr) * (1.0 / GLA_TAU)
        nd = 2 * nh_b
        a_c = ps[:, rank:rank + nd]
        b_c = ps[:, rank + nd:rank + 2 * nd]
        sc_ref[bi, :, 0:nd] = -jnp.exp(arow_ref[...]) * _softplus(a_c + dtrow_ref[...])
        sc_ref[bi, :, nd:2 * nd] = _sigmoid(b_c)
        sr_ref[bi, 0, 0:nd, :] = -jnp.exp(acol_ref[...]) * _softplus(pst[0:nd] + dtcol_ref[...])
        sr_ref[bi, 0, nd:2 * nd, :] = _sigmoid(pst[nd:2 * nd])

    n_conv = conv_dim // grp
    n_light = (o_conv + v_b) // grp
    us = {0: conv_dot(0)}
    ps = _dot(hb, ws_ref[...])
    pst = _dot_nt(wst_ref[...], hb)
    if n_conv > 1:
        us[1] = conv_dot(1)
    g_next = 0
    for k in range(n_conv):
        if k + 2 < n_conv:
            us[k + 2] = conv_dot(k + 2)
        conv_group(us.pop(k), k)
        if k == 1 or n_conv == 1:
            small_group(ps, pst)
        todo = (n_light - g_next + (n_conv - k) - 1) // (n_conv - k) if k >= n_conv // 2 else 1
        for _ in range(min(todo, n_light - g_next)):
            light_group(g_next)
            g_next += 1
    while g_next < n_light:
        light_group(g_next)
        g_next += 1


def _in_proj(stream, mod, layer, bsz, params, dims):
    g1, wm, ws, wst, wlr, blr, cw, a_row, dt_row, a_col, dt_col = params
    head, body = stream
    b, _, d = body.shape
    nb = body.shape[1] // TOKEN_BLOCK + 1
    s = nb * TOKEN_BLOCK
    qk_a, v_a, rank, conv_dim, qk_b, v_b, nh_b = dims
    nd = 2 * nh_b
    tb = TOKEN_BLOCK
    lay = lambda a: pl.BlockSpec((None,) + a.shape[1:], lambda i, j: (layer,) + (0,) * (a.ndim - 1))
    nr = math.gcd(b, IN_ROWS)
    tok = lambda n: pl.BlockSpec((nr, tb, n), lambda i, j: (i, j, 0))
    mod_specs = [pl.BlockSpec((1, 1, 6 * d), lambda i, j, k=k: (
        layer * MOD_ROWS + jnp.where(j == 0, bsz, i * nr + k), 0, 0)) for k in range(nr)]
    out_shape = [
        jax.ShapeDtypeStruct((b, s, 2 * qk_a + v_a), BF16), jax.ShapeDtypeStruct((b, s, v_a), BF16),
        jax.ShapeDtypeStruct((b, s, 2 * qk_a), F32),
        jax.ShapeDtypeStruct((b, s, conv_dim), BF16), jax.ShapeDtypeStruct((b, s, v_b), BF16),
        jax.ShapeDtypeStruct((b, s, 2 * nd), F32),
        jax.ShapeDtypeStruct((b, nb, 2 * nd, tb), F32),
    ]
    out_specs = [tok(2 * qk_a + v_a), tok(v_a), tok(2 * qk_a), tok(conv_dim), tok(v_b), tok(2 * nd),
                 pl.BlockSpec((nr, 1, 2 * nd, tb), lambda i, j: (i, j, 0, 0))]
    kern = functools.partial(_in_proj_kernel, d=d, qk_a=qk_a, v_a=v_a, rank=rank, conv_dim=conv_dim,
                             qk_b=qk_b, v_b=v_b, nh_b=nh_b)
    return pl.pallas_call(
        kern, out_shape=out_shape, grid=(b // nr, nb),
        in_specs=[
            pl.BlockSpec((nr, tb, d), lambda i, j: (i, 0, 0)),
            pl.BlockSpec((nr, tb, d), lambda i, j: (i, jnp.maximum(j - 1, 0), 0)), *mod_specs, lay(g1),
            lay(wm), lay(ws), lay(wst), lay(wlr), lay(blr), lay(cw), lay(a_row), lay(dt_row), lay(a_col), lay(dt_col),
        ],
        out_specs=out_specs,
        compiler_params=pltpu.CompilerParams(dimension_semantics=("parallel", "parallel"),
                                             vmem_limit_bytes=VMEM_LIMIT),
        name="in_proj",
    )(head, body, *([mod] * nr), g1, wm, ws, wst, wlr, blr, cw, a_row, dt_row, a_col, dt_col)


def _bwd_block(j, nb):
    return jnp.where(j == 0, 0, nb - j)


class _Cols:
    def __init__(self, ref, off):
        self.ref, self.off = ref, off

    def __getitem__(self, idx):
        z, rows, cols = idx
        return self.ref[z, rows, self.off + cols.start:self.off + cols.stop]


def _gla_stages(refs, st_ref, dk, dv, bi):
    tb = refs[0][3].shape[1]
    nchunk = tb // CHUNK
    npair = (GLA_HEADS * dk) // LANES
    hpp = LANES // dk
    lane = lax.broadcasted_iota(jnp.int32, (1, LANES), 1)
    lms = [(lane >= hh * dk) & (lane < (hh + 1) * dk) for hh in range(hpp)]
    mid = CHUNK // 2
    chains = [(dr, pr) for dr in range(2) for pr in range(npair)]
    units = [(dr, pr, c) for (dr, pr) in chains for c in range(nchunk)]
    bcum, q_dec, k_hat, q_mid, k_mid, a_last, sc, intra, dst = ({} for _ in range(9))

    def cumulative_decay():
        for dr in range(2):
            both = _cumsum_dot(_tri_blocks(tb, dr == 1).astype(BF16), refs[dr][3][bi], True)
            for pr in range(npair):
                bcum[dr, pr] = both[:, pr * LANES:(pr + 1) * LANES]

    def decayed_operands():
        for u in units:
            dr, pr, c = u
            rows = slice(c * CHUNK, (c + 1) * CHUNK)
            lanes = slice(pr * LANES, (pr + 1) * LANES)
            last = c * CHUNK + (0 if dr == 1 else CHUNK - 1)
            b = bcum[dr, pr][rows]
            b_last = bcum[dr, pr][last:last + 1]
            b_mid = bcum[dr, pr][c * CHUNK + mid:c * CHUNK + mid + 1]
            qc = refs[dr][0][bi, rows, lanes].astype(F32)
            kc = refs[dr][1][bi, rows, lanes].astype(F32)
            q_dec[u] = qc * jnp.exp(b)
            k_hat[u] = (kc * jnp.exp(b_last - b)).astype(BF16)
            q_mid[u] = qc * jnp.exp(b - b_mid)
            k_mid[u] = (kc * jnp.exp(b_mid - b)).astype(BF16)
            a_last[u] = jnp.exp(b_last)

    def per_head(m):
        return jnp.concatenate([jnp.where(lm, m, 0.0).astype(BF16) for lm in lms], axis=0)

    def scores():
        for u in units:
            sc[u] = _dot_nt(per_head(q_mid[u]), k_mid[u])

    def intra_and_increments():
        for u in units:
            dr, pr, c = u
            rows = slice(c * CHUNK, (c + 1) * CHUNK)
            causal = _tri(CHUNK, dr == 1, False)
            v_grp = refs[dr][2][bi, rows, pr * hpp * dv:(pr + 1) * hpp * dv]
            dh = _dot_tn(v_grp, k_hat[u])
            acc = None
            for hh in range(hpp):
                pm = jnp.where(causal, sc[u][hh * CHUNK:(hh + 1) * CHUNK], 0.0).astype(BF16)
                intra[u, hh] = _dot(pm, v_grp[:, hh * dv:(hh + 1) * dv])
                part = dh[hh * dv:(hh + 1) * dv]
                acc = part if acc is None else jnp.where(lms[hh], part, acc)
            dst[u] = acc

    def recurrence():
        st = {ch: st_ref[bi, ch[0], ch[1]] for ch in chains}
        for ci in range(nchunk):
            for ch in chains:
                dr, pr = ch
                c = nchunk - 1 - ci if dr == 1 else ci
                u = (dr, pr, c)
                rows = slice(c * CHUNK, (c + 1) * CHUNK)
                inter = _dot_nt(per_head(q_dec[u]), st[ch].astype(BF16))
                for hh in range(hpp):
                    head = pr * hpp + hh
                    refs[dr][4][bi, rows, head * dv:(head + 1) * dv] = (
                        intra[u, hh] + inter[hh * CHUNK:(hh + 1) * CHUNK]).astype(BF16)
                st[ch] = st[ch] * a_last[u] + dst[u]
        for ch in chains:
            st_ref[bi, ch[0], ch[1]] = st[ch]

    return [cumulative_decay, decayed_operands, scores, intra_and_increments, recurrence]


def _gdn_stages(refs, st_ref, dk, dv, nh, bi):
    tb = refs[0][3].shape[1]
    nchunk = tb // CHUNK
    eye = _tri(CHUNK, False, False) & _tri(CHUNK, True, False)
    chains = [(dr, hd) for dr in range(2) for hd in range(nh)]
    units = [(dr, c, hd) for dr in range(2) for c in range(nchunk) for hd in range(nh)]
    g_cols, g_rows, small, kk, qk, x, aqk, rhs, q_dec, k_dec, a_last = ({} for _ in range(11))
    t, uw, kd_uw, aq_uw, lhs, decay = ({} for _ in range(6))

    def cumulative_decay():
        for dr in range(2):
            small[dr] = refs[dr][3][bi]
            g_cols[dr] = _cumsum_dot(_tri_blocks(tb, dr == 1).astype(BF16), small[dr], True)
            g_rows[dr] = _cumsum_dot(_tri_blocks(tb, dr != 1).astype(BF16), refs[dr][4][bi, 0], False)

    def grams():
        for u in units:
            dr, c, hd = u
            rows = slice(c * CHUNK, (c + 1) * CHUNK)
            kh = refs[dr][1][bi, rows, hd * dk:(hd + 1) * dk]
            qh = refs[dr][0][bi, rows, hd * dk:(hd + 1) * dk]
            kq = _dot_nt(jnp.concatenate([kh, qh], axis=0), kh)
            kk[u], qk[u] = kq[:CHUNK], kq[CHUNK:]

    def solve_operands():
        for u in units:
            dr, c, hd = u
            rows = slice(c * CHUNK, (c + 1) * CHUNK)
            ia = dr * nh + hd
            ib = 2 * nh + dr * nh + hd
            g_c = g_cols[dr][rows, ia:ia + 1]
            g_r = g_rows[dr][ia:ia + 1, rows]
            be_c = small[dr][rows, ib:ib + 1]
            decay[u] = jnp.exp(jnp.minimum(g_c - g_r, 0.0))
            x[u] = jnp.where(_tri(CHUNK, dr == 1, True), be_c * kk[u] * decay[u], 0.0)

    def other_operands():
        for u in units:
            dr, c, hd = u
            rows = slice(c * CHUNK, (c + 1) * CHUNK)
            ia = dr * nh + hd
            ib = 2 * nh + dr * nh + hd
            last = c * CHUNK + (0 if dr == 1 else CHUNK - 1)
            g_c = g_cols[dr][rows, ia:ia + 1]
            be_c = small[dr][rows, ib:ib + 1]
            g_last = g_cols[dr][last:last + 1, ia:ia + 1]
            aqk[u] = jnp.where(_tri(CHUNK, dr == 1, False), qk[u] * decay[u], 0.0).astype(BF16)
            eg = jnp.exp(g_c)
            qh = refs[dr][0][bi, rows, hd * dk:(hd + 1) * dk].astype(F32)
            khf = refs[dr][1][bi, rows, hd * dk:(hd + 1) * dk].astype(F32)
            vh = refs[dr][2][bi, rows, hd * dv:(hd + 1) * dv].astype(F32)
            rhs[u] = jnp.concatenate([(be_c * vh).astype(BF16), ((be_c * eg) * khf).astype(BF16)], axis=1)
            q_dec[u] = qh * eg
            k_dec[u] = khf * jnp.exp(g_last - g_c)
            a_last[u] = jnp.exp(g_last)

    def inverse_start():
        for u in units:
            t[u] = jnp.where(eye, 1.0, jnp.where(_couple(CHUNK, 1, u[0] == 1), -x[u], 0.0))

    def inverse_level(s):
        a = {u: jnp.where(_couple(CHUNK, s, u[0] == 1), x[u], 0.0).astype(BF16) for u in units}
        tb16 = {u: t[u].astype(BF16) for u in units}
        ta = {u: _dot(tb16[u], a[u]).astype(BF16) for u in units}
        for u in units:
            t[u] = t[u] - _dot(ta[u], tb16[u])

    levels = []
    s = 2
    while s < CHUNK:
        levels.append(functools.partial(inverse_level, s))
        s *= 2

    def solve():
        for u in units:
            uw[u] = _dot(t[u].astype(BF16), rhs[u]).astype(BF16)

    def fold():
        for u in units:
            both = _dot(jnp.concatenate([k_dec[u].T.astype(BF16), aqk[u]], axis=0), uw[u])
            kd_uw[u], aq_uw[u] = both[:dk], both[dk:]
        for u in units:
            lhs[u] = jnp.concatenate([(-kd_uw[u][:, dv:]).astype(BF16),
                                      (q_dec[u] - aq_uw[u][:, dv:]).astype(BF16)], axis=0)

    def recurrence():
        st = {ch: st_ref[bi, ch[0], ch[1]] for ch in chains}
        for ci in range(nchunk):
            for ch in chains:
                dr, hd = ch
                c = nchunk - 1 - ci if dr == 1 else ci
                u = (dr, c, hd)
                rows = slice(c * CHUNK, (c + 1) * CHUNK)
                res = _dot(lhs[u], st[ch].astype(BF16))
                refs[dr][5][bi, rows, hd * dv:(hd + 1) * dv] = (res[dk:] + aq_uw[u][:, :dv]).astype(BF16)
                st[ch] = st[ch] * a_last[u] + res[:dk] + kd_uw[u][:, :dv]
        for ch in chains:
            st_ref[bi, ch[0], ch[1]] = st[ch]

    return [cumulative_decay, grams, solve_operands, other_operands, inverse_start, *levels,
            solve, fold, recurrence]


def _scan_kernel(af, alf, ab, alb, bf, bcf, brf, bb, bcb, brb,
                 oaf, oab, obf, obb, sta_ref, stb_ref, *, dk_a, dv_a, dk_b, dv_b, nh_b):
    @pl.when(pl.program_id(1) == 0)
    def _():
        sta_ref[...] = jnp.zeros_like(sta_ref)
        stb_ref[...] = jnp.zeros_like(stb_ref)

    qk_a, qk_b = GLA_HEADS * dk_a, nh_b * dk_b
    qkv = lambda ref, qk: (_Cols(ref, 0), _Cols(ref, qk), _Cols(ref, 2 * qk))
    refs_a = ((*qkv(af, qk_a), alf, oaf), (*qkv(ab, qk_a), alb, oab))
    refs_b = ((*qkv(bf, qk_b), bcf, brf, obf), (*qkv(bb, qk_b), bcb, brb, obb))
    stages = {}
    for bi in range(af.shape[0]):
        stages["a", bi] = _gla_stages(refs_a, sta_ref, dk_a, dv_a, bi)
        stages["b", bi] = _gdn_stages(refs_b, stb_ref, dk_b, dv_b, nh_b, bi)
    lag = SCAN_ROW_LAG
    for pos in range(len(SCAN_ORDER) + lag * (af.shape[0] - 1)):
        for bi in range(af.shape[0]):
            if 0 <= pos - lag * bi < len(SCAN_ORDER):
                name, idx = SCAN_ORDER[pos - lag * bi]
                stages[name, bi][idx]()


def _scan(gqkv, gla, dqkv, small_c, small_r, dims):
    b, s, _ = gqkv.shape
    qk_a, v_a, _, _, qk_b, v_b, _ = dims
    nb = s // TOKEN_BLOCK
    tb = TOKEN_BLOCK
    ns = small_c.shape[2]
    nr = math.gcd(b, SCAN_ROWS)
    fwd = lambda n, col=0: pl.BlockSpec((nr, tb, n), lambda i, j: (i, j, col))
    bwd = lambda n, col=0: pl.BlockSpec((nr, tb, n), lambda i, j: (i, _bwd_block(j, nb), col))
    rfwd = pl.BlockSpec((nr, 1, ns, tb), lambda i, j: (i, j, 0, 0))
    rbwd = pl.BlockSpec((nr, 1, ns, tb), lambda i, j: (i, _bwd_block(j, nb), 0, 0))
    dk_a, dv_a = qk_a // GLA_HEADS, v_a // GLA_HEADS
    dk_b, dv_b = qk_b // GDN_HEADS, v_b // GDN_HEADS
    out = lambda v: jax.ShapeDtypeStruct((b, s, v), BF16)
    return pl.pallas_call(
        functools.partial(_scan_kernel, dk_a=dk_a, dv_a=dv_a, dk_b=dk_b, dv_b=dv_b, nh_b=GDN_HEADS),
        out_shape=[out(v_a), out(v_a), out(v_b), out(v_b)],
        grid=(b // nr, nb),
        in_specs=[fwd(2 * qk_a + v_a), fwd(qk_a, 0), bwd(2 * qk_a + v_a), bwd(qk_a, 1),
                  fwd(2 * qk_b + v_b), fwd(ns), rfwd, bwd(2 * qk_b + v_b), bwd(ns), rbwd],
        out_specs=[fwd(v_a), bwd(v_a), fwd(v_b), bwd(v_b)],
        scratch_shapes=[pltpu.VMEM((nr, 2, qk_a // LANES, dv_a, LANES), F32),
                        pltpu.VMEM((nr, 2, GDN_HEADS, dk_b, dv_b), F32)],
        compiler_params=pltpu.CompilerParams(dimension_semantics=("parallel", "arbitrary"),
                                             vmem_limit_bytes=VMEM_LIMIT),
        name="scan",
    )(gqkv, gla, gqkv, gla, dqkv, small_c, small_r, dqkv, small_c, small_r)


def _post_kernel(x_ref, mod_ref, *refs, d, nh_a, nh_b, final, nsub):
    mix = [refs[6 * i:6 * i + 6] for i in range(nsub)]
    na_ref, nb_ref, wo_ref, g2_ref, w1_ref, w2_ref, fg_ref, o_ref = refs[6 * nsub:]
    tb = x_ref.shape[1] // nsub
    m = mod_ref[0]
    gt1 = m[:, 2 * d:3 * d]
    sh2, sc2, gt2 = m[:, 3 * d:4 * d], m[:, 4 * d:5 * d], m[:, 5 * d:6 * d]
    slab = 2 * LANES
    x1, hb, act = {}, {}, {}

    def merge_and_project(i):
        af_ref, ab_ref, bf_ref, bb_ref, ga_ref, gb_ref = mix[i]
        acc = None
        row = 0
        for (f_ref, b_ref, g_ref, n_ref, nh) in ((af_ref, ab_ref, ga_ref, na_ref, nh_a),
                                                 (bf_ref, bb_ref, gb_ref, nb_ref, nh_b)):
            width = f_ref.shape[2]
            hv = width // nh
            for c0 in range(0, width, slab):
                o = f_ref[0, :, c0:c0 + slab].astype(F32) + b_ref[0, :, c0:c0 + slab].astype(F32)
                gate = _silu(g_ref[0, :, c0:c0 + slab].astype(F32))
                parts = [(_rms(o[:, k:k + hv]) * n_ref[...] * gate[:, k:k + hv]).astype(BF16)
                         for k in range(0, slab, hv)]
                part = _dot(jnp.concatenate(parts, axis=1), wo_ref[row:row + slab, :])
                acc = part if acc is None else acc + part
                row += slab
        x1[i] = x_ref[0, i * tb:(i + 1) * tb, :] + gt1 * acc
        hb[i] = (_rms(x1[i]) * (g2_ref[...] * (1.0 + sc2)) + sh2).astype(BF16)

    def mlp_up(i):
        a = jnp.maximum(_dot(hb[i], w1_ref[...]), 0.0)
        act[i] = (a * a).astype(BF16)

    def mlp_down(i):
        x2 = x1[i] + gt2 * _dot(act[i], w2_ref[...])
        if final:
            x2 = _rms(x2) * fg_ref[...]
        o_ref[0, i * tb:(i + 1) * tb, :] = x2

    merge_and_project(0)
    for i in range(nsub):
        mlp_up(i)
        if i + 1 < nsub:
            merge_and_project(i + 1)
        mlp_down(i)


def _post(stream, part, mod, layer, final, oaf, oab, obf, obb, gg, dg, params, fg, nh_a, nh_b):
    na, nb_, wo, g2, w1, w2 = params
    ctx_arr, lat_arr = stream
    b, t, d = lat_arr.shape
    tb = TOKEN_BLOCK
    ctx = part == "ctx"
    nsub = 1 if ctx else math.gcd(t // tb, POST_SUB)
    nstep = 1 if ctx else t // (nsub * tb)
    first = 0 if ctx else 1
    lay = lambda a: pl.BlockSpec((None,) + a.shape[1:], lambda i, j: (layer,) + (0,) * (a.ndim - 1))
    weight = lambda a: pl.BlockSpec((None,) + a.shape[1:], lambda i, j: (layer,) + (0,) * (a.ndim - 1),
                                    pipeline_mode=pl.Buffered(1))
    bsz = b
    mod_spec = pl.BlockSpec((1, 1, 6 * d), lambda i, j: (layer * MOD_ROWS + (bsz if ctx else i), 0, 0))
    va, vb = oaf.shape[2], obf.shape[2]
    mix_specs, mix_args = [], []
    for k in range(nsub):
        blk = lambda n, k=k: pl.BlockSpec((1, tb, n), lambda i, j: (i, first + j * nsub + k, 0))
        mix_specs += [blk(va), blk(va), blk(vb), blk(vb), blk(va), blk(vb)]
        mix_args += [oaf, oab, obf, obb, gg, dg]
    return pl.pallas_call(
        functools.partial(_post_kernel, d=d, nh_a=nh_a, nh_b=nh_b, final=final, nsub=nsub),
        out_shape=jax.ShapeDtypeStruct((b, nstep * nsub * tb, d), F32),
        grid=(b, nstep),
        in_specs=[pl.BlockSpec((1, nsub * tb, d), lambda i, j: (i, j, 0)), mod_spec, *mix_specs,
                  lay(na), lay(nb_), weight(wo), lay(g2), weight(w1), weight(w2),
                  pl.BlockSpec((1, d), lambda i, j: (0, 0))],
        out_specs=pl.BlockSpec((1, nsub * tb, d), lambda i, j: (i, j, 0)),
        compiler_params=pltpu.CompilerParams(dimension_semantics=("parallel", "parallel"),
                                             vmem_limit_bytes=VMEM_LIMIT),
        name="post_ctx" if ctx else "post",
    )(ctx_arr if ctx else lat_arr, mod, *mix_args, na, nb_, wo, g2, w1, w2, fg.reshape(1, d))


def kernel(x, c, ctx, c_ctx, w_ada, b_ada, norm1_g, norm2_g, w_in, gla_w_lr, gla_b_lr, gdn_conv_w,
           gdn_a_log, gdn_dt_bias, gla_norm_g, gdn_norm_g, w_out, w_ff1, w_ff2, final_norm_g):
    bsz, t, d = x.shape
    depth = w_ada.shape[0]
    assert ctx.shape[1] == TOKEN_BLOCK and t % TOKEN_BLOCK == 0 and TOKEN_BLOCK % GRID_W == 0
    assert bsz + 1 <= MOD_ROWS
    qk_a = gla_w_lr.shape[3]
    rank = gla_w_lr.shape[2]
    v_a = gla_norm_g.shape[1] * GLA_HEADS
    conv_dim = gdn_conv_w.shape[2]
    v_b = gdn_norm_g.shape[1] * GDN_HEADS
    qk_b = (conv_dim - v_b) // 2
    nd = 2 * GDN_HEADS
    dims = (qk_a, v_a, rank, conv_dim, qk_b, v_b, GDN_HEADS)

    o_r = 2 * qk_a + 2 * v_a
    o_c = o_r + rank
    o_g = o_c + conv_dim
    o_s = o_g + v_b
    assert w_in.shape[2] == o_s + 2 * nd
    w_main = jnp.concatenate([w_in[:, :, :o_r], w_in[:, :, o_c:o_s]], axis=2).astype(BF16)
    w_sm = jnp.concatenate([w_in[:, :, o_r:o_c], w_in[:, :, o_s:]], axis=2)
    w_small = jnp.pad(w_sm, ((0, 0), (0, 0), (0, LANES - w_sm.shape[2]))).astype(BF16)
    w_small_t = jnp.swapaxes(w_in[:, :, o_s:], 1, 2).astype(BF16)

    cc = jnp.concatenate([c, c_ctx[None, :], jnp.zeros((MOD_ROWS - bsz - 1, d), F32)], axis=0)
    mod = _modulation(cc, w_ada, b_ada).reshape(depth * MOD_ROWS, 1, 6 * d)

    in_params = (norm1_g.reshape(depth, 1, d), w_main, w_small, w_small_t, gla_w_lr.astype(BF16),
                 gla_b_lr.reshape(depth, 2, 1, qk_a), gdn_conv_w,
                 gdn_a_log.reshape(depth, 1, nd), gdn_dt_bias.reshape(depth, 1, nd),
                 gdn_a_log.reshape(depth, nd, 1), gdn_dt_bias.reshape(depth, nd, 1))
    post_params = (gla_norm_g.reshape(depth, 1, -1), gdn_norm_g.reshape(depth, 1, -1), w_out.astype(BF16),
                   norm2_g.reshape(depth, 1, d), w_ff1.astype(BF16), w_ff2.astype(BF16))

    stream = (ctx, x)
    for l in range(depth):
        last = l == depth - 1
        gqkv, gg, gla, dqkv, dg, small_c, small_r = _in_proj(stream, mod, l, bsz, in_params, dims)
        oaf, oab, obf, obb = _scan(gqkv, gla, dqkv, small_c, small_r, dims)
        args = (mod, l, last, oaf, oab, obf, obb, gg, dg, post_params, final_norm_g, GLA_HEADS, GDN_HEADS)
        xs = _post(stream, "latent", *args)
        if not last:
            stream = (_post(stream, "ctx", *args), xs)
    return xs
```

```python
import functools
import math

import jax
import jax.numpy as jnp
from jax import lax
from jax.experimental import pallas as pl
from jax.experimental.pallas import tpu as pltpu

EPS = 1e-6
GRID_W = 64
GLA_HEADS = 4
GDN_HEADS = 4
GLA_TAU = 16.0
TOKEN_BLOCK = 256
CHUNK = 64
LANES = 128
SUBLANES = 8
POST_SUB = 2
SCAN_ROWS = 2
IN_ROWS = 4
SCAN_ROW_LAG = 7
MOD_ROWS = 16
VMEM_LIMIT = 56 * 1024 * 1024
SCAN_ORDER = (("b", 0), ("b", 1), ("a", 0), ("a", 1), ("a", 2), ("a", 3), ("b", 2), ("b", 3), ("b", 4),
              ("b", 5), ("b", 6), ("a", 4), ("b", 7), ("b", 8), ("b", 9), ("b", 10), ("b", 11), ("b", 12))

F32 = jnp.float32
BF16 = jnp.bfloat16


def _dot(a, b):
    return jnp.dot(a, b, preferred_element_type=F32)


def _dot_nt(a, b):
    return lax.dot_general(a, b, (((1,), (1,)), ((), ())), preferred_element_type=F32)


def _dot_tn(a, b):
    return lax.dot_general(a, b, (((0,), (0,)), ((), ())), preferred_element_type=F32)


def _silu(x):
    return x / (1.0 + jnp.exp(-x))


def _sigmoid(x):
    return 1.0 / (1.0 + jnp.exp(-x))


def _softplus(x):
    return jnp.maximum(x, 0.0) + jnp.log(1.0 + jnp.exp(-jnp.abs(x)))


def _log_sigmoid(x):
    return jnp.minimum(x, 0.0) - jnp.log(1.0 + jnp.exp(-jnp.abs(x)))


def _rms(x):
    return x * lax.rsqrt(jnp.mean(x * x, axis=-1, keepdims=True) + EPS)


def _cumsum_dot(tri_bf16, x, left):
    hi = x.astype(BF16)
    lo = (x - hi.astype(F32)).astype(BF16)
    if left:
        return _dot(tri_bf16, hi) + _dot(tri_bf16, lo)
    return _dot(hi, tri_bf16) + _dot(lo, tri_bf16)


def _tri(n, upper, strict):
    r = lax.broadcasted_iota(jnp.int32, (n, n), 0)
    c = lax.broadcasted_iota(jnp.int32, (n, n), 1)
    if upper:
        m = (r < c) if strict else (r <= c)
    else:
        m = (r > c) if strict else (r >= c)
    return m


def _couple(n, s, upper):
    r = lax.broadcasted_iota(jnp.int32, (n, n), 0)
    c = lax.broadcasted_iota(jnp.int32, (n, n), 1)
    same = (r & ~(2 * s - 1)) == (c & ~(2 * s - 1))
    r_hi, c_hi = (r & s) != 0, (c & s) != 0
    return same & ((~r_hi & c_hi) if upper else (r_hi & ~c_hi))


def _tri_blocks(n, upper):
    r = lax.broadcasted_iota(jnp.int32, (n, n), 0)
    c = lax.broadcasted_iota(jnp.int32, (n, n), 1)
    same = (r & ~(CHUNK - 1)) == (c & ~(CHUNK - 1))
    return same & ((r <= c) if upper else (r >= c))


def _mod_kernel(cc_ref, w_ref, b_ref, o_ref):
    s = _silu(cc_ref[...]).astype(BF16)
    o_ref[0] = _dot(s, w_ref[0].astype(BF16)) + b_ref[0]


def _modulation(cc, w_ada, b_ada):
    depth, d, d6 = w_ada.shape
    nblk = d6 // d
    return pl.pallas_call(
        _mod_kernel,
        out_shape=jax.ShapeDtypeStruct((depth, MOD_ROWS, d6), F32),
        grid=(depth, nblk),
        in_specs=[
            pl.BlockSpec((MOD_ROWS, d), lambda l, n: (0, 0)),
            pl.BlockSpec((1, d, d), lambda l, n: (l, 0, n)),
            pl.BlockSpec((1, 1, d), lambda l, n: (l, 0, n)),
        ],
        out_specs=pl.BlockSpec((1, MOD_ROWS, d), lambda l, n: (l, 0, n)),
        compiler_params=pltpu.CompilerParams(dimension_semantics=("parallel", "parallel")),
        name="adaln_modulation",
    )(cc, w_ada, b_ada.reshape(depth, 1, d6))


def _in_proj_kernel(c_ref, x_ref, *refs, d, qk_a, v_a, rank, conv_dim, qk_b, v_b, nh_b):
    nr = x_ref.shape[0]
    mod_refs = refs[:nr]
    (g1_ref, wm_ref, ws_ref, wst_ref, wlr_ref, blr_ref, cw_ref, arow_ref, dtrow_ref, acol_ref, dtcol_ref,
     gqkv_ref, gg_ref, gla_ref, dqkv_ref, dg_ref, sc_ref, sr_ref) = refs[nr:]
    for bi in range(nr):
        _in_proj_row(bi, c_ref, x_ref, mod_refs[bi], g1_ref, wm_ref, ws_ref, wst_ref, wlr_ref, blr_ref, cw_ref,
                     arow_ref, dtrow_ref, acol_ref, dtcol_ref, gqkv_ref, gg_ref, gla_ref, dqkv_ref, dg_ref,
                     sc_ref, sr_ref, d=d, qk_a=qk_a, v_a=v_a, rank=rank, conv_dim=conv_dim, qk_b=qk_b, v_b=v_b,
                     nh_b=nh_b)


def _in_proj_row(bi, c_ref, x_ref, mod_ref, g1_ref, wm_ref, ws_ref, wst_ref, wlr_ref, blr_ref, cw_ref,
                 arow_ref, dtrow_ref, acol_ref, dtcol_ref,
                 gqkv_ref, gg_ref, gla_ref, dqkv_ref, dg_ref, sc_ref, sr_ref, *, d, qk_a, v_a, rank, conv_dim,
                 qk_b, v_b, nh_b):
    j = pl.program_id(1)
    x = jnp.where(j == 0, c_ref[bi], x_ref[bi])
    m = mod_ref[0]
    sh1, sc1 = m[:, 0:d], m[:, d:2 * d]
    h = _rms(x) * (g1_ref[...] * (1.0 + sc1)) + sh1
    hb = h.astype(BF16)
    o_conv = 2 * qk_a + 2 * v_a
    o_gate = o_conv + conv_dim
    grp = 2 * LANES
    tb = x.shape[0]
    t = lax.broadcasted_iota(jnp.int32, (tb, 1), 0)
    seg_mask = jnp.where(j == 0, tb - 1, GRID_W - 1)
    first = (t & seg_mask) == 0
    last = (t & seg_mask) == seg_mask
    dkh = qk_b // nh_b

    def conv_dot(k):
        return _dot(hb, wm_ref[:, o_conv + k * grp:o_conv + (k + 1) * grp])

    def zero_rows(a, mask, at):
        pieces = []
        for s0 in range(0, tb, GRID_W):
            r0 = s0 + at
            pieces += [a[s0:r0], jnp.where(mask[r0:r0 + SUBLANES], 0.0, a[r0:r0 + SUBLANES]),
                       a[r0 + SUBLANES:s0 + GRID_W]]
        return jnp.concatenate([p for p in pieces if p.shape[0]], axis=0)

    def conv_group(u, k):
        up = zero_rows(pltpu.roll(u, 1, 0), first, 0)
        un = zero_rows(pltpu.roll(u, tb - 1, 0), last, GRID_W - SUBLANES)
        cw = cw_ref[:, k * grp:(k + 1) * grp]
        s = _silu(cw[0:1] * up + cw[1:2] * u + cw[2:3] * un)
        c0 = k * grp
        if c0 < 2 * qk_b:
            for hh in range(grp // dkh):
                sh = s[:, hh * dkh:(hh + 1) * dkh]
                inv = lax.rsqrt(jnp.sum(sh * sh, axis=-1, keepdims=True) + EPS)
                if c0 < qk_b:
                    inv = inv * (float(dkh) ** -0.5)
                dqkv_ref[bi, :, c0 + hh * dkh:c0 + (hh + 1) * dkh] = (sh * inv).astype(BF16)
        else:
            dqkv_ref[bi, :, c0:c0 + grp] = s.astype(BF16)

    def light_group(k):
        c0 = k * grp
        w0 = c0 if c0 < o_conv else o_gate + (c0 - o_conv)
        pa = _dot(hb, wm_ref[:, w0:w0 + grp])
        if c0 < qk_a:
            gqkv_ref[bi, :, c0:c0 + grp] = (pa * (float(qk_a // GLA_HEADS) ** -0.5)).astype(BF16)
        elif c0 < 2 * qk_a + v_a:
            gqkv_ref[bi, :, c0:c0 + grp] = pa.astype(BF16)
        elif c0 < o_conv:
            gg_ref[bi, :, c0 - 2 * qk_a - v_a:c0 - 2 * qk_a - v_a + grp] = pa.astype(BF16)
        else:
            dg_ref[bi, :, c0 - o_conv:c0 - o_conv + grp] = pa.astype(BF16)

    def small_group(ps, pst):
        r_a = ps[:, 0:rank].astype(BF16)
        for dr in range(2):
            lr = _dot(r_a, wlr_ref[dr]) + blr_ref[dr]
            gla_ref[bi, :, dr * qk_a:(dr + 1) * qk_a] = _log_sigmoid(lr) * (1.0 / GLA_TAU)
        nd = 2 * nh_b
        a_c = ps[:, rank:rank + nd]
        b_c = ps[:, rank + nd:rank + 2 * nd]
        sc_ref[bi, :, 0:nd] = -jnp.exp(arow_ref[...]) * _softplus(a_c + dtrow_ref[...])
        sc_ref[bi, :, nd:2 * nd] = _sigmoid(b_c)
        sr_ref[bi, 0, 0:nd, :] = -jnp.exp(acol_ref[...]) * _softplus(pst[0:nd] + dtcol_ref[...])
        sr_ref[bi, 0, nd:2 * nd, :] = _sigmoid(pst[nd:2 * nd])

    n_conv = conv_dim // grp
    n_light = (o_conv + v_b) // grp
    us = {0: conv_dot(0)}
    ps = _dot(hb, ws_ref[...])
    pst = _dot_nt(wst_ref[...], hb)
    if n_conv > 1:
        us[1] = conv_dot(1)
    g_next = 0
    for k in range(n_conv):
        if k + 2 < n_conv:
            us[k + 2] = conv_dot(k + 2)
        conv_group(us.pop(k), k)
        if k == 1 or n_conv == 1:
            small_group(ps, pst)
        todo = (n_light - g_next + (n_conv - k) - 1) // (n_conv - k)
        for _ in range(min(todo, n_light - g_next)):
            light_group(g_next)
            g_next += 1
    while g_next < n_light:
        light_group(g_next)
        g_next += 1


def _in_proj(stream, mod, layer, bsz, params, dims):
    g1, wm, ws, wst, wlr, blr, cw, a_row, dt_row, a_col, dt_col = params
    head, body = stream
    b, _, d = body.shape
    nb = body.shape[1] // TOKEN_BLOCK + 1
    s = nb * TOKEN_BLOCK
    qk_a, v_a, rank, conv_dim, qk_b, v_b, nh_b = dims
    nd = 2 * nh_b
    tb = TOKEN_BLOCK
    lay = lambda a: pl.BlockSpec((None,) + a.shape[1:], lambda i, j: (layer,) + (0,) * (a.ndim - 1))
    nr = math.gcd(b, IN_ROWS)
    tok = lambda n: pl.BlockSpec((nr, tb, n), lambda i, j: (i, j, 0))
    mod_specs = [pl.BlockSpec((1, 1, 6 * d), lambda i, j, k=k: (
        layer * MOD_ROWS + jnp.where(j == 0, bsz, i * nr + k), 0, 0)) for k in range(nr)]
    out_shape = [
        jax.ShapeDtypeStruct((b, s, 2 * qk_a + v_a), BF16), jax.ShapeDtypeStruct((b, s, v_a), BF16),
        jax.ShapeDtypeStruct((b, s, 2 * qk_a), F32),
        jax.ShapeDtypeStruct((b, s, conv_dim), BF16), jax.ShapeDtypeStruct((b, s, v_b), BF16),
        jax.ShapeDtypeStruct((b, s, 2 * nd), F32),
        jax.ShapeDtypeStruct((b, nb, 2 * nd, tb), F32),
    ]
    out_specs = [tok(2 * qk_a + v_a), tok(v_a), tok(2 * qk_a), tok(conv_dim), tok(v_b), tok(2 * nd),
                 pl.BlockSpec((nr, 1, 2 * nd, tb), lambda i, j: (i, j, 0, 0))]
    kern = functools.partial(_in_proj_kernel, d=d, qk_a=qk_a, v_a=v_a, rank=rank, conv_dim=conv_dim,
                             qk_b=qk_b, v_b=v_b, nh_b=nh_b)
    return pl.pallas_call(
        kern, out_shape=out_shape, grid=(b // nr, nb),
        in_specs=[
            pl.BlockSpec((nr, tb, d), lambda i, j: (i, 0, 0)),
            pl.BlockSpec((nr, tb, d), lambda i, j: (i, jnp.maximum(j - 1, 0), 0)), *mod_specs, lay(g1),
            lay(wm), lay(ws), lay(wst), lay(wlr), lay(blr), lay(cw), lay(a_row), lay(dt_row), lay(a_col), lay(dt_col),
        ],
        out_specs=out_specs,
        compiler_params=pltpu.CompilerParams(dimension_semantics=("parallel", "parallel"),
                                             vmem_limit_bytes=VMEM_LIMIT),
        name="in_proj",
    )(head, body, *([mod] * nr), g1, wm, ws, wst, wlr, blr, cw, a_row, dt_row, a_col, dt_col)


def _bwd_block(j, nb):
    return jnp.where(j == 0, 0, nb - j)


class _Cols:
    def __init__(self, ref, off):
        self.ref, self.off = ref, off

    def __getitem__(self, idx):
        z, rows, cols = idx
        return self.ref[z, rows, self.off + cols.start:self.off + cols.stop]


def _gla_stages(refs, st_ref, dk, dv, bi):
    tb = refs[0][3].shape[1]
    nchunk = tb // CHUNK
    npair = (GLA_HEADS * dk) // LANES
    hpp = LANES // dk
    lane = lax.broadcasted_iota(jnp.int32, (1, LANES), 1)
    lms = [(lane >= hh * dk) & (lane < (hh + 1) * dk) for hh in range(hpp)]
    mid = CHUNK // 2
    chains = [(dr, pr) for dr in range(2) for pr in range(npair)]
    units = [(dr, pr, c) for (dr, pr) in chains for c in range(nchunk)]
    bcum, q_dec, k_hat, q_mid, k_mid, a_last, sc, intra, dst = ({} for _ in range(9))

    def cumulative_decay():
        for dr in range(2):
            both = _cumsum_dot(_tri_blocks(tb, dr == 1).astype(BF16), refs[dr][3][bi], True)
            for pr in range(npair):
                bcum[dr, pr] = both[:, pr * LANES:(pr + 1) * LANES]

    def decayed_operands():
        for u in units:
            dr, pr, c = u
            rows = slice(c * CHUNK, (c + 1) * CHUNK)
            lanes = slice(pr * LANES, (pr + 1) * LANES)
            last = c * CHUNK + (0 if dr == 1 else CHUNK - 1)
            b = bcum[dr, pr][rows]
            b_last = bcum[dr, pr][last:last + 1]
            b_mid = bcum[dr, pr][c * CHUNK + mid:c * CHUNK + mid + 1]
            qc = refs[dr][0][bi, rows, lanes].astype(F32)
            kc = refs[dr][1][bi, rows, lanes].astype(F32)
            q_dec[u] = qc * jnp.exp(b)
            k_hat[u] = (kc * jnp.exp(b_last - b)).astype(BF16)
            q_mid[u] = qc * jnp.exp(b - b_mid)
            k_mid[u] = (kc * jnp.exp(b_mid - b)).astype(BF16)
            a_last[u] = jnp.exp(b_last)

    def per_head(m):
        return jnp.concatenate([jnp.where(lm, m, 0.0).astype(BF16) for lm in lms], axis=0)

    def scores():
        for u in units:
            sc[u] = _dot_nt(per_head(q_mid[u]), k_mid[u])

    def intra_and_increments():
        for u in units:
            dr, pr, c = u
            rows = slice(c * CHUNK, (c + 1) * CHUNK)
            causal = _tri(CHUNK, dr == 1, False)
            v_grp = refs[dr][2][bi, rows, pr * hpp * dv:(pr + 1) * hpp * dv]
            dh = _dot_tn(v_grp, k_hat[u])
            acc = None
            for hh in range(hpp):
                pm = jnp.where(causal, sc[u][hh * CHUNK:(hh + 1) * CHUNK], 0.0).astype(BF16)
                intra[u, hh] = _dot(pm, v_grp[:, hh * dv:(hh + 1) * dv])
                part = dh[hh * dv:(hh + 1) * dv]
                acc = part if acc is None else jnp.where(lms[hh], part, acc)
            dst[u] = acc

    def recurrence():
        st = {ch: st_ref[bi, ch[0], ch[1]] for ch in chains}
        for ci in range(nchunk):
            for ch in chains:
                dr, pr = ch
                c = nchunk - 1 - ci if dr == 1 else ci
                u = (dr, pr, c)
                rows = slice(c * CHUNK, (c + 1) * CHUNK)
                inter = _dot_nt(per_head(q_dec[u]), st[ch].astype(BF16))
                for hh in range(hpp):
                    head = pr * hpp + hh
                    refs[dr][4][bi, rows, head * dv:(head + 1) * dv] = (
                        intra[u, hh] + inter[hh * CHUNK:(hh + 1) * CHUNK]).astype(BF16)
                st[ch] = st[ch] * a_last[u] + dst[u]
        for ch in chains:
            st_ref[bi, ch[0], ch[1]] = st[ch]

    return [cumulative_decay, decayed_operands, scores, intra_and_increments, recurrence]


def _gdn_stages(refs, st_ref, dk, dv, nh, bi):
    tb = refs[0][3].shape[1]
    nchunk = tb // CHUNK
    eye = _tri(CHUNK, False, False) & _tri(CHUNK, True, False)
    chains = [(dr, hd) for dr in range(2) for hd in range(nh)]
    units = [(dr, c, hd) for dr in range(2) for c in range(nchunk) for hd in range(nh)]
    g_cols, g_rows, small, kk, qk, x, aqk, rhs, q_dec, k_dec, a_last = ({} for _ in range(11))
    t, uw, kd_uw, aq_uw, lhs, decay = ({} for _ in range(6))

    def cumulative_decay():
        for dr in range(2):
            small[dr] = refs[dr][3][bi]
            g_cols[dr] = _cumsum_dot(_tri_blocks(tb, dr == 1).astype(BF16), small[dr], True)
            g_rows[dr] = _cumsum_dot(_tri_blocks(tb, dr != 1).astype(BF16), refs[dr][4][bi, 0], False)

    def grams():
        for u in units:
            dr, c, hd = u
            rows = slice(c * CHUNK, (c + 1) * CHUNK)
            kh = refs[dr][1][bi, rows, hd * dk:(hd + 1) * dk]
            qh = refs[dr][0][bi, rows, hd * dk:(hd + 1) * dk]
            kq = _dot_nt(jnp.concatenate([kh, qh], axis=0), kh)
            kk[u], qk[u] = kq[:CHUNK], kq[CHUNK:]

    def solve_operands():
        for u in units:
            dr, c, hd = u
            rows = slice(c * CHUNK, (c + 1) * CHUNK)
            ia = dr * nh + hd
            ib = 2 * nh + dr * nh + hd
            g_c = g_cols[dr][rows, ia:ia + 1]
            g_r = g_rows[dr][ia:ia + 1, rows]
            be_c = small[dr][rows, ib:ib + 1]
            decay[u] = jnp.exp(jnp.minimum(g_c - g_r, 0.0))
            x[u] = jnp.where(_tri(CHUNK, dr == 1, True), be_c * kk[u] * decay[u], 0.0)

    def other_operands():
        for u in units:
            dr, c, hd = u
            rows = slice(c * CHUNK, (c + 1) * CHUNK)
            ia = dr * nh + hd
            ib = 2 * nh + dr * nh + hd
            last = c * CHUNK + (0 if dr == 1 else CHUNK - 1)
            g_c = g_cols[dr][rows, ia:ia + 1]
            be_c = small[dr][rows, ib:ib + 1]
            g_last = g_cols[dr][last:last + 1, ia:ia + 1]
            aqk[u] = jnp.where(_tri(CHUNK, dr == 1, False), qk[u] * decay[u], 0.0).astype(BF16)
            eg = jnp.exp(g_c)
            qh = refs[dr][0][bi, rows, hd * dk:(hd + 1) * dk].astype(F32)
            khf = refs[dr][1][bi, rows, hd * dk:(hd + 1) * dk].astype(F32)
            vh = refs[dr][2][bi, rows, hd * dv:(hd + 1) * dv].astype(F32)
            rhs[u] = jnp.concatenate([(be_c * vh).astype(BF16), ((be_c * eg) * khf).astype(BF16)], axis=1)
            q_dec[u] = qh * eg
            k_dec[u] = khf * jnp.exp(g_last - g_c)
            a_last[u] = jnp.exp(g_last)

    def inverse_start():
        for u in units:
            t[u] = jnp.where(eye, 1.0, jnp.where(_couple(CHUNK, 1, u[0] == 1), -x[u], 0.0))

    def inverse_level(s):
        a = {u: jnp.where(_couple(CHUNK, s, u[0] == 1), x[u], 0.0).astype(BF16) for u in units}
        tb16 = {u: t[u].astype(BF16) for u in units}
        ta = {u: _dot(tb16[u], a[u]).astype(BF16) for u in units}
        for u in units:
            t[u] = t[u] - _dot(ta[u], tb16[u])

    levels = []
    s = 2
    while s < CHUNK:
        levels.append(functools.partial(inverse_level, s))
        s *= 2

    def solve():
        for u in units:
            uw[u] = _dot(t[u].astype(BF16), rhs[u]).astype(BF16)

    def fold():
        for u in units:
            both = _dot(jnp.concatenate([k_dec[u].T.astype(BF16), aqk[u]], axis=0), uw[u])
            kd_uw[u], aq_uw[u] = both[:dk], both[dk:]
        for u in units:
            lhs[u] = jnp.concatenate([(-kd_uw[u][:, dv:]).astype(BF16),
                                      (q_dec[u] - aq_uw[u][:, dv:]).astype(BF16)], axis=0)

    def recurrence():
        st = {ch: st_ref[bi, ch[0], ch[1]] for ch in chains}
        for ci in range(nchunk):
            for ch in chains:
                dr, hd = ch
                c = nchunk - 1 - ci if dr == 1 else ci
                u = (dr, c, hd)
                rows = slice(c * CHUNK, (c + 1) * CHUNK)
                res = _dot(lhs[u], st[ch].astype(BF16))
                refs[dr][5][bi, rows, hd * dv:(hd + 1) * dv] = (res[dk:] + aq_uw[u][:, :dv]).astype(BF16)
                st[ch] = st[ch] * a_last[u] + res[:dk] + kd_uw[u][:, :dv]
        for ch in chains:
            st_ref[bi, ch[0], ch[1]] = st[ch]

    return [cumulative_decay, grams, solve_operands, other_operands, inverse_start, *levels,
            solve, fold, recurrence]


def _scan_kernel(af, alf, ab, alb, bf, bcf, brf, bb, bcb, brb,
                 oaf, oab, obf, obb, sta_ref, stb_ref, *, dk_a, dv_a, dk_b, dv_b, nh_b):
    @pl.when(pl.program_id(1) == 0)
    def _():
        sta_ref[...] = jnp.zeros_like(sta_ref)
        stb_ref[...] = jnp.zeros_like(stb_ref)

    qk_a, qk_b = GLA_HEADS * dk_a, nh_b * dk_b
    qkv = lambda ref, qk: (_Cols(ref, 0), _Cols(ref, qk), _Cols(ref, 2 * qk))
    refs_a = ((*qkv(af, qk_a), alf, oaf), (*qkv(ab, qk_a), alb, oab))
    refs_b = ((*qkv(bf, qk_b), bcf, brf, obf), (*qkv(bb, qk_b), bcb, brb, obb))
    stages = {}
    for bi in range(af.shape[0]):
        stages["a", bi] = _gla_stages(refs_a, sta_ref, dk_a, dv_a, bi)
        stages["b", bi] = _gdn_stages(refs_b, stb_ref, dk_b, dv_b, nh_b, bi)
    lag = SCAN_ROW_LAG
    for pos in range(len(SCAN_ORDER) + lag * (af.shape[0] - 1)):
        for bi in range(af.shape[0]):
            if 0 <= pos - lag * bi < len(SCAN_ORDER):
                name, idx = SCAN_ORDER[pos - lag * bi]
                stages[name, bi][idx]()


def _scan(gqkv, gla, dqkv, small_c, small_r, dims):
    b, s, _ = gqkv.shape
    qk_a, v_a, _, _, qk_b, v_b, _ = dims
    nb = s // TOKEN_BLOCK
    tb = TOKEN_BLOCK
    ns = small_c.shape[2]
    nr = math.gcd(b, SCAN_ROWS)
    fwd = lambda n, col=0: pl.BlockSpec((nr, tb, n), lambda i, j: (i, j, col))
    bwd = lambda n, col=0: pl.BlockSpec((nr, tb, n), lambda i, j: (i, _bwd_block(j, nb), col))
    rfwd = pl.BlockSpec((nr, 1, ns, tb), lambda i, j: (i, j, 0, 0))
    rbwd = pl.BlockSpec((nr, 1, ns, tb), lambda i, j: (i, _bwd_block(j, nb), 0, 0))
    dk_a, dv_a = qk_a // GLA_HEADS, v_a // GLA_HEADS
    dk_b, dv_b = qk_b // GDN_HEADS, v_b // GDN_HEADS
    out = lambda v: jax.ShapeDtypeStruct((b, s, v), BF16)
    return pl.pallas_call(
        functools.partial(_scan_kernel, dk_a=dk_a, dv_a=dv_a, dk_b=dk_b, dv_b=dv_b, nh_b=GDN_HEADS),
        out_shape=[out(v_a), out(v_a), out(v_b), out(v_b)],
        grid=(b // nr, nb),
        in_specs=[fwd(2 * qk_a + v_a), fwd(qk_a, 0), bwd(2 * qk_a + v_a), bwd(qk_a, 1),
                  fwd(2 * qk_b + v_b), fwd(ns), rfwd, bwd(2 * qk_b + v_b), bwd(ns), rbwd],
        out_specs=[fwd(v_a), bwd(v_a), fwd(v_b), bwd(v_b)],
        scratch_shapes=[pltpu.VMEM((nr, 2, qk_a // LANES, dv_a, LANES), F32),
                        pltpu.VMEM((nr, 2, GDN_HEADS, dk_b, dv_b), F32)],
        compiler_params=pltpu.CompilerParams(dimension_semantics=("parallel", "arbitrary"),
                                             vmem_limit_bytes=VMEM_LIMIT),
        name="scan",
    )(gqkv, gla, gqkv, gla, dqkv, small_c, small_r, dqkv, small_c, small_r)


def _post_kernel(x_ref, mod_ref, *refs, d, nh_a, nh_b, final, nsub):
    mix = [refs[6 * i:6 * i + 6] for i in range(nsub)]
    na_ref, nb_ref, wo_ref, g2_ref, w1_ref, w2_ref, fg_ref, o_ref = refs[6 * nsub:]
    tb = x_ref.shape[1] // nsub
    m = mod_ref[0]
    gt1 = m[:, 2 * d:3 * d]
    sh2, sc2, gt2 = m[:, 3 * d:4 * d], m[:, 4 * d:5 * d], m[:, 5 * d:6 * d]
    slab = 2 * LANES
    x1, hb, act = {}, {}, {}

    def merge_and_project(i):
        af_ref, ab_ref, bf_ref, bb_ref, ga_ref, gb_ref = mix[i]
        acc = None
        row = 0
        for (f_ref, b_ref, g_ref, n_ref, nh) in ((af_ref, ab_ref, ga_ref, na_ref, nh_a),
                                                 (bf_ref, bb_ref, gb_ref, nb_ref, nh_b)):
            width = f_ref.shape[2]
            hv = width // nh
            for c0 in range(0, width, slab):
                o = f_ref[0, :, c0:c0 + slab].astype(F32) + b_ref[0, :, c0:c0 + slab].astype(F32)
                gate = _silu(g_ref[0, :, c0:c0 + slab].astype(F32))
                parts = [(_rms(o[:, k:k + hv]) * n_ref[...] * gate[:, k:k + hv]).astype(BF16)
                         for k in range(0, slab, hv)]
                part = _dot(jnp.concatenate(parts, axis=1), wo_ref[row:row + slab, :])
                acc = part if acc is None else acc + part
                row += slab
        x1[i] = x_ref[0, i * tb:(i + 1) * tb, :] + gt1 * acc
        hb[i] = (_rms(x1[i]) * (g2_ref[...] * (1.0 + sc2)) + sh2).astype(BF16)

    def mlp_up(i):
        a = jnp.maximum(_dot(hb[i], w1_ref[...]), 0.0)
        act[i] = (a * a).astype(BF16)

    def mlp_down(i):
        x2 = x1[i] + gt2 * _dot(act[i], w2_ref[...])
        if final:
            x2 = _rms(x2) * fg_ref[...]
        o_ref[0, i * tb:(i + 1) * tb, :] = x2

    merge_and_project(0)
    for i in range(nsub):
        mlp_up(i)
        if i + 1 < nsub:
            merge_and_project(i + 1)
        mlp_down(i)


def _post(stream, part, mod, layer, final, oaf, oab, obf, obb, gg, dg, params, fg, nh_a, nh_b):
    na, nb_, wo, g2, w1, w2 = params
    ctx_arr, lat_arr = stream
    b, t, d = lat_arr.shape
    tb = TOKEN_BLOCK
    ctx = part == "ctx"
    nsub = 1 if ctx else math.gcd(t // tb, POST_SUB)
    nstep = 1 if ctx else t // (nsub * tb)
    first = 0 if ctx else 1
    lay = lambda a: pl.BlockSpec((None,) + a.shape[1:], lambda i, j: (layer,) + (0,) * (a.ndim - 1))
    weight = lambda a: pl.BlockSpec((None,) + a.shape[1:], lambda i, j: (layer,) + (0,) * (a.ndim - 1),
                                    pipeline_mode=pl.Buffered(1))
    bsz = b
    mod_spec = pl.BlockSpec((1, 1, 6 * d), lambda i, j: (layer * MOD_ROWS + (bsz if ctx else i), 0, 0))
    va, vb = oaf.shape[2], obf.shape[2]
    mix_specs, mix_args = [], []
    for k in range(nsub):
        blk = lambda n, k=k: pl.BlockSpec((1, tb, n), lambda i, j: (i, first + j * nsub + k, 0))
        mix_specs += [blk(va), blk(va), blk(vb), blk(vb), blk(va), blk(vb)]
        mix_args += [oaf, oab, obf, obb, gg, dg]
    return pl.pallas_call(
        functools.partial(_post_kernel, d=d, nh_a=nh_a, nh_b=nh_b, final=final, nsub=nsub),
        out_shape=jax.ShapeDtypeStruct((b, nstep * nsub * tb, d), F32),
        grid=(b, nstep),
        in_specs=[pl.BlockSpec((1, nsub * tb, d), lambda i, j: (i, j, 0)), mod_spec, *mix_specs,
                  lay(na), lay(nb_), weight(wo), lay(g2), weight(w1), weight(w2),
                  pl.BlockSpec((1, d), lambda i, j: (0, 0))],
        out_specs=pl.BlockSpec((1, nsub * tb, d), lambda i, j: (i, j, 0)),
        compiler_params=pltpu.CompilerParams(dimension_semantics=("parallel", "parallel"),
                                             vmem_limit_bytes=VMEM_LIMIT),
        name="post_ctx" if ctx else "post",
    )(ctx_arr if ctx else lat_arr, mod, *mix_args, na, nb_, wo, g2, w1, w2, fg.reshape(1, d))


def kernel(x, c, ctx, c_ctx, w_ada, b_ada, norm1_g, norm2_g, w_in, gla_w_lr, gla_b_lr, gdn_conv_w,
           gdn_a_log, gdn_dt_bias, gla_norm_g, gdn_norm_g, w_out, w_ff1, w_ff2, final_norm_g):
    bsz, t, d = x.shape
    depth = w_ada.shape[0]
    assert ctx.shape[1] == TOKEN_BLOCK and t % TOKEN_BLOCK == 0 and TOKEN_BLOCK % GRID_W == 0
    assert bsz + 1 <= MOD_ROWS
    qk_a = gla_w_lr.shape[3]
    rank = gla_w_lr.shape[2]
    v_a = gla_norm_g.shape[1] * GLA_HEADS
    conv_dim = gdn_conv_w.shape[2]
    v_b = gdn_norm_g.shape[1] * GDN_HEADS
    qk_b = (conv_dim - v_b) // 2
    nd = 2 * GDN_HEADS
    dims = (qk_a, v_a, rank, conv_dim, qk_b, v_b, GDN_HEADS)

    o_r = 2 * qk_a + 2 * v_a
    o_c = o_r + rank
    o_g = o_c + conv_dim
    o_s = o_g + v_b
    assert w_in.shape[2] == o_s + 2 * nd
    w_main = jnp.concatenate([w_in[:, :, :o_r], w_in[:, :, o_c:o_s]], axis=2).astype(BF16)
    w_sm = jnp.concatenate([w_in[:, :, o_r:o_c], w_in[:, :, o_s:]], axis=2)
    w_small = jnp.pad(w_sm, ((0, 0), (0, 0), (0, LANES - w_sm.shape[2]))).astype(BF16)
    w_small_t = jnp.swapaxes(w_in[:, :, o_s:], 1, 2).astype(BF16)

    cc = jnp.concatenate([c, c_ctx[None, :], jnp.zeros((MOD_ROWS - bsz - 1, d), F32)], axis=0)
    mod = _modulation(cc, w_ada, b_ada).reshape(depth * MOD_ROWS, 1, 6 * d)

    in_params = (norm1_g.reshape(depth, 1, d), w_main, w_small, w_small_t, gla_w_lr.astype(BF16),
                 gla_b_lr.reshape(depth, 2, 1, qk_a), gdn_conv_w,
                 gdn_a_log.reshape(depth, 1, nd), gdn_dt_bias.reshape(depth, 1, nd),
                 gdn_a_log.reshape(depth, nd, 1), gdn_dt_bias.reshape(depth, nd, 1))
    post_params = (gla_norm_g.reshape(depth, 1, -1), gdn_norm_g.reshape(depth, 1, -1), w_out.astype(BF16),
                   norm2_g.reshape(depth, 1, d), w_ff1.astype(BF16), w_ff2.astype(BF16))

    stream = (ctx, x)
    for l in range(depth):
        last = l == depth - 1
        gqkv, gg, gla, dqkv, dg, small_c, small_r = _in_proj(stream, mod, l, bsz, in_params, dims)
        oaf, oab, obf, obb = _scan(gqkv, gla, dqkv, small_c, small_r, dims)
        args = (mod, l, last, oaf, oab, obf, obb, gg, dg, post_params, final_norm_g, GLA_HEADS, GDN_HEADS)
        xs = _post(stream, "latent", *args)
        if not last:
            stream = (_post(stream, "ctx", *args), xs)
    return xs
```

```python
import functools
import math

import jax
import jax.numpy as jnp
from jax import lax
from jax.experimental import pallas as pl
from jax.experimental.pallas import tpu as pltpu

EPS = 1e-6
GRID_W = 64
GLA_HEADS = 4
GDN_HEADS = 4
GLA_TAU = 16.0
TOKEN_BLOCK = 256
CHUNK = 64
LANES = 128
SUBLANES = 8
POST_SUB = 2
SCAN_ROWS = 2
IN_ROWS = 4
SCAN_ROW_LAG = 6
MOD_ROWS = 16
VMEM_LIMIT = 56 * 1024 * 1024
SCAN_ORDER = (("b", 0), ("b", 1), ("a", 0), ("a", 1), ("a", 2), ("a", 3), ("b", 2), ("b", 3), ("b", 4),
              ("b", 5), ("b", 6), ("a", 4), ("b", 7), ("b", 8), ("b", 9), ("b", 10), ("b", 11), ("b", 12))

F32 = jnp.float32
BF16 = jnp.bfloat16


def _dot(a, b):
    return jnp.dot(a, b, preferred_element_type=F32)


def _dot_nt(a, b):
    return lax.dot_general(a, b, (((1,), (1,)), ((), ())), preferred_element_type=F32)


def _dot_tn(a, b):
    return lax.dot_general(a, b, (((0,), (0,)), ((), ())), preferred_element_type=F32)


def _silu(x):
    return x / (1.0 + jnp.exp(-x))


def _sigmoid(x):
    return 1.0 / (1.0 + jnp.exp(-x))


def _softplus(x):
    return jnp.maximum(x, 0.0) + jnp.log(1.0 + jnp.exp(-jnp.abs(x)))


def _log_sigmoid(x):
    return jnp.minimum(x, 0.0) - jnp.log(1.0 + jnp.exp(-jnp.abs(x)))


def _rms(x):
    return x * lax.rsqrt(jnp.mean(x * x, axis=-1, keepdims=True) + EPS)


def _cumsum_dot(tri_bf16, x, left):
    hi = x.astype(BF16)
    lo = (x - hi.astype(F32)).astype(BF16)
    if left:
        return _dot(tri_bf16, hi) + _dot(tri_bf16, lo)
    return _dot(hi, tri_bf16) + _dot(lo, tri_bf16)


def _tri(n, upper, strict):
    r = lax.broadcasted_iota(jnp.int32, (n, n), 0)
    c = lax.broadcasted_iota(jnp.int32, (n, n), 1)
    if upper:
        m = (r < c) if strict else (r <= c)
    else:
        m = (r > c) if strict else (r >= c)
    return m


def _couple(n, s, upper):
    r = lax.broadcasted_iota(jnp.int32, (n, n), 0)
    c = lax.broadcasted_iota(jnp.int32, (n, n), 1)
    same = (r & ~(2 * s - 1)) == (c & ~(2 * s - 1))
    r_hi, c_hi = (r & s) != 0, (c & s) != 0
    return same & ((~r_hi & c_hi) if upper else (r_hi & ~c_hi))


def _tri_blocks(n, upper):
    r = lax.broadcasted_iota(jnp.int32, (n, n), 0)
    c = lax.broadcasted_iota(jnp.int32, (n, n), 1)
    same = (r & ~(CHUNK - 1)) == (c & ~(CHUNK - 1))
    return same & ((r <= c) if upper else (r >= c))


def _mod_kernel(cc_ref, w_ref, b_ref, o_ref):
    s = _silu(cc_ref[...]).astype(BF16)
    o_ref[0] = _dot(s, w_ref[0].astype(BF16)) + b_ref[0]


def _modulation(cc, w_ada, b_ada):
    depth, d, d6 = w_ada.shape
    nblk = d6 // d
    return pl.pallas_call(
        _mod_kernel,
        out_shape=jax.ShapeDtypeStruct((depth, MOD_ROWS, d6), F32),
        grid=(depth, nblk),
        in_specs=[
            pl.BlockSpec((MOD_ROWS, d), lambda l, n: (0, 0)),
            pl.BlockSpec((1, d, d), lambda l, n: (l, 0, n)),
            pl.BlockSpec((1, 1, d), lambda l, n: (l, 0, n)),
        ],
        out_specs=pl.BlockSpec((1, MOD_ROWS, d), lambda l, n: (l, 0, n)),
        compiler_params=pltpu.CompilerParams(dimension_semantics=("parallel", "parallel")),
        name="adaln_modulation",
    )(cc, w_ada, b_ada.reshape(depth, 1, d6))


def _in_proj_kernel(c_ref, x_ref, *refs, d, qk_a, v_a, rank, conv_dim, qk_b, v_b, nh_b):
    nr = x_ref.shape[0]
    mod_refs = refs[:nr]
    (g1_ref, wm_ref, ws_ref, wst_ref, wlr_ref, blr_ref, cw_ref, arow_ref, dtrow_ref, acol_ref, dtcol_ref,
     gqkv_ref, gg_ref, gla_ref, dqkv_ref, dg_ref, sc_ref, sr_ref) = refs[nr:]
    for bi in range(nr):
        _in_proj_row(bi, c_ref, x_ref, mod_refs[bi], g1_ref, wm_ref, ws_ref, wst_ref, wlr_ref, blr_ref, cw_ref,
                     arow_ref, dtrow_ref, acol_ref, dtcol_ref, gqkv_ref, gg_ref, gla_ref, dqkv_ref, dg_ref,
                     sc_ref, sr_ref, d=d, qk_a=qk_a, v_a=v_a, rank=rank, conv_dim=conv_dim, qk_b=qk_b, v_b=v_b,
                     nh_b=nh_b)


def _in_proj_row(bi, c_ref, x_ref, mod_ref, g1_ref, wm_ref, ws_ref, wst_ref, wlr_ref, blr_ref, cw_ref,
                 arow_ref, dtrow_ref, acol_ref, dtcol_ref,
                 gqkv_ref, gg_ref, gla_ref, dqkv_ref, dg_ref, sc_ref, sr_ref, *, d, qk_a, v_a, rank, conv_dim,
                 qk_b, v_b, nh_b):
    j = pl.program_id(1)
    x = jnp.where(j == 0, c_ref[bi], x_ref[bi])
    m = mod_ref[0]
    sh1, sc1 = m[:, 0:d], m[:, d:2 * d]
    h = _rms(x) * (g1_ref[...] * (1.0 + sc1)) + sh1
    hb = h.astype(BF16)
    o_conv = 2 * qk_a + 2 * v_a
    o_gate = o_conv + conv_dim
    grp = 2 * LANES
    tb = x.shape[0]
    t = lax.broadcasted_iota(jnp.int32, (tb, 1), 0)
    seg_mask = jnp.where(j == 0, tb - 1, GRID_W - 1)
    first = (t & seg_mask) == 0
    last = (t & seg_mask) == seg_mask
    dkh = qk_b // nh_b

    def conv_dot(k):
        return _dot(hb, wm_ref[:, o_conv + k * grp:o_conv + (k + 1) * grp])

    def zero_rows(a, mask, at):
        pieces = []
        for s0 in range(0, tb, GRID_W):
            r0 = s0 + at
            pieces += [a[s0:r0], jnp.where(mask[r0:r0 + SUBLANES], 0.0, a[r0:r0 + SUBLANES]),
                       a[r0 + SUBLANES:s0 + GRID_W]]
        return jnp.concatenate([p for p in pieces if p.shape[0]], axis=0)

    def conv_group(u, k):
        up = zero_rows(pltpu.roll(u, 1, 0), first, 0)
        un = zero_rows(pltpu.roll(u, tb - 1, 0), last, GRID_W - SUBLANES)
        cw = cw_ref[:, k * grp:(k + 1) * grp]
        s = _silu(cw[0:1] * up + cw[1:2] * u + cw[2:3] * un)
        c0 = k * grp
        if c0 < 2 * qk_b:
            for hh in range(grp // dkh):
                sh = s[:, hh * dkh:(hh + 1) * dkh]
                inv = lax.rsqrt(jnp.sum(sh * sh, axis=-1, keepdims=True) + EPS)
                if c0 < qk_b:
                    inv = inv * (float(dkh) ** -0.5)
                dqkv_ref[bi, :, c0 + hh * dkh:c0 + (hh + 1) * dkh] = (sh * inv).astype(BF16)
        else:
            dqkv_ref[bi, :, c0:c0 + grp] = s.astype(BF16)

    def light_group(k):
        c0 = k * grp
        w0 = c0 if c0 < o_conv else o_gate + (c0 - o_conv)
        pa = _dot(hb, wm_ref[:, w0:w0 + grp])
        if c0 < qk_a:
            gqkv_ref[bi, :, c0:c0 + grp] = (pa * (float(qk_a // GLA_HEADS) ** -0.5)).astype(BF16)
        elif c0 < 2 * qk_a + v_a:
            gqkv_ref[bi, :, c0:c0 + grp] = pa.astype(BF16)
        elif c0 < o_conv:
            gg_ref[bi, :, c0 - 2 * qk_a - v_a:c0 - 2 * qk_a - v_a + grp] = pa.astype(BF16)
        else:
            dg_ref[bi, :, c0 - o_conv:c0 - o_conv + grp] = pa.astype(BF16)

    def small_group(ps, pst):
        r_a = ps[:, 0:rank].astype(BF16)
        for dr in range(2):
            lr = _dot(r_a, wlr_ref[dr]) + blr_ref[dr]
            gla_ref[bi, :, dr * qk_a:(dr + 1) * qk_a] = _log_sigmoid(lr) * (1.0 / GLA_TAU)
        nd = 2 * nh_b
        a_c = ps[:, rank:rank + nd]
        b_c = ps[:, rank + nd:rank + 2 * nd]
        sc_ref[bi, :, 0:nd] = -jnp.exp(arow_ref[...]) * _softplus(a_c + dtrow_ref[...])
        sc_ref[bi, :, nd:2 * nd] = _sigmoid(b_c)
        sr_ref[bi, 0, 0:nd, :] = -jnp.exp(acol_ref[...]) * _softplus(pst[0:nd] + dtcol_ref[...])
        sr_ref[bi, 0, nd:2 * nd, :] = _sigmoid(pst[nd:2 * nd])

    n_conv = conv_dim // grp
    n_light = (o_conv + v_b) // grp
    us = {0: conv_dot(0)}
    ps = _dot(hb, ws_ref[...])
    pst = _dot_nt(wst_ref[...], hb)
    if n_conv > 1:
        us[1] = conv_dot(1)
    g_next = 0
    for k in range(n_conv):
        if k + 2 < n_conv:
            us[k + 2] = conv_dot(k + 2)
        conv_group(us.pop(k), k)
        if k == 0:
            small_group(ps, pst)
        todo = (n_light - g_next + (n_conv - k) - 1) // (n_conv - k) if k >= n_conv // 2 else 1
        for _ in range(min(todo, n_light - g_next)):
            light_group(g_next)
            g_next += 1
    while g_next < n_light:
        light_group(g_next)
        g_next += 1


def _in_proj(stream, mod, layer, bsz, params, dims):
    g1, wm, ws, wst, wlr, blr, cw, a_row, dt_row, a_col, dt_col = params
    head, body = stream
    b, _, d = body.shape
    nb = body.shape[1] // TOKEN_BLOCK + 1
    s = nb * TOKEN_BLOCK
    qk_a, v_a, rank, conv_dim, qk_b, v_b, nh_b = dims
    nd = 2 * nh_b
    tb = TOKEN_BLOCK
    lay = lambda a: pl.BlockSpec((None,) + a.shape[1:], lambda i, j: (layer,) + (0,) * (a.ndim - 1))
    nr = math.gcd(b, IN_ROWS)
    tok = lambda n: pl.BlockSpec((nr, tb, n), lambda i, j: (i, j, 0))
    mod_specs = [pl.BlockSpec((1, 1, 6 * d), lambda i, j, k=k: (
        layer * MOD_ROWS + jnp.where(j == 0, bsz, i * nr + k), 0, 0)) for k in range(nr)]
    out_shape = [
        jax.ShapeDtypeStruct((b, s, 2 * qk_a + v_a), BF16), jax.ShapeDtypeStruct((b, s, v_a), BF16),
        jax.ShapeDtypeStruct((b, s, 2 * qk_a), F32),
        jax.ShapeDtypeStruct((b, s, conv_dim), BF16), jax.ShapeDtypeStruct((b, s, v_b), BF16),
        jax.ShapeDtypeStruct((b, s, 2 * nd), F32),
        jax.ShapeDtypeStruct((b, nb, 2 * nd, tb), F32),
    ]
    out_specs = [tok(2 * qk_a + v_a), tok(v_a), tok(2 * qk_a), tok(conv_dim), tok(v_b), tok(2 * nd),
                 pl.BlockSpec((nr, 1, 2 * nd, tb), lambda i, j: (i, j, 0, 0))]
    kern = functools.partial(_in_proj_kernel, d=d, qk_a=qk_a, v_a=v_a, rank=rank, conv_dim=conv_dim,
                             qk_b=qk_b, v_b=v_b, nh_b=nh_b)
    return pl.pallas_call(
        kern, out_shape=out_shape, grid=(b // nr, nb),
        in_specs=[
            pl.BlockSpec((nr, tb, d), lambda i, j: (i, 0, 0)),
            pl.BlockSpec((nr, tb, d), lambda i, j: (i, jnp.maximum(j - 1, 0), 0)), *mod_specs, lay(g1),
            lay(wm), lay(ws), lay(wst), lay(wlr), lay(blr), lay(cw), lay(a_row), lay(dt_row), lay(a_col), lay(dt_col),
        ],
        out_specs=out_specs,
        compiler_params=pltpu.CompilerParams(dimension_semantics=("parallel", "parallel"),
                                             vmem_limit_bytes=VMEM_LIMIT),
        name="in_proj",
    )(head, body, *([mod] * nr), g1, wm, ws, wst, wlr, blr, cw, a_row, dt_row, a_col, dt_col)


def _bwd_block(j, nb):
    return jnp.where(j == 0, 0, nb - j)


class _Cols:
    def __init__(self, ref, off):
        self.ref, self.off = ref, off

    def __getitem__(self, idx):
        z, rows, cols = idx
        return self.ref[z, rows, self.off + cols.start:self.off + cols.stop]


def _gla_stages(refs, st_ref, dk, dv, bi):
    tb = refs[0][3].shape[1]
    nchunk = tb // CHUNK
    npair = (GLA_HEADS * dk) // LANES
    hpp = LANES // dk
    lane = lax.broadcasted_iota(jnp.int32, (1, LANES), 1)
    lms = [(lane >= hh * dk) & (lane < (hh + 1) * dk) for hh in range(hpp)]
    mid = CHUNK // 2
    chains = [(dr, pr) for dr in range(2) for pr in range(npair)]
    units = [(dr, pr, c) for (dr, pr) in chains for c in range(nchunk)]
    bcum, q_dec, k_hat, q_mid, k_mid, a_last, sc, intra, dst = ({} for _ in range(9))

    def cumulative_decay():
        for dr in range(2):
            both = _cumsum_dot(_tri_blocks(tb, dr == 1).astype(BF16), refs[dr][3][bi], True)
            for pr in range(npair):
                bcum[dr, pr] = both[:, pr * LANES:(pr + 1) * LANES]

    def decayed_operands():
        for u in units:
            dr, pr, c = u
            rows = slice(c * CHUNK, (c + 1) * CHUNK)
            lanes = slice(pr * LANES, (pr + 1) * LANES)
            last = c * CHUNK + (0 if dr == 1 else CHUNK - 1)
            b = bcum[dr, pr][rows]
            b_last = bcum[dr, pr][last:last + 1]
            b_mid = bcum[dr, pr][c * CHUNK + mid:c * CHUNK + mid + 1]
            qc = refs[dr][0][bi, rows, lanes].astype(F32)
            kc = refs[dr][1][bi, rows, lanes].astype(F32)
            q_dec[u] = qc * jnp.exp(b)
            k_hat[u] = (kc * jnp.exp(b_last - b)).astype(BF16)
            q_mid[u] = qc * jnp.exp(b - b_mid)
            k_mid[u] = (kc * jnp.exp(b_mid - b)).astype(BF16)
            a_last[u] = jnp.exp(b_last)

    def per_head(m):
        return jnp.concatenate([jnp.where(lm, m, 0.0).astype(BF16) for lm in lms], axis=0)

    def scores():
        for u in units:
            sc[u] = _dot_nt(per_head(q_mid[u]), k_mid[u])

    def intra_and_increments():
        for u in units:
            dr, pr, c = u
            rows = slice(c * CHUNK, (c + 1) * CHUNK)
            causal = _tri(CHUNK, dr == 1, False)
            v_grp = refs[dr][2][bi, rows, pr * hpp * dv:(pr + 1) * hpp * dv]
            dh = _dot_tn(v_grp, k_hat[u])
            acc = None
            for hh in range(hpp):
                pm = jnp.where(causal, sc[u][hh * CHUNK:(hh + 1) * CHUNK], 0.0).astype(BF16)
                intra[u, hh] = _dot(pm, v_grp[:, hh * dv:(hh + 1) * dv])
                part = dh[hh * dv:(hh + 1) * dv]
                acc = part if acc is None else jnp.where(lms[hh], part, acc)
            dst[u] = acc

    def recurrence():
        st = {ch: st_ref[bi, ch[0], ch[1]] for ch in chains}
        for ci in range(nchunk):
            for ch in chains:
                dr, pr = ch
                c = nchunk - 1 - ci if dr == 1 else ci
                u = (dr, pr, c)
                rows = slice(c * CHUNK, (c + 1) * CHUNK)
                inter = _dot_nt(per_head(q_dec[u]), st[ch].astype(BF16))
                for hh in range(hpp):
                    head = pr * hpp + hh
                    refs[dr][4][bi, rows, head * dv:(head + 1) * dv] = (
                        intra[u, hh] + inter[hh * CHUNK:(hh + 1) * CHUNK]).astype(BF16)
                st[ch] = st[ch] * a_last[u] + dst[u]
        for ch in chains:
            st_ref[bi, ch[0], ch[1]] = st[ch]

    return [cumulative_decay, decayed_operands, scores, intra_and_increments, recurrence]


def _gdn_stages(refs, st_ref, dk, dv, nh, bi):
    tb = refs[0][3].shape[1]
    nchunk = tb // CHUNK
    eye = _tri(CHUNK, False, False) & _tri(CHUNK, True, False)
    chains = [(dr, hd) for dr in range(2) for hd in range(nh)]
    units = [(dr, c, hd) for dr in range(2) for c in range(nchunk) for hd in range(nh)]
    g_cols, g_rows, small, kk, qk, x, aqk, rhs, q_dec, k_dec, a_last = ({} for _ in range(11))
    t, uw, kd_uw, aq_uw, lhs, decay = ({} for _ in range(6))

    def cumulative_decay():
        for dr in range(2):
            small[dr] = refs[dr][3][bi]
            g_cols[dr] = _cumsum_dot(_tri_blocks(tb, dr == 1).astype(BF16), small[dr], True)
            g_rows[dr] = _cumsum_dot(_tri_blocks(tb, dr != 1).astype(BF16), refs[dr][4][bi, 0], False)

    def grams():
        for u in units:
            dr, c, hd = u
            rows = slice(c * CHUNK, (c + 1) * CHUNK)
            kh = refs[dr][1][bi, rows, hd * dk:(hd + 1) * dk]
            qh = refs[dr][0][bi, rows, hd * dk:(hd + 1) * dk]
            kq = _dot_nt(jnp.concatenate([kh, qh], axis=0), kh)
            kk[u], qk[u] = kq[:CHUNK], kq[CHUNK:]

    def solve_operands():
        for u in units:
            dr, c, hd = u
            rows = slice(c * CHUNK, (c + 1) * CHUNK)
            ia = dr * nh + hd
            ib = 2 * nh + dr * nh + hd
            g_c = g_cols[dr][rows, ia:ia + 1]
            g_r = g_rows[dr][ia:ia + 1, rows]
            be_c = small[dr][rows, ib:ib + 1]
            decay[u] = jnp.exp(jnp.minimum(g_c - g_r, 0.0))
            x[u] = jnp.where(_tri(CHUNK, dr == 1, True), be_c * kk[u] * decay[u], 0.0)

    def other_operands():
        for u in units:
            dr, c, hd = u
            rows = slice(c * CHUNK, (c + 1) * CHUNK)
            ia = dr * nh + hd
            ib = 2 * nh + dr * nh + hd
            last = c * CHUNK + (0 if dr == 1 else CHUNK - 1)
            g_c = g_cols[dr][rows, ia:ia + 1]
            be_c = small[dr][rows, ib:ib + 1]
            g_last = g_cols[dr][last:last + 1, ia:ia + 1]
            aqk[u] = jnp.where(_tri(CHUNK, dr == 1, False), qk[u] * decay[u], 0.0).astype(BF16)
            eg = jnp.exp(g_c)
            qh = refs[dr][0][bi, rows, hd * dk:(hd + 1) * dk].astype(F32)
            khf = refs[dr][1][bi, rows, hd * dk:(hd + 1) * dk].astype(F32)
            vh = refs[dr][2][bi, rows, hd * dv:(hd + 1) * dv].astype(F32)
            rhs[u] = jnp.concatenate([(be_c * vh).astype(BF16), ((be_c * eg) * khf).astype(BF16)], axis=1)
            q_dec[u] = qh * eg
            k_dec[u] = khf * jnp.exp(g_last - g_c)
            a_last[u] = jnp.exp(g_last)

    def inverse_start():
        for u in units:
            t[u] = jnp.where(eye, 1.0, jnp.where(_couple(CHUNK, 1, u[0] == 1), -x[u], 0.0))

    def inverse_level(s):
        a = {u: jnp.where(_couple(CHUNK, s, u[0] == 1), x[u], 0.0).astype(BF16) for u in units}
        tb16 = {u: t[u].astype(BF16) for u in units}
        ta = {u: _dot(tb16[u], a[u]).astype(BF16) for u in units}
        for u in units:
            t[u] = t[u] - _dot(ta[u], tb16[u])

    levels = []
    s = 2
    while s < CHUNK:
        levels.append(functools.partial(inverse_level, s))
        s *= 2

    def solve():
        for u in units:
            uw[u] = _dot(t[u].astype(BF16), rhs[u]).astype(BF16)

    def fold():
        for u in units:
            both = _dot(jnp.concatenate([k_dec[u].T.astype(BF16), aqk[u]], axis=0), uw[u])
            kd_uw[u], aq_uw[u] = both[:dk], both[dk:]
        for u in units:
            lhs[u] = jnp.concatenate([(-kd_uw[u][:, dv:]).astype(BF16),
                                      (q_dec[u] - aq_uw[u][:, dv:]).astype(BF16)], axis=0)

    def recurrence():
        st = {ch: st_ref[bi, ch[0], ch[1]] for ch in chains}
        for ci in range(nchunk):
            for ch in chains:
                dr, hd = ch
                c = nchunk - 1 - ci if dr == 1 else ci
                u = (dr, c, hd)
                rows = slice(c * CHUNK, (c + 1) * CHUNK)
                res = _dot(lhs[u], st[ch].astype(BF16))
                refs[dr][5][bi, rows, hd * dv:(hd + 1) * dv] = (res[dk:] + aq_uw[u][:, :dv]).astype(BF16)
                st[ch] = st[ch] * a_last[u] + res[:dk] + kd_uw[u][:, :dv]
        for ch in chains:
            st_ref[bi, ch[0], ch[1]] = st[ch]

    return [cumulative_decay, grams, solve_operands, other_operands, inverse_start, *levels,
            solve, fold, recurrence]


def _scan_kernel(af, alf, ab, alb, bf, bcf, brf, bb, bcb, brb,
                 oaf, oab, obf, obb, sta_ref, stb_ref, *, dk_a, dv_a, dk_b, dv_b, nh_b):
    @pl.when(pl.program_id(1) == 0)
    def _():
        sta_ref[...] = jnp.zeros_like(sta_ref)
        stb_ref[...] = jnp.zeros_like(stb_ref)

    qk_a, qk_b = GLA_HEADS * dk_a, nh_b * dk_b
    qkv = lambda ref, qk: (_Cols(ref, 0), _Cols(ref, qk), _Cols(ref, 2 * qk))
    refs_a = ((*qkv(af, qk_a), alf, oaf), (*qkv(ab, qk_a), alb, oab))
    refs_b = ((*qkv(bf, qk_b), bcf, brf, obf), (*qkv(bb, qk_b), bcb, brb, obb))
    stages = {}
    for bi in range(af.shape[0]):
        stages["a", bi] = _gla_stages(refs_a, sta_ref, dk_a, dv_a, bi)
        stages["b", bi] = _gdn_stages(refs_b, stb_ref, dk_b, dv_b, nh_b, bi)
    lag = SCAN_ROW_LAG
    for pos in range(len(SCAN_ORDER) + lag * (af.shape[0] - 1)):
        for bi in range(af.shape[0]):
            if 0 <= pos - lag * bi < len(SCAN_ORDER):
                name, idx = SCAN_ORDER[pos - lag * bi]
                stages[name, bi][idx]()


def _scan(gqkv, gla, dqkv, small_c, small_r, dims):
    b, s, _ = gqkv.shape
    qk_a, v_a, _, _, qk_b, v_b, _ = dims
    nb = s // TOKEN_BLOCK
    tb = TOKEN_BLOCK
    ns = small_c.shape[2]
    nr = math.gcd(b, SCAN_ROWS)
    fwd = lambda n, col=0: pl.BlockSpec((nr, tb, n), lambda i, j: (i, j, col))
    bwd = lambda n, col=0: pl.BlockSpec((nr, tb, n), lambda i, j: (i, _bwd_block(j, nb), col))
    rfwd = pl.BlockSpec((nr, 1, ns, tb), lambda i, j: (i, j, 0, 0))
    rbwd = pl.BlockSpec((nr, 1, ns, tb), lambda i, j: (i, _bwd_block(j, nb), 0, 0))
    dk_a, dv_a = qk_a // GLA_HEADS, v_a // GLA_HEADS
    dk_b, dv_b = qk_b // GDN_HEADS, v_b // GDN_HEADS
    out = lambda v: jax.ShapeDtypeStruct((b, s, v), BF16)
    return pl.pallas_call(
        functools.partial(_scan_kernel, dk_a=dk_a, dv_a=dv_a, dk_b=dk_b, dv_b=dv_b, nh_b=GDN_HEADS),
        out_shape=[out(v_a), out(v_a), out(v_b), out(v_b)],
        grid=(b // nr, nb),
        in_specs=[fwd(2 * qk_a + v_a), fwd(qk_a, 0), bwd(2 * qk_a + v_a), bwd(qk_a, 1),
                  fwd(2 * qk_b + v_b), fwd(ns), rfwd, bwd(2 * qk_b + v_b), bwd(ns), rbwd],
        out_specs=[fwd(v_a), bwd(v_a), fwd(v_b), bwd(v_b)],
        scratch_shapes=[pltpu.VMEM((nr, 2, qk_a // LANES, dv_a, LANES), F32),
                        pltpu.VMEM((nr, 2, GDN_HEADS, dk_b, dv_b), F32)],
        compiler_params=pltpu.CompilerParams(dimension_semantics=("parallel", "arbitrary"),
                                             vmem_limit_bytes=VMEM_LIMIT),
        name="scan",
    )(gqkv, gla, gqkv, gla, dqkv, small_c, small_r, dqkv, small_c, small_r)


def _post_kernel(x_ref, mod_ref, *refs, d, nh_a, nh_b, final, nsub):
    mix = [refs[6 * i:6 * i + 6] for i in range(nsub)]
    na_ref, nb_ref, wo_ref, g2_ref, w1_ref, w2_ref, fg_ref, o_ref = refs[6 * nsub:]
    tb = x_ref.shape[1] // nsub
    m = mod_ref[0]
    gt1 = m[:, 2 * d:3 * d]
    sh2, sc2, gt2 = m[:, 3 * d:4 * d], m[:, 4 * d:5 * d], m[:, 5 * d:6 * d]
    slab = 2 * LANES
    x1, hb, act = {}, {}, {}

    def merge_and_project(i):
        af_ref, ab_ref, bf_ref, bb_ref, ga_ref, gb_ref = mix[i]
        acc = None
        row = 0
        for (f_ref, b_ref, g_ref, n_ref, nh) in ((af_ref, ab_ref, ga_ref, na_ref, nh_a),
                                                 (bf_ref, bb_ref, gb_ref, nb_ref, nh_b)):
            width = f_ref.shape[2]
            hv = width // nh
            for c0 in range(0, width, slab):
                o = f_ref[0, :, c0:c0 + slab].astype(F32) + b_ref[0, :, c0:c0 + slab].astype(F32)
                gate = _silu(g_ref[0, :, c0:c0 + slab].astype(F32))
                parts = [(_rms(o[:, k:k + hv]) * n_ref[...] * gate[:, k:k + hv]).astype(BF16)
                         for k in range(0, slab, hv)]
                part = _dot(jnp.concatenate(parts, axis=1), wo_ref[row:row + slab, :])
                acc = part if acc is None else acc + part
                row += slab
        x1[i] = x_ref[0, i * tb:(i + 1) * tb, :] + gt1 * acc
        hb[i] = (_rms(x1[i]) * (g2_ref[...] * (1.0 + sc2)) + sh2).astype(BF16)

    def mlp_up(i):
        a = jnp.maximum(_dot(hb[i], w1_ref[...]), 0.0)
        act[i] = (a * a).astype(BF16)

    def mlp_down(i):
        x2 = x1[i] + gt2 * _dot(act[i], w2_ref[...])
        if final:
            x2 = _rms(x2) * fg_ref[...]
        o_ref[0, i * tb:(i + 1) * tb, :] = x2

    merge_and_project(0)
    for i in range(nsub):
        mlp_up(i)
        if i + 1 < nsub:
            merge_and_project(i + 1)
        mlp_down(i)


def _post(stream, part, mod, layer, final, oaf, oab, obf, obb, gg, dg, params, fg, nh_a, nh_b):
    na, nb_, wo, g2, w1, w2 = params
    ctx_arr, lat_arr = stream
    b, t, d = lat_arr.shape
    tb = TOKEN_BLOCK
    ctx = part == "ctx"
    nsub = 1 if ctx else math.gcd(t // tb, POST_SUB)
    nstep = 1 if ctx else t // (nsub * tb)
    first = 0 if ctx else 1
    lay = lambda a: pl.BlockSpec((None,) + a.shape[1:], lambda i, j: (layer,) + (0,) * (a.ndim - 1))
    weight = lambda a: pl.BlockSpec((None,) + a.shape[1:], lambda i, j: (layer,) + (0,) * (a.ndim - 1),
                                    pipeline_mode=pl.Buffered(1))
    bsz = b
    mod_spec = pl.BlockSpec((1, 1, 6 * d), lambda i, j: (layer * MOD_ROWS + (bsz if ctx else i), 0, 0))
    va, vb = oaf.shape[2], obf.shape[2]
    mix_specs, mix_args = [], []
    for k in range(nsub):
        blk = lambda n, k=k: pl.BlockSpec((1, tb, n), lambda i, j: (i, first + j * nsub + k, 0))
        mix_specs += [blk(va), blk(va), blk(vb), blk(vb), blk(va), blk(vb)]
        mix_args += [oaf, oab, obf, obb, gg, dg]
    return pl.pallas_call(
        functools.partial(_post_kernel, d=d, nh_a=nh_a, nh_b=nh_b, final=final, nsub=nsub),
        out_shape=jax.ShapeDtypeStruct((b, nstep * nsub * tb, d), F32),
        grid=(b, nstep),
        in_specs=[pl.BlockSpec((1, nsub * tb, d), lambda i, j: (i, j, 0)), mod_spec, *mix_specs,
                  lay(na), lay(nb_), weight(wo), lay(g2), weight(w1), weight(w2),
                  pl.BlockSpec((1, d), lambda i, j: (0, 0))],
        out_specs=pl.BlockSpec((1, nsub * tb, d), lambda i, j: (i, j, 0)),
        compiler_params=pltpu.CompilerParams(dimension_semantics=("parallel", "parallel"),
                                             vmem_limit_bytes=VMEM_LIMIT),
        name="post_ctx" if ctx else "post",
    )(ctx_arr if ctx else lat_arr, mod, *mix_args, na, nb_, wo, g2, w1, w2, fg.reshape(1, d))


def kernel(x, c, ctx, c_ctx, w_ada, b_ada, norm1_g, norm2_g, w_in, gla_w_lr, gla_b_lr, gdn_conv_w,
           gdn_a_log, gdn_dt_bias, gla_norm_g, gdn_norm_g, w_out, w_ff1, w_ff2, final_norm_g):
    bsz, t, d = x.shape
    depth = w_ada.shape[0]
    assert ctx.shape[1] == TOKEN_BLOCK and t % TOKEN_BLOCK == 0 and TOKEN_BLOCK % GRID_W == 0
    assert bsz + 1 <= MOD_ROWS
    qk_a = gla_w_lr.shape[3]
    rank = gla_w_lr.shape[2]
    v_a = gla_norm_g.shape[1] * GLA_HEADS
    conv_dim = gdn_conv_w.shape[2]
    v_b = gdn_norm_g.shape[1] * GDN_HEADS
    qk_b = (conv_dim - v_b) // 2
    nd = 2 * GDN_HEADS
    dims = (qk_a, v_a, rank, conv_dim, qk_b, v_b, GDN_HEADS)

    o_r = 2 * qk_a + 2 * v_a
    o_c = o_r + rank
    o_g = o_c + conv_dim
    o_s = o_g + v_b
    assert w_in.shape[2] == o_s + 2 * nd
    w_main = jnp.concatenate([w_in[:, :, :o_r], w_in[:, :, o_c:o_s]], axis=2).astype(BF16)
    w_sm = jnp.concatenate([w_in[:, :, o_r:o_c], w_in[:, :, o_s:]], axis=2)
    w_small = jnp.pad(w_sm, ((0, 0), (0, 0), (0, LANES - w_sm.shape[2]))).astype(BF16)
    w_small_t = jnp.swapaxes(w_in[:, :, o_s:], 1, 2).astype(BF16)

    cc = jnp.concatenate([c, c_ctx[None, :], jnp.zeros((MOD_ROWS - bsz - 1, d), F32)], axis=0)
    mod = _modulation(cc, w_ada, b_ada).reshape(depth * MOD_ROWS, 1, 6 * d)

    in_params = (norm1_g.reshape(depth, 1, d), w_main, w_small, w_small_t, gla_w_lr.astype(BF16),
                 gla_b_lr.reshape(depth, 2, 1, qk_a), gdn_conv_w,
                 gdn_a_log.reshape(depth, 1, nd), gdn_dt_bias.reshape(depth, 1, nd),
                 gdn_a_log.reshape(depth, nd, 1), gdn_dt_bias.reshape(depth, nd, 1))
    post_params = (gla_norm_g.reshape(depth, 1, -1), gdn_norm_g.reshape(depth, 1, -1), w_out.astype(BF16),
                   norm2_g.reshape(depth, 1, d), w_ff1.astype(BF16), w_ff2.astype(BF16))

    stream = (ctx, x)
    for l in range(depth):
        last = l == depth - 1
        gqkv, gg, gla, dqkv, dg, small_c, small_r = _in_proj(stream, mod, l, bsz, in_params, dims)
        oaf, oab, obf, obb = _scan(gqkv, gla, dqkv, small_c, small_r, dims)
        args = (mod, l, last, oaf, oab, obf, obb, gg, dg, post_params, final_norm_g, GLA_HEADS, GDN_HEADS)
        xs = _post(stream, "latent", *args)
        if not last:
            stream = (_post(stream, "ctx", *args), xs)
    return xs
```

```python
import functools
import math

import jax
import jax.numpy as jnp
from jax import lax
from jax.experimental import pallas as pl
from jax.experimental.pallas import tpu as pltpu

EPS = 1e-6
GRID_W = 64
GLA_HEADS = 4
GDN_HEADS = 4
GLA_TAU = 16.0
TOKEN_BLOCK = 256
CHUNK = 64
LANES = 128
SUBLANES = 8
POST_SUB = 2
SCAN_ROWS = 2
IN_ROWS = 4
SCAN_ROW_LAG = 6
MOD_ROWS = 16
VMEM_LIMIT = 56 * 1024 * 1024
SCAN_ORDER = (("b", 0), ("b", 1), ("a", 0), ("a", 1), ("a", 2), ("a", 3), ("b", 2), ("b", 4),
              ("b", 5), ("b", 6), ("a", 4), ("b", 3), ("b", 7), ("b", 8), ("b", 9), ("b", 10), ("b", 11), ("b", 12))

F32 = jnp.float32
BF16 = jnp.bfloat16


def _dot(a, b):
    return jnp.dot(a, b, preferred_element_type=F32)


def _dot_nt(a, b):
    return lax.dot_general(a, b, (((1,), (1,)), ((), ())), preferred_element_type=F32)


def _dot_tn(a, b):
    return lax.dot_general(a, b, (((0,), (0,)), ((), ())), preferred_element_type=F32)


def _silu(x):
    return x / (1.0 + jnp.exp(-x))


def _sigmoid(x):
    return 1.0 / (1.0 + jnp.exp(-x))


def _softplus(x):
    return jnp.maximum(x, 0.0) + jnp.log(1.0 + jnp.exp(-jnp.abs(x)))


def _log_sigmoid(x):
    return jnp.minimum(x, 0.0) - jnp.log(1.0 + jnp.exp(-jnp.abs(x)))


def _rms(x):
    return x * lax.rsqrt(jnp.mean(x * x, axis=-1, keepdims=True) + EPS)


def _cumsum_dot(tri_bf16, x, left):
    hi = x.astype(BF16)
    lo = (x - hi.astype(F32)).astype(BF16)
    if left:
        return _dot(tri_bf16, hi) + _dot(tri_bf16, lo)
    return _dot(hi, tri_bf16) + _dot(lo, tri_bf16)


def _tri(n, upper, strict):
    r = lax.broadcasted_iota(jnp.int32, (n, n), 0)
    c = lax.broadcasted_iota(jnp.int32, (n, n), 1)
    if upper:
        m = (r < c) if strict else (r <= c)
    else:
        m = (r > c) if strict else (r >= c)
    return m


def _couple(n, s, upper):
    r = lax.broadcasted_iota(jnp.int32, (n, n), 0)
    c = lax.broadcasted_iota(jnp.int32, (n, n), 1)
    same = (r & ~(2 * s - 1)) == (c & ~(2 * s - 1))
    r_hi, c_hi = (r & s) != 0, (c & s) != 0
    return same & ((~r_hi & c_hi) if upper else (r_hi & ~c_hi))


def _tri_blocks(n, upper):
    r = lax.broadcasted_iota(jnp.int32, (n, n), 0)
    c = lax.broadcasted_iota(jnp.int32, (n, n), 1)
    same = (r & ~(CHUNK - 1)) == (c & ~(CHUNK - 1))
    return same & ((r <= c) if upper else (r >= c))


def _mod_kernel(cc_ref, w_ref, b_ref, o_ref):
    s = _silu(cc_ref[...]).astype(BF16)
    o_ref[0] = _dot(s, w_ref[0].astype(BF16)) + b_ref[0]


def _modulation(cc, w_ada, b_ada):
    depth, d, d6 = w_ada.shape
    nblk = d6 // d
    return pl.pallas_call(
        _mod_kernel,
        out_shape=jax.ShapeDtypeStruct((depth, MOD_ROWS, d6), F32),
        grid=(depth, nblk),
        in_specs=[
            pl.BlockSpec((MOD_ROWS, d), lambda l, n: (0, 0)),
            pl.BlockSpec((1, d, d), lambda l, n: (l, 0, n)),
            pl.BlockSpec((1, 1, d), lambda l, n: (l, 0, n)),
        ],
        out_specs=pl.BlockSpec((1, MOD_ROWS, d), lambda l, n: (l, 0, n)),
        compiler_params=pltpu.CompilerParams(dimension_semantics=("parallel", "parallel")),
        name="adaln_modulation",
    )(cc, w_ada, b_ada.reshape(depth, 1, d6))


def _in_proj_kernel(c_ref, x_ref, *refs, d, qk_a, v_a, rank, conv_dim, qk_b, v_b, nh_b):
    nr = x_ref.shape[0]
    mod_refs = refs[:nr]
    (g1_ref, wm_ref, ws_ref, wst_ref, wlr_ref, blr_ref, cw_ref, arow_ref, dtrow_ref, acol_ref, dtcol_ref,
     gqkv_ref, gg_ref, gla_ref, dqkv_ref, dg_ref, sc_ref, sr_ref) = refs[nr:]
    for bi in range(nr):
        _in_proj_row(bi, c_ref, x_ref, mod_refs[bi], g1_ref, wm_ref, ws_ref, wst_ref, wlr_ref, blr_ref, cw_ref,
                     arow_ref, dtrow_ref, acol_ref, dtcol_ref, gqkv_ref, gg_ref, gla_ref, dqkv_ref, dg_ref,
                     sc_ref, sr_ref, d=d, qk_a=qk_a, v_a=v_a, rank=rank, conv_dim=conv_dim, qk_b=qk_b, v_b=v_b,
                     nh_b=nh_b)


def _in_proj_row(bi, c_ref, x_ref, mod_ref, g1_ref, wm_ref, ws_ref, wst_ref, wlr_ref, blr_ref, cw_ref,
                 arow_ref, dtrow_ref, acol_ref, dtcol_ref,
                 gqkv_ref, gg_ref, gla_ref, dqkv_ref, dg_ref, sc_ref, sr_ref, *, d, qk_a, v_a, rank, conv_dim,
                 qk_b, v_b, nh_b):
    j = pl.program_id(1)
    x = jnp.where(j == 0, c_ref[bi], x_ref[bi])
    m = mod_ref[0]
    sh1, sc1 = m[:, 0:d], m[:, d:2 * d]
    h = _rms(x) * (g1_ref[...] * (1.0 + sc1)) + sh1
    hb = h.astype(BF16)
    o_conv = 2 * qk_a + 2 * v_a
    o_gate = o_conv + conv_dim
    grp = 2 * LANES
    tb = x.shape[0]
    t = lax.broadcasted_iota(jnp.int32, (tb, 1), 0)
    seg_mask = jnp.where(j == 0, tb - 1, GRID_W - 1)
    first = (t & seg_mask) == 0
    last = (t & seg_mask) == seg_mask
    dkh = qk_b // nh_b

    def conv_dot(k):
        return _dot(hb, wm_ref[:, o_conv + k * grp:o_conv + (k + 1) * grp])

    def zero_rows(a, mask, at):
        pieces = []
        for s0 in range(0, tb, GRID_W):
            r0 = s0 + at
            pieces += [a[s0:r0], jnp.where(mask[r0:r0 + SUBLANES], 0.0, a[r0:r0 + SUBLANES]),
                       a[r0 + SUBLANES:s0 + GRID_W]]
        return jnp.concatenate([p for p in pieces if p.shape[0]], axis=0)

    def conv_group(u, k):
        up = zero_rows(pltpu.roll(u, 1, 0), first, 0)
        un = zero_rows(pltpu.roll(u, tb - 1, 0), last, GRID_W - SUBLANES)
        cw = cw_ref[:, k * grp:(k + 1) * grp]
        s = _silu(cw[0:1] * up + cw[1:2] * u + cw[2:3] * un)
        c0 = k * grp
        if c0 < 2 * qk_b:
            for hh in range(grp // dkh):
                sh = s[:, hh * dkh:(hh + 1) * dkh]
                inv = lax.rsqrt(jnp.sum(sh * sh, axis=-1, keepdims=True) + EPS)
                if c0 < qk_b:
                    inv = inv * (float(dkh) ** -0.5)
                dqkv_ref[bi, :, c0 + hh * dkh:c0 + (hh + 1) * dkh] = (sh * inv).astype(BF16)
        else:
            dqkv_ref[bi, :, c0:c0 + grp] = s.astype(BF16)

    def light_group(k):
        c0 = k * grp
        w0 = c0 if c0 < o_conv else o_gate + (c0 - o_conv)
        pa = _dot(hb, wm_ref[:, w0:w0 + grp])
        if c0 < qk_a:
            gqkv_ref[bi, :, c0:c0 + grp] = (pa * (float(qk_a // GLA_HEADS) ** -0.5)).astype(BF16)
        elif c0 < 2 * qk_a + v_a:
            gqkv_ref[bi, :, c0:c0 + grp] = pa.astype(BF16)
        elif c0 < o_conv:
            gg_ref[bi, :, c0 - 2 * qk_a - v_a:c0 - 2 * qk_a - v_a + grp] = pa.astype(BF16)
        else:
            dg_ref[bi, :, c0 - o_conv:c0 - o_conv + grp] = pa.astype(BF16)

    def small_group(ps, pst):
        r_a = ps[:, 0:rank].astype(BF16)
        for dr in range(2):
            lr = _dot(r_a, wlr_ref[dr]) + blr_ref[dr]
            gla_ref[bi, :, dr * qk_a:(dr + 1) * qk_a] = _log_sigmoid(lr) * (1.0 / GLA_TAU)
        nd = 2 * nh_b
        a_c = ps[:, rank:rank + nd]
        b_c = ps[:, rank + nd:rank + 2 * nd]
        sc_ref[bi, :, 0:nd] = -jnp.exp(arow_ref[...]) * _softplus(a_c + dtrow_ref[...])
        sc_ref[bi, :, nd:2 * nd] = _sigmoid(b_c)
        sr_ref[bi, 0, 0:nd, :] = -jnp.exp(acol_ref[...]) * _softplus(pst[0:nd] + dtcol_ref[...])
        sr_ref[bi, 0, nd:2 * nd, :] = _sigmoid(pst[nd:2 * nd])

    n_conv = conv_dim // grp
    n_light = (o_conv + v_b) // grp
    us = {0: conv_dot(0)}
    ps = _dot(hb, ws_ref[...])
    pst = _dot_nt(wst_ref[...], hb)
    if n_conv > 1:
        us[1] = conv_dot(1)
    g_next = 0
    for k in range(n_conv):
        if k + 2 < n_conv:
            us[k + 2] = conv_dot(k + 2)
        conv_group(us.pop(k), k)
        if k == 1 or n_conv == 1:
            small_group(ps, pst)
        todo = (n_light - g_next + (n_conv - k) - 1) // (n_conv - k) if k >= n_conv // 2 else 1
        for _ in range(min(todo, n_light - g_next)):
            light_group(g_next)
            g_next += 1
    while g_next < n_light:
        light_group(g_next)
        g_next += 1


def _in_proj(stream, mod, layer, bsz, params, dims):
    g1, wm, ws, wst, wlr, blr, cw, a_row, dt_row, a_col, dt_col = params
    head, body = stream
    b, _, d = body.shape
    nb = body.shape[1] // TOKEN_BLOCK + 1
    s = nb * TOKEN_BLOCK
    qk_a, v_a, rank, conv_dim, qk_b, v_b, nh_b = dims
    nd = 2 * nh_b
    tb = TOKEN_BLOCK
    lay = lambda a: pl.BlockSpec((None,) + a.shape[1:], lambda i, j: (layer,) + (0,) * (a.ndim - 1))
    nr = math.gcd(b, IN_ROWS)
    tok = lambda n: pl.BlockSpec((nr, tb, n), lambda i, j: (i, j, 0))
    mod_specs = [pl.BlockSpec((1, 1, 6 * d), lambda i, j, k=k: (
        layer * MOD_ROWS + jnp.where(j == 0, bsz, i * nr + k), 0, 0)) for k in range(nr)]
    out_shape = [
        jax.ShapeDtypeStruct((b, s, 2 * qk_a + v_a), BF16), jax.ShapeDtypeStruct((b, s, v_a), BF16),
        jax.ShapeDtypeStruct((b, s, 2 * qk_a), F32),
        jax.ShapeDtypeStruct((b, s, conv_dim), BF16), jax.ShapeDtypeStruct((b, s, v_b), BF16),
        jax.ShapeDtypeStruct((b, s, 2 * nd), F32),
        jax.ShapeDtypeStruct((b, nb, 2 * nd, tb), F32),
    ]
    out_specs = [tok(2 * qk_a + v_a), tok(v_a), tok(2 * qk_a), tok(conv_dim), tok(v_b), tok(2 * nd),
                 pl.BlockSpec((nr, 1, 2 * nd, tb), lambda i, j: (i, j, 0, 0))]
    kern = functools.partial(_in_proj_kernel, d=d, qk_a=qk_a, v_a=v_a, rank=rank, conv_dim=conv_dim,
                             qk_b=qk_b, v_b=v_b, nh_b=nh_b)
    return pl.pallas_call(
        kern, out_shape=out_shape, grid=(b // nr, nb),
        in_specs=[
            pl.BlockSpec((nr, tb, d), lambda i, j: (i, 0, 0)),
            pl.BlockSpec((nr, tb, d), lambda i, j: (i, jnp.maximum(j - 1, 0), 0)), *mod_specs, lay(g1),
            lay(wm), lay(ws), lay(wst), lay(wlr), lay(blr), lay(cw), lay(a_row), lay(dt_row), lay(a_col), lay(dt_col),
        ],
        out_specs=out_specs,
        compiler_params=pltpu.CompilerParams(dimension_semantics=("parallel", "parallel"),
                                             vmem_limit_bytes=VMEM_LIMIT),
        name="in_proj",
    )(head, body, *([mod] * nr), g1, wm, ws, wst, wlr, blr, cw, a_row, dt_row, a_col, dt_col)


def _bwd_block(j, nb):
    return jnp.where(j == 0, 0, nb - j)


class _Cols:
    def __init__(self, ref, off):
        self.ref, self.off = ref, off

    def __getitem__(self, idx):
        z, rows, cols = idx
        return self.ref[z, rows, self.off + cols.start:self.off + cols.stop]


def _gla_stages(refs, st_ref, dk, dv, bi):
    tb = refs[0][3].shape[1]
    nchunk = tb // CHUNK
    npair = (GLA_HEADS * dk) // LANES
    hpp = LANES // dk
    lane = lax.broadcasted_iota(jnp.int32, (1, LANES), 1)
    lms = [(lane >= hh * dk) & (lane < (hh + 1) * dk) for hh in range(hpp)]
    mid = CHUNK // 2
    chains = [(dr, pr) for dr in range(2) for pr in range(npair)]
    units = [(dr, pr, c) for (dr, pr) in chains for c in range(nchunk)]
    bcum, q_dec, k_hat, q_mid, k_mid, a_last, sc, intra, dst = ({} for _ in range(9))

    def cumulative_decay():
        for dr in range(2):
            both = _cumsum_dot(_tri_blocks(tb, dr == 1).astype(BF16), refs[dr][3][bi], True)
            for pr in range(npair):
                bcum[dr, pr] = both[:, pr * LANES:(pr + 1) * LANES]

    def decayed_operands():
        for u in units:
            dr, pr, c = u
            rows = slice(c * CHUNK, (c + 1) * CHUNK)
            lanes = slice(pr * LANES, (pr + 1) * LANES)
            last = c * CHUNK + (0 if dr == 1 else CHUNK - 1)
            b = bcum[dr, pr][rows]
            b_last = bcum[dr, pr][last:last + 1]
            b_mid = bcum[dr, pr][c * CHUNK + mid:c * CHUNK + mid + 1]
            qc = refs[dr][0][bi, rows, lanes].astype(F32)
            kc = refs[dr][1][bi, rows, lanes].astype(F32)
            q_dec[u] = qc * jnp.exp(b)
            k_hat[u] = (kc * jnp.exp(b_last - b)).astype(BF16)
            q_mid[u] = qc * jnp.exp(b - b_mid)
            k_mid[u] = (kc * jnp.exp(b_mid - b)).astype(BF16)
            a_last[u] = jnp.exp(b_last)

    def per_head(m):
        return jnp.concatenate([jnp.where(lm, m, 0.0).astype(BF16) for lm in lms], axis=0)

    def scores():
        for u in units:
            sc[u] = _dot_nt(per_head(q_mid[u]), k_mid[u])

    def intra_and_increments():
        for u in units:
            dr, pr, c = u
            rows = slice(c * CHUNK, (c + 1) * CHUNK)
            causal = _tri(CHUNK, dr == 1, False)
            v_grp = refs[dr][2][bi, rows, pr * hpp * dv:(pr + 1) * hpp * dv]
            dh = _dot_tn(v_grp, k_hat[u])
            acc = None
            for hh in range(hpp):
                pm = jnp.where(causal, sc[u][hh * CHUNK:(hh + 1) * CHUNK], 0.0).astype(BF16)
                intra[u, hh] = _dot(pm, v_grp[:, hh * dv:(hh + 1) * dv])
                part = dh[hh * dv:(hh + 1) * dv]
                acc = part if acc is None else jnp.where(lms[hh], part, acc)
            dst[u] = acc

    def recurrence():
        st = {ch: st_ref[bi, ch[0], ch[1]] for ch in chains}
        for ci in range(nchunk):
            for ch in chains:
                dr, pr = ch
                c = nchunk - 1 - ci if dr == 1 else ci
                u = (dr, pr, c)
                rows = slice(c * CHUNK, (c + 1) * CHUNK)
                inter = _dot_nt(per_head(q_dec[u]), st[ch].astype(BF16))
                for hh in range(hpp):
                    head = pr * hpp + hh
                    refs[dr][4][bi, rows, head * dv:(head + 1) * dv] = (
                        intra[u, hh] + inter[hh * CHUNK:(hh + 1) * CHUNK]).astype(BF16)
                st[ch] = st[ch] * a_last[u] + dst[u]
        for ch in chains:
            st_ref[bi, ch[0], ch[1]] = st[ch]

    return [cumulative_decay, decayed_operands, scores, intra_and_increments, recurrence]


def _gdn_stages(refs, st_ref, dk, dv, nh, bi):
    tb = refs[0][3].shape[1]
    nchunk = tb // CHUNK
    eye = _tri(CHUNK, False, False) & _tri(CHUNK, True, False)
    chains = [(dr, hd) for dr in range(2) for hd in range(nh)]
    units = [(dr, c, hd) for dr in range(2) for c in range(nchunk) for hd in range(nh)]
    g_cols, g_rows, small, kk, qk, x, aqk, rhs, q_dec, k_dec, a_last = ({} for _ in range(11))
    t, uw, kd_uw, aq_uw, lhs, decay = ({} for _ in range(6))

    def cumulative_decay():
        for dr in range(2):
            small[dr] = refs[dr][3][bi]
            g_cols[dr] = _cumsum_dot(_tri_blocks(tb, dr == 1).astype(BF16), small[dr], True)
            g_rows[dr] = _cumsum_dot(_tri_blocks(tb, dr != 1).astype(BF16), refs[dr][4][bi, 0], False)

    def grams():
        for u in units:
            dr, c, hd = u
            rows = slice(c * CHUNK, (c + 1) * CHUNK)
            kh = refs[dr][1][bi, rows, hd * dk:(hd + 1) * dk]
            qh = refs[dr][0][bi, rows, hd * dk:(hd + 1) * dk]
            kq = _dot_nt(jnp.concatenate([kh, qh], axis=0), kh)
            kk[u], qk[u] = kq[:CHUNK], kq[CHUNK:]

    def solve_operands():
        for u in units:
            dr, c, hd = u
            rows = slice(c * CHUNK, (c + 1) * CHUNK)
            ia = dr * nh + hd
            ib = 2 * nh + dr * nh + hd
            g_c = g_cols[dr][rows, ia:ia + 1]
            g_r = g_rows[dr][ia:ia + 1, rows]
            be_c = small[dr][rows, ib:ib + 1]
            decay[u] = jnp.exp(jnp.minimum(g_c - g_r, 0.0))
            x[u] = jnp.where(_tri(CHUNK, dr == 1, True), be_c * kk[u] * decay[u], 0.0)

    def other_operands():
        for u in units:
            dr, c, hd = u
            rows = slice(c * CHUNK, (c + 1) * CHUNK)
            ia = dr * nh + hd
            ib = 2 * nh + dr * nh + hd
            last = c * CHUNK + (0 if dr == 1 else CHUNK - 1)
            g_c = g_cols[dr][rows, ia:ia + 1]
            be_c = small[dr][rows, ib:ib + 1]
            g_last = g_cols[dr][last:last + 1, ia:ia + 1]
            aqk[u] = jnp.where(_tri(CHUNK, dr == 1, False), qk[u] * decay[u], 0.0).astype(BF16)
            eg = jnp.exp(g_c)
            qh = refs[dr][0][bi, rows, hd * dk:(hd + 1) * dk].astype(F32)
            khf = refs[dr][1][bi, rows, hd * dk:(hd + 1) * dk].astype(F32)
            vh = refs[dr][2][bi, rows, hd * dv:(hd + 1) * dv].astype(F32)
            rhs[u] = jnp.concatenate([(be_c * vh).astype(BF16), ((be_c * eg) * khf).astype(BF16)], axis=1)
            q_dec[u] = qh * eg
            k_dec[u] = khf * jnp.exp(g_last - g_c)
            a_last[u] = jnp.exp(g_last)

    def inverse_start():
        for u in units:
            t[u] = jnp.where(eye, 1.0, jnp.where(_couple(CHUNK, 1, u[0] == 1), -x[u], 0.0))

    def inverse_level(s):
        a = {u: jnp.where(_couple(CHUNK, s, u[0] == 1), x[u], 0.0).astype(BF16) for u in units}
        tb16 = {u: t[u].astype(BF16) for u in units}
        ta = {u: _dot(tb16[u], a[u]).astype(BF16) for u in units}
        for u in units:
            t[u] = t[u] - _dot(ta[u], tb16[u])

    levels = []
    s = 2
    while s < CHUNK:
        levels.append(functools.partial(inverse_level, s))
        s *= 2

    def solve():
        for u in units:
            uw[u] = _dot(t[u].astype(BF16), rhs[u]).astype(BF16)

    def fold():
        for u in units:
            both = _dot(jnp.concatenate([k_dec[u].T.astype(BF16), aqk[u]], axis=0), uw[u])
            kd_uw[u], aq_uw[u] = both[:dk], both[dk:]
        for u in units:
            lhs[u] = jnp.concatenate([(-kd_uw[u][:, dv:]).astype(BF16),
                                      (q_dec[u] - aq_uw[u][:, dv:]).astype(BF16)], axis=0)

    def recurrence():
        st = {ch: st_ref[bi, ch[0], ch[1]] for ch in chains}
        for ci in range(nchunk):
            for ch in chains:
                dr, hd = ch
                c = nchunk - 1 - ci if dr == 1 else ci
                u = (dr, c, hd)
                rows = slice(c * CHUNK, (c + 1) * CHUNK)
                res = _dot(lhs[u], st[ch].astype(BF16))
                refs[dr][5][bi, rows, hd * dv:(hd + 1) * dv] = (res[dk:] + aq_uw[u][:, :dv]).astype(BF16)
                st[ch] = st[ch] * a_last[u] + res[:dk] + kd_uw[u][:, :dv]
        for ch in chains:
            st_ref[bi, ch[0], ch[1]] = st[ch]

    return [cumulative_decay, grams, solve_operands, other_operands, inverse_start, *levels,
            solve, fold, recurrence]


def _scan_kernel(af, alf, ab, alb, bf, bcf, brf, bb, bcb, brb,
                 oaf, oab, obf, obb, sta_ref, stb_ref, *, dk_a, dv_a, dk_b, dv_b, nh_b):
    @pl.when(pl.program_id(1) == 0)
    def _():
        sta_ref[...] = jnp.zeros_like(sta_ref)
        stb_ref[...] = jnp.zeros_like(stb_ref)

    qk_a, qk_b = GLA_HEADS * dk_a, nh_b * dk_b
    qkv = lambda ref, qk: (_Cols(ref, 0), _Cols(ref, qk), _Cols(ref, 2 * qk))
    refs_a = ((*qkv(af, qk_a), alf, oaf), (*qkv(ab, qk_a), alb, oab))
    refs_b = ((*qkv(bf, qk_b), bcf, brf, obf), (*qkv(bb, qk_b), bcb, brb, obb))
    stages = {}
    for bi in range(af.shape[0]):
        stages["a", bi] = _gla_stages(refs_a, sta_ref, dk_a, dv_a, bi)
        stages["b", bi] = _gdn_stages(refs_b, stb_ref, dk_b, dv_b, nh_b, bi)
    lag = SCAN_ROW_LAG
    for pos in range(len(SCAN_ORDER) + lag * (af.shape[0] - 1)):
        for bi in range(af.shape[0]):
            if 0 <= pos - lag * bi < len(SCAN_ORDER):
                name, idx = SCAN_ORDER[pos - lag * bi]
                stages[name, bi][idx]()


def _scan(gqkv, gla, dqkv, small_c, small_r, dims):
    b, s, _ = gqkv.shape
    qk_a, v_a, _, _, qk_b, v_b, _ = dims
    nb = s // TOKEN_BLOCK
    tb = TOKEN_BLOCK
    ns = small_c.shape[2]
    nr = math.gcd(b, SCAN_ROWS)
    fwd = lambda n, col=0: pl.BlockSpec((nr, tb, n), lambda i, j: (i, j, col))
    bwd = lambda n, col=0: pl.BlockSpec((nr, tb, n), lambda i, j: (i, _bwd_block(j, nb), col))
    rfwd = pl.BlockSpec((nr, 1, ns, tb), lambda i, j: (i, j, 0, 0))
    rbwd = pl.BlockSpec((nr, 1, ns, tb), lambda i, j: (i, _bwd_block(j, nb), 0, 0))
    dk_a, dv_a = qk_a // GLA_HEADS, v_a // GLA_HEADS
    dk_b, dv_b = qk_b // GDN_HEADS, v_b // GDN_HEADS
    out = lambda v: jax.ShapeDtypeStruct((b, s, v), BF16)
    return pl.pallas_call(
        functools.partial(_scan_kernel, dk_a=dk_a, dv_a=dv_a, dk_b=dk_b, dv_b=dv_b, nh_b=GDN_HEADS),
        out_shape=[out(v_a), out(v_a), out(v_b), out(v_b)],
        grid=(b // nr, nb),
        in_specs=[fwd(2 * qk_a + v_a), fwd(qk_a, 0), bwd(2 * qk_a + v_a), bwd(qk_a, 1),
                  fwd(2 * qk_b + v_b), fwd(ns), rfwd, bwd(2 * qk_b + v_b), bwd(ns), rbwd],
        out_specs=[fwd(v_a), bwd(v_a), fwd(v_b), bwd(v_b)],
        scratch_shapes=[pltpu.VMEM((nr, 2, qk_a // LANES, dv_a, LANES), F32),
                        pltpu.VMEM((nr, 2, GDN_HEADS, dk_b, dv_b), F32)],
        compiler_params=pltpu.CompilerParams(dimension_semantics=("parallel", "arbitrary"),
                                             vmem_limit_bytes=VMEM_LIMIT),
        name="scan",
    )(gqkv, gla, gqkv, gla, dqkv, small_c, small_r, dqkv, small_c, small_r)


def _post_kernel(x_ref, mod_ref, *refs, d, nh_a, nh_b, final, nsub):
    mix = [refs[6 * i:6 * i + 6] for i in range(nsub)]
    na_ref, nb_ref, wo_ref, g2_ref, w1_ref, w2_ref, fg_ref, o_ref = refs[6 * nsub:]
    tb = x_ref.shape[1] // nsub
    m = mod_ref[0]
    gt1 = m[:, 2 * d:3 * d]
    sh2, sc2, gt2 = m[:, 3 * d:4 * d], m[:, 4 * d:5 * d], m[:, 5 * d:6 * d]
    slab = 2 * LANES
    x1, hb, act = {}, {}, {}

    def merge_and_project(i):
        af_ref, ab_ref, bf_ref, bb_ref, ga_ref, gb_ref = mix[i]
        acc = None
        row = 0
        for (f_ref, b_ref, g_ref, n_ref, nh) in ((af_ref, ab_ref, ga_ref, na_ref, nh_a),
                                                 (bf_ref, bb_ref, gb_ref, nb_ref, nh_b)):
            width = f_ref.shape[2]
            hv = width // nh
            for c0 in range(0, width, slab):
                o = f_ref[0, :, c0:c0 + slab].astype(F32) + b_ref[0, :, c0:c0 + slab].astype(F32)
                gate = _silu(g_ref[0, :, c0:c0 + slab].astype(F32))
                parts = [(_rms(o[:, k:k + hv]) * n_ref[...] * gate[:, k:k + hv]).astype(BF16)
                         for k in range(0, slab, hv)]
                part = _dot(jnp.concatenate(parts, axis=1), wo_ref[row:row + slab, :])
                acc = part if acc is None else acc + part
                row += slab
        x1[i] = x_ref[0, i * tb:(i + 1) * tb, :] + gt1 * acc
        hb[i] = (_rms(x1[i]) * (g2_ref[...] * (1.0 + sc2)) + sh2).astype(BF16)

    def mlp_up(i):
        a = jnp.maximum(_dot(hb[i], w1_ref[...]), 0.0)
        act[i] = (a * a).astype(BF16)

    def mlp_down(i):
        x2 = x1[i] + gt2 * _dot(act[i], w2_ref[...])
        if final:
            x2 = _rms(x2) * fg_ref[...]
        o_ref[0, i * tb:(i + 1) * tb, :] = x2

    merge_and_project(0)
    for i in range(nsub):
        mlp_up(i)
        if i + 1 < nsub:
            merge_and_project(i + 1)
        mlp_down(i)


def _post(stream, part, mod, layer, final, oaf, oab, obf, obb, gg, dg, params, fg, nh_a, nh_b):
    na, nb_, wo, g2, w1, w2 = params
    ctx_arr, lat_arr = stream
    b, t, d = lat_arr.shape
    tb = TOKEN_BLOCK
    ctx = part == "ctx"
    nsub = 1 if ctx else math.gcd(t // tb, POST_SUB)
    nstep = 1 if ctx else t // (nsub * tb)
    first = 0 if ctx else 1
    lay = lambda a: pl.BlockSpec((None,) + a.shape[1:], lambda i, j: (layer,) + (0,) * (a.ndim - 1))
    weight = lambda a: pl.BlockSpec((None,) + a.shape[1:], lambda i, j: (layer,) + (0,) * (a.ndim - 1),
                                    pipeline_mode=pl.Buffered(1))
    bsz = b
    mod_spec = pl.BlockSpec((1, 1, 6 * d), lambda i, j: (layer * MOD_ROWS + (bsz if ctx else i), 0, 0))
    va, vb = oaf.shape[2], obf.shape[2]
    mix_specs, mix_args = [], []
    for k in range(nsub):
        blk = lambda n, k=k: pl.BlockSpec((1, tb, n), lambda i, j: (i, first + j * nsub + k, 0))
        mix_specs += [blk(va), blk(va), blk(vb), blk(vb), blk(va), blk(vb)]
        mix_args += [oaf, oab, obf, obb, gg, dg]
    return pl.pallas_call(
        functools.partial(_post_kernel, d=d, nh_a=nh_a, nh_b=nh_b, final=final, nsub=nsub),
        out_shape=jax.ShapeDtypeStruct((b, nstep * nsub * tb, d), F32),
        grid=(b, nstep),
        in_specs=[pl.BlockSpec((1, nsub * tb, d), lambda i, j: (i, j, 0)), mod_spec, *mix_specs,
                  lay(na), lay(nb_), weight(wo), lay(g2), weight(w1), weight(w2),
                  pl.BlockSpec((1, d), lambda i, j: (0, 0))],
        out_specs=pl.BlockSpec((1, nsub * tb, d), lambda i, j: (i, j, 0)),
        compiler_params=pltpu.CompilerParams(dimension_semantics=("parallel", "parallel"),
                                             vmem_limit_bytes=VMEM_LIMIT),
        name="post_ctx" if ctx else "post",
    )(ctx_arr if ctx else lat_arr, mod, *mix_args, na, nb_, wo, g2, w1, w2, fg.reshape(1, d))


def kernel(x, c, ctx, c_ctx, w_ada, b_ada, norm1_g, norm2_g, w_in, gla_w_lr, gla_b_lr, gdn_conv_w,
           gdn_a_log, gdn_dt_bias, gla_norm_g, gdn_norm_g, w_out, w_ff1, w_ff2, final_norm_g):
    bsz, t, d = x.shape
    depth = w_ada.shape[0]
    assert ctx.shape[1] == TOKEN_BLOCK and t % TOKEN_BLOCK == 0 and TOKEN_BLOCK % GRID_W == 0
    assert bsz + 1 <= MOD_ROWS
    qk_a = gla_w_lr.shape[3]
    rank = gla_w_lr.shape[2]
    v_a = gla_norm_g.shape[1] * GLA_HEADS
    conv_dim = gdn_conv_w.shape[2]
    v_b = gdn_norm_g.shape[1] * GDN_HEADS
    qk_b = (conv_dim - v_b) // 2
    nd = 2 * GDN_HEADS
    dims = (qk_a, v_a, rank, conv_dim, qk_b, v_b, GDN_HEADS)

    o_r = 2 * qk_a + 2 * v_a
    o_c = o_r + rank
    o_g = o_c + conv_dim
    o_s = o_g + v_b
    assert w_in.shape[2] == o_s + 2 * nd
    w_main = jnp.concatenate([w_in[:, :, :o_r], w_in[:, :, o_c:o_s]], axis=2).astype(BF16)
    w_sm = jnp.concatenate([w_in[:, :, o_r:o_c], w_in[:, :, o_s:]], axis=2)
    w_small = jnp.pad(w_sm, ((0, 0), (0, 0), (0, LANES - w_sm.shape[2]))).astype(BF16)
    w_small_t = jnp.swapaxes(w_in[:, :, o_s:], 1, 2).astype(BF16)

    cc = jnp.concatenate([c, c_ctx[None, :], jnp.zeros((MOD_ROWS - bsz - 1, d), F32)], axis=0)
    mod = _modulation(cc, w_ada, b_ada).reshape(depth * MOD_ROWS, 1, 6 * d)

    in_params = (norm1_g.reshape(depth, 1, d), w_main, w_small, w_small_t, gla_w_lr.astype(BF16),
                 gla_b_lr.reshape(depth, 2, 1, qk_a), gdn_conv_w,
                 gdn_a_log.reshape(depth, 1, nd), gdn_dt_bias.reshape(depth, 1, nd),
                 gdn_a_log.reshape(depth, nd, 1), gdn_dt_bias.reshape(depth, nd, 1))
    post_params = (gla_norm_g.reshape(depth, 1, -1), gdn_norm_g.reshape(depth, 1, -1), w_out.astype(BF16),
                   norm2_g.reshape(depth, 1, d), w_ff1.astype(BF16), w_ff2.astype(BF16))

    stream = (ctx, x)
    for l in range(depth):
        last = l == depth - 1
        gqkv, gg, gla, dqkv, dg, small_c, small_r = _in_proj(stream, mod, l, bsz, in_params, dims)
        oaf, oab, obf, obb = _scan(gqkv, gla, dqkv, small_c, small_r, dims)
        args = (mod, l, last, oaf, oab, obf, obb, gg, dg, post_params, final_norm_g, GLA_HEADS, GDN_HEADS)
        xs = _post(stream, "latent", *args)
        if not last:
            stream = (_post(stream, "ctx", *args), xs)
    return xs
```

```python
import functools
import math

import jax
import jax.numpy as jnp
from jax import lax
from jax.experimental import pallas as pl
from jax.experimental.pallas import tpu as pltpu

EPS = 1e-6
GRID_W = 64
GLA_HEADS = 4
GDN_HEADS = 4
GLA_TAU = 16.0
TOKEN_BLOCK = 256
CHUNK = 64
LANES = 128
SUBLANES = 8
POST_SUB = 2
SCAN_ROWS = 2
IN_ROWS = 4
SCAN_ROW_LAG = 5
MOD_ROWS = 16
VMEM_LIMIT = 56 * 1024 * 1024
SCAN_ORDER = (("b", 0), ("b", 1), ("a", 0), ("a", 1), ("a", 2), ("a", 3), ("b", 2), ("b", 3), ("b", 4),
              ("b", 5), ("b", 6), ("a", 4), ("b", 7), ("b", 8), ("b", 9), ("b", 10), ("b", 11), ("b", 12))

F32 = jnp.float32
BF16 = jnp.bfloat16


def _dot(a, b):
    return jnp.dot(a, b, preferred_element_type=F32)


def _dot_nt(a, b):
    return lax.dot_general(a, b, (((1,), (1,)), ((), ())), preferred_element_type=F32)


def _dot_tn(a, b):
    return lax.dot_general(a, b, (((0,), (0,)), ((), ())), preferred_element_type=F32)


def _silu(x):
    return x / (1.0 + jnp.exp(-x))


def _sigmoid(x):
    return 1.0 / (1.0 + jnp.exp(-x))


def _softplus(x):
    return jnp.maximum(x, 0.0) + jnp.log(1.0 + jnp.exp(-jnp.abs(x)))


def _log_sigmoid(x):
    return jnp.minimum(x, 0.0) - jnp.log(1.0 + jnp.exp(-jnp.abs(x)))


def _rms(x):
    return x * lax.rsqrt(jnp.mean(x * x, axis=-1, keepdims=True) + EPS)


def _cumsum_dot(tri_bf16, x, left):
    hi = x.astype(BF16)
    lo = (x - hi.astype(F32)).astype(BF16)
    if left:
        return _dot(tri_bf16, hi) + _dot(tri_bf16, lo)
    return _dot(hi, tri_bf16) + _dot(lo, tri_bf16)


def _tri(n, upper, strict):
    r = lax.broadcasted_iota(jnp.int32, (n, n), 0)
    c = lax.broadcasted_iota(jnp.int32, (n, n), 1)
    if upper:
        m = (r < c) if strict else (r <= c)
    else:
        m = (r > c) if strict else (r >= c)
    return m


def _couple(n, s, upper):
    r = lax.broadcasted_iota(jnp.int32, (n, n), 0)
    c = lax.broadcasted_iota(jnp.int32, (n, n), 1)
    same = (r & ~(2 * s - 1)) == (c & ~(2 * s - 1))
    r_hi, c_hi = (r & s) != 0, (c & s) != 0
    return same & ((~r_hi & c_hi) if upper else (r_hi & ~c_hi))


def _tri_blocks(n, upper):
    r = lax.broadcasted_iota(jnp.int32, (n, n), 0)
    c = lax.broadcasted_iota(jnp.int32, (n, n), 1)
    same = (r & ~(CHUNK - 1)) == (c & ~(CHUNK - 1))
    return same & ((r <= c) if upper else (r >= c))


def _mod_kernel(cc_ref, w_ref, b_ref, o_ref):
    s = _silu(cc_ref[...]).astype(BF16)
    o_ref[0] = _dot(s, w_ref[0].astype(BF16)) + b_ref[0]


def _modulation(cc, w_ada, b_ada):
    depth, d, d6 = w_ada.shape
    nblk = d6 // d
    return pl.pallas_call(
        _mod_kernel,
        out_shape=jax.ShapeDtypeStruct((depth, MOD_ROWS, d6), F32),
        grid=(depth, nblk),
        in_specs=[
            pl.BlockSpec((MOD_ROWS, d), lambda l, n: (0, 0)),
            pl.BlockSpec((1, d, d), lambda l, n: (l, 0, n)),
            pl.BlockSpec((1, 1, d), lambda l, n: (l, 0, n)),
        ],
        out_specs=pl.BlockSpec((1, MOD_ROWS, d), lambda l, n: (l, 0, n)),
        compiler_params=pltpu.CompilerParams(dimension_semantics=("parallel", "parallel")),
        name="adaln_modulation",
    )(cc, w_ada, b_ada.reshape(depth, 1, d6))


def _in_proj_kernel(c_ref, x_ref, *refs, d, qk_a, v_a, rank, conv_dim, qk_b, v_b, nh_b):
    nr = x_ref.shape[0]
    mod_refs = refs[:nr]
    (g1_ref, wm_ref, ws_ref, wst_ref, wlr_ref, blr_ref, cw_ref, arow_ref, dtrow_ref, acol_ref, dtcol_ref,
     gqkv_ref, gg_ref, gla_ref, dqkv_ref, dg_ref, sc_ref, sr_ref) = refs[nr:]
    for bi in range(nr):
        _in_proj_row(bi, c_ref, x_ref, mod_refs[bi], g1_ref, wm_ref, ws_ref, wst_ref, wlr_ref, blr_ref, cw_ref,
                     arow_ref, dtrow_ref, acol_ref, dtcol_ref, gqkv_ref, gg_ref, gla_ref, dqkv_ref, dg_ref,
                     sc_ref, sr_ref, d=d, qk_a=qk_a, v_a=v_a, rank=rank, conv_dim=conv_dim, qk_b=qk_b, v_b=v_b,
                     nh_b=nh_b)


def _in_proj_row(bi, c_ref, x_ref, mod_ref, g1_ref, wm_ref, ws_ref, wst_ref, wlr_ref, blr_ref, cw_ref,
                 arow_ref, dtrow_ref, acol_ref, dtcol_ref,
                 gqkv_ref, gg_ref, gla_ref, dqkv_ref, dg_ref, sc_ref, sr_ref, *, d, qk_a, v_a, rank, conv_dim,
                 qk_b, v_b, nh_b):
    j = pl.program_id(1)
    x = jnp.where(j == 0, c_ref[bi], x_ref[bi])
    m = mod_ref[0]
    sh1, sc1 = m[:, 0:d], m[:, d:2 * d]
    h = _rms(x) * (g1_ref[...] * (1.0 + sc1)) + sh1
    hb = h.astype(BF16)
    o_conv = 2 * qk_a + 2 * v_a
    o_gate = o_conv + conv_dim
    grp = 2 * LANES
    tb = x.shape[0]
    t = lax.broadcasted_iota(jnp.int32, (tb, 1), 0)
    seg_mask = jnp.where(j == 0, tb - 1, GRID_W - 1)
    first = (t & seg_mask) == 0
    last = (t & seg_mask) == seg_mask
    dkh = qk_b // nh_b

    def conv_dot(k):
        return _dot(hb, wm_ref[:, o_conv + k * grp:o_conv + (k + 1) * grp])

    def zero_rows(a, mask, at):
        pieces = []
        for s0 in range(0, tb, GRID_W):
            r0 = s0 + at
            pieces += [a[s0:r0], jnp.where(mask[r0:r0 + SUBLANES], 0.0, a[r0:r0 + SUBLANES]),
                       a[r0 + SUBLANES:s0 + GRID_W]]
        return jnp.concatenate([p for p in pieces if p.shape[0]], axis=0)

    def conv_group(u, k):
        up = zero_rows(pltpu.roll(u, 1, 0), first, 0)
        un = zero_rows(pltpu.roll(u, tb - 1, 0), last, GRID_W - SUBLANES)
        cw = cw_ref[:, k * grp:(k + 1) * grp]
        s = _silu(cw[0:1] * up + cw[1:2] * u + cw[2:3] * un)
        c0 = k * grp
        if c0 < 2 * qk_b:
            for hh in range(grp // dkh):
                sh = s[:, hh * dkh:(hh + 1) * dkh]
                inv = lax.rsqrt(jnp.sum(sh * sh, axis=-1, keepdims=True) + EPS)
                if c0 < qk_b:
                    inv = inv * (float(dkh) ** -0.5)
                dqkv_ref[bi, :, c0 + hh * dkh:c0 + (hh + 1) * dkh] = (sh * inv).astype(BF16)
        else:
            dqkv_ref[bi, :, c0:c0 + grp] = s.astype(BF16)

    def light_group(k):
        c0 = k * grp
        w0 = c0 if c0 < o_conv else o_gate + (c0 - o_conv)
        pa = _dot(hb, wm_ref[:, w0:w0 + grp])
        if c0 < qk_a:
            gqkv_ref[bi, :, c0:c0 + grp] = (pa * (float(qk_a // GLA_HEADS) ** -0.5)).astype(BF16)
        elif c0 < 2 * qk_a + v_a:
            gqkv_ref[bi, :, c0:c0 + grp] = pa.astype(BF16)
        elif c0 < o_conv:
            gg_ref[bi, :, c0 - 2 * qk_a - v_a:c0 - 2 * qk_a - v_a + grp] = pa.astype(BF16)
        else:
            dg_ref[bi, :, c0 - o_conv:c0 - o_conv + grp] = pa.astype(BF16)

    def small_group(ps, pst):
        r_a = ps[:, 0:rank].astype(BF16)
        for dr in range(2):
            lr = _dot(r_a, wlr_ref[dr]) + blr_ref[dr]
            gla_ref[bi, :, dr * qk_a:(dr + 1) * qk_a] = _log_sigmoid(lr) * (1.0 / GLA_TAU)
        nd = 2 * nh_b
        a_c = ps[:, rank:rank + nd]
        b_c = ps[:, rank + nd:rank + 2 * nd]
        sc_ref[bi, :, 0:nd] = -jnp.exp(arow_ref[...]) * _softplus(a_c + dtrow_ref[...])
        sc_ref[bi, :, nd:2 * nd] = _sigmoid(b_c)
        sr_ref[bi, 0, 0:nd, :] = -jnp.exp(acol_ref[...]) * _softplus(pst[0:nd] + dtcol_ref[...])
        sr_ref[bi, 0, nd:2 * nd, :] = _sigmoid(pst[nd:2 * nd])

    n_conv = conv_dim // grp
    n_light = (o_conv + v_b) // grp
    us = {0: conv_dot(0)}
    ps = _dot(hb, ws_ref[...])
    pst = _dot_nt(wst_ref[...], hb)
    if n_conv > 1:
        us[1] = conv_dot(1)
    g_next = 0
    for k in range(n_conv):
        if k + 2 < n_conv:
            us[k + 2] = conv_dot(k + 2)
        conv_group(us.pop(k), k)
        if k == 1 or n_conv == 1:
            small_group(ps, pst)
        todo = (n_light - g_next + (n_conv - k) - 1) // (n_conv - k) if k >= n_conv // 2 else 1
        for _ in range(min(todo, n_light - g_next)):
            light_group(g_next)
            g_next += 1
    while g_next < n_light:
        light_group(g_next)
        g_next += 1


def _in_proj(stream, mod, layer, bsz, params, dims):
    g1, wm, ws, wst, wlr, blr, cw, a_row, dt_row, a_col, dt_col = params
    head, body = stream
    b, _, d = body.shape
    nb = body.shape[1] // TOKEN_BLOCK + 1
    s = nb * TOKEN_BLOCK
    qk_a, v_a, rank, conv_dim, qk_b, v_b, nh_b = dims
    nd = 2 * nh_b
    tb = TOKEN_BLOCK
    lay = lambda a: pl.BlockSpec((None,) + a.shape[1:], lambda i, j: (layer,) + (0,) * (a.ndim - 1))
    nr = math.gcd(b, IN_ROWS)
    tok = lambda n: pl.BlockSpec((nr, tb, n), lambda i, j: (i, j, 0))
    mod_specs = [pl.BlockSpec((1, 1, 6 * d), lambda i, j, k=k: (
        layer * MOD_ROWS + jnp.where(j == 0, bsz, i * nr + k), 0, 0)) for k in range(nr)]
    out_shape = [
        jax.ShapeDtypeStruct((b, s, 2 * qk_a + v_a), BF16), jax.ShapeDtypeStruct((b, s, v_a), BF16),
        jax.ShapeDtypeStruct((b, s, 2 * qk_a), F32),
        jax.ShapeDtypeStruct((b, s, conv_dim), BF16), jax.ShapeDtypeStruct((b, s, v_b), BF16),
        jax.ShapeDtypeStruct((b, s, 2 * nd), F32),
        jax.ShapeDtypeStruct((b, nb, 2 * nd, tb), F32),
    ]
    out_specs = [tok(2 * qk_a + v_a), tok(v_a), tok(2 * qk_a), tok(conv_dim), tok(v_b), tok(2 * nd),
                 pl.BlockSpec((nr, 1, 2 * nd, tb), lambda i, j: (i, j, 0, 0))]
    kern = functools.partial(_in_proj_kernel, d=d, qk_a=qk_a, v_a=v_a, rank=rank, conv_dim=conv_dim,
                             qk_b=qk_b, v_b=v_b, nh_b=nh_b)
    return pl.pallas_call(
        kern, out_shape=out_shape, grid=(b // nr, nb),
        in_specs=[
            pl.BlockSpec((nr, tb, d), lambda i, j: (i, 0, 0)),
            pl.BlockSpec((nr, tb, d), lambda i, j: (i, jnp.maximum(j - 1, 0), 0)), *mod_specs, lay(g1),
            lay(wm), lay(ws), lay(wst), lay(wlr), lay(blr), lay(cw), lay(a_row), lay(dt_row), lay(a_col), lay(dt_col),
        ],
        out_specs=out_specs,
        compiler_params=pltpu.CompilerParams(dimension_semantics=("parallel", "parallel"),
                                             vmem_limit_bytes=VMEM_LIMIT),
        name="in_proj",
    )(head, body, *([mod] * nr), g1, wm, ws, wst, wlr, blr, cw, a_row, dt_row, a_col, dt_col)


def _bwd_block(j, nb):
    return jnp.where(j == 0, 0, nb - j)


class _Cols:
    def __init__(self, ref, off):
        self.ref, self.off = ref, off

    def __getitem__(self, idx):
        z, rows, cols = idx
        return self.ref[z, rows, self.off + cols.start:self.off + cols.stop]


def _gla_stages(refs, st_ref, dk, dv, bi):
    tb = refs[0][3].shape[1]
    nchunk = tb // CHUNK
    npair = (GLA_HEADS * dk) // LANES
    hpp = LANES // dk
    lane = lax.broadcasted_iota(jnp.int32, (1, LANES), 1)
    lms = [(lane >= hh * dk) & (lane < (hh + 1) * dk) for hh in range(hpp)]
    mid = CHUNK // 2
    chains = [(dr, pr) for dr in range(2) for pr in range(npair)]
    units = [(dr, pr, c) for (dr, pr) in chains for c in range(nchunk)]
    bcum, q_dec, k_hat, q_mid, k_mid, a_last, sc, intra, dst = ({} for _ in range(9))

    def cumulative_decay():
        for dr in range(2):
            both = _cumsum_dot(_tri_blocks(tb, dr == 1).astype(BF16), refs[dr][3][bi], True)
            for pr in range(npair):
                bcum[dr, pr] = both[:, pr * LANES:(pr + 1) * LANES]

    def decayed_operands():
        for u in units:
            dr, pr, c = u
            rows = slice(c * CHUNK, (c + 1) * CHUNK)
            lanes = slice(pr * LANES, (pr + 1) * LANES)
            last = c * CHUNK + (0 if dr == 1 else CHUNK - 1)
            b = bcum[dr, pr][rows]
            b_last = bcum[dr, pr][last:last + 1]
            b_mid = bcum[dr, pr][c * CHUNK + mid:c * CHUNK + mid + 1]
            qc = refs[dr][0][bi, rows, lanes].astype(F32)
            kc = refs[dr][1][bi, rows, lanes].astype(F32)
            q_dec[u] = qc * jnp.exp(b)
            k_hat[u] = (kc * jnp.exp(b_last - b)).astype(BF16)
            q_mid[u] = qc * jnp.exp(b - b_mid)
            k_mid[u] = (kc * jnp.exp(b_mid - b)).astype(BF16)
            a_last[u] = jnp.exp(b_last)

    def per_head(m):
        return jnp.concatenate([jnp.where(lm, m, 0.0).astype(BF16) for lm in lms], axis=0)

    def scores():
        for u in units:
            sc[u] = _dot_nt(per_head(q_mid[u]), k_mid[u])

    def intra_and_increments():
        for u in units:
            dr, pr, c = u
            rows = slice(c * CHUNK, (c + 1) * CHUNK)
            causal = _tri(CHUNK, dr == 1, False)
            v_grp = refs[dr][2][bi, rows, pr * hpp * dv:(pr + 1) * hpp * dv]
            dh = _dot_tn(v_grp, k_hat[u])
            acc = None
            for hh in range(hpp):
                pm = jnp.where(causal, sc[u][hh * CHUNK:(hh + 1) * CHUNK], 0.0).astype(BF16)
                intra[u, hh] = _dot(pm, v_grp[:, hh * dv:(hh + 1) * dv])
                part = dh[hh * dv:(hh + 1) * dv]
                acc = part if acc is None else jnp.where(lms[hh], part, acc)
            dst[u] = acc

    def recurrence():
        st = {ch: st_ref[bi, ch[0], ch[1]] for ch in chains}
        for ci in range(nchunk):
            for ch in chains:
                dr, pr = ch
                c = nchunk - 1 - ci if dr == 1 else ci
                u = (dr, pr, c)
                rows = slice(c * CHUNK, (c + 1) * CHUNK)
                inter = _dot_nt(per_head(q_dec[u]), st[ch].astype(BF16))
                for hh in range(hpp):
                    head = pr * hpp + hh
                    refs[dr][4][bi, rows, head * dv:(head + 1) * dv] = (
                        intra[u, hh] + inter[hh * CHUNK:(hh + 1) * CHUNK]).astype(BF16)
                st[ch] = st[ch] * a_last[u] + dst[u]
        for ch in chains:
            st_ref[bi, ch[0], ch[1]] = st[ch]

    return [cumulative_decay, decayed_operands, scores, intra_and_increments, recurrence]


def _gdn_stages(refs, st_ref, dk, dv, nh, bi):
    tb = refs[0][3].shape[1]
    nchunk = tb // CHUNK
    eye = _tri(CHUNK, False, False) & _tri(CHUNK, True, False)
    chains = [(dr, hd) for dr in range(2) for hd in range(nh)]
    units = [(dr, c, hd) for dr in range(2) for c in range(nchunk) for hd in range(nh)]
    g_cols, g_rows, small, kk, qk, x, aqk, rhs, q_dec, k_dec, a_last = ({} for _ in range(11))
    t, uw, kd_uw, aq_uw, lhs, decay = ({} for _ in range(6))

    def cumulative_decay():
        for dr in range(2):
            small[dr] = refs[dr][3][bi]
            g_cols[dr] = _cumsum_dot(_tri_blocks(tb, dr == 1).astype(BF16), small[dr], True)
            g_rows[dr] = _cumsum_dot(_tri_blocks(tb, dr != 1).astype(BF16), refs[dr][4][bi, 0], False)

    def grams():
        for u in units:
            dr, c, hd = u
            rows = slice(c * CHUNK, (c + 1) * CHUNK)
            kh = refs[dr][1][bi, rows, hd * dk:(hd + 1) * dk]
            qh = refs[dr][0][bi, rows, hd * dk:(hd + 1) * dk]
            kq = _dot_nt(jnp.concatenate([kh, qh], axis=0), kh)
            kk[u], qk[u] = kq[:CHUNK], kq[CHUNK:]

    def solve_operands():
        for u in units:
            dr, c, hd = u
            rows = slice(c * CHUNK, (c + 1) * CHUNK)
            ia = dr * nh + hd
            ib = 2 * nh + dr * nh + hd
            g_c = g_cols[dr][rows, ia:ia + 1]
            g_r = g_rows[dr][ia:ia + 1, rows]
            be_c = small[dr][rows, ib:ib + 1]
            decay[u] = jnp.exp(jnp.minimum(g_c - g_r, 0.0))
            x[u] = jnp.where(_tri(CHUNK, dr == 1, True), be_c * kk[u] * decay[u], 0.0)

    def other_operands():
        for u in units:
            dr, c, hd = u
            rows = slice(c * CHUNK, (c + 1) * CHUNK)
            ia = dr * nh + hd
            ib = 2 * nh + dr * nh + hd
            last = c * CHUNK + (0 if dr == 1 else CHUNK - 1)
            g_c = g_cols[dr][rows, ia:ia + 1]
            be_c = small[dr][rows, ib:ib + 1]
            g_last = g_cols[dr][last:last + 1, ia:ia + 1]
            aqk[u] = jnp.where(_tri(CHUNK, dr == 1, False), qk[u] * decay[u], 0.0).astype(BF16)
            eg = jnp.exp(g_c)
            qh = refs[dr][0][bi, rows, hd * dk:(hd + 1) * dk].astype(F32)
            khf = refs[dr][1][bi, rows, hd * dk:(hd + 1) * dk].astype(F32)
            vh = refs[dr][2][bi, rows, hd * dv:(hd + 1) * dv].astype(F32)
            rhs[u] = jnp.concatenate([(be_c * vh).astype(BF16), ((be_c * eg) * khf).astype(BF16)], axis=1)
            q_dec[u] = qh * eg
            k_dec[u] = khf * jnp.exp(g_last - g_c)
            a_last[u] = jnp.exp(g_last)

    def inverse_start():
        for u in units:
            t[u] = jnp.where(eye, 1.0, jnp.where(_couple(CHUNK, 1, u[0] == 1), -x[u], 0.0))

    def inverse_level(s):
        a = {u: jnp.where(_couple(CHUNK, s, u[0] == 1), x[u], 0.0).astype(BF16) for u in units}
        tb16 = {u: t[u].astype(BF16) for u in units}
        ta = {u: _dot(tb16[u], a[u]).astype(BF16) for u in units}
        for u in units:
            t[u] = t[u] - _dot(ta[u], tb16[u])

    levels = []
    s = 2
    while s < CHUNK:
        levels.append(functools.partial(inverse_level, s))
        s *= 2

    def solve():
        for u in units:
            uw[u] = _dot(t[u].astype(BF16), rhs[u]).astype(BF16)

    def fold():
        for u in units:
            both = _dot(jnp.concatenate([k_dec[u].T.astype(BF16), aqk[u]], axis=0), uw[u])
            kd_uw[u], aq_uw[u] = both[:dk], both[dk:]
        for u in units:
            lhs[u] = jnp.concatenate([(-kd_uw[u][:, dv:]).astype(BF16),
                                      (q_dec[u] - aq_uw[u][:, dv:]).astype(BF16)], axis=0)

    def recurrence():
        st = {ch: st_ref[bi, ch[0], ch[1]] for ch in chains}
        for ci in range(nchunk):
            for ch in chains:
                dr, hd = ch
                c = nchunk - 1 - ci if dr == 1 else ci
                u = (dr, c, hd)
                rows = slice(c * CHUNK, (c + 1) * CHUNK)
                res = _dot(lhs[u], st[ch].astype(BF16))
                refs[dr][5][bi, rows, hd * dv:(hd + 1) * dv] = (res[dk:] + aq_uw[u][:, :dv]).astype(BF16)
                st[ch] = st[ch] * a_last[u] + res[:dk] + kd_uw[u][:, :dv]
        for ch in chains:
            st_ref[bi, ch[0], ch[1]] = st[ch]

    return [cumulative_decay, grams, solve_operands, other_operands, inverse_start, *levels,
            solve, fold, recurrence]


def _scan_kernel(af, alf, ab, alb, bf, bcf, brf, bb, bcb, brb,
                 oaf, oab, obf, obb, sta_ref, stb_ref, *, dk_a, dv_a, dk_b, dv_b, nh_b):
    @pl.when(pl.program_id(1) == 0)
    def _():
        sta_ref[...] = jnp.zeros_like(sta_ref)
        stb_ref[...] = jnp.zeros_like(stb_ref)

    qk_a, qk_b = GLA_HEADS * dk_a, nh_b * dk_b
    qkv = lambda ref, qk: (_Cols(ref, 0), _Cols(ref, qk), _Cols(ref, 2 * qk))
    refs_a = ((*qkv(af, qk_a), alf, oaf), (*qkv(ab, qk_a), alb, oab))
    refs_b = ((*qkv(bf, qk_b), bcf, brf, obf), (*qkv(bb, qk_b), bcb, brb, obb))
    stages = {}
    for bi in range(af.shape[0]):
        stages["a", bi] = _gla_stages(refs_a, sta_ref, dk_a, dv_a, bi)
        stages["b", bi] = _gdn_stages(refs_b, stb_ref, dk_b, dv_b, nh_b, bi)
    lag = SCAN_ROW_LAG
    for pos in range(len(SCAN_ORDER) + lag * (af.shape[0] - 1)):
        for bi in range(af.shape[0]):
            if 0 <= pos - lag * bi < len(SCAN_ORDER):
                name, idx = SCAN_ORDER[pos - lag * bi]
                stages[name, bi][idx]()


def _scan(gqkv, gla, dqkv, small_c, small_r, dims):
    b, s, _ = gqkv.shape
    qk_a, v_a, _, _, qk_b, v_b, _ = dims
    nb = s // TOKEN_BLOCK
    tb = TOKEN_BLOCK
    ns = small_c.shape[2]
    nr = math.gcd(b, SCAN_ROWS)
    fwd = lambda n, col=0: pl.BlockSpec((nr, tb, n), lambda i, j: (i, j, col))
    bwd = lambda n, col=0: pl.BlockSpec((nr, tb, n), lambda i, j: (i, _bwd_block(j, nb), col))
    rfwd = pl.BlockSpec((nr, 1, ns, tb), lambda i, j: (i, j, 0, 0))
    rbwd = pl.BlockSpec((nr, 1, ns, tb), lambda i, j: (i, _bwd_block(j, nb), 0, 0))
    dk_a, dv_a = qk_a // GLA_HEADS, v_a // GLA_HEADS
    dk_b, dv_b = qk_b // GDN_HEADS, v_b // GDN_HEADS
    out = lambda v: jax.ShapeDtypeStruct((b, s, v), BF16)
    return pl.pallas_call(
        functools.partial(_scan_kernel, dk_a=dk_a, dv_a=dv_a, dk_b=dk_b, dv_b=dv_b, nh_b=GDN_HEADS),
        out_shape=[out(v_a), out(v_a), out(v_b), out(v_b)],
        grid=(b // nr, nb),
        in_specs=[fwd(2 * qk_a + v_a), fwd(qk_a, 0), bwd(2 * qk_a + v_a), bwd(qk_a, 1),
                  fwd(2 * qk_b + v_b), fwd(ns), rfwd, bwd(2 * qk_b + v_b), bwd(ns), rbwd],
        out_specs=[fwd(v_a), bwd(v_a), fwd(v_b), bwd(v_b)],
        scratch_shapes=[pltpu.VMEM((nr, 2, qk_a // LANES, dv_a, LANES), F32),
                        pltpu.VMEM((nr, 2, GDN_HEADS, dk_b, dv_b), F32)],
        compiler_params=pltpu.CompilerParams(dimension_semantics=("parallel", "arbitrary"),
                                             vmem_limit_bytes=VMEM_LIMIT),
        name="scan",
    )(gqkv, gla, gqkv, gla, dqkv, small_c, small_r, dqkv, small_c, small_r)


def _post_kernel(x_ref, mod_ref, *refs, d, nh_a, nh_b, final, nsub):
    mix = [refs[6 * i:6 * i + 6] for i in range(nsub)]
    na_ref, nb_ref, wo_ref, g2_ref, w1_ref, w2_ref, fg_ref, o_ref = refs[6 * nsub:]
    tb = x_ref.shape[1] // nsub
    m = mod_ref[0]
    gt1 = m[:, 2 * d:3 * d]
    sh2, sc2, gt2 = m[:, 3 * d:4 * d], m[:, 4 * d:5 * d], m[:, 5 * d:6 * d]
    slab = 2 * LANES
    x1, hb, act = {}, {}, {}

    def merge_and_project(i):
        af_ref, ab_ref, bf_ref, bb_ref, ga_ref, gb_ref = mix[i]
        acc = None
        row = 0
        for (f_ref, b_ref, g_ref, n_ref, nh) in ((af_ref, ab_ref, ga_ref, na_ref, nh_a),
                                                 (bf_ref, bb_ref, gb_ref, nb_ref, nh_b)):
            width = f_ref.shape[2]
            hv = width // nh
            for c0 in range(0, width, slab):
                o = f_ref[0, :, c0:c0 + slab].astype(F32) + b_ref[0, :, c0:c0 + slab].astype(F32)
                gate = _silu(g_ref[0, :, c0:c0 + slab].astype(F32))
                parts = [(_rms(o[:, k:k + hv]) * n_ref[...] * gate[:, k:k + hv]).astype(BF16)
                         for k in range(0, slab, hv)]
                part = _dot(jnp.concatenate(parts, axis=1), wo_ref[row:row + slab, :])
                acc = part if acc is None else acc + part
                row += slab
        x1[i] = x_ref[0, i * tb:(i + 1) * tb, :] + gt1 * acc
        hb[i] = (_rms(x1[i]) * (g2_ref[...] * (1.0 + sc2)) + sh2).astype(BF16)

    def mlp_up(i):
        a = jnp.maximum(_dot(hb[i], w1_ref[...]), 0.0)
        act[i] = (a * a).astype(BF16)

    def mlp_down(i):
        x2 = x1[i] + gt2 * _dot(act[i], w2_ref[...])
        if final:
            x2 = _rms(x2) * fg_ref[...]
        o_ref[0, i * tb:(i + 1) * tb, :] = x2

    merge_and_project(0)
    for i in range(nsub):
        mlp_up(i)
        if i + 1 < nsub:
            merge_and_project(i + 1)
        mlp_down(i)


def _post(stream, part, mod, layer, final, oaf, oab, obf, obb, gg, dg, params, fg, nh_a, nh_b):
    na, nb_, wo, g2, w1, w2 = params
    ctx_arr, lat_arr = stream
    b, t, d = lat_arr.shape
    tb = TOKEN_BLOCK
    ctx = part == "ctx"
    nsub = 1 if ctx else math.gcd(t // tb, POST_SUB)
    nstep = 1 if ctx else t // (nsub * tb)
    first = 0 if ctx else 1
    lay = lambda a: pl.BlockSpec((None,) + a.shape[1:], lambda i, j: (layer,) + (0,) * (a.ndim - 1))
    weight = lambda a: pl.BlockSpec((None,) + a.shape[1:], lambda i, j: (layer,) + (0,) * (a.ndim - 1),
                                    pipeline_mode=pl.Buffered(1))
    bsz = b
    mod_spec = pl.BlockSpec((1, 1, 6 * d), lambda i, j: (layer * MOD_ROWS + (bsz if ctx else i), 0, 0))
    va, vb = oaf.shape[2], obf.shape[2]
    mix_specs, mix_args = [], []
    for k in range(nsub):
        blk = lambda n, k=k: pl.BlockSpec((1, tb, n), lambda i, j: (i, first + j * nsub + k, 0))
        mix_specs += [blk(va), blk(va), blk(vb), blk(vb), blk(va), blk(vb)]
        mix_args += [oaf, oab, obf, obb, gg, dg]
    return pl.pallas_call(
        functools.partial(_post_kernel, d=d, nh_a=nh_a, nh_b=nh_b, final=final, nsub=nsub),
        out_shape=jax.ShapeDtypeStruct((b, nstep * nsub * tb, d), F32),
        grid=(b, nstep),
        in_specs=[pl.BlockSpec((1, nsub * tb, d), lambda i, j: (i, j, 0)), mod_spec, *mix_specs,
                  lay(na), lay(nb_), weight(wo), lay(g2), weight(w1), weight(w2),
                  pl.BlockSpec((1, d), lambda i, j: (0, 0))],
        out_specs=pl.BlockSpec((1, nsub * tb, d), lambda i, j: (i, j, 0)),
        compiler_params=pltpu.CompilerParams(dimension_semantics=("parallel", "parallel"),
                                             vmem_limit_bytes=VMEM_LIMIT),
        name="post_ctx" if ctx else "post",
    )(ctx_arr if ctx else lat_arr, mod, *mix_args, na, nb_, wo, g2, w1, w2, fg.reshape(1, d))


def kernel(x, c, ctx, c_ctx, w_ada, b_ada, norm1_g, norm2_g, w_in, gla_w_lr, gla_b_lr, gdn_conv_w,
           gdn_a_log, gdn_dt_bias, gla_norm_g, gdn_norm_g, w_out, w_ff1, w_ff2, final_norm_g):
    bsz, t, d = x.shape
    depth = w_ada.shape[0]
    assert ctx.shape[1] == TOKEN_BLOCK and t % TOKEN_BLOCK == 0 and TOKEN_BLOCK % GRID_W == 0
    assert bsz + 1 <= MOD_ROWS
    qk_a = gla_w_lr.shape[3]
    rank = gla_w_lr.shape[2]
    v_a = gla_norm_g.shape[1] * GLA_HEADS
    conv_dim = gdn_conv_w.shape[2]
    v_b = gdn_norm_g.shape[1] * GDN_HEADS
    qk_b = (conv_dim - v_b) // 2
    nd = 2 * GDN_HEADS
    dims = (qk_a, v_a, rank, conv_dim, qk_b, v_b, GDN_HEADS)

    o_r = 2 * qk_a + 2 * v_a
    o_c = o_r + rank
    o_g = o_c + conv_dim
    o_s = o_g + v_b
    assert w_in.shape[2] == o_s + 2 * nd
    w_main = jnp.concatenate([w_in[:, :, :o_r], w_in[:, :, o_c:o_s]], axis=2).astype(BF16)
    w_sm = jnp.concatenate([w_in[:, :, o_r:o_c], w_in[:, :, o_s:]], axis=2)
    w_small = jnp.pad(w_sm, ((0, 0), (0, 0), (0, LANES - w_sm.shape[2]))).astype(BF16)
    w_small_t = jnp.swapaxes(w_in[:, :, o_s:], 1, 2).astype(BF16)

    cc = jnp.concatenate([c, c_ctx[None, :], jnp.zeros((MOD_ROWS - bsz - 1, d), F32)], axis=0)
    mod = _modulation(cc, w_ada, b_ada).reshape(depth * MOD_ROWS, 1, 6 * d)

    in_params = (norm1_g.reshape(depth, 1, d), w_main, w_small, w_small_t, gla_w_lr.astype(BF16),
                 gla_b_lr.reshape(depth, 2, 1, qk_a), gdn_conv_w,
                 gdn_a_log.reshape(depth, 1, nd), gdn_dt_bias.reshape(depth, 1, nd),
                 gdn_a_log.reshape(depth, nd, 1), gdn_dt_bias.reshape(depth, nd, 1))
    post_params = (gla_norm_g.reshape(depth, 1, -1), gdn_norm_g.reshape(depth, 1, -1), w_out.astype(BF16),
                   norm2_g.reshape(depth, 1, d), w_ff1.astype(BF16), w_ff2.astype(BF16))

    stream = (ctx, x)
    for l in range(depth):
        last = l == depth - 1
        gqkv, gg, gla, dqkv, dg, small_c, small_r = _in_proj(stream, mod, l, bsz, in_params, dims)
        oaf, oab, obf, obb = _scan(gqkv, gla, dqkv, small_c, small_r, dims)
        args = (mod, l, last, oaf, oab, obf, obb, gg, dg, post_params, final_norm_g, GLA_HEADS, GDN_HEADS)
        xs = _post(stream, "latent", *args)
        if not last:
            stream = (_post(stream, "ctx", *args), xs)
    return xs
```

```python
import functools
import math

import jax
import jax.numpy as jnp
from jax import lax
from jax.experimental import pallas as pl
from jax.experimental.pallas import tpu as pltpu

EPS = 1e-6
GRID_W = 64
GLA_HEADS = 4
GDN_HEADS = 4
GLA_TAU = 16.0
TOKEN_BLOCK = 256
CHUNK = 64
LANES = 128
SUBLANES = 8
POST_SUB = 2
SCAN_ROWS = 2
IN_ROWS = 4
SCAN_ROW_LAG = 6
MOD_ROWS = 16
VMEM_LIMIT = 56 * 1024 * 1024
SCAN_ORDER = (("b", 0), ("b", 1), ("a", 0), ("a", 1), ("a", 2), ("a", 3), ("b", 2), ("b", 3), ("b", 4),
              ("b", 5), ("b", 6), ("a", 4), ("b", 7), ("b", 8), ("b", 9), ("b", 10), ("b", 11), ("b", 12))

F32 = jnp.float32
BF16 = jnp.bfloat16


def _dot(a, b):
    return jnp.dot(a, b, preferred_element_type=F32)


def _dot_nt(a, b):
    return lax.dot_general(a, b, (((1,), (1,)), ((), ())), preferred_element_type=F32)


def _dot_tn(a, b):
    return lax.dot_general(a, b, (((0,), (0,)), ((), ())), preferred_element_type=F32)


def _silu(x):
    return x / (1.0 + jnp.exp(-x))


def _sigmoid(x):
    return 1.0 / (1.0 + jnp.exp(-x))


def _softplus(x):
    return jnp.maximum(x, 0.0) + jnp.log(1.0 + jnp.exp(-jnp.abs(x)))


def _log_sigmoid(x):
    return jnp.minimum(x, 0.0) - jnp.log(1.0 + jnp.exp(-jnp.abs(x)))


def _rms(x):
    return x * lax.rsqrt(jnp.mean(x * x, axis=-1, keepdims=True) + EPS)


def _cumsum_dot(tri_bf16, x, left):
    hi = x.astype(BF16)
    lo = (x - hi.astype(F32)).astype(BF16)
    if left:
        return _dot(tri_bf16, hi) + _dot(tri_bf16, lo)
    return _dot(hi, tri_bf16) + _dot(lo, tri_bf16)


def _tri(n, upper, strict):
    r = lax.broadcasted_iota(jnp.int32, (n, n), 0)
    c = lax.broadcasted_iota(jnp.int32, (n, n), 1)
    if upper:
        m = (r < c) if strict else (r <= c)
    else:
        m = (r > c) if strict else (r >= c)
    return m


def _couple(n, s, upper):
    r = lax.broadcasted_iota(jnp.int32, (n, n), 0)
    c = lax.broadcasted_iota(jnp.int32, (n, n), 1)
    same = (r & ~(2 * s - 1)) == (c & ~(2 * s - 1))
    r_hi, c_hi = (r & s) != 0, (c & s) != 0
    return same & ((~r_hi & c_hi) if upper else (r_hi & ~c_hi))


def _tri_blocks(n, upper):
    r = lax.broadcasted_iota(jnp.int32, (n, n), 0)
    c = lax.broadcasted_iota(jnp.int32, (n, n), 1)
    same = (r & ~(CHUNK - 1)) == (c & ~(CHUNK - 1))
    return same & ((r <= c) if upper else (r >= c))


def _mod_kernel(cc_ref, w_ref, b_ref, o_ref):
    s = _silu(cc_ref[...]).astype(BF16)
    o_ref[0] = _dot(s, w_ref[0].astype(BF16)) + b_ref[0]


def _modulation(cc, w_ada, b_ada):
    depth, d, d6 = w_ada.shape
    nblk = d6 // d
    return pl.pallas_call(
        _mod_kernel,
        out_shape=jax.ShapeDtypeStruct((depth, MOD_ROWS, d6), F32),
        grid=(depth, nblk),
        in_specs=[
            pl.BlockSpec((MOD_ROWS, d), lambda l, n: (0, 0)),
            pl.BlockSpec((1, d, d), lambda l, n: (l, 0, n)),
            pl.BlockSpec((1, 1, d), lambda l, n: (l, 0, n)),
        ],
        out_specs=pl.BlockSpec((1, MOD_ROWS, d), lambda l, n: (l, 0, n)),
        compiler_params=pltpu.CompilerParams(dimension_semantics=("parallel", "parallel")),
        name="adaln_modulation",
    )(cc, w_ada, b_ada.reshape(depth, 1, d6))


def _in_proj_kernel(c_ref, x_ref, *refs, d, qk_a, v_a, rank, conv_dim, qk_b, v_b, nh_b):
    nr = x_ref.shape[0]
    mod_refs = refs[:nr]
    (g1_ref, wm_ref, ws_ref, wst_ref, wlr_ref, blr_ref, cw_ref, arow_ref, dtrow_ref, acol_ref, dtcol_ref,
     gqkv_ref, gg_ref, gla_ref, dqkv_ref, dg_ref, sc_ref, sr_ref) = refs[nr:]
    for bi in range(nr):
        _in_proj_row(bi, c_ref, x_ref, mod_refs[bi], g1_ref, wm_ref, ws_ref, wst_ref, wlr_ref, blr_ref, cw_ref,
                     arow_ref, dtrow_ref, acol_ref, dtcol_ref, gqkv_ref, gg_ref, gla_ref, dqkv_ref, dg_ref,
                     sc_ref, sr_ref, d=d, qk_a=qk_a, v_a=v_a, rank=rank, conv_dim=conv_dim, qk_b=qk_b, v_b=v_b,
                     nh_b=nh_b)


def _in_proj_row(bi, c_ref, x_ref, mod_ref, g1_ref, wm_ref, ws_ref, wst_ref, wlr_ref, blr_ref, cw_ref,
                 arow_ref, dtrow_ref, acol_ref, dtcol_ref,
                 gqkv_ref, gg_ref, gla_ref, dqkv_ref, dg_ref, sc_ref, sr_ref, *, d, qk_a, v_a, rank, conv_dim,
                 qk_b, v_b, nh_b):
    j = pl.program_id(1)
    x = jnp.where(j == 0, c_ref[bi], x_ref[bi])
    m = mod_ref[0]
    sh1, sc1 = m[:, 0:d], m[:, d:2 * d]
    h = _rms(x) * (g1_ref[...] * (1.0 + sc1)) + sh1
    hb = h.astype(BF16)
    o_conv = 2 * qk_a + 2 * v_a
    o_gate = o_conv + conv_dim
    grp = 2 * LANES
    tb = x.shape[0]
    t = lax.broadcasted_iota(jnp.int32, (tb, 1), 0)
    seg_mask = jnp.where(j == 0, tb - 1, GRID_W - 1)
    first = (t & seg_mask) == 0
    last = (t & seg_mask) == seg_mask
    dkh = qk_b // nh_b

    def conv_dot(k):
        return _dot(hb, wm_ref[:, o_conv + k * grp:o_conv + (k + 1) * grp])

    def zero_rows(a, mask, at):
        pieces = []
        for s0 in range(0, tb, GRID_W):
            r0 = s0 + at
            pieces += [a[s0:r0], jnp.where(mask[r0:r0 + SUBLANES], 0.0, a[r0:r0 + SUBLANES]),
                       a[r0 + SUBLANES:s0 + GRID_W]]
        return jnp.concatenate([p for p in pieces if p.shape[0]], axis=0)

    def conv_group(u, k):
        up = zero_rows(pltpu.roll(u, 1, 0), first, 0)
        un = zero_rows(pltpu.roll(u, tb - 1, 0), last, GRID_W - SUBLANES)
        cw = cw_ref[:, k * grp:(k + 1) * grp]
        s = _silu(cw[0:1] * up + cw[1:2] * u + cw[2:3] * un)
        c0 = k * grp
        if c0 < 2 * qk_b:
            for hh in range(grp // dkh):
                sh = s[:, hh * dkh:(hh + 1) * dkh]
                inv = lax.rsqrt(jnp.sum(sh * sh, axis=-1, keepdims=True) + EPS)
                if c0 < qk_b:
                    inv = inv * (float(dkh) ** -0.5)
                dqkv_ref[bi, :, c0 + hh * dkh:c0 + (hh + 1) * dkh] = (sh * inv).astype(BF16)
        else:
            dqkv_ref[bi, :, c0:c0 + grp] = s.astype(BF16)

    def light_group(k):
        c0 = k * grp
        w0 = c0 if c0 < o_conv else o_gate + (c0 - o_conv)
        pa = _dot(hb, wm_ref[:, w0:w0 + grp])
        if c0 < qk_a:
            gqkv_ref[bi, :, c0:c0 + grp] = (pa * (float(qk_a // GLA_HEADS) ** -0.5)).astype(BF16)
        elif c0 < 2 * qk_a + v_a:
            gqkv_ref[bi, :, c0:c0 + grp] = pa.astype(BF16)
        elif c0 < o_conv:
            gg_ref[bi, :, c0 - 2 * qk_a - v_a:c0 - 2 * qk_a - v_a + grp] = pa.astype(BF16)
        else:
            dg_ref[bi, :, c0 - o_conv:c0 - o_conv + grp] = pa.astype(BF16)

    def small_group(ps, pst):
        r_a = ps[:, 0:rank].astype(BF16)
        for dr in range(2):
            lr = _dot(r_a, wlr_ref[dr]) + blr_ref[dr]
            gla_ref[bi, :, dr * qk_a:(dr + 1) * qk_a] = _log_sigmoid(lr) * (1.0 / GLA_TAU)
        nd = 2 * nh_b
        a_c = ps[:, rank:rank + nd]
        b_c = ps[:, rank + nd:rank + 2 * nd]
        sc_ref[bi, :, 0:nd] = -jnp.exp(arow_ref[...]) * _softplus(a_c + dtrow_ref[...])
        sc_ref[bi, :, nd:2 * nd] = _sigmoid(b_c)
        sr_ref[bi, 0, 0:nd, :] = -jnp.exp(acol_ref[...]) * _softplus(pst[0:nd] + dtcol_ref[...])
        sr_ref[bi, 0, nd:2 * nd, :] = _sigmoid(pst[nd:2 * nd])

    n_conv = conv_dim // grp
    n_light = (o_conv + v_b) // grp
    us = {0: conv_dot(0)}
    ps = _dot(hb, ws_ref[...])
    pst = _dot_nt(wst_ref[...], hb)
    if n_conv > 1:
        us[1] = conv_dot(1)
    g_next = 0
    for k in range(n_conv):
        if k + 2 < n_conv:
            us[k + 2] = conv_dot(k + 2)
        conv_group(us.pop(k), k)
        if k == 1 or n_conv == 1:
            small_group(ps, pst)
        todo = (n_light - g_next + (n_conv - k) - 1) // (n_conv - k) if k >= n_conv // 2 else 1
        for _ in range(min(todo, n_light - g_next)):
            light_group(g_next)
            g_next += 1
    while g_next < n_light:
        light_group(g_next)
        g_next += 1


def _in_proj(stream, mod, layer, bsz, params, dims):
    g1, wm, ws, wst, wlr, blr, cw, a_row, dt_row, a_col, dt_col = params
    head, body, body_off = stream
    b, _, d = body.shape
    nb = body.shape[1] // TOKEN_BLOCK + body_off
    s = nb * TOKEN_BLOCK
    qk_a, v_a, rank, conv_dim, qk_b, v_b, nh_b = dims
    nd = 2 * nh_b
    tb = TOKEN_BLOCK
    lay = lambda a: pl.BlockSpec((None,) + a.shape[1:], lambda i, j: (layer,) + (0,) * (a.ndim - 1))
    nr = math.gcd(b, IN_ROWS)
    tok = lambda n: pl.BlockSpec((nr, tb, n), lambda i, j: (i, j, 0))
    mod_specs = [pl.BlockSpec((1, 1, 6 * d), lambda i, j, k=k: (
        layer * MOD_ROWS + jnp.where(j == 0, bsz, i * nr + k), 0, 0)) for k in range(nr)]
    out_shape = [
        jax.ShapeDtypeStruct((b, s, 2 * qk_a + v_a), BF16), jax.ShapeDtypeStruct((b, s, v_a), BF16),
        jax.ShapeDtypeStruct((b, s, 2 * qk_a), F32),
        jax.ShapeDtypeStruct((b, s, conv_dim), BF16), jax.ShapeDtypeStruct((b, s, v_b), BF16),
        jax.ShapeDtypeStruct((b, s, 2 * nd), F32),
        jax.ShapeDtypeStruct((b, nb, 2 * nd, tb), F32),
    ]
    out_specs = [tok(2 * qk_a + v_a), tok(v_a), tok(2 * qk_a), tok(conv_dim), tok(v_b), tok(2 * nd),
                 pl.BlockSpec((nr, 1, 2 * nd, tb), lambda i, j: (i, j, 0, 0))]
    kern = functools.partial(_in_proj_kernel, d=d, qk_a=qk_a, v_a=v_a, rank=rank, conv_dim=conv_dim,
                             qk_b=qk_b, v_b=v_b, nh_b=nh_b)
    return pl.pallas_call(
        kern, out_shape=out_shape, grid=(b // nr, nb),
        in_specs=[
            pl.BlockSpec((nr, tb, d), lambda i, j: (i, 0, 0)),
            pl.BlockSpec((nr, tb, d), lambda i, j: (i, jnp.maximum(j - body_off, 0), 0)), *mod_specs, lay(g1),
            lay(wm), lay(ws), lay(wst), lay(wlr), lay(blr), lay(cw), lay(a_row), lay(dt_row), lay(a_col), lay(dt_col),
        ],
        out_specs=out_specs,
        compiler_params=pltpu.CompilerParams(dimension_semantics=("parallel", "parallel"),
                                             vmem_limit_bytes=VMEM_LIMIT),
        name="in_proj",
    )(head, body, *([mod] * nr), g1, wm, ws, wst, wlr, blr, cw, a_row, dt_row, a_col, dt_col)


def _bwd_block(j, nb):
    return jnp.where(j == 0, 0, nb - j)


class _Cols:
    def __init__(self, ref, off):
        self.ref, self.off = ref, off

    def __getitem__(self, idx):
        z, rows, cols = idx
        return self.ref[z, rows, self.off + cols.start:self.off + cols.stop]


def _gla_stages(refs, st_ref, dk, dv, bi):
    tb = refs[0][3].shape[1]
    nchunk = tb // CHUNK
    npair = (GLA_HEADS * dk) // LANES
    hpp = LANES // dk
    lane = lax.broadcasted_iota(jnp.int32, (1, LANES), 1)
    lms = [(lane >= hh * dk) & (lane < (hh + 1) * dk) for hh in range(hpp)]
    mid = CHUNK // 2
    chains = [(dr, pr) for dr in range(2) for pr in range(npair)]
    units = [(dr, pr, c) for (dr, pr) in chains for c in range(nchunk)]
    bcum, q_dec, k_hat, q_mid, k_mid, a_last, sc, intra, dst = ({} for _ in range(9))

    def cumulative_decay():
        for dr in range(2):
            both = _cumsum_dot(_tri_blocks(tb, dr == 1).astype(BF16), refs[dr][3][bi], True)
            for pr in range(npair):
                bcum[dr, pr] = both[:, pr * LANES:(pr + 1) * LANES]

    def decayed_operands():
        for u in units:
            dr, pr, c = u
            rows = slice(c * CHUNK, (c + 1) * CHUNK)
            lanes = slice(pr * LANES, (pr + 1) * LANES)
            last = c * CHUNK + (0 if dr == 1 else CHUNK - 1)
            b = bcum[dr, pr][rows]
            b_last = bcum[dr, pr][last:last + 1]
            b_mid = bcum[dr, pr][c * CHUNK + mid:c * CHUNK + mid + 1]
            qc = refs[dr][0][bi, rows, lanes].astype(F32)
            kc = refs[dr][1][bi, rows, lanes].astype(F32)
            q_dec[u] = qc * jnp.exp(b)
            k_hat[u] = (kc * jnp.exp(b_last - b)).astype(BF16)
            q_mid[u] = qc * jnp.exp(b - b_mid)
            k_mid[u] = (kc * jnp.exp(b_mid - b)).astype(BF16)
            a_last[u] = jnp.exp(b_last)

    def per_head(m):
        return jnp.concatenate([jnp.where(lm, m, 0.0).astype(BF16) for lm in lms], axis=0)

    def scores():
        for u in units:
            sc[u] = _dot_nt(per_head(q_mid[u]), k_mid[u])

    def intra_and_increments():
        for u in units:
            dr, pr, c = u
            rows = slice(c * CHUNK, (c + 1) * CHUNK)
            causal = _tri(CHUNK, dr == 1, False)
            v_grp = refs[dr][2][bi, rows, pr * hpp * dv:(pr + 1) * hpp * dv]
            dh = _dot_tn(v_grp, k_hat[u])
            acc = None
            for hh in range(hpp):
                pm = jnp.where(causal, sc[u][hh * CHUNK:(hh + 1) * CHUNK], 0.0).astype(BF16)
                intra[u, hh] = _dot(pm, v_grp[:, hh * dv:(hh + 1) * dv])
                part = dh[hh * dv:(hh + 1) * dv]
                acc = part if acc is None else jnp.where(lms[hh], part, acc)
            dst[u] = acc

    def recurrence():
        st = {ch: st_ref[bi, ch[0], ch[1]] for ch in chains}
        for ci in range(nchunk):
            for ch in chains:
                dr, pr = ch
                c = nchunk - 1 - ci if dr == 1 else ci
                u = (dr, pr, c)
                rows = slice(c * CHUNK, (c + 1) * CHUNK)
                inter = _dot_nt(per_head(q_dec[u]), st[ch].astype(BF16))
                for hh in range(hpp):
                    head = pr * hpp + hh
                    refs[dr][4][bi, rows, head * dv:(head + 1) * dv] = (
                        intra[u, hh] + inter[hh * CHUNK:(hh + 1) * CHUNK]).astype(BF16)
                st[ch] = st[ch] * a_last[u] + dst[u]
        for ch in chains:
            st_ref[bi, ch[0], ch[1]] = st[ch]

    return [cumulative_decay, decayed_operands, scores, intra_and_increments, recurrence]


def _gdn_stages(refs, st_ref, dk, dv, nh, bi):
    tb = refs[0][3].shape[1]
    nchunk = tb // CHUNK
    eye = _tri(CHUNK, False, False) & _tri(CHUNK, True, False)
    chains = [(dr, hd) for dr in range(2) for hd in range(nh)]
    units = [(dr, c, hd) for dr in range(2) for c in range(nchunk) for hd in range(nh)]
    g_cols, g_rows, small, kk, qk, x, aqk, rhs, q_dec, k_dec, a_last = ({} for _ in range(11))
    t, uw, kd_uw, aq_uw, lhs, decay = ({} for _ in range(6))

    def cumulative_decay():
        for dr in range(2):
            small[dr] = refs[dr][3][bi]
            g_cols[dr] = _cumsum_dot(_tri_blocks(tb, dr == 1).astype(BF16), small[dr], True)
            g_rows[dr] = _cumsum_dot(_tri_blocks(tb, dr != 1).astype(BF16), refs[dr][4][bi, 0], False)

    def grams():
        for u in units:
            dr, c, hd = u
            rows = slice(c * CHUNK, (c + 1) * CHUNK)
            kh = refs[dr][1][bi, rows, hd * dk:(hd + 1) * dk]
            qh = refs[dr][0][bi, rows, hd * dk:(hd + 1) * dk]
            kq = _dot_nt(jnp.concatenate([kh, qh], axis=0), kh)
            kk[u], qk[u] = kq[:CHUNK], kq[CHUNK:]

    def solve_operands():
        for u in units:
            dr, c, hd = u
            rows = slice(c * CHUNK, (c + 1) * CHUNK)
            ia = dr * nh + hd
            ib = 2 * nh + dr * nh + hd
            g_c = g_cols[dr][rows, ia:ia + 1]
            g_r = g_rows[dr][ia:ia + 1, rows]
            be_c = small[dr][rows, ib:ib + 1]
            decay[u] = jnp.exp(jnp.minimum(g_c - g_r, 0.0))
            x[u] = jnp.where(_tri(CHUNK, dr == 1, True), be_c * kk[u] * decay[u], 0.0)

    def other_operands():
        for u in units:
            dr, c, hd = u
            rows = slice(c * CHUNK, (c + 1) * CHUNK)
            ia = dr * nh + hd
            ib = 2 * nh + dr * nh + hd
            last = c * CHUNK + (0 if dr == 1 else CHUNK - 1)
            g_c = g_cols[dr][rows, ia:ia + 1]
            be_c = small[dr][rows, ib:ib + 1]
            g_last = g_cols[dr][last:last + 1, ia:ia + 1]
            aqk[u] = jnp.where(_tri(CHUNK, dr == 1, False), qk[u] * decay[u], 0.0).astype(BF16)
            eg = jnp.exp(g_c)
            qh = refs[dr][0][bi, rows, hd * dk:(hd + 1) * dk].astype(F32)
            khf = refs[dr][1][bi, rows, hd * dk:(hd + 1) * dk].astype(F32)
            vh = refs[dr][2][bi, rows, hd * dv:(hd + 1) * dv].astype(F32)
            rhs[u] = jnp.concatenate([(be_c * vh).astype(BF16), ((be_c * eg) * khf).astype(BF16)], axis=1)
            q_dec[u] = qh * eg
            k_dec[u] = khf * jnp.exp(g_last - g_c)
            a_last[u] = jnp.exp(g_last)

    def inverse_start():
        for u in units:
            t[u] = jnp.where(eye, 1.0, jnp.where(_couple(CHUNK, 1, u[0] == 1), -x[u], 0.0))

    def inverse_level(s):
        a = {u: jnp.where(_couple(CHUNK, s, u[0] == 1), x[u], 0.0).astype(BF16) for u in units}
        tb16 = {u: t[u].astype(BF16) for u in units}
        ta = {u: _dot(tb16[u], a[u]).astype(BF16) for u in units}
        for u in units:
            t[u] = t[u] - _dot(ta[u], tb16[u])

    levels = []
    s = 2
    while s < CHUNK:
        levels.append(functools.partial(inverse_level, s))
        s *= 2

    def solve():
        for u in units:
            uw[u] = _dot(t[u].astype(BF16), rhs[u]).astype(BF16)

    def fold():
        for u in units:
            both = _dot(jnp.concatenate([k_dec[u].T.astype(BF16), aqk[u]], axis=0), uw[u])
            kd_uw[u], aq_uw[u] = both[:dk], both[dk:]
        for u in units:
            lhs[u] = jnp.concatenate([(-kd_uw[u][:, dv:]).astype(BF16),
                                      (q_dec[u] - aq_uw[u][:, dv:]).astype(BF16)], axis=0)

    def recurrence():
        st = {ch: st_ref[bi, ch[0], ch[1]] for ch in chains}
        for ci in range(nchunk):
            for ch in chains:
                dr, hd = ch
                c = nchunk - 1 - ci if dr == 1 else ci
                u = (dr, c, hd)
                rows = slice(c * CHUNK, (c + 1) * CHUNK)
                res = _dot(lhs[u], st[ch].astype(BF16))
                refs[dr][5][bi, rows, hd * dv:(hd + 1) * dv] = (res[dk:] + aq_uw[u][:, :dv]).astype(BF16)
                st[ch] = st[ch] * a_last[u] + res[:dk] + kd_uw[u][:, :dv]
        for ch in chains:
            st_ref[bi, ch[0], ch[1]] = st[ch]

    return [cumulative_decay, grams, solve_operands, other_operands, inverse_start, *levels,
            solve, fold, recurrence]


def _scan_kernel(af, alf, ab, alb, bf, bcf, brf, bb, bcb, brb,
                 oaf, oab, obf, obb, sta_ref, stb_ref, *, dk_a, dv_a, dk_b, dv_b, nh_b):
    @pl.when(pl.program_id(1) == 0)
    def _():
        sta_ref[...] = jnp.zeros_like(sta_ref)
        stb_ref[...] = jnp.zeros_like(stb_ref)

    qk_a, qk_b = GLA_HEADS * dk_a, nh_b * dk_b
    qkv = lambda ref, qk: (_Cols(ref, 0), _Cols(ref, qk), _Cols(ref, 2 * qk))
    refs_a = ((*qkv(af, qk_a), alf, oaf), (*qkv(ab, qk_a), alb, oab))
    refs_b = ((*qkv(bf, qk_b), bcf, brf, obf), (*qkv(bb, qk_b), bcb, brb, obb))
    stages = {}
    for bi in range(af.shape[0]):
        stages["a", bi] = _gla_stages(refs_a, sta_ref, dk_a, dv_a, bi)
        stages["b", bi] = _gdn_stages(refs_b, stb_ref, dk_b, dv_b, nh_b, bi)
    lag = SCAN_ROW_LAG
    for pos in range(len(SCAN_ORDER) + lag * (af.shape[0] - 1)):
        for bi in range(af.shape[0]):
            if 0 <= pos - lag * bi < len(SCAN_ORDER):
                name, idx = SCAN_ORDER[pos - lag * bi]
                stages[name, bi][idx]()


def _scan(gqkv, gla, dqkv, small_c, small_r, dims):
    b, s, _ = gqkv.shape
    qk_a, v_a, _, _, qk_b, v_b, _ = dims
    nb = s // TOKEN_BLOCK
    tb = TOKEN_BLOCK
    ns = small_c.shape[2]
    nr = math.gcd(b, SCAN_ROWS)
    fwd = lambda n, col=0: pl.BlockSpec((nr, tb, n), lambda i, j: (i, j, col))
    bwd = lambda n, col=0: pl.BlockSpec((nr, tb, n), lambda i, j: (i, _bwd_block(j, nb), col))
    rfwd = pl.BlockSpec((nr, 1, ns, tb), lambda i, j: (i, j, 0, 0))
    rbwd = pl.BlockSpec((nr, 1, ns, tb), lambda i, j: (i, _bwd_block(j, nb), 0, 0))
    dk_a, dv_a = qk_a // GLA_HEADS, v_a // GLA_HEADS
    dk_b, dv_b = qk_b // GDN_HEADS, v_b // GDN_HEADS
    out = lambda v: jax.ShapeDtypeStruct((b, s, v), BF16)
    return pl.pallas_call(
        functools.partial(_scan_kernel, dk_a=dk_a, dv_a=dv_a, dk_b=dk_b, dv_b=dv_b, nh_b=GDN_HEADS),
        out_shape=[out(v_a), out(v_a), out(v_b), out(v_b)],
        grid=(b // nr, nb),
        in_specs=[fwd(2 * qk_a + v_a), fwd(qk_a, 0), bwd(2 * qk_a + v_a), bwd(qk_a, 1),
                  fwd(2 * qk_b + v_b), fwd(ns), rfwd, bwd(2 * qk_b + v_b), bwd(ns), rbwd],
        out_specs=[fwd(v_a), bwd(v_a), fwd(v_b), bwd(v_b)],
        scratch_shapes=[pltpu.VMEM((nr, 2, qk_a // LANES, dv_a, LANES), F32),
                        pltpu.VMEM((nr, 2, GDN_HEADS, dk_b, dv_b), F32)],
        compiler_params=pltpu.CompilerParams(dimension_semantics=("parallel", "arbitrary"),
                                             vmem_limit_bytes=VMEM_LIMIT),
        name="scan",
    )(gqkv, gla, gqkv, gla, dqkv, small_c, small_r, dqkv, small_c, small_r)


def _post_kernel(c_ref, x_ref, *refs, d, nh_a, nh_b, final, nsub):
    mod_refs = refs[:nsub]
    af_ref, ab_ref, bf_ref, bb_ref, ga_ref, gb_ref = refs[nsub:nsub + 6]
    na_ref, nb_ref, wo_ref, g2_ref, w1_ref, w2_ref, fg_ref, o_ref = refs[nsub + 6:]
    j = pl.program_id(1)
    slab = 2 * LANES
    x1, hb, act = {}, {}, {}

    def merge_and_project(i):
        m = mod_refs[i][0]
        acc = None
        row = 0
        for (f_ref, b_ref, g_ref, n_ref, nh) in ((af_ref, ab_ref, ga_ref, na_ref, nh_a),
                                                 (bf_ref, bb_ref, gb_ref, nb_ref, nh_b)):
            width = f_ref.shape[2]
            hv = width // nh
            for c0 in range(0, width, slab):
                o = f_ref[i, :, c0:c0 + slab].astype(F32) + b_ref[i, :, c0:c0 + slab].astype(F32)
                gate = _silu(g_ref[i, :, c0:c0 + slab].astype(F32))
                parts = [(_rms(o[:, k:k + hv]) * n_ref[...] * gate[:, k:k + hv]).astype(BF16)
                         for k in range(0, slab, hv)]
                part = _dot(jnp.concatenate(parts, axis=1), wo_ref[row:row + slab, :])
                acc = part if acc is None else acc + part
                row += slab
        x_in = x_ref[i] if final else jnp.where(j == 0, c_ref[i], x_ref[i])
        x1[i] = x_in + m[:, 2 * d:3 * d] * acc
        hb[i] = (_rms(x1[i]) * (g2_ref[...] * (1.0 + m[:, 4 * d:5 * d])) + m[:, 3 * d:4 * d]).astype(BF16)

    def mlp_up(i):
        a = jnp.maximum(_dot(hb[i], w1_ref[...]), 0.0)
        act[i] = (a * a).astype(BF16)

    def mlp_down(i):
        x2 = x1[i] + mod_refs[i][0][:, 5 * d:6 * d] * _dot(act[i], w2_ref[...])
        if final:
            x2 = _rms(x2) * fg_ref[...]
        o_ref[i] = x2

    merge_and_project(0)
    for i in range(nsub):
        mlp_up(i)
        if i + 1 < nsub:
            merge_and_project(i + 1)
        mlp_down(i)


def _post(stream, mod, layer, final, oaf, oab, obf, obb, gg, dg, params, fg, nh_a, nh_b):
    na, nb_, wo, g2, w1, w2 = params
    head, body, body_off = stream
    b, _, d = body.shape
    tb = TOKEN_BLOCK
    nsub = math.gcd(b, POST_SUB)
    first = 1 if final else 0
    nblk = body.shape[1] // tb + body_off - first
    lay = lambda a: pl.BlockSpec((None,) + a.shape[1:], lambda i, j: (layer,) + (0,) * (a.ndim - 1))
    weight = lambda a: pl.BlockSpec((None,) + a.shape[1:], lambda i, j: (layer,) + (0,) * (a.ndim - 1),
                                    pipeline_mode=pl.Buffered(1))
    bsz = b
    mod_specs = [pl.BlockSpec((1, 1, 6 * d), lambda i, j, k=k: (
        layer * MOD_ROWS + jnp.where(j + first == 0, bsz, i * nsub + k), 0, 0)) for k in range(nsub)]
    blk = lambda n: pl.BlockSpec((nsub, tb, n), lambda i, j: (i, j + first, 0))
    va, vb = oaf.shape[2], obf.shape[2]
    return pl.pallas_call(
        functools.partial(_post_kernel, d=d, nh_a=nh_a, nh_b=nh_b, final=final, nsub=nsub),
        out_shape=jax.ShapeDtypeStruct((b, nblk * tb, d), F32),
        grid=(b // nsub, nblk),
        in_specs=[pl.BlockSpec((nsub, tb, d), lambda i, j: (i, 0, 0)),
                  pl.BlockSpec((nsub, tb, d), lambda i, j: (i, jnp.maximum(j + first - body_off, 0), 0)),
                  *mod_specs, blk(va), blk(va), blk(vb), blk(vb), blk(va), blk(vb),
                  lay(na), lay(nb_), weight(wo), lay(g2), weight(w1), weight(w2),
                  pl.BlockSpec((1, d), lambda i, j: (0, 0))],
        out_specs=pl.BlockSpec((nsub, tb, d), lambda i, j: (i, j, 0)),
        compiler_params=pltpu.CompilerParams(dimension_semantics=("parallel", "parallel"),
                                             vmem_limit_bytes=VMEM_LIMIT),
        name="post",
    )(head, body, *([mod] * nsub), oaf, oab, obf, obb, gg, dg, na, nb_, wo, g2, w1, w2, fg.reshape(1, d))


def kernel(x, c, ctx, c_ctx, w_ada, b_ada, norm1_g, norm2_g, w_in, gla_w_lr, gla_b_lr, gdn_conv_w,
           gdn_a_log, gdn_dt_bias, gla_norm_g, gdn_norm_g, w_out, w_ff1, w_ff2, final_norm_g):
    bsz, t, d = x.shape
    depth = w_ada.shape[0]
    assert ctx.shape[1] == TOKEN_BLOCK and t % TOKEN_BLOCK == 0 and TOKEN_BLOCK % GRID_W == 0
    assert bsz + 1 <= MOD_ROWS
    qk_a = gla_w_lr.shape[3]
    rank = gla_w_lr.shape[2]
    v_a = gla_norm_g.shape[1] * GLA_HEADS
    conv_dim = gdn_conv_w.shape[2]
    v_b = gdn_norm_g.shape[1] * GDN_HEADS
    qk_b = (conv_dim - v_b) // 2
    nd = 2 * GDN_HEADS
    dims = (qk_a, v_a, rank, conv_dim, qk_b, v_b, GDN_HEADS)

    o_r = 2 * qk_a + 2 * v_a
    o_c = o_r + rank
    o_g = o_c + conv_dim
    o_s = o_g + v_b
    assert w_in.shape[2] == o_s + 2 * nd
    w_main = jnp.concatenate([w_in[:, :, :o_r], w_in[:, :, o_c:o_s]], axis=2).astype(BF16)
    w_sm = jnp.concatenate([w_in[:, :, o_r:o_c], w_in[:, :, o_s:]], axis=2)
    w_small = jnp.pad(w_sm, ((0, 0), (0, 0), (0, LANES - w_sm.shape[2]))).astype(BF16)
    w_small_t = jnp.swapaxes(w_in[:, :, o_s:], 1, 2).astype(BF16)

    cc = jnp.concatenate([c, c_ctx[None, :], jnp.zeros((MOD_ROWS - bsz - 1, d), F32)], axis=0)
    mod = _modulation(cc, w_ada, b_ada).reshape(depth * MOD_ROWS, 1, 6 * d)

    in_params = (norm1_g.reshape(depth, 1, d), w_main, w_small, w_small_t, gla_w_lr.astype(BF16),
                 gla_b_lr.reshape(depth, 2, 1, qk_a), gdn_conv_w,
                 gdn_a_log.reshape(depth, 1, nd), gdn_dt_bias.reshape(depth, 1, nd),
                 gdn_a_log.reshape(depth, nd, 1), gdn_dt_bias.reshape(depth, nd, 1))
    post_params = (gla_norm_g.reshape(depth, 1, -1), gdn_norm_g.reshape(depth, 1, -1), w_out.astype(BF16),
                   norm2_g.reshape(depth, 1, d), w_ff1.astype(BF16), w_ff2.astype(BF16))

    stream = (ctx, x, 1)
    for l in range(depth):
        last = l == depth - 1
        gqkv, gg, gla, dqkv, dg, small_c, small_r = _in_proj(stream, mod, l, bsz, in_params, dims)
        oaf, oab, obf, obb = _scan(gqkv, gla, dqkv, small_c, small_r, dims)
        args = (mod, l, last, oaf, oab, obf, obb, gg, dg, post_params, final_norm_g, GLA_HEADS, GDN_HEADS)
        xs = _post(stream, *args)
        stream = (xs, xs, 0)
    return xs
```
